```python
import jax, jax.numpy as jnp
from jax import lax
import numpy as np

D_MODEL = 1024
BATCH = 8
SEQ = 2048
DEPTH = 1

HEAD_DIM = 64
N_FOX_HEADS = 8
N_SWA_HEADS = 8
N_SWA_KV_HEADS = 2
SWA_GROUP = N_SWA_HEADS // N_SWA_KV_HEADS
D_FOX = N_FOX_HEADS * HEAD_DIM
D_SWA = N_SWA_HEADS * HEAD_DIM
D_SWA_KV = N_SWA_KV_HEADS * HEAD_DIM
D_MIX = D_FOX + D_SWA
D_IN = 3 * D_FOX + N_FOX_HEADS + D_SWA + 2 * D_SWA_KV
D_FF = 4 * D_MODEL
D_PLE = 256
WINDOW = 128
Q_BLOCK = 128
N_BUCKETS = 32
MAX_DISTANCE = 128
RMS_EPS = 1e-6

kernel_name = "hymba_fox_swa_sandwich_ple_layer"


def rms_norm(x, g):
    xf = x.astype(jnp.float32)
    y = xf * lax.rsqrt(jnp.mean(xf * xf, axis=-1, keepdims=True) + RMS_EPS)
    return (y * g.astype(jnp.float32)).astype(x.dtype)


def t5_bucket(n):
    max_exact = N_BUCKETS // 2
    large = max_exact + (np.log(np.maximum(n, 1) / max_exact)
                         / np.log(MAX_DISTANCE / max_exact)
                         * (N_BUCKETS - max_exact)).astype(np.int32)
    large = np.minimum(large, N_BUCKETS - 1)
    return np.where(n < max_exact, n, large).astype(np.int32)


def forgetting_attention(q, k, v, log_f):
    B, S, H, D = q.shape
    c = jnp.cumsum(log_f, axis=1).transpose(0, 2, 1)
    scale = D ** -0.5
    outs = []
    for blk in range(S // Q_BLOCK):
        lo, hi = blk * Q_BLOCK, (blk + 1) * Q_BLOCK
        s = jnp.einsum('bqhd,bkhd->bhqk', q[:, lo:hi], k[:, :hi]).astype(jnp.float32) * scale
        s = s + c[:, :, lo:hi, None] - c[:, :, None, :hi]
        causal = np.arange(lo, hi)[:, None] >= np.arange(hi)[None, :]
        s = jnp.where(causal, s, -jnp.inf)
        probs = jax.nn.softmax(s, axis=-1).astype(v.dtype)
        outs.append(jnp.einsum('bhqk,bkhd->bqhd', probs, v[:, :hi]))
    return jnp.concatenate(outs, axis=1).reshape(B, S, H * D)


def sliding_window_attention(q, k, v, sinks, rel_bias):
    B, S, Hq, D = q.shape
    Hkv, G, Q = N_SWA_KV_HEADS, SWA_GROUP, Q_BLOCK
    nb = S // Q
    qb = q.reshape(B, nb, Q, Hkv, G, D)

    def band(t):
        pad = jnp.pad(t, ((0, 0), (Q, 0), (0, 0), (0, 0)))
        prev = pad[:, :S].reshape(B, nb, Q, Hkv, D)
        cur = t.reshape(B, nb, Q, Hkv, D)
        return jnp.concatenate([prev, cur], axis=2)

    kb, vb = band(k), band(v)
    s = jnp.einsum('bnqkgd,bnskd->bnkgqs', qb, kb).astype(jnp.float32) * (D ** -0.5)
    i = np.arange(Q)[:, None]
    j = np.arange(2 * Q)[None, :]
    dist = i + Q - j
    in_window = (dist >= 0) & (dist < WINDOW)
    bucket = t5_bucket(np.clip(dist, 0, None))
    bias = jnp.transpose(rel_bias[bucket], (2, 0, 1)).reshape(Hkv, G, Q, 2 * Q)
    s = s + bias.astype(jnp.float32)
    key_pos = np.arange(nb)[:, None] * Q - Q + np.arange(2 * Q)[None, :]
    valid = in_window[None, :, :] & (key_pos >= 0)[:, None, :]
    s = jnp.where(valid[None, :, None, None], s, -jnp.inf)
    sink = sinks.astype(jnp.float32).reshape(Hkv, G)[None, None, :, :, None, None]
    m = jnp.maximum(jnp.max(s, axis=-1, keepdims=True), sink)
    e = jnp.exp(s - m)
    denom = jnp.sum(e, axis=-1, keepdims=True) + jnp.exp(sink - m)
    probs = (e / denom).astype(v.dtype)
    o = jnp.einsum('bnkgqs,bnskd->bnqkgd', probs, vb)
    return o.reshape(B, S, Hq * D)


def setup_inputs(seed: int = 0) -> dict:
    key = jax.random.key(seed)
    ks = jax.random.split(key, 20)
    f32 = jnp.float32
    nrm = lambda k, shape, s: jax.random.normal(k, shape, f32) * s
    gain = lambda k, shape: 1.0 + 0.05 * jax.random.normal(k, shape, f32)
    return {
        "x": jax.random.normal(ks[0], (BATCH, SEQ, D_MODEL), f32),
        "p": jax.random.normal(ks[1], (DEPTH, BATCH, SEQ, D_PLE), f32),
        "w_in": nrm(ks[2], (DEPTH, D_MODEL, D_IN), D_MODEL ** -0.5),
        "b_forget": 1.0 + 0.5 * jax.random.normal(ks[3], (DEPTH, N_FOX_HEADS), f32),
        "w_out": nrm(ks[4], (DEPTH, D_MIX, D_MODEL), D_MIX ** -0.5),
        "rel_bias": nrm(ks[5], (N_BUCKETS, N_SWA_HEADS), 0.5),
        "swa_sinks": nrm(ks[6], (DEPTH, N_SWA_HEADS), 0.5),
        "g_attn_pre": gain(ks[7], (DEPTH, D_MODEL)),
        "g_attn_post": gain(ks[8], (DEPTH, D_MODEL)),
        "w_ff1": nrm(ks[9], (DEPTH, D_MODEL, D_FF), D_MODEL ** -0.5),
        "w_ff2": nrm(ks[10], (DEPTH, D_FF, D_MODEL), D_FF ** -0.5),
        "g_ff_pre": gain(ks[11], (DEPTH, D_MODEL)),
        "g_ff_post": gain(ks[12], (DEPTH, D_MODEL)),
        "w_ple": nrm(ks[13], (DEPTH, D_PLE, D_MODEL), D_PLE ** -0.5),
        "w_ple_gate": nrm(ks[14], (DEPTH, D_MODEL, D_MODEL), D_MODEL ** -0.5),
        "g_ple_post": gain(ks[15], (DEPTH, D_MODEL)),
    }


def reference(x, p, w_in, b_forget, w_out, rel_bias, swa_sinks, g_attn_pre, g_attn_post,
              w_ff1, w_ff2, g_ff_pre, g_ff_post, w_ple, w_ple_gate, g_ple_post):
    B, S, _ = x.shape
    h = x
    splits = np.cumsum([D_FOX, D_FOX, D_FOX, N_FOX_HEADS, D_SWA, D_SWA_KV])
    for i in range(DEPTH):
        a = rms_norm(h, g_attn_pre[i])
        z = a @ w_in[i]
        fq, fk, fv, ff, sq, sk, sv = jnp.split(z, splits, axis=-1)
        log_f = jax.nn.log_sigmoid(ff.astype(jnp.float32) + b_forget[i].astype(jnp.float32))
        fox = forgetting_attention(
            fq.reshape(B, S, N_FOX_HEADS, HEAD_DIM),
            fk.reshape(B, S, N_FOX_HEADS, HEAD_DIM),
            fv.reshape(B, S, N_FOX_HEADS, HEAD_DIM), log_f)
        swa = sliding_window_attention(
            sq.reshape(B, S, N_SWA_HEADS, HEAD_DIM),
            sk.reshape(B, S, N_SWA_KV_HEADS, HEAD_DIM),
            sv.reshape(B, S, N_SWA_KV_HEADS, HEAD_DIM), swa_sinks[i], rel_bias)
        mix = jnp.concatenate([fox, swa], axis=-1) @ w_out[i]
        h = h + rms_norm(mix, g_attn_post[i])
        m = rms_norm(h, g_ff_pre[i])
        y = jnp.square(jax.nn.relu(m @ w_ff1[i])) @ w_ff2[i]
        h = h + rms_norm(y, g_ff_post[i])
        gate = jax.nn.sigmoid(h @ w_ple_gate[i])
        e = (p[i].astype(h.dtype) @ w_ple[i]) * gate
        h = h + rms_norm(e, g_ple_post[i])
    return h
```

```python
import functools

import jax
import jax.numpy as jnp
import numpy as np
from jax import lax
from jax.experimental import pallas as pl
from jax.experimental.pallas import tpu as pltpu

D_MODEL = 1024
HEAD_DIM = 64
N_FOX_HEADS = 8
N_SWA_HEADS = 8
N_SWA_KV_HEADS = 2
SWA_GROUP = N_SWA_HEADS // N_SWA_KV_HEADS
D_FOX = N_FOX_HEADS * HEAD_DIM
D_SWA = N_SWA_HEADS * HEAD_DIM
D_SWA_KV = N_SWA_KV_HEADS * HEAD_DIM
D_FF = 4 * D_MODEL
D_PLE = 256
WINDOW = 128
Q_BLOCK = 128
N_BUCKETS = 32
MAX_DISTANCE = 128
RMS_EPS = 1e-6

LANES = 128
N_PAIRS = N_FOX_HEADS // 2
N_SPLIT = 3
NEG_BIG = -1e30

PROJ_TM = 512
FOX_T = 256
POST_TM = 512
FF_CHUNK = 1024
VMEM_LIMIT = 56 * 1024 * 1024

C_FQ, C_FK, C_FV = 0, D_FOX, 2 * D_FOX
C_SQ = 3 * D_FOX
C_SK = C_SQ + D_SWA
C_SV = C_SK + D_SWA_KV
C_FF = C_SV + D_SWA_KV
D_PROJ = C_FF + LANES


def _rms(v):
    return v * lax.rsqrt(jnp.mean(v * v, axis=-1, keepdims=True) + RMS_EPS)


def _dot(a, b):
    return jnp.dot(a, b, preferred_element_type=jnp.float32)


def _dot_nt(a, b):
    return lax.dot_general(a, b, (((1,), (1,)), ((), ())),
                           preferred_element_type=jnp.float32)


def _proj_kernel(x_ref, g_ref, w_ref, bf_ref, tri_ref, pq_ref, pk_ref, oq_ref, ok_ref,
                 fq_ref, fk_ref, fv_ref, qa_ref, ka_ref, sq_ref, sk_ref, sv_ref,
                 carry_ref):
    @pl.when(pl.program_id(1) == 0)
    def _():
        carry_ref[...] = jnp.zeros_like(carry_ref)

    a = (_rms(x_ref[...]) * g_ref[...]).astype(jnp.bfloat16)

    def proj(lo, width):
        return _dot(a, w_ref[:, lo:lo + width])

    fq_ref[...] = proj(C_FQ, D_FOX).astype(jnp.bfloat16)
    fk_ref[...] = proj(C_FK, D_FOX).astype(jnp.bfloat16)
    fv_ref[...] = proj(C_FV, D_FOX).astype(jnp.bfloat16)
    sq_ref[...] = proj(C_SQ, D_SWA).astype(jnp.bfloat16)

    lane = lax.broadcasted_iota(jnp.int32, (PROJ_TM, LANES), 1)
    for lo, out_ref in ((C_SK, sk_ref), (C_SV, sv_ref)):
        z = proj(lo, D_SWA_KV)
        zr = pltpu.roll(z, HEAD_DIM, 1)
        out_ref[:, :LANES] = jnp.where(lane < HEAD_DIM, z, zr).astype(jnp.bfloat16)
        out_ref[:, LANES:] = jnp.where(lane < HEAD_DIM, zr, z).astype(jnp.bfloat16)

    v = proj(C_FF, LANES) + bf_ref[...]
    logf = jnp.minimum(v, 0.0) - jnp.log1p(jnp.exp(-jnp.abs(v)))
    c = carry_ref[...] + sum(_dot(tri_ref[...], t) for t in _split_bf16(logf))
    carry_ref[...] = c[PROJ_TM - 1:PROJ_TM, :]
    cs = jnp.concatenate(_split_bf16(c), axis=-1)
    qa_ref[...] = (_dot(cs, pq_ref[...]) + oq_ref[...]).astype(jnp.bfloat16)
    ka_ref[...] = (_dot(cs, pk_ref[...]) + ok_ref[...]).astype(jnp.bfloat16)


def _split_bf16(v):
    terms = []
    r = v
    for _ in range(N_SPLIT):
        t = r.astype(jnp.bfloat16)
        terms.append(t)
        r = r - t.astype(jnp.float32)
    return terms


def _aug_constants():
    pq = np.zeros((N_SPLIT * LANES, N_PAIRS * LANES), np.float32)
    pk = np.zeros_like(pq)
    oq = np.zeros((1, N_PAIRS * LANES), np.float32)
    ok = np.zeros_like(oq)
    for pair in range(N_PAIRS):
        for e in range(2):
            head = 2 * pair + e
            base = pair * LANES + 2 * N_SPLIT * e
            for t in range(N_SPLIT):
                pq[t * LANES + head, base + t] = 1.0
                ok[0, base + t] = 1.0
                oq[0, base + N_SPLIT + t] = 1.0
                pk[t * LANES + head, base + N_SPLIT + t] = -1.0
    return pq, pk, oq, ok


def _proj_call(x2, g_pre, w_proj, b_ff, batch, seq):
    n = batch * seq
    steps = seq // PROJ_TM
    pq, pk, oq, ok = _aug_constants()
    tri = np.tril(np.ones((PROJ_TM, PROJ_TM), np.float32))
    row = lambda b, s: (b * steps + s, 0)
    const = lambda b, s: (0, 0)
    bf16 = jnp.bfloat16

    def out(width):
        return (jax.ShapeDtypeStruct((n, width), bf16), pl.BlockSpec((PROJ_TM, width), row))

    outs = [out(D_FOX), out(D_FOX), out(D_FOX), out(N_PAIRS * LANES), out(N_PAIRS * LANES),
            out(D_SWA), out(2 * LANES), out(2 * LANES)]
    return pl.pallas_call(
        _proj_kernel,
        grid=(batch, steps),
        in_specs=[
            pl.BlockSpec((PROJ_TM, D_MODEL), row),
            pl.BlockSpec((1, D_MODEL), const),
            pl.BlockSpec((D_MODEL, D_PROJ), const),
            pl.BlockSpec((1, LANES), const),
            pl.BlockSpec((PROJ_TM, PROJ_TM), const),
            pl.BlockSpec(pq.shape, const),
            pl.BlockSpec(pk.shape, const),
            pl.BlockSpec(oq.shape, const),
            pl.BlockSpec(ok.shape, const),
        ],
        out_specs=[o[1] for o in outs],
        out_shape=[o[0] for o in outs],
        scratch_shapes=[pltpu.VMEM((1, LANES), jnp.float32)],
        compiler_params=pltpu.CompilerParams(
            dimension_semantics=("parallel", "arbitrary"),
            vmem_limit_bytes=VMEM_LIMIT),
        name="proj",
    )(x2, g_pre, w_proj, b_ff, jnp.asarray(tri, bf16), jnp.asarray(pq, bf16),
      jnp.asarray(pk, bf16), jnp.asarray(oq), jnp.asarray(ok))


def _fox_kernel(q_ref, qa_ref, k_ref, ka_ref, v_ref, o_ref, m_ref, l_ref, acc_ref):
    t = FOX_T
    i = pl.program_id(2)
    q = q_ref[0]
    qa = qa_ref[0]
    lane = lax.broadcasted_iota(jnp.int32, (t, LANES), 1)
    zero = jnp.zeros_like(q)
    n_aug = 2 * N_SPLIT
    q0 = jnp.concatenate([jnp.where(lane < HEAD_DIM, q, zero),
                          jnp.where(lane < n_aug, qa, zero)], axis=-1)
    q1 = jnp.concatenate([jnp.where(lane >= HEAD_DIM, q, zero),
                          jnp.where((lane >= n_aug) & (lane < 2 * n_aug), qa, zero)], axis=-1)
    qq = jnp.concatenate([q0, q1], axis=0)

    m_ref[...] = jnp.full_like(m_ref, NEG_BIG)
    l_ref[...] = jnp.zeros_like(l_ref)
    acc_ref[...] = jnp.zeros_like(acc_ref)

    def tile(j, masked):
        rows = pl.ds(pl.multiple_of(j * t, t), t)
        kk = jnp.concatenate([k_ref[0, rows, :], ka_ref[0, rows, :]], axis=-1)
        s = _dot_nt(qq, kk)
        if masked:
            r = lax.broadcasted_iota(jnp.int32, (2 * t, t), 0)
            c = lax.broadcasted_iota(jnp.int32, (2 * t, t), 1)
            r = jnp.where(r >= t, r - t, r)
            s = jnp.where(r >= c, s, NEG_BIG)
        m_old = m_ref[...]
        m_new = jnp.maximum(m_old, jnp.max(s, axis=-1, keepdims=True))
        alpha = jnp.exp(m_old - m_new)
        p = jnp.exp(s - m_new)
        l_ref[...] = alpha * l_ref[...] + jnp.sum(p, axis=-1, keepdims=True)
        acc_ref[...] = alpha * acc_ref[...] + _dot(p.astype(jnp.bfloat16), v_ref[0, rows, :])
        m_ref[...] = m_new

    def body(j, carry):
        tile(j, False)
        return carry

    lax.fori_loop(0, i, body, 0)
    tile(i, True)

    o = acc_ref[...] / l_ref[...]
    o_ref[0] = jnp.where(lane < HEAD_DIM, o[:t], o[t:]).astype(o_ref.dtype)


def _fox_call(fq, qa, fk, ka, fv, batch, seq):
    t = FOX_T
    shape3 = (batch, seq, N_PAIRS * LANES)
    q_spec = pl.BlockSpec((1, t, LANES), lambda b, p, i: (b, i, p))
    kv_spec = pl.BlockSpec((1, seq, LANES), lambda b, p, i: (b, 0, p))
    return pl.pallas_call(
        _fox_kernel,
        grid=(batch, N_PAIRS, seq // t),
        in_specs=[q_spec, q_spec, kv_spec, kv_spec, kv_spec],
        out_specs=q_spec,
        out_shape=jax.ShapeDtypeStruct(shape3, jnp.bfloat16),
        scratch_shapes=[pltpu.VMEM((2 * t, 1), jnp.float32),
                        pltpu.VMEM((2 * t, 1), jnp.float32),
                        pltpu.VMEM((2 * t, LANES), jnp.float32)],
        compiler_params=pltpu.CompilerParams(
            dimension_semantics=("parallel", "parallel", "parallel"),
            vmem_limit_bytes=VMEM_LIMIT),
        name="fox",
    )(fq.reshape(shape3), qa.reshape(shape3), fk.reshape(shape3), ka.reshape(shape3),
      fv.reshape(shape3))


def _t5_bucket(n):
    max_exact = N_BUCKETS // 2
    large = max_exact + (np.log(np.maximum(n, 1) / max_exact)
                         / np.log(MAX_DISTANCE / max_exact)
                         * (N_BUCKETS - max_exact)).astype(np.int32)
    large = np.minimum(large, N_BUCKETS - 1)
    return np.where(n < max_exact, n, large).astype(np.int32)


def _band_buckets():
    i = np.arange(Q_BLOCK)[:, None]
    j = np.arange(2 * Q_BLOCK)[None, :]
    dist = i + Q_BLOCK - j
    in_window = (dist >= 0) & (dist < WINDOW)
    return np.where(in_window, _t5_bucket(np.clip(dist, 0, None)), -1).astype(np.int32)


def _swa_kernel(bucket_ref, rel_ref, sink_ref, q_ref, k_ref, v_ref, o_ref, bias_ref, *, seq):
    qb = Q_BLOCK
    kv = pl.program_id(1)

    @pl.when((pl.program_id(0) == 0) & (kv == 0))
    def _():
        bucket = bucket_ref[...]
        for h in range(N_SWA_HEADS):
            bias = jnp.full(bucket.shape, NEG_BIG, jnp.float32)
            for b in range(N_BUCKETS):
                bias = jnp.where(bucket == b, rel_ref[b, h], bias)
            bias_ref[h] = bias

    lane = lax.broadcasted_iota(jnp.int32, (qb, LANES), 1)
    lower = lane < HEAD_DIM

    def block(n, first):
        rows = pl.ds(pl.multiple_of(n * qb, qb), qb)
        if first:
            band = rows
            cols = slice(qb, 2 * qb)
        else:
            band = pl.ds(pl.multiple_of((n - 1) * qb, qb), 2 * qb)
            cols = slice(0, 2 * qb)
        kb = k_ref[0, band, :]
        vb = v_ref[0, band, :]
        for pair in range(SWA_GROUP // 2):
            qp = q_ref[0, rows, pair * LANES:(pair + 1) * LANES]
            zero = jnp.zeros_like(qp)
            qs = jnp.concatenate([jnp.where(lower, qp, zero), jnp.where(lower, zero, qp)], axis=0)
            s = _dot_nt(qs, kb)
            outs = []
            for e in range(2):
                h = kv * SWA_GROUP + 2 * pair + e
                sink = sink_ref[h]
                sh = s[e * qb:(e + 1) * qb] + bias_ref[h, :, cols]
                m = jnp.maximum(jnp.max(sh, axis=-1, keepdims=True), sink)
                p = jnp.exp(sh - m)
                denom = jnp.sum(p, axis=-1, keepdims=True) + jnp.exp(sink - m)
                outs.append(_dot(p.astype(jnp.bfloat16), vb) / denom)
            o_ref[0, rows, pair * LANES:(pair + 1) * LANES] = (
                jnp.where(lower, outs[0], outs[1]).astype(o_ref.dtype))

    block(0, True)

    def body(n, carry):
        block(n, False)
        return carry

    lax.fori_loop(1, seq // qb, body, 0)


def _swa_call(sq, sk, sv, rel_bias, sinks, batch, seq):
    bucket = _band_buckets()
    width = SWA_GROUP * HEAD_DIM
    smem = pl.BlockSpec(memory_space=pltpu.SMEM)
    q_spec = pl.BlockSpec((1, seq, width), lambda b, kv: (b, 0, kv))
    kv_spec = pl.BlockSpec((1, seq, LANES), lambda b, kv: (b, 0, kv))
    return pl.pallas_call(
        functools.partial(_swa_kernel, seq=seq),
        grid=(batch, N_SWA_KV_HEADS),
        in_specs=[pl.BlockSpec(bucket.shape, lambda b, kv: (0, 0)), smem, smem,
                  q_spec, kv_spec, kv_spec],
        out_specs=q_spec,
        out_shape=jax.ShapeDtypeStruct((batch, seq, D_SWA), jnp.bfloat16),
        scratch_shapes=[pltpu.VMEM((N_SWA_HEADS,) + bucket.shape, jnp.float32)],
        compiler_params=pltpu.CompilerParams(
            dimension_semantics=("arbitrary", "arbitrary"),
            vmem_limit_bytes=VMEM_LIMIT),
        name="swa",
    )(jnp.asarray(bucket), rel_bias, sinks, sq.reshape(batch, seq, D_SWA),
      sk.reshape(batch, seq, 2 * LANES), sv.reshape(batch, seq, 2 * LANES))


def _post_kernel(fox_ref, swa_ref, x_ref, p_ref, wo_ref, w1_ref, w2_ref, wg_ref, wp_ref,
                 gains_ref, o_ref):
    bf16 = jnp.bfloat16
    mix = _dot(jnp.concatenate([fox_ref[...], swa_ref[...]], axis=-1), wo_ref[...])
    h = x_ref[...] + _rms(mix) * gains_ref[0:1, :]

    m = (_rms(h) * gains_ref[1:2, :]).astype(bf16)
    y = jnp.zeros((POST_TM, D_MODEL), jnp.float32)
    for c in range(D_FF // FF_CHUNK):
        cols = slice(c * FF_CHUNK, (c + 1) * FF_CHUNK)
        hid = jnp.square(jnp.maximum(_dot(m, w1_ref[:, cols]), 0.0))
        y = y + _dot(hid.astype(bf16), w2_ref[cols, :])
    h = h + _rms(y) * gains_ref[2:3, :]

    gate = jax.nn.sigmoid(_dot(h.astype(bf16), wg_ref[...]))
    e = _dot(p_ref[...].astype(bf16), wp_ref[...]) * gate
    o_ref[...] = h + _rms(e) * gains_ref[3:4, :]


def _post_call(fox, swa, x2, p2, wo, w1, w2, wg, wp, gains):
    n = x2.shape[0]
    row = lambda i: (i, 0)
    const = lambda i: (0, 0)

    def resident(shape):
        return pl.BlockSpec(shape, const, pipeline_mode=pl.Buffered(1))

    return pl.pallas_call(
        _post_kernel,
        grid=(n // POST_TM,),
        in_specs=[
            pl.BlockSpec((POST_TM, D_FOX), row),
            pl.BlockSpec((POST_TM, D_SWA), row),
            pl.BlockSpec((POST_TM, D_MODEL), row),
            pl.BlockSpec((POST_TM, D_PLE), row),
            resident(wo.shape), resident(w1.shape), resident(w2.shape),
            resident(wg.shape), resident(wp.shape), resident(gains.shape),
        ],
        out_specs=pl.BlockSpec((POST_TM, D_MODEL), row),
        out_shape=jax.ShapeDtypeStruct((n, D_MODEL), jnp.float32),
        compiler_params=pltpu.CompilerParams(
            dimension_semantics=("parallel",),
            vmem_limit_bytes=VMEM_LIMIT),
        name="post",
    )(fox, swa, x2, p2, wo, w1, w2, wg, wp, gains)


def _proj_weight(w_in):
    scale = HEAD_DIM ** -0.5
    splits = np.cumsum([D_FOX, D_FOX, D_FOX, N_FOX_HEADS, D_SWA, D_SWA_KV])
    fq, fk, fv, ff, sq, sk, sv = jnp.split(w_in, splits, axis=-1)
    ff = jnp.pad(ff, ((0, 0), (0, LANES - N_FOX_HEADS)))
    return jnp.concatenate([fq * scale, fk, fv, sq * scale, sk, sv, ff],
                           axis=-1).astype(jnp.bfloat16)


def kernel(x, p, w_in, b_forget, w_out, rel_bias, swa_sinks, g_attn_pre, g_attn_post,
           w_ff1, w_ff2, g_ff_pre, g_ff_post, w_ple, w_ple_gate, g_ple_post):
    batch, seq, _ = x.shape
    n = batch * seq
    bf16 = jnp.bfloat16
    h = x.reshape(n, D_MODEL)
    for i in range(p.shape[0]):
        b_ff = jnp.pad(b_forget[i], (0, LANES - N_FOX_HEADS)).reshape(1, LANES)
        fq, fk, fv, qa, ka, sq, sk, sv = _proj_call(
            h, g_attn_pre[i].reshape(1, D_MODEL), _proj_weight(w_in[i]), b_ff, batch, seq)
        fox = _fox_call(fq, qa, fk, ka, fv, batch, seq).reshape(n, D_FOX)
        swa = _swa_call(sq, sk, sv, rel_bias, swa_sinks[i], batch, seq).reshape(n, D_SWA)
        gains = jnp.stack([g_attn_post[i], g_ff_pre[i], g_ff_post[i], g_ple_post[i]])
        h = _post_call(fox, swa, h, p[i].reshape(n, D_PLE),
                       w_out[i].astype(bf16), w_ff1[i].astype(bf16), w_ff2[i].astype(bf16),
                       w_ple_gate[i].astype(bf16), w_ple[i].astype(bf16), gains)
    return h.reshape(batch, seq, D_MODEL)
```

```python
import functools

import jax
import jax.numpy as jnp
import numpy as np
from jax import lax
from jax.experimental import pallas as pl
from jax.experimental.pallas import tpu as pltpu

D_MODEL = 1024
HEAD_DIM = 64
N_FOX_HEADS = 8
N_SWA_HEADS = 8
N_SWA_KV_HEADS = 2
SWA_GROUP = N_SWA_HEADS // N_SWA_KV_HEADS
D_FOX = N_FOX_HEADS * HEAD_DIM
D_SWA = N_SWA_HEADS * HEAD_DIM
D_SWA_KV = N_SWA_KV_HEADS * HEAD_DIM
D_FF = 4 * D_MODEL
D_PLE = 256
WINDOW = 128
Q_BLOCK = 128
N_BUCKETS = 32
MAX_DISTANCE = 128
RMS_EPS = 1e-6

LANES = 128
N_PAIRS = N_FOX_HEADS // 2
N_SPLIT = 3
NEG_BIG = -1e30
LOG2E = float(np.log2(np.e))

PROJ_TM = 512
FOX_T = 512
POST_TM = 512
FF_CHUNK = 1024
VMEM_LIMIT = 56 * 1024 * 1024

C_FQ, C_FK, C_FV = 0, D_FOX, 2 * D_FOX
C_SQ = 3 * D_FOX
C_SK = C_SQ + D_SWA
C_SV = C_SK + D_SWA_KV
C_FF = C_SV + D_SWA_KV
D_PROJ = C_FF + LANES


def _rms(v):
    return v * lax.rsqrt(jnp.mean(v * v, axis=-1, keepdims=True) + RMS_EPS)


def _dot(a, b):
    return jnp.dot(a, b, preferred_element_type=jnp.float32)


def _dot_nt(a, b):
    return lax.dot_general(a, b, (((1,), (1,)), ((), ())),
                           preferred_element_type=jnp.float32)


def _proj_kernel(x_ref, g_ref, w_ref, bf_ref, tri_ref, pq_ref, pk_ref, oq_ref, ok_ref,
                 fq_ref, fk_ref, fv_ref, qa_ref, ka_ref, sq_ref, sk_ref, sv_ref,
                 carry_ref):
    @pl.when(pl.program_id(1) == 0)
    def _():
        carry_ref[...] = jnp.zeros_like(carry_ref)

    a = (_rms(x_ref[...]) * g_ref[...]).astype(jnp.bfloat16)

    def proj(lo, width):
        return _dot(a, w_ref[:, lo:lo + width])

    fq_ref[...] = proj(C_FQ, D_FOX).astype(jnp.bfloat16)
    fk_ref[...] = proj(C_FK, D_FOX).astype(jnp.bfloat16)
    fv_ref[...] = proj(C_FV, D_FOX).astype(jnp.bfloat16)
    sq_ref[...] = proj(C_SQ, D_SWA).astype(jnp.bfloat16)

    lane = lax.broadcasted_iota(jnp.int32, (PROJ_TM, LANES), 1)
    for lo, out_ref in ((C_SK, sk_ref), (C_SV, sv_ref)):
        z = proj(lo, D_SWA_KV)
        zr = pltpu.roll(z, HEAD_DIM, 1)
        out_ref[:, :LANES] = jnp.where(lane < HEAD_DIM, z, zr).astype(jnp.bfloat16)
        out_ref[:, LANES:] = jnp.where(lane < HEAD_DIM, zr, z).astype(jnp.bfloat16)

    v = proj(C_FF, LANES) + bf_ref[...]
    logf = jnp.minimum(v, 0.0) - jnp.log1p(jnp.exp(-jnp.abs(v)))
    c = carry_ref[...] + sum(_dot(tri_ref[...], t) for t in _split_bf16(logf * LOG2E))
    carry_ref[...] = c[PROJ_TM - 1:PROJ_TM, :]
    cs = jnp.concatenate(_split_bf16(c), axis=-1)
    qa_ref[...] = (_dot(cs, pq_ref[...]) + oq_ref[...]).astype(jnp.bfloat16)
    ka_ref[...] = (_dot(cs, pk_ref[...]) + ok_ref[...]).astype(jnp.bfloat16)


def _split_bf16(v):
    terms = []
    r = v
    for _ in range(N_SPLIT):
        t = r.astype(jnp.bfloat16)
        terms.append(t)
        r = r - t.astype(jnp.float32)
    return terms


def _aug_constants():
    pq = np.zeros((N_SPLIT * LANES, N_PAIRS * LANES), np.float32)
    pk = np.zeros_like(pq)
    oq = np.zeros((1, N_PAIRS * LANES), np.float32)
    ok = np.zeros_like(oq)
    for pair in range(N_PAIRS):
        for e in range(2):
            head = 2 * pair + e
            base = pair * LANES + 2 * N_SPLIT * e
            for t in range(N_SPLIT):
                pq[t * LANES + head, base + t] = 1.0
                ok[0, base + t] = 1.0
                oq[0, base + N_SPLIT + t] = 1.0
                pk[t * LANES + head, base + N_SPLIT + t] = -1.0
    return pq, pk, oq, ok


def _proj_call(x2, g_pre, w_proj, b_ff, batch, seq):
    n = batch * seq
    steps = seq // PROJ_TM
    pq, pk, oq, ok = _aug_constants()
    tri = np.tril(np.ones((PROJ_TM, PROJ_TM), np.float32))
    row = lambda b, s: (b * steps + s, 0)
    const = lambda b, s: (0, 0)
    bf16 = jnp.bfloat16

    def out(width):
        return (jax.ShapeDtypeStruct((n, width), bf16), pl.BlockSpec((PROJ_TM, width), row))

    outs = [out(D_FOX), out(D_FOX), out(D_FOX), out(N_PAIRS * LANES), out(N_PAIRS * LANES),
            out(D_SWA), out(2 * LANES), out(2 * LANES)]
    return pl.pallas_call(
        _proj_kernel,
        grid=(batch, steps),
        in_specs=[
            pl.BlockSpec((PROJ_TM, D_MODEL), row),
            pl.BlockSpec((1, D_MODEL), const),
            pl.BlockSpec((D_MODEL, D_PROJ), const),
            pl.BlockSpec((1, LANES), const),
            pl.BlockSpec((PROJ_TM, PROJ_TM), const),
            pl.BlockSpec(pq.shape, const),
            pl.BlockSpec(pk.shape, const),
            pl.BlockSpec(oq.shape, const),
            pl.BlockSpec(ok.shape, const),
        ],
        out_specs=[o[1] for o in outs],
        out_shape=[o[0] for o in outs],
        scratch_shapes=[pltpu.VMEM((1, LANES), jnp.float32)],
        compiler_params=pltpu.CompilerParams(
            dimension_semantics=("parallel", "arbitrary"),
            vmem_limit_bytes=VMEM_LIMIT),
        name="proj",
    )(x2, g_pre, w_proj, b_ff, jnp.asarray(tri, bf16), jnp.asarray(pq, bf16),
      jnp.asarray(pk, bf16), jnp.asarray(oq), jnp.asarray(ok))


def _fox_kernel(q_ref, qa_ref, k_ref, ka_ref, v_ref, o_ref, qm_ref, m_ref, acc_ref):
    t = FOX_T
    i = pl.program_id(2)
    q = q_ref[0]
    qa = qa_ref[0]
    lane = lax.broadcasted_iota(jnp.int32, (t, LANES), 1)
    lower = lane < HEAD_DIM
    zero = jnp.zeros_like(q)
    n_aug = 2 * N_SPLIT
    qm_ref[0] = jnp.concatenate([jnp.where(lower, q, zero),
                                 jnp.where(lane < n_aug, qa, zero)], axis=-1)
    qm_ref[1] = jnp.concatenate([jnp.where(lower, zero, q),
                                 jnp.where((lane >= n_aug) & (lane < 2 * n_aug), qa, zero)],
                                axis=-1)
    m_ref[...] = jnp.full_like(m_ref, NEG_BIG)
    acc_ref[...] = jnp.zeros_like(acc_ref)

    def tile(j, masked):
        rows = pl.ds(pl.multiple_of(j * t, t), t)
        kk = jnp.concatenate([k_ref[0, rows, :], ka_ref[0, rows, :]], axis=-1)
        v = v_ref[0, rows, :]
        one = jnp.ones_like(v)
        if masked:
            causal = (lax.broadcasted_iota(jnp.int32, (t, t), 0)
                      >= lax.broadcasted_iota(jnp.int32, (t, t), 1))
        for e in range(2):
            s = _dot_nt(qm_ref[e], kk)
            if masked:
                s = jnp.where(causal, s, NEG_BIG)
            m_old = m_ref[e]
            m_new = jnp.maximum(m_old, jnp.max(s, axis=-1, keepdims=True))
            p = jnp.concatenate(
                [jnp.exp2(s[:, c * LANES:(c + 1) * LANES] - m_new) for c in range(t // LANES)],
                axis=-1).astype(jnp.bfloat16)
            ve = jnp.where(lower, v, one) if e == 0 else jnp.where(lower, one, v)
            acc_ref[e] = jnp.exp2(m_old - m_new) * acc_ref[e] + _dot(p, ve)
            m_ref[e] = m_new

    def body(j, carry):
        tile(j, False)
        return carry

    lax.fori_loop(0, i, body, 0)
    tile(i, True)

    outs = []
    for e in range(2):
        acc = acc_ref[e]
        outs.append(acc / pltpu.roll(acc, HEAD_DIM, 1))
    o_ref[0] = jnp.where(lower, outs[0], outs[1]).astype(o_ref.dtype)


def _fox_call(fq, qa, fk, ka, fv, batch, seq):
    t = FOX_T
    shape3 = (batch, seq, N_PAIRS * LANES)
    q_spec = pl.BlockSpec((1, t, LANES), lambda b, p, i: (b, i, p))
    kv_spec = pl.BlockSpec((1, seq, LANES), lambda b, p, i: (b, 0, p))
    return pl.pallas_call(
        _fox_kernel,
        grid=(batch, N_PAIRS, seq // t),
        in_specs=[q_spec, q_spec, kv_spec, kv_spec, kv_spec],
        out_specs=q_spec,
        out_shape=jax.ShapeDtypeStruct(shape3, jnp.bfloat16),
        scratch_shapes=[pltpu.VMEM((2, t, 2 * LANES), jnp.bfloat16),
                        pltpu.VMEM((2, t, LANES), jnp.float32),
                        pltpu.VMEM((2, t, LANES), jnp.float32)],
        compiler_params=pltpu.CompilerParams(
            dimension_semantics=("parallel", "parallel", "parallel"),
            vmem_limit_bytes=VMEM_LIMIT),
        name="fox",
    )(fq.reshape(shape3), qa.reshape(shape3), fk.reshape(shape3), ka.reshape(shape3),
      fv.reshape(shape3))


def _t5_bucket(n):
    max_exact = N_BUCKETS // 2
    large = max_exact + (np.log(np.maximum(n, 1) / max_exact)
                         / np.log(MAX_DISTANCE / max_exact)
                         * (N_BUCKETS - max_exact)).astype(np.int32)
    large = np.minimum(large, N_BUCKETS - 1)
    return np.where(n < max_exact, n, large).astype(np.int32)


def _band_buckets():
    i = np.arange(Q_BLOCK)[:, None]
    j = np.arange(2 * Q_BLOCK)[None, :]
    dist = i + Q_BLOCK - j
    in_window = (dist >= 0) & (dist < WINDOW)
    return np.where(in_window, _t5_bucket(np.clip(dist, 0, None)), -1).astype(np.int32)


def _swa_kernel(bucket_ref, rel_ref, sink_ref, q_ref, k_ref, v_ref, o_ref, bias_ref, *, seq):
    qb = Q_BLOCK
    kv = pl.program_id(1)

    @pl.when((pl.program_id(0) == 0) & (kv == 0))
    def _():
        bucket = bucket_ref[...]
        for h in range(N_SWA_HEADS):
            bias = jnp.full(bucket.shape, NEG_BIG, jnp.float32)
            for b in range(N_BUCKETS):
                bias = jnp.where(bucket == b, rel_ref[b, h], bias)
            bias_ref[h] = bias

    lane = lax.broadcasted_iota(jnp.int32, (qb, LANES), 1)
    lower = lane < HEAD_DIM

    def block(n, first):
        rows = pl.ds(pl.multiple_of(n * qb, qb), qb)
        if first:
            band = rows
            cols = slice(qb, 2 * qb)
        else:
            band = pl.ds(pl.multiple_of((n - 1) * qb, qb), 2 * qb)
            cols = slice(0, 2 * qb)
        kb = k_ref[0, band, :]
        vb = v_ref[0, band, :]
        for pair in range(SWA_GROUP // 2):
            qp = q_ref[0, rows, pair * LANES:(pair + 1) * LANES]
            zero = jnp.zeros_like(qp)
            qs = jnp.concatenate([jnp.where(lower, qp, zero), jnp.where(lower, zero, qp)], axis=0)
            s = _dot_nt(qs, kb)
            outs = []
            for e in range(2):
                h = kv * SWA_GROUP + 2 * pair + e
                sink = sink_ref[h]
                sh = s[e * qb:(e + 1) * qb] + bias_ref[h, :, cols]
                m = jnp.maximum(jnp.max(sh, axis=-1, keepdims=True), sink)
                p = jnp.exp(sh - m)
                denom = jnp.sum(p, axis=-1, keepdims=True) + jnp.exp(sink - m)
                outs.append(_dot(p.astype(jnp.bfloat16), vb) / denom)
            o_ref[0, rows, pair * LANES:(pair + 1) * LANES] = (
                jnp.where(lower, outs[0], outs[1]).astype(o_ref.dtype))

    block(0, True)

    def body(n, carry):
        block(n, False)
        return carry

    lax.fori_loop(1, seq // qb, body, 0)


def _swa_call(sq, sk, sv, rel_bias, sinks, batch, seq):
    bucket = _band_buckets()
    width = SWA_GROUP * HEAD_DIM
    smem = pl.BlockSpec(memory_space=pltpu.SMEM)
    q_spec = pl.BlockSpec((1, seq, width), lambda b, kv: (b, 0, kv))
    kv_spec = pl.BlockSpec((1, seq, LANES), lambda b, kv: (b, 0, kv))
    return pl.pallas_call(
        functools.partial(_swa_kernel, seq=seq),
        grid=(batch, N_SWA_KV_HEADS),
        in_specs=[pl.BlockSpec(bucket.shape, lambda b, kv: (0, 0)), smem, smem,
                  q_spec, kv_spec, kv_spec],
        out_specs=q_spec,
        out_shape=jax.ShapeDtypeStruct((batch, seq, D_SWA), jnp.bfloat16),
        scratch_shapes=[pltpu.VMEM((N_SWA_HEADS,) + bucket.shape, jnp.float32)],
        compiler_params=pltpu.CompilerParams(
            dimension_semantics=("arbitrary", "arbitrary"),
            vmem_limit_bytes=VMEM_LIMIT),
        name="swa",
    )(jnp.asarray(bucket), rel_bias, sinks, sq.reshape(batch, seq, D_SWA),
      sk.reshape(batch, seq, 2 * LANES), sv.reshape(batch, seq, 2 * LANES))


def _post_kernel(fox_ref, swa_ref, x_ref, p_ref, wo_ref, w1_ref, w2_ref, wg_ref, wp_ref,
                 gains_ref, o_ref):
    bf16 = jnp.bfloat16
    mix = _dot(jnp.concatenate([fox_ref[...], swa_ref[...]], axis=-1), wo_ref[...])
    h = x_ref[...] + _rms(mix) * gains_ref[0:1, :]

    m = (_rms(h) * gains_ref[1:2, :]).astype(bf16)
    y = jnp.zeros((POST_TM, D_MODEL), jnp.float32)
    for c in range(D_FF // FF_CHUNK):
        cols = slice(c * FF_CHUNK, (c + 1) * FF_CHUNK)
        hid = jnp.square(jnp.maximum(_dot(m, w1_ref[:, cols]), 0.0))
        y = y + _dot(hid.astype(bf16), w2_ref[cols, :])
    h = h + _rms(y) * gains_ref[2:3, :]

    gate = jax.nn.sigmoid(_dot(h.astype(bf16), wg_ref[...]))
    e = _dot(p_ref[...].astype(bf16), wp_ref[...]) * gate
    o_ref[...] = h + _rms(e) * gains_ref[3:4, :]


def _post_call(fox, swa, x2, p2, wo, w1, w2, wg, wp, gains):
    n = x2.shape[0]
    row = lambda i: (i, 0)
    const = lambda i: (0, 0)

    def resident(shape):
        return pl.BlockSpec(shape, const, pipeline_mode=pl.Buffered(1))

    return pl.pallas_call(
        _post_kernel,
        grid=(n // POST_TM,),
        in_specs=[
            pl.BlockSpec((POST_TM, D_FOX), row),
            pl.BlockSpec((POST_TM, D_SWA), row),
            pl.BlockSpec((POST_TM, D_MODEL), row),
            pl.BlockSpec((POST_TM, D_PLE), row),
            resident(wo.shape), resident(w1.shape), resident(w2.shape),
            resident(wg.shape), resident(wp.shape), resident(gains.shape),
        ],
        out_specs=pl.BlockSpec((POST_TM, D_MODEL), row),
        out_shape=jax.ShapeDtypeStruct((n, D_MODEL), jnp.float32),
        compiler_params=pltpu.CompilerParams(
            dimension_semantics=("parallel",),
            vmem_limit_bytes=VMEM_LIMIT),
        name="post",
    )(fox, swa, x2, p2, wo, w1, w2, wg, wp, gains)


def _proj_weight(w_in):
    scale = HEAD_DIM ** -0.5
    splits = np.cumsum([D_FOX, D_FOX, D_FOX, N_FOX_HEADS, D_SWA, D_SWA_KV])
    fq, fk, fv, ff, sq, sk, sv = jnp.split(w_in, splits, axis=-1)
    ff = jnp.pad(ff, ((0, 0), (0, LANES - N_FOX_HEADS)))
    return jnp.concatenate([fq * (scale * LOG2E), fk, fv, sq * scale, sk, sv, ff],
                           axis=-1).astype(jnp.bfloat16)


def kernel(x, p, w_in, b_forget, w_out, rel_bias, swa_sinks, g_attn_pre, g_attn_post,
           w_ff1, w_ff2, g_ff_pre, g_ff_post, w_ple, w_ple_gate, g_ple_post):
    batch, seq, _ = x.shape
    n = batch * seq
    bf16 = jnp.bfloat16
    h = x.reshape(n, D_MODEL)
    for i in range(p.shape[0]):
        b_ff = jnp.pad(b_forget[i], (0, LANES - N_FOX_HEADS)).reshape(1, LANES)
        fq, fk, fv, qa, ka, sq, sk, sv = _proj_call(
            h, g_attn_pre[i].reshape(1, D_MODEL), _proj_weight(w_in[i]), b_ff, batch, seq)
        fox = _fox_call(fq, qa, fk, ka, fv, batch, seq).reshape(n, D_FOX)
        swa = _swa_call(sq, sk, sv, rel_bias, swa_sinks[i], batch, seq).reshape(n, D_SWA)
        gains = jnp.stack([g_attn_post[i], g_ff_pre[i], g_ff_post[i], g_ple_post[i]])
        h = _post_call(fox, swa, h, p[i].reshape(n, D_PLE),
                       w_out[i].astype(bf16), w_ff1[i].astype(bf16), w_ff2[i].astype(bf16),
                       w_ple_gate[i].astype(bf16), w_ple[i].astype(bf16), gains)
    return h.reshape(batch, seq, D_MODEL)
```

```python
import functools

import jax
import jax.numpy as jnp
import numpy as np
from jax import lax
from jax.experimental import pallas as pl
from jax.experimental.pallas import tpu as pltpu

D_MODEL = 1024
HEAD_DIM = 64
N_FOX_HEADS = 8
N_SWA_HEADS = 8
N_SWA_KV_HEADS = 2
SWA_GROUP = N_SWA_HEADS // N_SWA_KV_HEADS
D_FOX = N_FOX_HEADS * HEAD_DIM
D_SWA = N_SWA_HEADS * HEAD_DIM
D_SWA_KV = N_SWA_KV_HEADS * HEAD_DIM
D_FF = 4 * D_MODEL
D_PLE = 256
WINDOW = 128
Q_BLOCK = 128
N_BUCKETS = 32
MAX_DISTANCE = 128
RMS_EPS = 1e-6

LANES = 128
N_PAIRS = N_FOX_HEADS // 2
N_SPLIT = 3
NEG_BIG = -1e30
LOG2E = float(np.log2(np.e))

PROJ_TM = 512
FOX_T = 512
SWA_UNROLL = 3
POST_TM = 512
FF_CHUNK = 1024
VMEM_LIMIT = 56 * 1024 * 1024

C_FQ, C_FK, C_FV = 0, D_FOX, 2 * D_FOX
C_SQ = 3 * D_FOX
C_SK = C_SQ + D_SWA
C_SV = C_SK + D_SWA_KV
C_FF = C_SV + D_SWA_KV
D_PROJ = C_FF + LANES


def _rms(v):
    return v * lax.rsqrt(jnp.mean(v * v, axis=-1, keepdims=True) + RMS_EPS)


def _dot(a, b):
    return jnp.dot(a, b, preferred_element_type=jnp.float32)


def _dot_nt(a, b):
    return lax.dot_general(a, b, (((1,), (1,)), ((), ())),
                           preferred_element_type=jnp.float32)


def _proj_kernel(x_ref, g_ref, w_ref, bf_ref, tri_ref, pq_ref, pk_ref, oq_ref, ok_ref,
                 fq_ref, fk_ref, fv_ref, qa_ref, ka_ref, sq_ref, sk_ref, sv_ref,
                 carry_ref):
    @pl.when(pl.program_id(1) == 0)
    def _():
        carry_ref[...] = jnp.zeros_like(carry_ref)

    a = (_rms(x_ref[...]) * g_ref[...]).astype(jnp.bfloat16)

    def proj(lo, width):
        return _dot(a, w_ref[:, lo:lo + width])

    fq_ref[...] = proj(C_FQ, D_FOX).astype(jnp.bfloat16)
    fk_ref[...] = proj(C_FK, D_FOX).astype(jnp.bfloat16)
    fv_ref[...] = proj(C_FV, D_FOX).astype(jnp.bfloat16)
    sq_ref[...] = proj(C_SQ, D_SWA).astype(jnp.bfloat16)

    lane = lax.broadcasted_iota(jnp.int32, (PROJ_TM, LANES), 1)
    for lo, out_ref in ((C_SK, sk_ref), (C_SV, sv_ref)):
        z = proj(lo, D_SWA_KV)
        zr = pltpu.roll(z, HEAD_DIM, 1)
        out_ref[:, :LANES] = jnp.where(lane < HEAD_DIM, z, zr).astype(jnp.bfloat16)
        out_ref[:, LANES:] = jnp.where(lane < HEAD_DIM, zr, z).astype(jnp.bfloat16)

    v = proj(C_FF, LANES) + bf_ref[...]
    logf = jnp.minimum(v, 0.0) - jnp.log1p(jnp.exp(-jnp.abs(v)))
    c = carry_ref[...] + sum(_dot(tri_ref[...], t) for t in _split_bf16(logf * LOG2E))
    carry_ref[...] = c[PROJ_TM - 1:PROJ_TM, :]
    cs = jnp.concatenate(_split_bf16(c), axis=-1)
    qa_ref[...] = (_dot(cs, pq_ref[...]) + oq_ref[...]).astype(jnp.bfloat16)
    ka_ref[...] = (_dot(cs, pk_ref[...]) + ok_ref[...]).astype(jnp.bfloat16)


def _split_bf16(v):
    terms = []
    r = v
    for _ in range(N_SPLIT):
        t = r.astype(jnp.bfloat16)
        terms.append(t)
        r = r - t.astype(jnp.float32)
    return terms


def _aug_constants():
    pq = np.zeros((N_SPLIT * LANES, N_PAIRS * LANES), np.float32)
    pk = np.zeros_like(pq)
    oq = np.zeros((1, N_PAIRS * LANES), np.float32)
    ok = np.zeros_like(oq)
    for pair in range(N_PAIRS):
        for e in range(2):
            head = 2 * pair + e
            base = pair * LANES + 2 * N_SPLIT * e
            for t in range(N_SPLIT):
                pq[t * LANES + head, base + t] = 1.0
                ok[0, base + t] = 1.0
                oq[0, base + N_SPLIT + t] = 1.0
                pk[t * LANES + head, base + N_SPLIT + t] = -1.0
    return pq, pk, oq, ok


def _proj_call(x2, g_pre, w_proj, b_ff, batch, seq):
    n = batch * seq
    steps = seq // PROJ_TM
    pq, pk, oq, ok = _aug_constants()
    tri = np.tril(np.ones((PROJ_TM, PROJ_TM), np.float32))
    row = lambda b, s: (b * steps + s, 0)
    const = lambda b, s: (0, 0)
    bf16 = jnp.bfloat16

    def out(width):
        return (jax.ShapeDtypeStruct((n, width), bf16), pl.BlockSpec((PROJ_TM, width), row))

    outs = [out(D_FOX), out(D_FOX), out(D_FOX), out(N_PAIRS * LANES), out(N_PAIRS * LANES),
            out(D_SWA), out(2 * LANES), out(2 * LANES)]
    return pl.pallas_call(
        _proj_kernel,
        grid=(batch, steps),
        in_specs=[
            pl.BlockSpec((PROJ_TM, D_MODEL), row),
            pl.BlockSpec((1, D_MODEL), const),
            pl.BlockSpec((D_MODEL, D_PROJ), const),
            pl.BlockSpec((1, LANES), const),
            pl.BlockSpec((PROJ_TM, PROJ_TM), const),
            pl.BlockSpec(pq.shape, const),
            pl.BlockSpec(pk.shape, const),
            pl.BlockSpec(oq.shape, const),
            pl.BlockSpec(ok.shape, const),
        ],
        out_specs=[o[1] for o in outs],
        out_shape=[o[0] for o in outs],
        scratch_shapes=[pltpu.VMEM((1, LANES), jnp.float32)],
        compiler_params=pltpu.CompilerParams(
            dimension_semantics=("parallel", "arbitrary"),
            vmem_limit_bytes=VMEM_LIMIT),
        name="proj",
    )(x2, g_pre, w_proj, b_ff, jnp.asarray(tri, bf16), jnp.asarray(pq, bf16),
      jnp.asarray(pk, bf16), jnp.asarray(oq), jnp.asarray(ok))


def _fox_kernel(q_ref, qa_ref, k_ref, ka_ref, v_ref, o_ref, qm_ref, m_ref, acc_ref):
    t = FOX_T
    i = pl.program_id(2)
    q = q_ref[0]
    qa = qa_ref[0]
    lane = lax.broadcasted_iota(jnp.int32, (t, LANES), 1)
    lower = lane < HEAD_DIM
    zero = jnp.zeros_like(q)
    n_aug = 2 * N_SPLIT
    qm_ref[0] = jnp.concatenate([jnp.where(lower, q, zero),
                                 jnp.where(lane < n_aug, qa, zero)], axis=-1)
    qm_ref[1] = jnp.concatenate([jnp.where(lower, zero, q),
                                 jnp.where((lane >= n_aug) & (lane < 2 * n_aug), qa, zero)],
                                axis=-1)
    m_ref[...] = jnp.full_like(m_ref, NEG_BIG)
    acc_ref[...] = jnp.zeros_like(acc_ref)

    def tile(j, masked):
        rows = pl.ds(pl.multiple_of(j * t, t), t)
        kk = jnp.concatenate([k_ref[0, rows, :], ka_ref[0, rows, :]], axis=-1)
        v = v_ref[0, rows, :]
        one = jnp.ones_like(v)
        if masked:
            causal = (lax.broadcasted_iota(jnp.int32, (t, t), 0)
                      >= lax.broadcasted_iota(jnp.int32, (t, t), 1))
        for e in range(2):
            s = _dot_nt(qm_ref[e], kk)
            if masked:
                s = jnp.where(causal, s, NEG_BIG)
            m_old = m_ref[e]
            m_new = jnp.maximum(m_old, jnp.max(s, axis=-1, keepdims=True))
            p = jnp.concatenate(
                [jnp.exp2(s[:, c * LANES:(c + 1) * LANES] - m_new) for c in range(t // LANES)],
                axis=-1).astype(jnp.bfloat16)
            ve = jnp.where(lower, v, one) if e == 0 else jnp.where(lower, one, v)
            acc_ref[e] = jnp.exp2(m_old - m_new) * acc_ref[e] + _dot(p, ve)
            m_ref[e] = m_new

    def body(j, carry):
        tile(j, False)
        return carry

    lax.fori_loop(0, i, body, 0)
    tile(i, True)

    outs = []
    for e in range(2):
        acc = acc_ref[e]
        outs.append(acc / pltpu.roll(acc, HEAD_DIM, 1))
    o_ref[0] = jnp.where(lower, outs[0], outs[1]).astype(o_ref.dtype)


def _fox_call(fq, qa, fk, ka, fv, batch, seq):
    t = FOX_T
    shape3 = (batch, seq, N_PAIRS * LANES)
    q_spec = pl.BlockSpec((1, t, LANES), lambda b, p, i: (b, i, p))
    kv_spec = pl.BlockSpec((1, seq, LANES), lambda b, p, i: (b, 0, p))
    return pl.pallas_call(
        _fox_kernel,
        grid=(batch, N_PAIRS, seq // t),
        in_specs=[q_spec, q_spec, kv_spec, kv_spec, kv_spec],
        out_specs=q_spec,
        out_shape=jax.ShapeDtypeStruct(shape3, jnp.bfloat16),
        scratch_shapes=[pltpu.VMEM((2, t, 2 * LANES), jnp.bfloat16),
                        pltpu.VMEM((2, t, LANES), jnp.float32),
                        pltpu.VMEM((2, t, LANES), jnp.float32)],
        compiler_params=pltpu.CompilerParams(
            dimension_semantics=("parallel", "parallel", "parallel"),
            vmem_limit_bytes=VMEM_LIMIT),
        name="fox",
    )(fq.reshape(shape3), qa.reshape(shape3), fk.reshape(shape3), ka.reshape(shape3),
      fv.reshape(shape3))


def _t5_bucket(n):
    max_exact = N_BUCKETS // 2
    large = max_exact + (np.log(np.maximum(n, 1) / max_exact)
                         / np.log(MAX_DISTANCE / max_exact)
                         * (N_BUCKETS - max_exact)).astype(np.int32)
    large = np.minimum(large, N_BUCKETS - 1)
    return np.where(n < max_exact, n, large).astype(np.int32)


def _band_buckets():
    i = np.arange(Q_BLOCK)[:, None]
    j = np.arange(2 * Q_BLOCK)[None, :]
    dist = i + Q_BLOCK - j
    in_window = (dist >= 0) & (dist < WINDOW)
    return np.where(in_window, _t5_bucket(np.clip(dist, 0, None)), -1).astype(np.int32)


def _swa_kernel(bucket_ref, rel_ref, sink_ref, q_ref, k_ref, v_ref, o_ref, bias_ref, sinkc_ref,
                *, seq):
    qb = Q_BLOCK
    kv = pl.program_id(1)

    @pl.when((pl.program_id(0) == 0) & (kv == 0))
    def _():
        bucket = bucket_ref[...]
        for h in range(N_SWA_HEADS):
            bias = jnp.full(bucket.shape, NEG_BIG, jnp.float32)
            for b in range(N_BUCKETS):
                bias = jnp.where(bucket == b, rel_ref[b, h] * LOG2E, bias)
            group, g = divmod(h, SWA_GROUP)
            bias_ref[group, g * qb:(g + 1) * qb, :] = bias
            sinkc_ref[group, g * qb:(g + 1) * qb, :] = jnp.full((qb, LANES), sink_ref[h] * LOG2E)

    lower = lax.broadcasted_iota(jnp.int32, (qb, LANES), 1) < HEAD_DIM
    lower_all = lax.broadcasted_iota(jnp.int32, (SWA_GROUP * qb, LANES), 1) < HEAD_DIM

    def block(n, first):
        rows = pl.ds(pl.multiple_of(n * qb, qb), qb)
        if first:
            band, width, cols = rows, qb, slice(qb, 2 * qb)
        else:
            band = pl.ds(pl.multiple_of((n - 1) * qb, qb), 2 * qb)
            width, cols = 2 * qb, slice(0, 2 * qb)
        kb = k_ref[0, band, :]
        vb = v_ref[0, band, :]
        vb = jnp.where(lax.broadcasted_iota(jnp.int32, vb.shape, 1) < HEAD_DIM, vb,
                       jnp.ones_like(vb))
        parts = []
        for pair in range(SWA_GROUP // 2):
            qp = q_ref[0, rows, pair * LANES:(pair + 1) * LANES]
            zero = jnp.zeros_like(qp)
            parts += [jnp.where(lower, qp, zero), jnp.where(lower, zero, qp)]
        s = _dot_nt(jnp.concatenate(parts, axis=0), kb) + bias_ref[kv, :, cols]
        sink = sinkc_ref[kv]
        m = jnp.maximum(jnp.max(s, axis=-1, keepdims=True), sink)
        p = jnp.concatenate(
            [jnp.exp2(s[:, c * LANES:(c + 1) * LANES] - m) for c in range(width // LANES)],
            axis=-1).astype(jnp.bfloat16)
        acc = _dot(p, vb)
        acc = acc + jnp.where(lower_all, 0.0, jnp.exp2(sink - m))
        rolled = pltpu.roll(acc, HEAD_DIM, 1)
        for pair in range(SWA_GROUP // 2):
            even = slice(2 * pair * qb, (2 * pair + 1) * qb)
            odd = slice((2 * pair + 1) * qb, (2 * pair + 2) * qb)
            o_ref[0, rows, pair * LANES:(pair + 1) * LANES] = jnp.where(
                lower, acc[even] / rolled[even], rolled[odd] / acc[odd]).astype(o_ref.dtype)

    block(0, True)

    def body(it, carry):
        for u in range(SWA_UNROLL):
            block(1 + it * SWA_UNROLL + u, False)
        return carry

    lax.fori_loop(0, (seq // qb - 1) // SWA_UNROLL, body, 0)


def _swa_call(sq, sk, sv, rel_bias, sinks, batch, seq):
    assert (seq // Q_BLOCK - 1) % SWA_UNROLL == 0
    bucket = _band_buckets()
    width = SWA_GROUP * HEAD_DIM
    smem = pl.BlockSpec(memory_space=pltpu.SMEM)
    q_spec = pl.BlockSpec((1, seq, width), lambda b, kv: (b, 0, kv))
    kv_spec = pl.BlockSpec((1, seq, LANES), lambda b, kv: (b, 0, kv))
    return pl.pallas_call(
        functools.partial(_swa_kernel, seq=seq),
        grid=(batch, N_SWA_KV_HEADS),
        in_specs=[pl.BlockSpec(bucket.shape, lambda b, kv: (0, 0)), smem, smem,
                  q_spec, kv_spec, kv_spec],
        out_specs=q_spec,
        out_shape=jax.ShapeDtypeStruct((batch, seq, D_SWA), jnp.bfloat16),
        scratch_shapes=[
            pltpu.VMEM((N_SWA_KV_HEADS, SWA_GROUP * Q_BLOCK, 2 * Q_BLOCK), jnp.float32),
            pltpu.VMEM((N_SWA_KV_HEADS, SWA_GROUP * Q_BLOCK, LANES), jnp.float32)],
        compiler_params=pltpu.CompilerParams(
            dimension_semantics=("arbitrary", "arbitrary"),
            vmem_limit_bytes=VMEM_LIMIT),
        name="swa",
    )(jnp.asarray(bucket), rel_bias, sinks, sq.reshape(batch, seq, D_SWA),
      sk.reshape(batch, seq, 2 * LANES), sv.reshape(batch, seq, 2 * LANES))


def _post_kernel(fox_ref, swa_ref, x_ref, p_ref, wo_ref, w1_ref, w2_ref, wg_ref, wp_ref,
                 gains_ref, o_ref):
    bf16 = jnp.bfloat16
    mix = _dot(jnp.concatenate([fox_ref[...], swa_ref[...]], axis=-1), wo_ref[...])
    h = x_ref[...] + _rms(mix) * gains_ref[0:1, :]

    m = (_rms(h) * gains_ref[1:2, :]).astype(bf16)
    y = jnp.zeros((POST_TM, D_MODEL), jnp.float32)
    for c in range(D_FF // FF_CHUNK):
        cols = slice(c * FF_CHUNK, (c + 1) * FF_CHUNK)
        hid = jnp.square(jnp.maximum(_dot(m, w1_ref[:, cols]), 0.0))
        y = y + _dot(hid.astype(bf16), w2_ref[cols, :])
    h = h + _rms(y) * gains_ref[2:3, :]

    gate = jax.nn.sigmoid(_dot(h.astype(bf16), wg_ref[...]))
    e = _dot(p_ref[...].astype(bf16), wp_ref[...]) * gate
    o_ref[...] = h + _rms(e) * gains_ref[3:4, :]


def _post_call(fox, swa, x2, p2, wo, w1, w2, wg, wp, gains):
    n = x2.shape[0]
    row = lambda i: (i, 0)
    const = lambda i: (0, 0)

    def resident(shape):
        return pl.BlockSpec(shape, const, pipeline_mode=pl.Buffered(1))

    return pl.pallas_call(
        _post_kernel,
        grid=(n // POST_TM,),
        in_specs=[
            pl.BlockSpec((POST_TM, D_FOX), row),
            pl.BlockSpec((POST_TM, D_SWA), row),
            pl.BlockSpec((POST_TM, D_MODEL), row),
            pl.BlockSpec((POST_TM, D_PLE), row),
            resident(wo.shape), resident(w1.shape), resident(w2.shape),
            resident(wg.shape), resident(wp.shape), resident(gains.shape),
        ],
        out_specs=pl.BlockSpec((POST_TM, D_MODEL), row),
        out_shape=jax.ShapeDtypeStruct((n, D_MODEL), jnp.float32),
        compiler_params=pltpu.CompilerParams(
            dimension_semantics=("parallel",),
            vmem_limit_bytes=VMEM_LIMIT),
        name="post",
    )(fox, swa, x2, p2, wo, w1, w2, wg, wp, gains)


def _proj_weight(w_in):
    scale = HEAD_DIM ** -0.5
    splits = np.cumsum([D_FOX, D_FOX, D_FOX, N_FOX_HEADS, D_SWA, D_SWA_KV])
    fq, fk, fv, ff, sq, sk, sv = jnp.split(w_in, splits, axis=-1)
    ff = jnp.pad(ff, ((0, 0), (0, LANES - N_FOX_HEADS)))
    return jnp.concatenate([fq * (scale * LOG2E), fk, fv, sq * (scale * LOG2E), sk, sv, ff],
                           axis=-1).astype(jnp.bfloat16)


def kernel(x, p, w_in, b_forget, w_out, rel_bias, swa_sinks, g_attn_pre, g_attn_post,
           w_ff1, w_ff2, g_ff_pre, g_ff_post, w_ple, w_ple_gate, g_ple_post):
    batch, seq, _ = x.shape
    n = batch * seq
    bf16 = jnp.bfloat16
    h = x.reshape(n, D_MODEL)
    for i in range(p.shape[0]):
        b_ff = jnp.pad(b_forget[i], (0, LANES - N_FOX_HEADS)).reshape(1, LANES)
        fq, fk, fv, qa, ka, sq, sk, sv = _proj_call(
            h, g_attn_pre[i].reshape(1, D_MODEL), _proj_weight(w_in[i]), b_ff, batch, seq)
        fox = _fox_call(fq, qa, fk, ka, fv, batch, seq).reshape(n, D_FOX)
        swa = _swa_call(sq, sk, sv, rel_bias, swa_sinks[i], batch, seq).reshape(n, D_SWA)
        gains = jnp.stack([g_attn_post[i], g_ff_pre[i], g_ff_post[i], g_ple_post[i]])
        h = _post_call(fox, swa, h, p[i].reshape(n, D_PLE),
                       w_out[i].astype(bf16), w_ff1[i].astype(bf16), w_ff2[i].astype(bf16),
                       w_ple_gate[i].astype(bf16), w_ple[i].astype(bf16), gains)
    return h.reshape(batch, seq, D_MODEL)
```

```python
import functools

import jax
import jax.numpy as jnp
import numpy as np
from jax import lax
from jax.experimental import pallas as pl
from jax.experimental.pallas import tpu as pltpu

D_MODEL = 1024
HEAD_DIM = 64
N_FOX_HEADS = 8
N_SWA_HEADS = 8
N_SWA_KV_HEADS = 2
SWA_GROUP = N_SWA_HEADS // N_SWA_KV_HEADS
D_FOX = N_FOX_HEADS * HEAD_DIM
D_SWA = N_SWA_HEADS * HEAD_DIM
D_SWA_KV = N_SWA_KV_HEADS * HEAD_DIM
D_FF = 4 * D_MODEL
D_PLE = 256
WINDOW = 128
Q_BLOCK = 128
N_BUCKETS = 32
MAX_DISTANCE = 128
RMS_EPS = 1e-6

LANES = 128
N_PAIRS = N_FOX_HEADS // 2
N_SPLIT = 3
NEG_BIG = -1e30
LOG2E = float(np.log2(np.e))

PROJ_TM = 512
FOX_T = 512
SWA_UNROLL = 3
POST_TM = 512
FF_CHUNK = 1024
VMEM_LIMIT = 56 * 1024 * 1024

C_FQ, C_FK, C_FV = 0, D_FOX, 2 * D_FOX
C_SQ = 3 * D_FOX
C_SK = C_SQ + D_SWA
C_SV = C_SK + D_SWA_KV
C_FF = C_SV + D_SWA_KV
D_PROJ = C_FF + LANES


def _rms(v):
    return v * lax.rsqrt(jnp.mean(v * v, axis=-1, keepdims=True) + RMS_EPS)


def _dot(a, b):
    return jnp.dot(a, b, preferred_element_type=jnp.float32)


def _dot_nt(a, b):
    return lax.dot_general(a, b, (((1,), (1,)), ((), ())),
                           preferred_element_type=jnp.float32)


def _proj_kernel(x_ref, g_ref, w_ref, bf_ref, tri_ref, pq_ref, pk_ref, oq_ref, ok_ref,
                 fq_ref, fk_ref, fv_ref, qa_ref, ka_ref, sq_ref, sk_ref, sv_ref,
                 carry_ref):
    @pl.when(pl.program_id(1) == 0)
    def _():
        carry_ref[...] = jnp.zeros_like(carry_ref)

    a = (_rms(x_ref[...]) * g_ref[...]).astype(jnp.bfloat16)

    def proj(lo, width):
        return _dot(a, w_ref[:, lo:lo + width])

    fq_ref[...] = proj(C_FQ, D_FOX).astype(jnp.bfloat16)
    fk_ref[...] = proj(C_FK, D_FOX).astype(jnp.bfloat16)
    fv_ref[...] = proj(C_FV, D_FOX).astype(jnp.bfloat16)
    sq_ref[...] = proj(C_SQ, D_SWA).astype(jnp.bfloat16)

    lane = lax.broadcasted_iota(jnp.int32, (PROJ_TM, LANES), 1)
    for lo, out_ref in ((C_SK, sk_ref), (C_SV, sv_ref)):
        z = proj(lo, D_SWA_KV)
        zr = pltpu.roll(z, HEAD_DIM, 1)
        out_ref[:, :LANES] = jnp.where(lane < HEAD_DIM, z, zr).astype(jnp.bfloat16)
        out_ref[:, LANES:] = jnp.where(lane < HEAD_DIM, zr, z).astype(jnp.bfloat16)

    v = proj(C_FF, LANES) + bf_ref[...]
    logf = jnp.minimum(v, 0.0) - jnp.log1p(jnp.exp(-jnp.abs(v)))
    c = carry_ref[...] + sum(_dot(tri_ref[...], t) for t in _split_bf16(logf * LOG2E))
    carry_ref[...] = c[PROJ_TM - 1:PROJ_TM, :]
    cs = jnp.concatenate(_split_bf16(c), axis=-1)
    qa_ref[...] = (_dot(cs, pq_ref[...]) + oq_ref[...]).astype(jnp.bfloat16)
    ka_ref[...] = (_dot(cs, pk_ref[...]) + ok_ref[...]).astype(jnp.bfloat16)


def _split_bf16(v):
    terms = []
    r = v
    for _ in range(N_SPLIT):
        t = r.astype(jnp.bfloat16)
        terms.append(t)
        r = r - t.astype(jnp.float32)
    return terms


def _aug_constants():
    pq = np.zeros((N_SPLIT * LANES, N_PAIRS * LANES), np.float32)
    pk = np.zeros_like(pq)
    oq = np.zeros((1, N_PAIRS * LANES), np.float32)
    ok = np.zeros_like(oq)
    for pair in range(N_PAIRS):
        for e in range(2):
            head = 2 * pair + e
            base = pair * LANES + 2 * N_SPLIT * e
            for t in range(N_SPLIT):
                pq[t * LANES + head, base + t] = 1.0
                ok[0, base + t] = 1.0
                oq[0, base + N_SPLIT + t] = 1.0
                pk[t * LANES + head, base + N_SPLIT + t] = -1.0
    return pq, pk, oq, ok


def _proj_call(x2, g_pre, w_proj, b_ff, batch, seq):
    n = batch * seq
    steps = seq // PROJ_TM
    pq, pk, oq, ok = _aug_constants()
    tri = np.tril(np.ones((PROJ_TM, PROJ_TM), np.float32))
    row = lambda b, s: (b * steps + s, 0)
    const = lambda b, s: (0, 0)
    bf16 = jnp.bfloat16

    def out(width):
        return (jax.ShapeDtypeStruct((n, width), bf16), pl.BlockSpec((PROJ_TM, width), row))

    outs = [out(D_FOX), out(D_FOX), out(D_FOX), out(N_PAIRS * LANES), out(N_PAIRS * LANES),
            out(D_SWA), out(2 * LANES), out(2 * LANES)]
    return pl.pallas_call(
        _proj_kernel,
        grid=(batch, steps),
        in_specs=[
            pl.BlockSpec((PROJ_TM, D_MODEL), row),
            pl.BlockSpec((1, D_MODEL), const),
            pl.BlockSpec((D_MODEL, D_PROJ), const),
            pl.BlockSpec((1, LANES), const),
            pl.BlockSpec((PROJ_TM, PROJ_TM), const),
            pl.BlockSpec(pq.shape, const),
            pl.BlockSpec(pk.shape, const),
            pl.BlockSpec(oq.shape, const),
            pl.BlockSpec(ok.shape, const),
        ],
        out_specs=[o[1] for o in outs],
        out_shape=[o[0] for o in outs],
        scratch_shapes=[pltpu.VMEM((1, LANES), jnp.float32)],
        compiler_params=pltpu.CompilerParams(
            dimension_semantics=("parallel", "arbitrary"),
            vmem_limit_bytes=VMEM_LIMIT),
        name="proj",
    )(x2, g_pre, w_proj, b_ff, jnp.asarray(tri, bf16), jnp.asarray(pq, bf16),
      jnp.asarray(pk, bf16), jnp.asarray(oq), jnp.asarray(ok))


def _fox_kernel(q_ref, qa_ref, k_ref, ka_ref, v_ref, o_ref, qm_ref, s_ref, m_ref, acc_ref, *, seq):
    t = FOX_T
    lane = lax.broadcasted_iota(jnp.int32, (t, LANES), 1)
    lower = lane < HEAD_DIM
    n_aug = 2 * N_SPLIT
    causal = (lax.broadcasted_iota(jnp.int32, (t, t), 0)
              >= lax.broadcasted_iota(jnp.int32, (t, t), 1))

    def tile_rows(i):
        return slice(i * t, (i + 1) * t)

    def mask_queries(i):
        q = q_ref[0, tile_rows(i), :]
        qa = qa_ref[0, tile_rows(i), :]
        zero = jnp.zeros_like(q)
        qm_ref[i, 0] = jnp.concatenate([jnp.where(lower, q, zero),
                                        jnp.where(lane < n_aug, qa, zero)], axis=-1)
        qm_ref[i, 1] = jnp.concatenate(
            [jnp.where(lower, zero, q),
             jnp.where((lane >= n_aug) & (lane < 2 * n_aug), qa, zero)], axis=-1)

    def scores(i, j, slot):
        kk = jnp.concatenate([k_ref[0, tile_rows(j), :], ka_ref[0, tile_rows(j), :]], axis=-1)
        for e in range(2):
            s_ref[slot, e] = _dot_nt(qm_ref[i, e], kk)

    def update(i, j, slot):
        v = v_ref[0, tile_rows(j), :]
        one = jnp.ones_like(v)
        for e in range(2):
            s = s_ref[slot, e]
            if j == i:
                s = jnp.where(causal, s, NEG_BIG)
            row_max = jnp.max(s, axis=-1, keepdims=True)
            m_new = (jnp.broadcast_to(row_max, (t, LANES)) if j == 0
                     else jnp.maximum(m_ref[e], row_max))
            p = jnp.concatenate(
                [jnp.exp2(s[:, c * LANES:(c + 1) * LANES] - m_new) for c in range(t // LANES)],
                axis=-1).astype(jnp.bfloat16)
            ve = jnp.where(lower, v, one) if e == 0 else jnp.where(lower, one, v)
            pv = _dot(p, ve)
            acc_ref[e] = pv if j == 0 else jnp.exp2(m_ref[e] - m_new) * acc_ref[e] + pv
            m_ref[e] = m_new

    def finish(i):
        outs = []
        for e in range(2):
            acc = acc_ref[e]
            outs.append(acc / pltpu.roll(acc, HEAD_DIM, 1))
        o_ref[0, tile_rows(i), :] = jnp.where(lower, outs[0], outs[1]).astype(o_ref.dtype)

    n_tiles = seq // t
    for i in range(n_tiles):
        mask_queries(i)
    pairs = [(i, j) for i in range(n_tiles) for j in range(i + 1)]
    scores(*pairs[0], 0)
    for idx, (i, j) in enumerate(pairs):
        if idx + 1 < len(pairs):
            scores(*pairs[idx + 1], (idx + 1) % 2)
        update(i, j, idx % 2)
        if j == i:
            finish(i)


def _fox_call(fq, qa, fk, ka, fv, batch, seq):
    t = FOX_T
    shape3 = (batch, seq, N_PAIRS * LANES)
    spec = pl.BlockSpec((1, seq, LANES), lambda b, p: (b, 0, p))
    return pl.pallas_call(
        functools.partial(_fox_kernel, seq=seq),
        grid=(batch, N_PAIRS),
        in_specs=[spec] * 5,
        out_specs=spec,
        out_shape=jax.ShapeDtypeStruct(shape3, jnp.bfloat16),
        scratch_shapes=[pltpu.VMEM((seq // t, 2, t, 2 * LANES), jnp.bfloat16),
                        pltpu.VMEM((2, 2, t, t), jnp.float32),
                        pltpu.VMEM((2, t, LANES), jnp.float32),
                        pltpu.VMEM((2, t, LANES), jnp.float32)],
        compiler_params=pltpu.CompilerParams(
            dimension_semantics=("parallel", "parallel"),
            vmem_limit_bytes=VMEM_LIMIT),
        name="fox",
    )(fq.reshape(shape3), qa.reshape(shape3), fk.reshape(shape3), ka.reshape(shape3),
      fv.reshape(shape3))


def _t5_bucket(n):
    max_exact = N_BUCKETS // 2
    large = max_exact + (np.log(np.maximum(n, 1) / max_exact)
                         / np.log(MAX_DISTANCE / max_exact)
                         * (N_BUCKETS - max_exact)).astype(np.int32)
    large = np.minimum(large, N_BUCKETS - 1)
    return np.where(n < max_exact, n, large).astype(np.int32)


def _band_buckets():
    i = np.arange(Q_BLOCK)[:, None]
    j = np.arange(2 * Q_BLOCK)[None, :]
    dist = i + Q_BLOCK - j
    in_window = (dist >= 0) & (dist < WINDOW)
    return np.where(in_window, _t5_bucket(np.clip(dist, 0, None)), -1).astype(np.int32)


def _swa_kernel(bucket_ref, rel_ref, sink_ref, q_ref, k_ref, v_ref, o_ref, bias_ref, sinkc_ref,
                *, seq):
    qb = Q_BLOCK
    kv = pl.program_id(1)

    @pl.when((pl.program_id(0) == 0) & (kv == 0))
    def _():
        bucket = bucket_ref[...]
        for h in range(N_SWA_HEADS):
            bias = jnp.full(bucket.shape, NEG_BIG, jnp.float32)
            for b in range(N_BUCKETS):
                bias = jnp.where(bucket == b, rel_ref[b, h] * LOG2E, bias)
            group, g = divmod(h, SWA_GROUP)
            bias_ref[group, g * qb:(g + 1) * qb, :] = bias
            sinkc_ref[group, g * qb:(g + 1) * qb, :] = jnp.full((qb, LANES), sink_ref[h] * LOG2E)

    lower = lax.broadcasted_iota(jnp.int32, (qb, LANES), 1) < HEAD_DIM
    lower_all = lax.broadcasted_iota(jnp.int32, (SWA_GROUP * qb, LANES), 1) < HEAD_DIM

    def block(n, first):
        rows = pl.ds(pl.multiple_of(n * qb, qb), qb)
        if first:
            band, width, cols = rows, qb, slice(qb, 2 * qb)
        else:
            band = pl.ds(pl.multiple_of((n - 1) * qb, qb), 2 * qb)
            width, cols = 2 * qb, slice(0, 2 * qb)
        kb = k_ref[0, band, :]
        vb = v_ref[0, band, :]
        vb = jnp.where(lax.broadcasted_iota(jnp.int32, vb.shape, 1) < HEAD_DIM, vb,
                       jnp.ones_like(vb))
        parts = []
        for pair in range(SWA_GROUP // 2):
            qp = q_ref[0, rows, pair * LANES:(pair + 1) * LANES]
            zero = jnp.zeros_like(qp)
            parts += [jnp.where(lower, qp, zero), jnp.where(lower, zero, qp)]
        s = _dot_nt(jnp.concatenate(parts, axis=0), kb) + bias_ref[kv, :, cols]
        sink = sinkc_ref[kv]
        m = jnp.maximum(jnp.max(s, axis=-1, keepdims=True), sink)
        p = jnp.concatenate(
            [jnp.exp2(s[:, c * LANES:(c + 1) * LANES] - m) for c in range(width // LANES)],
            axis=-1).astype(jnp.bfloat16)
        acc = _dot(p, vb)
        acc = acc + jnp.where(lower_all, 0.0, jnp.exp2(sink - m))
        rolled = pltpu.roll(acc, HEAD_DIM, 1)
        for pair in range(SWA_GROUP // 2):
            even = slice(2 * pair * qb, (2 * pair + 1) * qb)
            odd = slice((2 * pair + 1) * qb, (2 * pair + 2) * qb)
            o_ref[0, rows, pair * LANES:(pair + 1) * LANES] = jnp.where(
                lower, acc[even] / rolled[even], rolled[odd] / acc[odd]).astype(o_ref.dtype)

    block(0, True)

    def body(it, carry):
        for u in range(SWA_UNROLL):
            block(1 + it * SWA_UNROLL + u, False)
        return carry

    lax.fori_loop(0, (seq // qb - 1) // SWA_UNROLL, body, 0)


def _swa_call(sq, sk, sv, rel_bias, sinks, batch, seq):
    assert (seq // Q_BLOCK - 1) % SWA_UNROLL == 0
    bucket = _band_buckets()
    width = SWA_GROUP * HEAD_DIM
    smem = pl.BlockSpec(memory_space=pltpu.SMEM)
    q_spec = pl.BlockSpec((1, seq, width), lambda b, kv: (b, 0, kv))
    kv_spec = pl.BlockSpec((1, seq, LANES), lambda b, kv: (b, 0, kv))
    return pl.pallas_call(
        functools.partial(_swa_kernel, seq=seq),
        grid=(batch, N_SWA_KV_HEADS),
        in_specs=[pl.BlockSpec(bucket.shape, lambda b, kv: (0, 0)), smem, smem,
                  q_spec, kv_spec, kv_spec],
        out_specs=q_spec,
        out_shape=jax.ShapeDtypeStruct((batch, seq, D_SWA), jnp.bfloat16),
        scratch_shapes=[
            pltpu.VMEM((N_SWA_KV_HEADS, SWA_GROUP * Q_BLOCK, 2 * Q_BLOCK), jnp.float32),
            pltpu.VMEM((N_SWA_KV_HEADS, SWA_GROUP * Q_BLOCK, LANES), jnp.float32)],
        compiler_params=pltpu.CompilerParams(
            dimension_semantics=("arbitrary", "arbitrary"),
            vmem_limit_bytes=VMEM_LIMIT),
        name="swa",
    )(jnp.asarray(bucket), rel_bias, sinks, sq.reshape(batch, seq, D_SWA),
      sk.reshape(batch, seq, 2 * LANES), sv.reshape(batch, seq, 2 * LANES))


def _post_kernel(fox_ref, swa_ref, x_ref, p_ref, wo_ref, w1_ref, w2_ref, wg_ref, wp_ref,
                 gains_ref, o_ref):
    bf16 = jnp.bfloat16
    mix = _dot(jnp.concatenate([fox_ref[...], swa_ref[...]], axis=-1), wo_ref[...])
    h = x_ref[...] + _rms(mix) * gains_ref[0:1, :]

    m = (_rms(h) * gains_ref[1:2, :]).astype(bf16)
    y = jnp.zeros((POST_TM, D_MODEL), jnp.float32)
    for c in range(D_FF // FF_CHUNK):
        cols = slice(c * FF_CHUNK, (c + 1) * FF_CHUNK)
        hid = jnp.square(jnp.maximum(_dot(m, w1_ref[:, cols]), 0.0))
        y = y + _dot(hid.astype(bf16), w2_ref[cols, :])
    h = h + _rms(y) * gains_ref[2:3, :]

    gate = jax.nn.sigmoid(_dot(h.astype(bf16), wg_ref[...]))
    e = _dot(p_ref[...].astype(bf16), wp_ref[...]) * gate
    o_ref[...] = h + _rms(e) * gains_ref[3:4, :]


def _post_call(fox, swa, x2, p2, wo, w1, w2, wg, wp, gains):
    n = x2.shape[0]
    row = lambda i: (i, 0)
    const = lambda i: (0, 0)

    def resident(shape):
        return pl.BlockSpec(shape, const, pipeline_mode=pl.Buffered(1))

    return pl.pallas_call(
        _post_kernel,
        grid=(n // POST_TM,),
        in_specs=[
            pl.BlockSpec((POST_TM, D_FOX), row),
            pl.BlockSpec((POST_TM, D_SWA), row),
            pl.BlockSpec((POST_TM, D_MODEL), row),
            pl.BlockSpec((POST_TM, D_PLE), row),
            resident(wo.shape), resident(w1.shape), resident(w2.shape),
            resident(wg.shape), resident(wp.shape), resident(gains.shape),
        ],
        out_specs=pl.BlockSpec((POST_TM, D_MODEL), row),
        out_shape=jax.ShapeDtypeStruct((n, D_MODEL), jnp.float32),
        compiler_params=pltpu.CompilerParams(
            dimension_semantics=("parallel",),
            vmem_limit_bytes=VMEM_LIMIT),
        name="post",
    )(fox, swa, x2, p2, wo, w1, w2, wg, wp, gains)


def _proj_weight(w_in):
    scale = HEAD_DIM ** -0.5
    splits = np.cumsum([D_FOX, D_FOX, D_FOX, N_FOX_HEADS, D_SWA, D_SWA_KV])
    fq, fk, fv, ff, sq, sk, sv = jnp.split(w_in, splits, axis=-1)
    ff = jnp.pad(ff, ((0, 0), (0, LANES - N_FOX_HEADS)))
    return jnp.concatenate([fq * (scale * LOG2E), fk, fv, sq * (scale * LOG2E), sk, sv, ff],
                           axis=-1).astype(jnp.bfloat16)


def kernel(x, p, w_in, b_forget, w_out, rel_bias, swa_sinks, g_attn_pre, g_attn_post,
           w_ff1, w_ff2, g_ff_pre, g_ff_post, w_ple, w_ple_gate, g_ple_post):
    batch, seq, _ = x.shape
    n = batch * seq
    bf16 = jnp.bfloat16
    h = x.reshape(n, D_MODEL)
    for i in range(p.shape[0]):
        b_ff = jnp.pad(b_forget[i], (0, LANES - N_FOX_HEADS)).reshape(1, LANES)
        fq, fk, fv, qa, ka, sq, sk, sv = _proj_call(
            h, g_attn_pre[i].reshape(1, D_MODEL), _proj_weight(w_in[i]), b_ff, batch, seq)
        fox = _fox_call(fq, qa, fk, ka, fv, batch, seq).reshape(n, D_FOX)
        swa = _swa_call(sq, sk, sv, rel_bias, swa_sinks[i], batch, seq).reshape(n, D_SWA)
        gains = jnp.stack([g_attn_post[i], g_ff_pre[i], g_ff_post[i], g_ple_post[i]])
        h = _post_call(fox, swa, h, p[i].reshape(n, D_PLE),
                       w_out[i].astype(bf16), w_ff1[i].astype(bf16), w_ff2[i].astype(bf16),
                       w_ple_gate[i].astype(bf16), w_ple[i].astype(bf16), gains)
    return h.reshape(batch, seq, D_MODEL)
```

```python
import functools

import jax
import jax.numpy as jnp
import numpy as np
from jax import lax
from jax.experimental import pallas as pl
from jax.experimental.pallas import tpu as pltpu

D_MODEL = 1024
HEAD_DIM = 64
N_FOX_HEADS = 8
N_SWA_HEADS = 8
N_SWA_KV_HEADS = 2
SWA_GROUP = N_SWA_HEADS // N_SWA_KV_HEADS
D_FOX = N_FOX_HEADS * HEAD_DIM
D_SWA = N_SWA_HEADS * HEAD_DIM
D_SWA_KV = N_SWA_KV_HEADS * HEAD_DIM
D_FF = 4 * D_MODEL
D_PLE = 256
WINDOW = 128
Q_BLOCK = 128
N_BUCKETS = 32
MAX_DISTANCE = 128
RMS_EPS = 1e-6

LANES = 128
N_PAIRS = N_FOX_HEADS // 2
N_SPLIT = 3
NEG_BIG = -1e30
LOG2E = float(np.log2(np.e))

PROJ_TM = 512
FOX_T = 512
POST_TM = 512
FF_CHUNK = 1024
VMEM_LIMIT = 56 * 1024 * 1024

C_FQ, C_FK, C_FV = 0, D_FOX, 2 * D_FOX
C_SQ = 3 * D_FOX
C_SK = C_SQ + D_SWA
C_SV = C_SK + D_SWA_KV
C_FF = C_SV + D_SWA_KV
D_PROJ = C_FF + LANES


def _rms(v):
    return v * lax.rsqrt(jnp.mean(v * v, axis=-1, keepdims=True) + RMS_EPS)


def _dot(a, b):
    return jnp.dot(a, b, preferred_element_type=jnp.float32)


def _dot_nt(a, b):
    return lax.dot_general(a, b, (((1,), (1,)), ((), ())),
                           preferred_element_type=jnp.float32)


def _proj_kernel(x_ref, g_ref, w_ref, bf_ref, tri_ref, pq_ref, pk_ref, oq_ref, ok_ref,
                 fq_ref, fk_ref, fv_ref, qa_ref, ka_ref, sq_ref, sk_ref, sv_ref,
                 carry_ref):
    @pl.when(pl.program_id(1) == 0)
    def _():
        carry_ref[...] = jnp.zeros_like(carry_ref)

    a = (_rms(x_ref[...]) * g_ref[...]).astype(jnp.bfloat16)

    def proj(lo, width):
        return _dot(a, w_ref[:, lo:lo + width])

    fq_ref[...] = proj(C_FQ, D_FOX).astype(jnp.bfloat16)
    fk_ref[...] = proj(C_FK, D_FOX).astype(jnp.bfloat16)
    fv_ref[...] = proj(C_FV, D_FOX).astype(jnp.bfloat16)
    sq_ref[...] = proj(C_SQ, D_SWA).astype(jnp.bfloat16)

    lane = lax.broadcasted_iota(jnp.int32, (PROJ_TM, LANES), 1)
    for lo, out_ref in ((C_SK, sk_ref), (C_SV, sv_ref)):
        z = proj(lo, D_SWA_KV)
        zr = pltpu.roll(z, HEAD_DIM, 1)
        out_ref[:, :LANES] = jnp.where(lane < HEAD_DIM, z, zr).astype(jnp.bfloat16)
        out_ref[:, LANES:] = jnp.where(lane < HEAD_DIM, zr, z).astype(jnp.bfloat16)

    v = proj(C_FF, LANES) + bf_ref[...]
    logf = jnp.minimum(v, 0.0) - jnp.log1p(jnp.exp(-jnp.abs(v)))
    c = carry_ref[...] + sum(_dot(tri_ref[...], t) for t in _split_bf16(logf * LOG2E))
    carry_ref[...] = c[PROJ_TM - 1:PROJ_TM, :]
    cs = jnp.concatenate(_split_bf16(c), axis=-1)
    qa_ref[...] = (_dot(cs, pq_ref[...]) + oq_ref[...]).astype(jnp.bfloat16)
    ka_ref[...] = (_dot(cs, pk_ref[...]) + ok_ref[...]).astype(jnp.bfloat16)


def _split_bf16(v):
    terms = []
    r = v
    for _ in range(N_SPLIT):
        t = r.astype(jnp.bfloat16)
        terms.append(t)
        r = r - t.astype(jnp.float32)
    return terms


def _aug_constants():
    pq = np.zeros((N_SPLIT * LANES, N_PAIRS * LANES), np.float32)
    pk = np.zeros_like(pq)
    oq = np.zeros((1, N_PAIRS * LANES), np.float32)
    ok = np.zeros_like(oq)
    for pair in range(N_PAIRS):
        for e in range(2):
            head = 2 * pair + e
            base = pair * LANES + 2 * N_SPLIT * e
            for t in range(N_SPLIT):
                pq[t * LANES + head, base + t] = 1.0
                ok[0, base + t] = 1.0
                oq[0, base + N_SPLIT + t] = 1.0
                pk[t * LANES + head, base + N_SPLIT + t] = -1.0
    return pq, pk, oq, ok


def _proj_call(x2, g_pre, w_proj, b_ff, batch, seq):
    n = batch * seq
    steps = seq // PROJ_TM
    pq, pk, oq, ok = _aug_constants()
    tri = np.tril(np.ones((PROJ_TM, PROJ_TM), np.float32))
    row = lambda b, s: (b * steps + s, 0)
    const = lambda b, s: (0, 0)
    bf16 = jnp.bfloat16

    def out(width):
        return (jax.ShapeDtypeStruct((n, width), bf16), pl.BlockSpec((PROJ_TM, width), row))

    outs = [out(D_FOX), out(D_FOX), out(D_FOX), out(N_PAIRS * LANES), out(N_PAIRS * LANES),
            out(D_SWA), out(2 * LANES), out(2 * LANES)]
    return pl.pallas_call(
        _proj_kernel,
        grid=(batch, steps),
        in_specs=[
            pl.BlockSpec((PROJ_TM, D_MODEL), row),
            pl.BlockSpec((1, D_MODEL), const),
            pl.BlockSpec((D_MODEL, D_PROJ), const),
            pl.BlockSpec((1, LANES), const),
            pl.BlockSpec((PROJ_TM, PROJ_TM), const),
            pl.BlockSpec(pq.shape, const),
            pl.BlockSpec(pk.shape, const),
            pl.BlockSpec(oq.shape, const),
            pl.BlockSpec(ok.shape, const),
        ],
        out_specs=[o[1] for o in outs],
        out_shape=[o[0] for o in outs],
        scratch_shapes=[pltpu.VMEM((1, LANES), jnp.float32)],
        compiler_params=pltpu.CompilerParams(
            dimension_semantics=("parallel", "arbitrary"),
            vmem_limit_bytes=VMEM_LIMIT),
        name="proj",
    )(x2, g_pre, w_proj, b_ff, jnp.asarray(tri, bf16), jnp.asarray(pq, bf16),
      jnp.asarray(pk, bf16), jnp.asarray(oq), jnp.asarray(ok))


def _fox_kernel(q_ref, qa_ref, k_ref, ka_ref, v_ref, o_ref, qm_ref, s_ref, m_ref, acc_ref, *, seq):
    t = FOX_T
    lane = lax.broadcasted_iota(jnp.int32, (t, LANES), 1)
    lower = lane < HEAD_DIM
    n_aug = 2 * N_SPLIT
    causal = (lax.broadcasted_iota(jnp.int32, (t, t), 0)
              >= lax.broadcasted_iota(jnp.int32, (t, t), 1))

    def tile_rows(i):
        return slice(i * t, (i + 1) * t)

    def mask_queries(i):
        q = q_ref[0, tile_rows(i), :]
        qa = qa_ref[0, tile_rows(i), :]
        zero = jnp.zeros_like(q)
        qm_ref[i, 0] = jnp.concatenate([jnp.where(lower, q, zero),
                                        jnp.where(lane < n_aug, qa, zero)], axis=-1)
        qm_ref[i, 1] = jnp.concatenate(
            [jnp.where(lower, zero, q),
             jnp.where((lane >= n_aug) & (lane < 2 * n_aug), qa, zero)], axis=-1)

    def scores(i, j, slot):
        kk = jnp.concatenate([k_ref[0, tile_rows(j), :], ka_ref[0, tile_rows(j), :]], axis=-1)
        for e in range(2):
            s_ref[slot, e] = _dot_nt(qm_ref[i, e], kk)

    def update(i, j, slot):
        v = v_ref[0, tile_rows(j), :]
        v = jnp.concatenate([v, jnp.ones_like(v)], axis=-1)
        for e in range(2):
            s = s_ref[slot, e]
            if j == i:
                s = jnp.where(causal, s, NEG_BIG)
            row_max = jnp.max(s, axis=-1, keepdims=True)
            m_new = (jnp.broadcast_to(row_max, (t, LANES)) if j == 0
                     else jnp.maximum(m_ref[e], row_max))
            p = jnp.concatenate(
                [jnp.exp2(s[:, c * LANES:(c + 1) * LANES] - m_new) for c in range(t // LANES)],
                axis=-1).astype(jnp.bfloat16)
            pv = _dot(p, v)
            if j > 0:
                alpha = jnp.exp2(m_ref[e] - m_new)
                pv = jnp.concatenate([alpha, alpha], axis=-1) * acc_ref[e] + pv
            acc_ref[e] = pv
            m_ref[e] = m_new

    def finish(i):
        outs = [acc_ref[e, :, :LANES] / acc_ref[e, :, LANES:] for e in range(2)]
        o_ref[0, tile_rows(i), :] = jnp.where(lower, outs[0], outs[1]).astype(o_ref.dtype)

    n_tiles = seq // t
    for i in range(n_tiles):
        mask_queries(i)
    pairs = [(i, j) for i in range(n_tiles) for j in range(i + 1)]
    scores(*pairs[0], 0)
    for idx, (i, j) in enumerate(pairs):
        if idx + 1 < len(pairs):
            scores(*pairs[idx + 1], (idx + 1) % 2)
        update(i, j, idx % 2)
        if j == i:
            finish(i)


def _fox_call(fq, qa, fk, ka, fv, batch, seq):
    t = FOX_T
    shape3 = (batch, seq, N_PAIRS * LANES)
    spec = pl.BlockSpec((1, seq, LANES), lambda b, p: (b, 0, p))
    return pl.pallas_call(
        functools.partial(_fox_kernel, seq=seq),
        grid=(batch, N_PAIRS),
        in_specs=[spec] * 5,
        out_specs=spec,
        out_shape=jax.ShapeDtypeStruct(shape3, jnp.bfloat16),
        scratch_shapes=[pltpu.VMEM((seq // t, 2, t, 2 * LANES), jnp.bfloat16),
                        pltpu.VMEM((2, 2, t, t), jnp.float32),
                        pltpu.VMEM((2, t, LANES), jnp.float32),
                        pltpu.VMEM((2, t, 2 * LANES), jnp.float32)],
        compiler_params=pltpu.CompilerParams(
            dimension_semantics=("parallel", "parallel"),
            vmem_limit_bytes=VMEM_LIMIT),
        name="fox",
    )(fq.reshape(shape3), qa.reshape(shape3), fk.reshape(shape3), ka.reshape(shape3),
      fv.reshape(shape3))


def _t5_bucket(n):
    max_exact = N_BUCKETS // 2
    large = max_exact + (np.log(np.maximum(n, 1) / max_exact)
                         / np.log(MAX_DISTANCE / max_exact)
                         * (N_BUCKETS - max_exact)).astype(np.int32)
    large = np.minimum(large, N_BUCKETS - 1)
    return np.where(n < max_exact, n, large).astype(np.int32)


def _band_buckets():
    i = np.arange(Q_BLOCK)[:, None]
    j = np.arange(2 * Q_BLOCK)[None, :]
    dist = i + Q_BLOCK - j
    in_window = (dist >= 0) & (dist < WINDOW)
    return np.where(in_window, _t5_bucket(np.clip(dist, 0, None)), -1).astype(np.int32)


def _swa_kernel(bucket_ref, rel_ref, sink_ref, q_ref, k_ref, v_ref, o_ref, bias_ref, sinkc_ref,
                s_ref, *, seq):
    qb = Q_BLOCK
    kv = pl.program_id(1)

    @pl.when((pl.program_id(0) == 0) & (kv == 0))
    def _():
        bucket = bucket_ref[...]
        for h in range(N_SWA_HEADS):
            bias = jnp.full(bucket.shape, NEG_BIG, jnp.float32)
            for b in range(N_BUCKETS):
                bias = jnp.where(bucket == b, rel_ref[b, h] * LOG2E, bias)
            group, g = divmod(h, SWA_GROUP)
            bias_ref[group, g * qb:(g + 1) * qb, :] = bias
            sinkc_ref[group, g * qb:(g + 1) * qb, :] = jnp.full((qb, LANES), sink_ref[h] * LOG2E)

    lower = lax.broadcasted_iota(jnp.int32, (qb, LANES), 1) < HEAD_DIM

    def band_of(n):
        if n == 0:
            return slice(0, qb), qb, slice(qb, 2 * qb)
        return slice((n - 1) * qb, (n + 1) * qb), 2 * qb, slice(0, 2 * qb)

    def scores(n, slot):
        rows = slice(n * qb, (n + 1) * qb)
        band, width, cols = band_of(n)
        parts = []
        for pair in range(SWA_GROUP // 2):
            qp = q_ref[0, rows, pair * LANES:(pair + 1) * LANES]
            zero = jnp.zeros_like(qp)
            parts += [jnp.where(lower, qp, zero), jnp.where(lower, zero, qp)]
        s_ref[slot, :, :width] = (_dot_nt(jnp.concatenate(parts, axis=0), k_ref[0, band, :])
                                  + bias_ref[kv, :, cols])

    def update(n, slot):
        rows = slice(n * qb, (n + 1) * qb)
        band, width, _ = band_of(n)
        vb = v_ref[0, band, :]
        vb = jnp.concatenate([vb, jnp.ones_like(vb)], axis=-1)
        s = s_ref[slot, :, :width]
        sink = sinkc_ref[kv]
        m = jnp.maximum(jnp.max(s, axis=-1, keepdims=True), sink)
        p = jnp.concatenate(
            [jnp.exp2(s[:, c * LANES:(c + 1) * LANES] - m) for c in range(width // LANES)],
            axis=-1).astype(jnp.bfloat16)
        acc = _dot(p, vb)
        out = acc[:, :LANES] / (acc[:, LANES:] + jnp.exp2(sink - m))
        for pair in range(SWA_GROUP // 2):
            even = slice(2 * pair * qb, (2 * pair + 1) * qb)
            odd = slice((2 * pair + 1) * qb, (2 * pair + 2) * qb)
            o_ref[0, rows, pair * LANES:(pair + 1) * LANES] = jnp.where(
                lower, out[even], out[odd]).astype(o_ref.dtype)

    n_blocks = seq // qb
    scores(0, 0)
    for n in range(n_blocks):
        if n + 1 < n_blocks:
            scores(n + 1, (n + 1) % 2)
        update(n, n % 2)


def _swa_call(sq, sk, sv, rel_bias, sinks, batch, seq):
    bucket = _band_buckets()
    width = SWA_GROUP * HEAD_DIM
    smem = pl.BlockSpec(memory_space=pltpu.SMEM)
    q_spec = pl.BlockSpec((1, seq, width), lambda b, kv: (b, 0, kv))
    kv_spec = pl.BlockSpec((1, seq, LANES), lambda b, kv: (b, 0, kv))
    return pl.pallas_call(
        functools.partial(_swa_kernel, seq=seq),
        grid=(batch, N_SWA_KV_HEADS),
        in_specs=[pl.BlockSpec(bucket.shape, lambda b, kv: (0, 0)), smem, smem,
                  q_spec, kv_spec, kv_spec],
        out_specs=q_spec,
        out_shape=jax.ShapeDtypeStruct((batch, seq, D_SWA), jnp.bfloat16),
        scratch_shapes=[
            pltpu.VMEM((N_SWA_KV_HEADS, SWA_GROUP * Q_BLOCK, 2 * Q_BLOCK), jnp.float32),
            pltpu.VMEM((N_SWA_KV_HEADS, SWA_GROUP * Q_BLOCK, LANES), jnp.float32),
            pltpu.VMEM((2, SWA_GROUP * Q_BLOCK, 2 * Q_BLOCK), jnp.float32)],
        compiler_params=pltpu.CompilerParams(
            dimension_semantics=("arbitrary", "arbitrary"),
            vmem_limit_bytes=VMEM_LIMIT),
        name="swa",
    )(jnp.asarray(bucket), rel_bias, sinks, sq.reshape(batch, seq, D_SWA),
      sk.reshape(batch, seq, 2 * LANES), sv.reshape(batch, seq, 2 * LANES))


def _post_kernel(fox_ref, swa_ref, x_ref, p_ref, wo_ref, w1_ref, w2_ref, wg_ref, wp_ref,
                 gains_ref, o_ref):
    bf16 = jnp.bfloat16
    mix = _dot(jnp.concatenate([fox_ref[...], swa_ref[...]], axis=-1), wo_ref[...])
    h = x_ref[...] + _rms(mix) * gains_ref[0:1, :]

    m = (_rms(h) * gains_ref[1:2, :]).astype(bf16)
    y = jnp.zeros((POST_TM, D_MODEL), jnp.float32)
    for c in range(D_FF // FF_CHUNK):
        cols = slice(c * FF_CHUNK, (c + 1) * FF_CHUNK)
        hid = jnp.square(jnp.maximum(_dot(m, w1_ref[:, cols]), 0.0))
        y = y + _dot(hid.astype(bf16), w2_ref[cols, :])
    h = h + _rms(y) * gains_ref[2:3, :]

    gate = jax.nn.sigmoid(_dot(h.astype(bf16), wg_ref[...]))
    e = _dot(p_ref[...].astype(bf16), wp_ref[...]) * gate
    o_ref[...] = h + _rms(e) * gains_ref[3:4, :]


def _post_call(fox, swa, x2, p2, wo, w1, w2, wg, wp, gains):
    n = x2.shape[0]
    row = lambda i: (i, 0)
    const = lambda i: (0, 0)

    def resident(shape):
        return pl.BlockSpec(shape, const, pipeline_mode=pl.Buffered(1))

    return pl.pallas_call(
        _post_kernel,
        grid=(n // POST_TM,),
        in_specs=[
            pl.BlockSpec((POST_TM, D_FOX), row),
            pl.BlockSpec((POST_TM, D_SWA), row),
            pl.BlockSpec((POST_TM, D_MODEL), row),
            pl.BlockSpec((POST_TM, D_PLE), row),
            resident(wo.shape), resident(w1.shape), resident(w2.shape),
            resident(wg.shape), resident(wp.shape), resident(gains.shape),
        ],
        out_specs=pl.BlockSpec((POST_TM, D_MODEL), row),
        out_shape=jax.ShapeDtypeStruct((n, D_MODEL), jnp.float32),
        compiler_params=pltpu.CompilerParams(
            dimension_semantics=("parallel",),
            vmem_limit_bytes=VMEM_LIMIT),
        name="post",
    )(fox, swa, x2, p2, wo, w1, w2, wg, wp, gains)


def _proj_weight(w_in):
    scale = HEAD_DIM ** -0.5
    splits = np.cumsum([D_FOX, D_FOX, D_FOX, N_FOX_HEADS, D_SWA, D_SWA_KV])
    fq, fk, fv, ff, sq, sk, sv = jnp.split(w_in, splits, axis=-1)
    ff = jnp.pad(ff, ((0, 0), (0, LANES - N_FOX_HEADS)))
    return jnp.concatenate([fq * (scale * LOG2E), fk, fv, sq * (scale * LOG2E), sk, sv, ff],
                           axis=-1).astype(jnp.bfloat16)


def kernel(x, p, w_in, b_forget, w_out, rel_bias, swa_sinks, g_attn_pre, g_attn_post,
           w_ff1, w_ff2, g_ff_pre, g_ff_post, w_ple, w_ple_gate, g_ple_post):
    batch, seq, _ = x.shape
    n = batch * seq
    bf16 = jnp.bfloat16
    h = x.reshape(n, D_MODEL)
    for i in range(p.shape[0]):
        b_ff = jnp.pad(b_forget[i], (0, LANES - N_FOX_HEADS)).reshape(1, LANES)
        fq, fk, fv, qa, ka, sq, sk, sv = _proj_call(
            h, g_attn_pre[i].reshape(1, D_MODEL), _proj_weight(w_in[i]), b_ff, batch, seq)
        fox = _fox_call(fq, qa, fk, ka, fv, batch, seq).reshape(n, D_FOX)
        swa = _swa_call(sq, sk, sv, rel_bias, swa_sinks[i], batch, seq).reshape(n, D_SWA)
        gains = jnp.stack([g_attn_post[i], g_ff_pre[i], g_ff_post[i], g_ple_post[i]])
        h = _post_call(fox, swa, h, p[i].reshape(n, D_PLE),
                       w_out[i].astype(bf16), w_ff1[i].astype(bf16), w_ff2[i].astype(bf16),
                       w_ple_gate[i].astype(bf16), w_ple[i].astype(bf16), gains)
    return h.reshape(batch, seq, D_MODEL)
```

```python
import functools

import jax
import jax.numpy as jnp
import numpy as np
from jax import lax
from jax.experimental import pallas as pl
from jax.experimental.pallas import tpu as pltpu

D_MODEL = 1024
HEAD_DIM = 64
N_FOX_HEADS = 8
N_SWA_HEADS = 8
N_SWA_KV_HEADS = 2
SWA_GROUP = N_SWA_HEADS // N_SWA_KV_HEADS
D_FOX = N_FOX_HEADS * HEAD_DIM
D_SWA = N_SWA_HEADS * HEAD_DIM
D_SWA_KV = N_SWA_KV_HEADS * HEAD_DIM
D_FF = 4 * D_MODEL
D_PLE = 256
WINDOW = 128
Q_BLOCK = 128
N_BUCKETS = 32
MAX_DISTANCE = 128
RMS_EPS = 1e-6

LANES = 128
N_PAIRS = N_FOX_HEADS // 2
N_SPLIT = 3
N_TERMS = HEAD_DIM // N_FOX_HEADS
TERM_SRC = (0, 0, 0, 1, 1, 1, 2, 2)
TERM_CUT = (0, 1, 2, 0, 1, 2, 0, 1)
NEG_BIG = -1e30
LOG2E = float(np.log2(np.e))

PROJ_TM = 512
FOX_T = 512
POST_TM = 512
FF_CHUNK = 1024
VMEM_LIMIT = 56 * 1024 * 1024

C_FQ, C_FK, C_FV = 0, D_FOX, 2 * D_FOX
C_SQ = 3 * D_FOX
C_SK = C_SQ + D_SWA
C_SV = C_SK + D_SWA_KV
C_FF = C_SV + D_SWA_KV
D_PROJ = C_FF + LANES


def _rms(v):
    return v * lax.rsqrt(jnp.mean(v * v, axis=-1, keepdims=True) + RMS_EPS)


def _dot(a, b):
    return jnp.dot(a, b, preferred_element_type=jnp.float32)


def _dot_nt(a, b):
    return lax.dot_general(a, b, (((1,), (1,)), ((), ())),
                           preferred_element_type=jnp.float32)


def _proj_kernel(x_ref, g_ref, w_ref, bf_ref, tri_ref,
                 fq_ref, fk_ref, fv_ref, qa_ref, ka_ref, sq_ref, sk_ref, sv_ref,
                 carry_ref):
    @pl.when(pl.program_id(1) == 0)
    def _():
        carry_ref[...] = jnp.zeros_like(carry_ref)

    a = (_rms(x_ref[...]) * g_ref[...]).astype(jnp.bfloat16)

    def proj(lo, width):
        return _dot(a, w_ref[:, lo:lo + width])

    fq_ref[...] = proj(C_FQ, D_FOX).astype(jnp.bfloat16)
    fk_ref[...] = proj(C_FK, D_FOX).astype(jnp.bfloat16)
    fv_ref[...] = proj(C_FV, D_FOX).astype(jnp.bfloat16)
    sq_ref[...] = proj(C_SQ, D_SWA).astype(jnp.bfloat16)

    lane = lax.broadcasted_iota(jnp.int32, (PROJ_TM, LANES), 1)
    lower = lane < HEAD_DIM
    kv = proj(C_SK, 2 * D_SWA_KV)
    for half, out_ref in enumerate((sk_ref, sv_ref)):
        z = kv[:, half * LANES:(half + 1) * LANES]
        zr = pltpu.roll(z, HEAD_DIM, 1)
        out_ref[:, :LANES] = jnp.where(lower, z, zr).astype(jnp.bfloat16)
        out_ref[:, LANES:] = jnp.where(lower, zr, z).astype(jnp.bfloat16)

    term = (lane // N_FOX_HEADS) % N_TERMS
    v = proj(C_FF, LANES) + bf_ref[...]
    logf = (jnp.minimum(v, 0.0) - jnp.log1p(jnp.exp(-jnp.abs(v)))) * LOG2E
    x = _pick_piece(_split_bf16(logf), term, TERM_SRC)
    c = carry_ref[...] + _dot(tri_ref[...], x)
    carry_ref[...] = c[PROJ_TM - 1:PROJ_TM, :]
    terms = _pick_piece(_split_bf16(c), term, TERM_CUT)
    one = jnp.ones_like(terms)
    qa_ref[...] = jnp.where(lower, terms, one)
    ka_ref[...] = jnp.where(lower, one, -terms)


def _split_bf16(v):
    pieces = []
    r = v
    for _ in range(N_SPLIT):
        t = r.astype(jnp.bfloat16)
        pieces.append(t)
        r = r - t.astype(jnp.float32)
    return pieces


def _pick_piece(pieces, term, piece_of_term):
    out = pieces[piece_of_term[-1]]
    for t in range(N_TERMS - 2, -1, -1):
        if piece_of_term[t] != piece_of_term[t + 1]:
            out = jnp.where(term <= t, pieces[piece_of_term[t]], out)
    return out


def _proj_call(x2, g_pre, w_proj, b_ff, batch, seq):
    n = batch * seq
    steps = seq // PROJ_TM
    tri = np.tril(np.ones((PROJ_TM, PROJ_TM), np.float32))
    row = lambda b, s: (b * steps + s, 0)
    const = lambda b, s: (0, 0)
    bf16 = jnp.bfloat16

    def out(width):
        return (jax.ShapeDtypeStruct((n, width), bf16), pl.BlockSpec((PROJ_TM, width), row))

    outs = [out(D_FOX), out(D_FOX), out(D_FOX), out(LANES), out(LANES),
            out(D_SWA), out(2 * LANES), out(2 * LANES)]
    return pl.pallas_call(
        _proj_kernel,
        grid=(batch, steps),
        in_specs=[
            pl.BlockSpec((PROJ_TM, D_MODEL), row),
            pl.BlockSpec((1, D_MODEL), const),
            pl.BlockSpec((D_MODEL, D_PROJ), const),
            pl.BlockSpec((1, LANES), const),
            pl.BlockSpec((PROJ_TM, PROJ_TM), const),
        ],
        out_specs=[o[1] for o in outs],
        out_shape=[o[0] for o in outs],
        scratch_shapes=[pltpu.VMEM((1, LANES), jnp.float32)],
        compiler_params=pltpu.CompilerParams(
            dimension_semantics=("parallel", "arbitrary"),
            vmem_limit_bytes=VMEM_LIMIT),
        name="proj",
    )(x2, g_pre, w_proj, b_ff, jnp.asarray(tri, bf16))


def _fox_kernel(q_ref, qa_ref, k_ref, ka_ref, v_ref, o_ref, qm_ref, s_ref, m_ref, acc_ref, *, seq):
    t = FOX_T
    lane = lax.broadcasted_iota(jnp.int32, (t, LANES), 1)
    lower = lane < HEAD_DIM
    head_of_lane = lane % N_FOX_HEADS
    first_head = 2 * pl.program_id(1)
    causal = (lax.broadcasted_iota(jnp.int32, (t, t), 0)
              >= lax.broadcasted_iota(jnp.int32, (t, t), 1))

    def tile_rows(i):
        return slice(i * t, (i + 1) * t)

    def mask_queries(i):
        q = q_ref[0, tile_rows(i), :]
        qa = qa_ref[0, tile_rows(i), :]
        zero = jnp.zeros_like(q)
        for e in range(2):
            qm_ref[i, e] = jnp.concatenate(
                [jnp.where(lower, q, zero) if e == 0 else jnp.where(lower, zero, q),
                 jnp.where(head_of_lane == first_head + e, qa, zero)], axis=-1)

    def scores(i, j, slot):
        kk = jnp.concatenate([k_ref[0, tile_rows(j), :], ka_ref[0, tile_rows(j), :]], axis=-1)
        for e in range(2):
            s_ref[slot, e] = _dot_nt(qm_ref[i, e], kk)

    def update(i, j, slot):
        v = v_ref[0, tile_rows(j), :]
        one = jnp.ones_like(v)
        for e in range(2):
            s = s_ref[slot, e]
            if j == i:
                s = jnp.where(causal, s, NEG_BIG)
            row_max = jnp.max(s, axis=-1, keepdims=True)
            m_new = (jnp.broadcast_to(row_max, (t, LANES)) if j == 0
                     else jnp.maximum(m_ref[e], row_max))
            p = jnp.concatenate(
                [jnp.exp2(s[:, c * LANES:(c + 1) * LANES] - m_new) for c in range(t // LANES)],
                axis=-1).astype(jnp.bfloat16)
            ve = jnp.where(lower, v, one) if e == 0 else jnp.where(lower, one, v)
            pv = _dot(p, ve)
            acc_ref[e] = pv if j == 0 else jnp.exp2(m_ref[e] - m_new) * acc_ref[e] + pv
            m_ref[e] = m_new

    def finish(i):
        outs = []
        for e in range(2):
            acc = acc_ref[e]
            outs.append(acc / pltpu.roll(acc, HEAD_DIM, 1))
        o_ref[0, tile_rows(i), :] = jnp.where(lower, outs[0], outs[1]).astype(o_ref.dtype)

    n_tiles = seq // t
    for i in range(n_tiles):
        mask_queries(i)
    pairs = [(i, j) for i in range(n_tiles) for j in range(i + 1)]
    scores(*pairs[0], 0)
    for idx, (i, j) in enumerate(pairs):
        if idx + 1 < len(pairs):
            scores(*pairs[idx + 1], (idx + 1) % 2)
        update(i, j, idx % 2)
        if j == i:
            finish(i)


def _fox_call(fq, qa, fk, ka, fv, batch, seq):
    t = FOX_T
    shape3 = (batch, seq, N_PAIRS * LANES)
    aug3 = (batch, seq, LANES)
    spec = pl.BlockSpec((1, seq, LANES), lambda b, p: (b, 0, p))
    aug_spec = pl.BlockSpec((1, seq, LANES), lambda b, p: (b, 0, 0))
    return pl.pallas_call(
        functools.partial(_fox_kernel, seq=seq),
        grid=(batch, N_PAIRS),
        in_specs=[spec, aug_spec, spec, aug_spec, spec],
        out_specs=spec,
        out_shape=jax.ShapeDtypeStruct(shape3, jnp.bfloat16),
        scratch_shapes=[pltpu.VMEM((seq // t, 2, t, 2 * LANES), jnp.bfloat16),
                        pltpu.VMEM((2, 2, t, t), jnp.float32),
                        pltpu.VMEM((2, t, LANES), jnp.float32),
                        pltpu.VMEM((2, t, LANES), jnp.float32)],
        compiler_params=pltpu.CompilerParams(
            dimension_semantics=("parallel", "parallel"),
            vmem_limit_bytes=VMEM_LIMIT),
        name="fox",
    )(fq.reshape(shape3), qa.reshape(aug3), fk.reshape(shape3), ka.reshape(aug3),
      fv.reshape(shape3))


def _t5_bucket(n):
    max_exact = N_BUCKETS // 2
    large = max_exact + (np.log(np.maximum(n, 1) / max_exact)
                         / np.log(MAX_DISTANCE / max_exact)
                         * (N_BUCKETS - max_exact)).astype(np.int32)
    large = np.minimum(large, N_BUCKETS - 1)
    return np.where(n < max_exact, n, large).astype(np.int32)


def _band_buckets():
    i = np.arange(Q_BLOCK)[:, None]
    j = np.arange(2 * Q_BLOCK)[None, :]
    dist = i + Q_BLOCK - j
    in_window = (dist >= 0) & (dist < WINDOW)
    return np.where(in_window, _t5_bucket(np.clip(dist, 0, None)), -1).astype(np.int32)


def _swa_kernel(bucket_ref, rel_ref, sink_ref, q_ref, k_ref, v_ref, o_ref, bias_ref, sinkc_ref,
                s_ref, *, seq):
    qb = Q_BLOCK
    kv = pl.program_id(1)

    @pl.when((pl.program_id(0) == 0) & (kv == 0))
    def _():
        bucket = bucket_ref[...]
        for h in range(N_SWA_HEADS):
            bias = jnp.full(bucket.shape, NEG_BIG, jnp.float32)
            for b in range(N_BUCKETS):
                bias = jnp.where(bucket == b, rel_ref[b, h] * LOG2E, bias)
            group, g = divmod(h, SWA_GROUP)
            bias_ref[group, g * qb:(g + 1) * qb, :] = bias
            sinkc_ref[group, g * qb:(g + 1) * qb, :] = jnp.full((qb, LANES), sink_ref[h] * LOG2E)

    lower = lax.broadcasted_iota(jnp.int32, (qb, LANES), 1) < HEAD_DIM

    def band_of(n):
        if n == 0:
            return slice(0, qb), qb, slice(qb, 2 * qb)
        return slice((n - 1) * qb, (n + 1) * qb), 2 * qb, slice(0, 2 * qb)

    def scores(n, slot):
        rows = slice(n * qb, (n + 1) * qb)
        band, width, cols = band_of(n)
        parts = []
        for pair in range(SWA_GROUP // 2):
            qp = q_ref[0, rows, pair * LANES:(pair + 1) * LANES]
            zero = jnp.zeros_like(qp)
            parts += [jnp.where(lower, qp, zero), jnp.where(lower, zero, qp)]
        s_ref[slot, :, :width] = (_dot_nt(jnp.concatenate(parts, axis=0), k_ref[0, band, :])
                                  + bias_ref[kv, :, cols])

    def update(n, slot):
        rows = slice(n * qb, (n + 1) * qb)
        band, width, _ = band_of(n)
        vb = v_ref[0, band, :]
        vb = jnp.concatenate([vb, jnp.ones_like(vb)], axis=-1)
        s = s_ref[slot, :, :width]
        sink = sinkc_ref[kv]
        m = jnp.maximum(jnp.max(s, axis=-1, keepdims=True), sink)
        p = jnp.concatenate(
            [jnp.exp2(s[:, c * LANES:(c + 1) * LANES] - m) for c in range(width // LANES)],
            axis=-1).astype(jnp.bfloat16)
        acc = _dot(p, vb)
        out = acc[:, :LANES] / (acc[:, LANES:] + jnp.exp2(sink - m))
        for pair in range(SWA_GROUP // 2):
            even = slice(2 * pair * qb, (2 * pair + 1) * qb)
            odd = slice((2 * pair + 1) * qb, (2 * pair + 2) * qb)
            o_ref[0, rows, pair * LANES:(pair + 1) * LANES] = jnp.where(
                lower, out[even], out[odd]).astype(o_ref.dtype)

    n_blocks = seq // qb
    scores(0, 0)
    for n in range(n_blocks):
        if n + 1 < n_blocks:
            scores(n + 1, (n + 1) % 2)
        update(n, n % 2)


def _swa_call(sq, sk, sv, rel_bias, sinks, batch, seq):
    bucket = _band_buckets()
    width = SWA_GROUP * HEAD_DIM
    smem = pl.BlockSpec(memory_space=pltpu.SMEM)
    q_spec = pl.BlockSpec((1, seq, width), lambda b, kv: (b, 0, kv))
    kv_spec = pl.BlockSpec((1, seq, LANES), lambda b, kv: (b, 0, kv))
    return pl.pallas_call(
        functools.partial(_swa_kernel, seq=seq),
        grid=(batch, N_SWA_KV_HEADS),
        in_specs=[pl.BlockSpec(bucket.shape, lambda b, kv: (0, 0)), smem, smem,
                  q_spec, kv_spec, kv_spec],
        out_specs=q_spec,
        out_shape=jax.ShapeDtypeStruct((batch, seq, D_SWA), jnp.bfloat16),
        scratch_shapes=[
            pltpu.VMEM((N_SWA_KV_HEADS, SWA_GROUP * Q_BLOCK, 2 * Q_BLOCK), jnp.float32),
            pltpu.VMEM((N_SWA_KV_HEADS, SWA_GROUP * Q_BLOCK, LANES), jnp.float32),
            pltpu.VMEM((2, SWA_GROUP * Q_BLOCK, 2 * Q_BLOCK), jnp.float32)],
        compiler_params=pltpu.CompilerParams(
            dimension_semantics=("arbitrary", "arbitrary"),
            vmem_limit_bytes=VMEM_LIMIT),
        name="swa",
    )(jnp.asarray(bucket), rel_bias, sinks, sq.reshape(batch, seq, D_SWA),
      sk.reshape(batch, seq, 2 * LANES), sv.reshape(batch, seq, 2 * LANES))


def _post_kernel(fox_ref, swa_ref, x_ref, p_ref, wo_ref, w1_ref, w2_ref, wg_ref, wp_ref,
                 gains_ref, o_ref):
    bf16 = jnp.bfloat16
    mix = _dot(jnp.concatenate([fox_ref[...], swa_ref[...]], axis=-1), wo_ref[...])
    h = x_ref[...] + _rms(mix) * gains_ref[0:1, :]

    m = (_rms(h) * gains_ref[1:2, :]).astype(bf16)
    y = jnp.zeros((POST_TM, D_MODEL), jnp.float32)
    for c in range(D_FF // FF_CHUNK):
        cols = slice(c * FF_CHUNK, (c + 1) * FF_CHUNK)
        hid = jnp.square(jnp.maximum(_dot(m, w1_ref[:, cols]), 0.0))
        y = y + _dot(hid.astype(bf16), w2_ref[cols, :])
    h = h + _rms(y) * gains_ref[2:3, :]

    gate = jax.nn.sigmoid(_dot(h.astype(bf16), wg_ref[...]))
    e = _dot(p_ref[...].astype(bf16), wp_ref[...]) * gate
    o_ref[...] = h + _rms(e) * gains_ref[3:4, :]


def _post_call(fox, swa, x2, p2, wo, w1, w2, wg, wp, gains):
    n = x2.shape[0]
    row = lambda i: (i, 0)
    const = lambda i: (0, 0)

    def resident(shape):
        return pl.BlockSpec(shape, const, pipeline_mode=pl.Buffered(1))

    return pl.pallas_call(
        _post_kernel,
        grid=(n // POST_TM,),
        in_specs=[
            pl.BlockSpec((POST_TM, D_FOX), row),
            pl.BlockSpec((POST_TM, D_SWA), row),
            pl.BlockSpec((POST_TM, D_MODEL), row),
            pl.BlockSpec((POST_TM, D_PLE), row),
            resident(wo.shape), resident(w1.shape), resident(w2.shape),
            resident(wg.shape), resident(wp.shape), resident(gains.shape),
        ],
        out_specs=pl.BlockSpec((POST_TM, D_MODEL), row),
        out_shape=jax.ShapeDtypeStruct((n, D_MODEL), jnp.float32),
        compiler_params=pltpu.CompilerParams(
            dimension_semantics=("parallel",),
            vmem_limit_bytes=VMEM_LIMIT),
        name="post",
    )(fox, swa, x2, p2, wo, w1, w2, wg, wp, gains)


def _gate_lanes(v):
    return jnp.tile(v, (1, LANES // N_FOX_HEADS))


def _proj_weight(w_in):
    scale = HEAD_DIM ** -0.5
    splits = np.cumsum([D_FOX, D_FOX, D_FOX, N_FOX_HEADS, D_SWA, D_SWA_KV])
    fq, fk, fv, ff, sq, sk, sv = jnp.split(w_in, splits, axis=-1)
    ff = _gate_lanes(ff)
    return jnp.concatenate([fq * (scale * LOG2E), fk, fv, sq * (scale * LOG2E), sk, sv, ff],
                           axis=-1).astype(jnp.bfloat16)


def kernel(x, p, w_in, b_forget, w_out, rel_bias, swa_sinks, g_attn_pre, g_attn_post,
           w_ff1, w_ff2, g_ff_pre, g_ff_post, w_ple, w_ple_gate, g_ple_post):
    batch, seq, _ = x.shape
    n = batch * seq
    bf16 = jnp.bfloat16
    h = x.reshape(n, D_MODEL)
    for i in range(p.shape[0]):
        b_ff = _gate_lanes(b_forget[i].reshape(1, N_FOX_HEADS))
        fq, fk, fv, qa, ka, sq, sk, sv = _proj_call(
            h, g_attn_pre[i].reshape(1, D_MODEL), _proj_weight(w_in[i]), b_ff, batch, seq)
        fox = _fox_call(fq, qa, fk, ka, fv, batch, seq).reshape(n, D_FOX)
        swa = _swa_call(sq, sk, sv, rel_bias, swa_sinks[i], batch, seq).reshape(n, D_SWA)
        gains = jnp.stack([g_attn_post[i], g_ff_pre[i], g_ff_post[i], g_ple_post[i]])
        h = _post_call(fox, swa, h, p[i].reshape(n, D_PLE),
                       w_out[i].astype(bf16), w_ff1[i].astype(bf16), w_ff2[i].astype(bf16),
                       w_ple_gate[i].astype(bf16), w_ple[i].astype(bf16), gains)
    return h.reshape(batch, seq, D_MODEL)
```

```python
import functools

import jax
import jax.numpy as jnp
import numpy as np
from jax import lax
from jax.experimental import pallas as pl
from jax.experimental.pallas import tpu as pltpu

D_MODEL = 1024
HEAD_DIM = 64
N_FOX_HEADS = 8
N_SWA_HEADS = 8
N_SWA_KV_HEADS = 2
SWA_GROUP = N_SWA_HEADS // N_SWA_KV_HEADS
D_FOX = N_FOX_HEADS * HEAD_DIM
D_SWA = N_SWA_HEADS * HEAD_DIM
D_SWA_KV = N_SWA_KV_HEADS * HEAD_DIM
D_FF = 4 * D_MODEL
D_PLE = 256
WINDOW = 128
Q_BLOCK = 128
N_BUCKETS = 32
MAX_DISTANCE = 128
RMS_EPS = 1e-6

LANES = 128
N_PAIRS = N_FOX_HEADS // 2
N_SPLIT = 3
N_TERMS = HEAD_DIM // N_FOX_HEADS
TERM_SRC = (0, 0, 0, 1, 1, 1, 2, 2)
TERM_CUT = (0, 1, 2, 0, 1, 2, 0, 1)
NEG_BIG = -1e30
LOG2E = float(np.log2(np.e))

PROJ_TM = 512
FOX_T = 512
FOX_DIAG = 256
POST_TM = 512
FF_CHUNK = 1024
VMEM_LIMIT = 56 * 1024 * 1024

C_FQ, C_FK, C_FV = 0, D_FOX, 2 * D_FOX
C_SQ = 3 * D_FOX
C_SK = C_SQ + D_SWA
C_SV = C_SK + D_SWA_KV
C_FF = C_SV + D_SWA_KV
D_PROJ = C_FF + LANES


def _rms(v):
    return v * lax.rsqrt(jnp.mean(v * v, axis=-1, keepdims=True) + RMS_EPS)


def _dot(a, b):
    return jnp.dot(a, b, preferred_element_type=jnp.float32)


def _dot_nt(a, b):
    return lax.dot_general(a, b, (((1,), (1,)), ((), ())),
                           preferred_element_type=jnp.float32)


def _proj_kernel(x_ref, g_ref, w_ref, bf_ref, tri_ref,
                 fq_ref, fk_ref, fv_ref, qa_ref, ka_ref, sq_ref, sk_ref, sv_ref,
                 carry_ref):
    @pl.when(pl.program_id(1) == 0)
    def _():
        carry_ref[...] = jnp.zeros_like(carry_ref)

    a = (_rms(x_ref[...]) * g_ref[...]).astype(jnp.bfloat16)

    def proj(lo, width):
        return _dot(a, w_ref[:, lo:lo + width])

    fq_ref[...] = proj(C_FQ, D_FOX).astype(jnp.bfloat16)
    fk_ref[...] = proj(C_FK, D_FOX).astype(jnp.bfloat16)
    fv_ref[...] = proj(C_FV, D_FOX).astype(jnp.bfloat16)
    sq_ref[...] = proj(C_SQ, D_SWA).astype(jnp.bfloat16)

    lane = lax.broadcasted_iota(jnp.int32, (PROJ_TM, LANES), 1)
    lower = lane < HEAD_DIM
    kv = proj(C_SK, 2 * D_SWA_KV)
    for half, out_ref in enumerate((sk_ref, sv_ref)):
        z = kv[:, half * LANES:(half + 1) * LANES]
        zr = pltpu.roll(z, HEAD_DIM, 1)
        out_ref[:, :LANES] = jnp.where(lower, z, zr).astype(jnp.bfloat16)
        out_ref[:, LANES:] = jnp.where(lower, zr, z).astype(jnp.bfloat16)

    term = (lane // N_FOX_HEADS) % N_TERMS
    v = proj(C_FF, LANES) + bf_ref[...]
    logf = (jnp.minimum(v, 0.0) - jnp.log1p(jnp.exp(-jnp.abs(v)))) * LOG2E
    x = _pick_piece(_split_bf16(logf), term, TERM_SRC)
    c = carry_ref[...] + _dot(tri_ref[...], x)
    carry_ref[...] = c[PROJ_TM - 1:PROJ_TM, :]
    terms = _pick_piece(_split_bf16(c), term, TERM_CUT)
    one = jnp.ones_like(terms)
    qa_ref[...] = jnp.where(lower, terms, one)
    ka_ref[...] = jnp.where(lower, one, -terms)


def _split_bf16(v):
    pieces = []
    r = v
    for _ in range(N_SPLIT):
        t = r.astype(jnp.bfloat16)
        pieces.append(t)
        r = r - t.astype(jnp.float32)
    return pieces


def _pick_piece(pieces, term, piece_of_term):
    out = pieces[piece_of_term[-1]]
    for t in range(N_TERMS - 2, -1, -1):
        if piece_of_term[t] != piece_of_term[t + 1]:
            out = jnp.where(term <= t, pieces[piece_of_term[t]], out)
    return out


def _proj_call(x2, g_pre, w_proj, b_ff, batch, seq):
    n = batch * seq
    steps = seq // PROJ_TM
    tri = np.tril(np.ones((PROJ_TM, PROJ_TM), np.float32))
    row = lambda b, s: (b * steps + s, 0)
    const = lambda b, s: (0, 0)
    bf16 = jnp.bfloat16

    def out(width):
        return (jax.ShapeDtypeStruct((n, width), bf16), pl.BlockSpec((PROJ_TM, width), row))

    outs = [out(D_FOX), out(D_FOX), out(D_FOX), out(LANES), out(LANES),
            out(D_SWA), out(2 * LANES), out(2 * LANES)]
    return pl.pallas_call(
        _proj_kernel,
        grid=(batch, steps),
        in_specs=[
            pl.BlockSpec((PROJ_TM, D_MODEL), row),
            pl.BlockSpec((1, D_MODEL), const),
            pl.BlockSpec((D_MODEL, D_PROJ), const),
            pl.BlockSpec((1, LANES), const),
            pl.BlockSpec((PROJ_TM, PROJ_TM), const),
        ],
        out_specs=[o[1] for o in outs],
        out_shape=[o[0] for o in outs],
        scratch_shapes=[pltpu.VMEM((1, LANES), jnp.float32)],
        compiler_params=pltpu.CompilerParams(
            dimension_semantics=("parallel", "arbitrary"),
            vmem_limit_bytes=VMEM_LIMIT),
        name="proj",
    )(x2, g_pre, w_proj, b_ff, jnp.asarray(tri, bf16))


def _fox_kernel(q_ref, qa_ref, k_ref, ka_ref, v_ref, o_ref, qm_ref, s_ref, m_ref, acc_ref, *, seq):
    t = FOX_T
    lane = lax.broadcasted_iota(jnp.int32, (t, LANES), 1)
    lower = lane < HEAD_DIM
    head_of_lane = lane % N_FOX_HEADS
    first_head = 2 * pl.program_id(1)
    d = FOX_DIAG
    causal = (lax.broadcasted_iota(jnp.int32, (d, d), 0)
              >= lax.broadcasted_iota(jnp.int32, (d, d), 1))

    def tile_rows(i):
        return slice(i * t, (i + 1) * t)

    def parts(i, j):
        if j < i:
            return [(0, t, t)]
        return [(r, r + d, r + d) for r in range(0, t, d)]

    def mask_queries(i):
        q = q_ref[0, tile_rows(i), :]
        qa = qa_ref[0, tile_rows(i), :]
        zero = jnp.zeros_like(q)
        for e in range(2):
            qm_ref[i, e] = jnp.concatenate(
                [jnp.where(lower, q, zero) if e == 0 else jnp.where(lower, zero, q),
                 jnp.where(head_of_lane == first_head + e, qa, zero)], axis=-1)

    def scores(i, j, slot):
        for r0, r1, nk in parts(i, j):
            keys = slice(j * t, j * t + nk)
            kk = jnp.concatenate([k_ref[0, keys, :], ka_ref[0, keys, :]], axis=-1)
            for e in range(2):
                s_ref[slot, e, r0:r1, :nk] = _dot_nt(qm_ref[i, e, r0:r1], kk)

    def update(i, j, slot):
        for r0, r1, nk in parts(i, j):
            v = v_ref[0, j * t:j * t + nk, :]
            one = jnp.ones_like(v)
            for e in range(2):
                s = s_ref[slot, e, r0:r1, :nk]
                if j == i:
                    last = jnp.where(causal, s[:, nk - d:], NEG_BIG)
                    s = last if nk == d else jnp.concatenate([s[:, :nk - d], last], axis=-1)
                row_max = jnp.max(s, axis=-1, keepdims=True)
                m_new = (jnp.broadcast_to(row_max, (r1 - r0, LANES)) if j == 0
                         else jnp.maximum(m_ref[e, r0:r1], row_max))
                p = jnp.concatenate(
                    [jnp.exp2(s[:, c * LANES:(c + 1) * LANES] - m_new)
                     for c in range(nk // LANES)], axis=-1).astype(jnp.bfloat16)
                v_lower = lax.broadcasted_iota(jnp.int32, v.shape, 1) < HEAD_DIM
                ve = jnp.where(v_lower, v, one) if e == 0 else jnp.where(v_lower, one, v)
                pv = _dot(p, ve)
                if j > 0:
                    pv = jnp.exp2(m_ref[e, r0:r1] - m_new) * acc_ref[e, r0:r1] + pv
                acc_ref[e, r0:r1] = pv
                m_ref[e, r0:r1] = m_new

    def finish(i):
        outs = []
        for e in range(2):
            acc = acc_ref[e]
            outs.append(acc / pltpu.roll(acc, HEAD_DIM, 1))
        o_ref[0, tile_rows(i), :] = jnp.where(lower, outs[0], outs[1]).astype(o_ref.dtype)

    n_tiles = seq // t
    for i in range(n_tiles):
        mask_queries(i)
    pairs = [(i, j) for i in range(n_tiles) for j in range(i + 1)]
    scores(*pairs[0], 0)
    for idx, (i, j) in enumerate(pairs):
        if idx + 1 < len(pairs):
            scores(*pairs[idx + 1], (idx + 1) % 2)
        update(i, j, idx % 2)
        if j == i:
            finish(i)


def _fox_call(fq, qa, fk, ka, fv, batch, seq):
    t = FOX_T
    shape3 = (batch, seq, N_PAIRS * LANES)
    aug3 = (batch, seq, LANES)
    spec = pl.BlockSpec((1, seq, LANES), lambda b, p: (b, 0, p))
    aug_spec = pl.BlockSpec((1, seq, LANES), lambda b, p: (b, 0, 0))
    return pl.pallas_call(
        functools.partial(_fox_kernel, seq=seq),
        grid=(batch, N_PAIRS),
        in_specs=[spec, aug_spec, spec, aug_spec, spec],
        out_specs=spec,
        out_shape=jax.ShapeDtypeStruct(shape3, jnp.bfloat16),
        scratch_shapes=[pltpu.VMEM((seq // t, 2, t, 2 * LANES), jnp.bfloat16),
                        pltpu.VMEM((2, 2, t, t), jnp.float32),
                        pltpu.VMEM((2, t, LANES), jnp.float32),
                        pltpu.VMEM((2, t, LANES), jnp.float32)],
        compiler_params=pltpu.CompilerParams(
            dimension_semantics=("parallel", "parallel"),
            vmem_limit_bytes=VMEM_LIMIT),
        name="fox",
    )(fq.reshape(shape3), qa.reshape(aug3), fk.reshape(shape3), ka.reshape(aug3),
      fv.reshape(shape3))


def _t5_bucket(n):
    max_exact = N_BUCKETS // 2
    large = max_exact + (np.log(np.maximum(n, 1) / max_exact)
                         / np.log(MAX_DISTANCE / max_exact)
                         * (N_BUCKETS - max_exact)).astype(np.int32)
    large = np.minimum(large, N_BUCKETS - 1)
    return np.where(n < max_exact, n, large).astype(np.int32)


def _band_buckets():
    i = np.arange(Q_BLOCK)[:, None]
    j = np.arange(2 * Q_BLOCK)[None, :]
    dist = i + Q_BLOCK - j
    in_window = (dist >= 0) & (dist < WINDOW)
    return np.where(in_window, _t5_bucket(np.clip(dist, 0, None)), -1).astype(np.int32)


def _swa_kernel(bucket_ref, rel_ref, sink_ref, q_ref, k_ref, v_ref, o_ref, bias_ref, sinkc_ref,
                s_ref, *, seq):
    qb = Q_BLOCK
    kv = pl.program_id(1)

    @pl.when((pl.program_id(0) == 0) & (kv == 0))
    def _():
        bucket = bucket_ref[...]
        for h in range(N_SWA_HEADS):
            bias = jnp.full(bucket.shape, NEG_BIG, jnp.float32)
            for b in range(N_BUCKETS):
                bias = jnp.where(bucket == b, rel_ref[b, h] * LOG2E, bias)
            group, g = divmod(h, SWA_GROUP)
            bias_ref[group, g * qb:(g + 1) * qb, :] = bias
            sinkc_ref[group, g * qb:(g + 1) * qb, :] = jnp.full((qb, LANES), sink_ref[h] * LOG2E)

    lower = lax.broadcasted_iota(jnp.int32, (qb, LANES), 1) < HEAD_DIM

    def band_of(n):
        if n == 0:
            return slice(0, qb), qb, slice(qb, 2 * qb)
        return slice((n - 1) * qb, (n + 1) * qb), 2 * qb, slice(0, 2 * qb)

    def scores(n, slot):
        rows = slice(n * qb, (n + 1) * qb)
        band, width, cols = band_of(n)
        parts = []
        for pair in range(SWA_GROUP // 2):
            qp = q_ref[0, rows, pair * LANES:(pair + 1) * LANES]
            zero = jnp.zeros_like(qp)
            parts += [jnp.where(lower, qp, zero), jnp.where(lower, zero, qp)]
        s_ref[slot, :, :width] = (_dot_nt(jnp.concatenate(parts, axis=0), k_ref[0, band, :])
                                  + bias_ref[kv, :, cols])

    def update(n, slot):
        rows = slice(n * qb, (n + 1) * qb)
        band, width, _ = band_of(n)
        vb = v_ref[0, band, :]
        vb = jnp.concatenate([vb, jnp.ones_like(vb)], axis=-1)
        s = s_ref[slot, :, :width]
        sink = sinkc_ref[kv]
        m = jnp.maximum(jnp.max(s, axis=-1, keepdims=True), sink)
        p = jnp.concatenate(
            [jnp.exp2(s[:, c * LANES:(c + 1) * LANES] - m) for c in range(width // LANES)],
            axis=-1).astype(jnp.bfloat16)
        acc = _dot(p, vb)
        out = acc[:, :LANES] / (acc[:, LANES:] + jnp.exp2(sink - m))
        for pair in range(SWA_GROUP // 2):
            even = slice(2 * pair * qb, (2 * pair + 1) * qb)
            odd = slice((2 * pair + 1) * qb, (2 * pair + 2) * qb)
            o_ref[0, rows, pair * LANES:(pair + 1) * LANES] = jnp.where(
                lower, out[even], out[odd]).astype(o_ref.dtype)

    n_blocks = seq // qb
    scores(0, 0)
    for n in range(n_blocks):
        if n + 1 < n_blocks:
            scores(n + 1, (n + 1) % 2)
        update(n, n % 2)


def _swa_call(sq, sk, sv, rel_bias, sinks, batch, seq):
    bucket = _band_buckets()
    width = SWA_GROUP * HEAD_DIM
    smem = pl.BlockSpec(memory_space=pltpu.SMEM)
    q_spec = pl.BlockSpec((1, seq, width), lambda b, kv: (b, 0, kv))
    kv_spec = pl.BlockSpec((1, seq, LANES), lambda b, kv: (b, 0, kv))
    return pl.pallas_call(
        functools.partial(_swa_kernel, seq=seq),
        grid=(batch, N_SWA_KV_HEADS),
        in_specs=[pl.BlockSpec(bucket.shape, lambda b, kv: (0, 0)), smem, smem,
                  q_spec, kv_spec, kv_spec],
        out_specs=q_spec,
        out_shape=jax.ShapeDtypeStruct((batch, seq, D_SWA), jnp.bfloat16),
        scratch_shapes=[
            pltpu.VMEM((N_SWA_KV_HEADS, SWA_GROUP * Q_BLOCK, 2 * Q_BLOCK), jnp.float32),
            pltpu.VMEM((N_SWA_KV_HEADS, SWA_GROUP * Q_BLOCK, LANES), jnp.float32),
            pltpu.VMEM((2, SWA_GROUP * Q_BLOCK, 2 * Q_BLOCK), jnp.float32)],
        compiler_params=pltpu.CompilerParams(
            dimension_semantics=("arbitrary", "arbitrary"),
            vmem_limit_bytes=VMEM_LIMIT),
        name="swa",
    )(jnp.asarray(bucket), rel_bias, sinks, sq.reshape(batch, seq, D_SWA),
      sk.reshape(batch, seq, 2 * LANES), sv.reshape(batch, seq, 2 * LANES))


def _post_kernel(fox_ref, swa_ref, x_ref, p_ref, wo_ref, w1_ref, w2_ref, wg_ref, wp_ref,
                 gains_ref, o_ref):
    bf16 = jnp.bfloat16
    mix = _dot(jnp.concatenate([fox_ref[...], swa_ref[...]], axis=-1), wo_ref[...])
    h = x_ref[...] + _rms(mix) * gains_ref[0:1, :]

    m = (_rms(h) * gains_ref[1:2, :]).astype(bf16)
    y = jnp.zeros((POST_TM, D_MODEL), jnp.float32)
    for c in range(D_FF // FF_CHUNK):
        cols = slice(c * FF_CHUNK, (c + 1) * FF_CHUNK)
        hid = jnp.square(jnp.maximum(_dot(m, w1_ref[:, cols]), 0.0))
        y = y + _dot(hid.astype(bf16), w2_ref[cols, :])
    h = h + _rms(y) * gains_ref[2:3, :]

    gate = jax.nn.sigmoid(_dot(h.astype(bf16), wg_ref[...]))
    e = _dot(p_ref[...].astype(bf16), wp_ref[...]) * gate
    o_ref[...] = h + _rms(e) * gains_ref[3:4, :]


def _post_call(fox, swa, x2, p2, wo, w1, w2, wg, wp, gains):
    n = x2.shape[0]
    row = lambda i: (i, 0)
    const = lambda i: (0, 0)

    def resident(shape):
        return pl.BlockSpec(shape, const, pipeline_mode=pl.Buffered(1))

    return pl.pallas_call(
        _post_kernel,
        grid=(n // POST_TM,),
        in_specs=[
            pl.BlockSpec((POST_TM, D_FOX), row),
            pl.BlockSpec((POST_TM, D_SWA), row),
            pl.BlockSpec((POST_TM, D_MODEL), row),
            pl.BlockSpec((POST_TM, D_PLE), row),
            resident(wo.shape), resident(w1.shape), resident(w2.shape),
            resident(wg.shape), resident(wp.shape), resident(gains.shape),
        ],
        out_specs=pl.BlockSpec((POST_TM, D_MODEL), row),
        out_shape=jax.ShapeDtypeStruct((n, D_MODEL), jnp.float32),
        compiler_params=pltpu.CompilerParams(
            dimension_semantics=("parallel",),
            vmem_limit_bytes=VMEM_LIMIT),
        name="post",
    )(fox, swa, x2, p2, wo, w1, w2, wg, wp, gains)


def _gate_lanes(v):
    return jnp.tile(v, (1, LANES // N_FOX_HEADS))


def _proj_weight(w_in):
    scale = HEAD_DIM ** -0.5
    splits = np.cumsum([D_FOX, D_FOX, D_FOX, N_FOX_HEADS, D_SWA, D_SWA_KV])
    fq, fk, fv, ff, sq, sk, sv = jnp.split(w_in, splits, axis=-1)
    ff = _gate_lanes(ff)
    return jnp.concatenate([fq * (scale * LOG2E), fk, fv, sq * (scale * LOG2E), sk, sv, ff],
                           axis=-1).astype(jnp.bfloat16)


def kernel(x, p, w_in, b_forget, w_out, rel_bias, swa_sinks, g_attn_pre, g_attn_post,
           w_ff1, w_ff2, g_ff_pre, g_ff_post, w_ple, w_ple_gate, g_ple_post):
    batch, seq, _ = x.shape
    n = batch * seq
    bf16 = jnp.bfloat16
    h = x.reshape(n, D_MODEL)
    for i in range(p.shape[0]):
        b_ff = _gate_lanes(b_forget[i].reshape(1, N_FOX_HEADS))
        fq, fk, fv, qa, ka, sq, sk, sv = _proj_call(
            h, g_attn_pre[i].reshape(1, D_MODEL), _proj_weight(w_in[i]), b_ff, batch, seq)
        fox = _fox_call(fq, qa, fk, ka, fv, batch, seq).reshape(n, D_FOX)
        swa = _swa_call(sq, sk, sv, rel_bias, swa_sinks[i], batch, seq).reshape(n, D_SWA)
        gains = jnp.stack([g_attn_post[i], g_ff_pre[i], g_ff_post[i], g_ple_post[i]])
        h = _post_call(fox, swa, h, p[i].reshape(n, D_PLE),
                       w_out[i].astype(bf16), w_ff1[i].astype(bf16), w_ff2[i].astype(bf16),
                       w_ple_gate[i].astype(bf16), w_ple[i].astype(bf16), gains)
    return h.reshape(batch, seq, D_MODEL)
```

```python
import functools

import jax
import jax.numpy as jnp
import numpy as np
from jax import lax
from jax.experimental import pallas as pl
from jax.experimental.pallas import tpu as pltpu

D_MODEL = 1024
HEAD_DIM = 64
N_FOX_HEADS = 8
N_SWA_HEADS = 8
N_SWA_KV_HEADS = 2
SWA_GROUP = N_SWA_HEADS // N_SWA_KV_HEADS
D_FOX = N_FOX_HEADS * HEAD_DIM
D_SWA = N_SWA_HEADS * HEAD_DIM
D_SWA_KV = N_SWA_KV_HEADS * HEAD_DIM
D_FF = 4 * D_MODEL
D_PLE = 256
WINDOW = 128
Q_BLOCK = 128
N_BUCKETS = 32
MAX_DISTANCE = 128
RMS_EPS = 1e-6

LANES = 128
N_PAIRS = N_FOX_HEADS // 2
N_SPLIT = 3
N_TERMS = HEAD_DIM // N_FOX_HEADS
TERM_SRC = (0, 0, 0, 1, 1, 1, 2, 2)
TERM_CUT = (0, 1, 2, 0, 1, 2, 0, 1)
NEG_BIG = -1e30
LOG2E = float(np.log2(np.e))

PROJ_TM = 512
FOX_T = 512
FOX_DIAG = 256
POST_TM = 512
FF_CHUNK = 1024
VMEM_LIMIT = 56 * 1024 * 1024

C_FQ, C_FK, C_FV = 0, D_FOX, 2 * D_FOX
C_SQ = 3 * D_FOX
C_SK = C_SQ + D_SWA
C_SV = C_SK + D_SWA_KV
C_FF = C_SV + D_SWA_KV
D_PROJ = C_FF + LANES


def _rms(v):
    return v * lax.rsqrt(jnp.mean(v * v, axis=-1, keepdims=True) + RMS_EPS)


def _dot(a, b):
    return jnp.dot(a, b, preferred_element_type=jnp.float32)


def _dot_nt(a, b):
    return lax.dot_general(a, b, (((1,), (1,)), ((), ())),
                           preferred_element_type=jnp.float32)


def _proj_kernel(x_ref, g_ref, win_ref, bf_ref, tri_ref, spread_ref,
                 fq_ref, fk_ref, fv_ref, qa_ref, ka_ref, sq_ref, sk_ref, sv_ref,
                 carry_ref, w_ref):
    @pl.when((pl.program_id(0) == 0) & (pl.program_id(1) == 0))
    def _():
        bf16 = jnp.bfloat16
        q_scale = HEAD_DIM ** -0.5 * LOG2E
        src = np.cumsum([0, D_FOX, D_FOX, D_FOX, N_FOX_HEADS, D_SWA])
        src_fq, src_fk, src_fv, src_ff, src_sq, src_skv = (int(c) for c in src)
        w_ref[:, C_FQ:C_FQ + D_FOX] = (win_ref[:, src_fq:src_fq + D_FOX] * q_scale).astype(bf16)
        w_ref[:, C_FK:C_FK + D_FOX] = win_ref[:, src_fk:src_fk + D_FOX].astype(bf16)
        w_ref[:, C_FV:C_FV + D_FOX] = win_ref[:, src_fv:src_fv + D_FOX].astype(bf16)
        w_ref[:, C_SQ:C_SQ + D_SWA] = (win_ref[:, src_sq:src_sq + D_SWA] * q_scale).astype(bf16)
        w_ref[:, C_SK:C_SK + 2 * D_SWA_KV] = win_ref[:, src_skv:src_skv + 2 * D_SWA_KV].astype(bf16)
        gate = win_ref[:, src_ff:src_ff + LANES].astype(bf16)
        w_ref[:, C_FF:C_FF + LANES] = _dot(gate, spread_ref[...]).astype(bf16)

    @pl.when(pl.program_id(1) == 0)
    def _():
        carry_ref[...] = jnp.zeros_like(carry_ref)

    a = (_rms(x_ref[...]) * g_ref[...]).astype(jnp.bfloat16)

    def proj(lo, width):
        return _dot(a, w_ref[:, lo:lo + width])

    fq_ref[...] = proj(C_FQ, D_FOX).astype(jnp.bfloat16)
    fk_ref[...] = proj(C_FK, D_FOX).astype(jnp.bfloat16)
    fv_ref[...] = proj(C_FV, D_FOX).astype(jnp.bfloat16)
    sq_ref[...] = proj(C_SQ, D_SWA).astype(jnp.bfloat16)

    lane = lax.broadcasted_iota(jnp.int32, (PROJ_TM, LANES), 1)
    lower = lane < HEAD_DIM
    kv = proj(C_SK, 2 * D_SWA_KV)
    for half, out_ref in enumerate((sk_ref, sv_ref)):
        z = kv[:, half * LANES:(half + 1) * LANES]
        zr = pltpu.roll(z, HEAD_DIM, 1)
        out_ref[:, :LANES] = jnp.where(lower, z, zr).astype(jnp.bfloat16)
        out_ref[:, LANES:] = jnp.where(lower, zr, z).astype(jnp.bfloat16)

    term = (lane // N_FOX_HEADS) % N_TERMS
    v = proj(C_FF, LANES) + bf_ref[...]
    logf = (jnp.minimum(v, 0.0) - jnp.log1p(jnp.exp(-jnp.abs(v)))) * LOG2E
    x = _pick_piece(_split_bf16(logf), term, TERM_SRC)
    c = carry_ref[...] + _dot(tri_ref[...], x)
    carry_ref[...] = c[PROJ_TM - 1:PROJ_TM, :]
    terms = _pick_piece(_split_bf16(c), term, TERM_CUT)
    one = jnp.ones_like(terms)
    qa_ref[...] = jnp.where(lower, terms, one)
    ka_ref[...] = jnp.where(lower, one, -terms)


def _split_bf16(v):
    pieces = []
    r = v
    for _ in range(N_SPLIT):
        t = r.astype(jnp.bfloat16)
        pieces.append(t)
        r = r - t.astype(jnp.float32)
    return pieces


def _pick_piece(pieces, term, piece_of_term):
    out = pieces[piece_of_term[-1]]
    for t in range(N_TERMS - 2, -1, -1):
        if piece_of_term[t] != piece_of_term[t + 1]:
            out = jnp.where(term <= t, pieces[piece_of_term[t]], out)
    return out


def _proj_call(x2, g_pre, w_in, b_ff, batch, seq):
    n = batch * seq
    steps = seq // PROJ_TM
    tri = np.tril(np.ones((PROJ_TM, PROJ_TM), np.float32))
    spread = (np.arange(LANES)[:, None] == np.arange(LANES)[None, :] % N_FOX_HEADS)
    spread = spread & (np.arange(LANES)[:, None] < N_FOX_HEADS)
    row = lambda b, s: (b * steps + s, 0)
    const = lambda b, s: (0, 0)
    bf16 = jnp.bfloat16

    def out(width):
        return (jax.ShapeDtypeStruct((n, width), bf16), pl.BlockSpec((PROJ_TM, width), row))

    outs = [out(D_FOX), out(D_FOX), out(D_FOX), out(LANES), out(LANES),
            out(D_SWA), out(2 * LANES), out(2 * LANES)]
    return pl.pallas_call(
        _proj_kernel,
        grid=(batch, steps),
        in_specs=[
            pl.BlockSpec((PROJ_TM, D_MODEL), row),
            pl.BlockSpec((1, D_MODEL), const),
            pl.BlockSpec(w_in.shape, const, pipeline_mode=pl.Buffered(1)),
            pl.BlockSpec((1, LANES), const),
            pl.BlockSpec((PROJ_TM, PROJ_TM), const),
            pl.BlockSpec((LANES, LANES), const),
        ],
        out_specs=[o[1] for o in outs],
        out_shape=[o[0] for o in outs],
        scratch_shapes=[pltpu.VMEM((1, LANES), jnp.float32),
                        pltpu.VMEM((D_MODEL, D_PROJ), bf16)],
        compiler_params=pltpu.CompilerParams(
            dimension_semantics=("arbitrary", "arbitrary"),
            vmem_limit_bytes=VMEM_LIMIT),
        name="proj",
    )(x2, g_pre, w_in, b_ff, jnp.asarray(tri, bf16), jnp.asarray(spread, bf16))


def _fox_kernel(q_ref, qa_ref, k_ref, ka_ref, v_ref, o_ref, qm_ref, s_ref, m_ref, acc_ref, *, seq):
    t = FOX_T
    lane = lax.broadcasted_iota(jnp.int32, (t, LANES), 1)
    lower = lane < HEAD_DIM
    head_of_lane = lane % N_FOX_HEADS
    first_head = 2 * pl.program_id(1)
    d = FOX_DIAG
    causal = (lax.broadcasted_iota(jnp.int32, (d, d), 0)
              >= lax.broadcasted_iota(jnp.int32, (d, d), 1))

    def tile_rows(i):
        return slice(i * t, (i + 1) * t)

    def parts(i, j):
        if j < i:
            return [(0, t, t)]
        return [(r, r + d, r + d) for r in range(0, t, d)]

    def mask_queries(i):
        q = q_ref[0, tile_rows(i), :]
        qa = qa_ref[0, tile_rows(i), :]
        zero = jnp.zeros_like(q)
        for e in range(2):
            qm_ref[i, e] = jnp.concatenate(
                [jnp.where(lower, q, zero) if e == 0 else jnp.where(lower, zero, q),
                 jnp.where(head_of_lane == first_head + e, qa, zero)], axis=-1)

    def scores(i, j, slot):
        for r0, r1, nk in parts(i, j):
            keys = slice(j * t, j * t + nk)
            kk = jnp.concatenate([k_ref[0, keys, :], ka_ref[0, keys, :]], axis=-1)
            for e in range(2):
                s_ref[slot, e, r0:r1, :nk] = _dot_nt(qm_ref[i, e, r0:r1], kk)

    def update(i, j, slot):
        for r0, r1, nk in parts(i, j):
            v = v_ref[0, j * t:j * t + nk, :]
            one = jnp.ones_like(v)
            for e in range(2):
                s = s_ref[slot, e, r0:r1, :nk]
                if j == i:
                    last = jnp.where(causal, s[:, nk - d:], NEG_BIG)
                    s = last if nk == d else jnp.concatenate([s[:, :nk - d], last], axis=-1)
                row_max = jnp.max(s, axis=-1, keepdims=True)
                m_new = (jnp.broadcast_to(row_max, (r1 - r0, LANES)) if j == 0
                         else jnp.maximum(m_ref[e, r0:r1], row_max))
                p = jnp.concatenate(
                    [jnp.exp2(s[:, c * LANES:(c + 1) * LANES] - m_new)
                     for c in range(nk // LANES)], axis=-1).astype(jnp.bfloat16)
                v_lower = lax.broadcasted_iota(jnp.int32, v.shape, 1) < HEAD_DIM
                ve = jnp.where(v_lower, v, one) if e == 0 else jnp.where(v_lower, one, v)
                pv = _dot(p, ve)
                if j > 0:
                    pv = jnp.exp2(m_ref[e, r0:r1] - m_new) * acc_ref[e, r0:r1] + pv
                acc_ref[e, r0:r1] = pv
                m_ref[e, r0:r1] = m_new

    def finish(i):
        outs = []
        for e in range(2):
            acc = acc_ref[e]
            outs.append(acc / pltpu.roll(acc, HEAD_DIM, 1))
        o_ref[0, tile_rows(i), :] = jnp.where(lower, outs[0], outs[1]).astype(o_ref.dtype)

    n_tiles = seq // t
    for i in range(n_tiles):
        mask_queries(i)
    pairs = [(i, j) for i in range(n_tiles) for j in range(i + 1)]
    scores(*pairs[0], 0)
    for idx, (i, j) in enumerate(pairs):
        if idx + 1 < len(pairs):
            scores(*pairs[idx + 1], (idx + 1) % 2)
        update(i, j, idx % 2)
        if j == i:
            finish(i)


def _fox_call(fq, qa, fk, ka, fv, batch, seq):
    t = FOX_T
    shape3 = (batch, seq, N_PAIRS * LANES)
    aug3 = (batch, seq, LANES)
    spec = pl.BlockSpec((1, seq, LANES), lambda b, p: (b, 0, p))
    aug_spec = pl.BlockSpec((1, seq, LANES), lambda b, p: (b, 0, 0))
    return pl.pallas_call(
        functools.partial(_fox_kernel, seq=seq),
        grid=(batch, N_PAIRS),
        in_specs=[spec, aug_spec, spec, aug_spec, spec],
        out_specs=spec,
        out_shape=jax.ShapeDtypeStruct(shape3, jnp.bfloat16),
        scratch_shapes=[pltpu.VMEM((seq // t, 2, t, 2 * LANES), jnp.bfloat16),
                        pltpu.VMEM((2, 2, t, t), jnp.float32),
                        pltpu.VMEM((2, t, LANES), jnp.float32),
                        pltpu.VMEM((2, t, LANES), jnp.float32)],
        compiler_params=pltpu.CompilerParams(
            dimension_semantics=("parallel", "parallel"),
            vmem_limit_bytes=VMEM_LIMIT),
        name="fox",
    )(fq.reshape(shape3), qa.reshape(aug3), fk.reshape(shape3), ka.reshape(aug3),
      fv.reshape(shape3))


def _t5_bucket(n):
    max_exact = N_BUCKETS // 2
    large = max_exact + (np.log(np.maximum(n, 1) / max_exact)
                         / np.log(MAX_DISTANCE / max_exact)
                         * (N_BUCKETS - max_exact)).astype(np.int32)
    large = np.minimum(large, N_BUCKETS - 1)
    return np.where(n < max_exact, n, large).astype(np.int32)


def _band_buckets():
    i = np.arange(Q_BLOCK)[:, None]
    j = np.arange(2 * Q_BLOCK)[None, :]
    dist = i + Q_BLOCK - j
    in_window = (dist >= 0) & (dist < WINDOW)
    return np.where(in_window, _t5_bucket(np.clip(dist, 0, None)), -1).astype(np.int32)


def _swa_kernel(bucket_ref, rel_ref, sink_ref, q_ref, k_ref, v_ref, o_ref, bias_ref, sinkc_ref,
                s_ref, *, seq):
    qb = Q_BLOCK
    kv = pl.program_id(1)

    @pl.when((pl.program_id(0) == 0) & (kv == 0))
    def _():
        bucket = bucket_ref[...]
        for h in range(N_SWA_HEADS):
            bias = jnp.full(bucket.shape, NEG_BIG, jnp.float32)
            for b in range(N_BUCKETS):
                bias = jnp.where(bucket == b, rel_ref[b, h] * LOG2E, bias)
            group, g = divmod(h, SWA_GROUP)
            bias_ref[group, g * qb:(g + 1) * qb, :] = bias
            sinkc_ref[group, g * qb:(g + 1) * qb, :] = jnp.full((qb, LANES), sink_ref[h] * LOG2E)

    lower = lax.broadcasted_iota(jnp.int32, (qb, LANES), 1) < HEAD_DIM

    def band_of(n):
        if n == 0:
            return slice(0, qb), qb, slice(qb, 2 * qb)
        return slice((n - 1) * qb, (n + 1) * qb), 2 * qb, slice(0, 2 * qb)

    def scores(n, slot):
        rows = slice(n * qb, (n + 1) * qb)
        band, width, cols = band_of(n)
        parts = []
        for pair in range(SWA_GROUP // 2):
            qp = q_ref[0, rows, pair * LANES:(pair + 1) * LANES]
            zero = jnp.zeros_like(qp)
            parts += [jnp.where(lower, qp, zero), jnp.where(lower, zero, qp)]
        s_ref[slot, :, :width] = (_dot_nt(jnp.concatenate(parts, axis=0), k_ref[0, band, :])
                                  + bias_ref[kv, :, cols])

    def update(n, slot):
        rows = slice(n * qb, (n + 1) * qb)
        band, width, _ = band_of(n)
        vb = v_ref[0, band, :]
        vb = jnp.concatenate([vb, jnp.ones_like(vb)], axis=-1)
        s = s_ref[slot, :, :width]
        sink = sinkc_ref[kv]
        m = jnp.maximum(jnp.max(s, axis=-1, keepdims=True), sink)
        p = jnp.concatenate(
            [jnp.exp2(s[:, c * LANES:(c + 1) * LANES] - m) for c in range(width // LANES)],
            axis=-1).astype(jnp.bfloat16)
        acc = _dot(p, vb)
        out = acc[:, :LANES] / (acc[:, LANES:] + jnp.exp2(sink - m))
        for pair in range(SWA_GROUP // 2):
            even = slice(2 * pair * qb, (2 * pair + 1) * qb)
            odd = slice((2 * pair + 1) * qb, (2 * pair + 2) * qb)
            o_ref[0, rows, pair * LANES:(pair + 1) * LANES] = jnp.where(
                lower, out[even], out[odd]).astype(o_ref.dtype)

    n_blocks = seq // qb
    scores(0, 0)
    for n in range(n_blocks):
        if n + 1 < n_blocks:
            scores(n + 1, (n + 1) % 2)
        update(n, n % 2)


def _swa_call(sq, sk, sv, rel_bias, sinks, batch, seq):
    bucket = _band_buckets()
    width = SWA_GROUP * HEAD_DIM
    smem = pl.BlockSpec(memory_space=pltpu.SMEM)
    q_spec = pl.BlockSpec((1, seq, width), lambda b, kv: (b, 0, kv))
    kv_spec = pl.BlockSpec((1, seq, LANES), lambda b, kv: (b, 0, kv))
    return pl.pallas_call(
        functools.partial(_swa_kernel, seq=seq),
        grid=(batch, N_SWA_KV_HEADS),
        in_specs=[pl.BlockSpec(bucket.shape, lambda b, kv: (0, 0)), smem, smem,
                  q_spec, kv_spec, kv_spec],
        out_specs=q_spec,
        out_shape=jax.ShapeDtypeStruct((batch, seq, D_SWA), jnp.bfloat16),
        scratch_shapes=[
            pltpu.VMEM((N_SWA_KV_HEADS, SWA_GROUP * Q_BLOCK, 2 * Q_BLOCK), jnp.float32),
            pltpu.VMEM((N_SWA_KV_HEADS, SWA_GROUP * Q_BLOCK, LANES), jnp.float32),
            pltpu.VMEM((2, SWA_GROUP * Q_BLOCK, 2 * Q_BLOCK), jnp.float32)],
        compiler_params=pltpu.CompilerParams(
            dimension_semantics=("arbitrary", "arbitrary"),
            vmem_limit_bytes=VMEM_LIMIT),
        name="swa",
    )(jnp.asarray(bucket), rel_bias, sinks, sq.reshape(batch, seq, D_SWA),
      sk.reshape(batch, seq, 2 * LANES), sv.reshape(batch, seq, 2 * LANES))


def _post_kernel(fox_ref, swa_ref, x_ref, p_ref, wo_ref, w1_ref, w2_ref, wg_ref, wp_ref,
                 gains_ref, o_ref):
    bf16 = jnp.bfloat16
    mix = _dot(jnp.concatenate([fox_ref[...], swa_ref[...]], axis=-1), wo_ref[...])
    h = x_ref[...] + _rms(mix) * gains_ref[0:1, :]

    m = (_rms(h) * gains_ref[1:2, :]).astype(bf16)
    y = jnp.zeros((POST_TM, D_MODEL), jnp.float32)
    for c in range(D_FF // FF_CHUNK):
        cols = slice(c * FF_CHUNK, (c + 1) * FF_CHUNK)
        hid = jnp.square(jnp.maximum(_dot(m, w1_ref[:, cols]), 0.0))
        y = y + _dot(hid.astype(bf16), w2_ref[cols, :])
    h = h + _rms(y) * gains_ref[2:3, :]

    gate = jax.nn.sigmoid(_dot(h.astype(bf16), wg_ref[...]))
    e = _dot(p_ref[...].astype(bf16), wp_ref[...]) * gate
    o_ref[...] = h + _rms(e) * gains_ref[3:4, :]


def _post_call(fox, swa, x2, p2, wo, w1, w2, wg, wp, gains):
    n = x2.shape[0]
    row = lambda i: (i, 0)
    const = lambda i: (0, 0)

    def resident(shape):
        return pl.BlockSpec(shape, const, pipeline_mode=pl.Buffered(1))

    return pl.pallas_call(
        _post_kernel,
        grid=(n // POST_TM,),
        in_specs=[
            pl.BlockSpec((POST_TM, D_FOX), row),
            pl.BlockSpec((POST_TM, D_SWA), row),
            pl.BlockSpec((POST_TM, D_MODEL), row),
            pl.BlockSpec((POST_TM, D_PLE), row),
            resident(wo.shape), resident(w1.shape), resident(w2.shape),
            resident(wg.shape), resident(wp.shape), resident(gains.shape),
        ],
        out_specs=pl.BlockSpec((POST_TM, D_MODEL), row),
        out_shape=jax.ShapeDtypeStruct((n, D_MODEL), jnp.float32),
        compiler_params=pltpu.CompilerParams(
            dimension_semantics=("parallel",),
            vmem_limit_bytes=VMEM_LIMIT),
        name="post",
    )(fox, swa, x2, p2, wo, w1, w2, wg, wp, gains)


def _gate_lanes(v):
    return jnp.tile(v, (1, LANES // N_FOX_HEADS))


def kernel(x, p, w_in, b_forget, w_out, rel_bias, swa_sinks, g_attn_pre, g_attn_post,
           w_ff1, w_ff2, g_ff_pre, g_ff_post, w_ple, w_ple_gate, g_ple_post):
    batch, seq, _ = x.shape
    n = batch * seq
    bf16 = jnp.bfloat16
    h = x.reshape(n, D_MODEL)
    for i in range(p.shape[0]):
        b_ff = _gate_lanes(b_forget[i].reshape(1, N_FOX_HEADS))
        fq, fk, fv, qa, ka, sq, sk, sv = _proj_call(
            h, g_attn_pre[i].reshape(1, D_MODEL), w_in[i], b_ff, batch, seq)
        fox = _fox_call(fq, qa, fk, ka, fv, batch, seq).reshape(n, D_FOX)
        swa = _swa_call(sq, sk, sv, rel_bias, swa_sinks[i], batch, seq).reshape(n, D_SWA)
        gains = jnp.stack([g_attn_post[i], g_ff_pre[i], g_ff_post[i], g_ple_post[i]])
        h = _post_call(fox, swa, h, p[i].reshape(n, D_PLE),
                       w_out[i].astype(bf16), w_ff1[i].astype(bf16), w_ff2[i].astype(bf16),
                       w_ple_gate[i].astype(bf16), w_ple[i].astype(bf16), gains)
    return h.reshape(batch, seq, D_MODEL)
```

```python
import functools

import jax
import jax.numpy as jnp
import numpy as np
from jax import lax
from jax.experimental import pallas as pl
from jax.experimental.pallas import tpu as pltpu

D_MODEL = 1024
HEAD_DIM = 64
N_FOX_HEADS = 8
N_SWA_HEADS = 8
N_SWA_KV_HEADS = 2
SWA_GROUP = N_SWA_HEADS // N_SWA_KV_HEADS
D_FOX = N_FOX_HEADS * HEAD_DIM
D_SWA = N_SWA_HEADS * HEAD_DIM
D_SWA_KV = N_SWA_KV_HEADS * HEAD_DIM
D_FF = 4 * D_MODEL
D_PLE = 256
WINDOW = 128
Q_BLOCK = 128
N_BUCKETS = 32
MAX_DISTANCE = 128
RMS_EPS = 1e-6

LANES = 128
N_PAIRS = N_FOX_HEADS // 2
N_SPLIT = 3
N_TERMS = HEAD_DIM // N_FOX_HEADS
TERM_SRC = (0, 0, 0, 1, 1, 1, 2, 2)
TERM_CUT = (0, 1, 2, 0, 1, 2, 0, 1)
NEG_BIG = -1e30
LOG2E = float(np.log2(np.e))

PROJ_TM = 512
FOX_T = 512
FOX_DIAG = 256
POST_TM = 512
FF_CHUNK = 1024
VMEM_LIMIT = 56 * 1024 * 1024

C_FQ, C_FK, C_FV = 0, D_FOX, 2 * D_FOX
C_SQ = 3 * D_FOX
C_SK = C_SQ + D_SWA
C_SV = C_SK + D_SWA_KV
C_FF = C_SV + D_SWA_KV
D_PROJ = C_FF + LANES


def _rms(v):
    return v * lax.rsqrt(jnp.mean(v * v, axis=-1, keepdims=True) + RMS_EPS)


def _dot(a, b):
    return jnp.dot(a, b, preferred_element_type=jnp.float32)


def _dot_nt(a, b):
    return lax.dot_general(a, b, (((1,), (1,)), ((), ())),
                           preferred_element_type=jnp.float32)


def _proj_kernel(x_ref, g_ref, wt_ref, bf_ref, tri_ref,
                 fq_ref, fk_ref, fv_ref, qa_ref, ka_ref, sq_ref, sk_ref, sv_ref,
                 carry_ref, w_ref):
    @pl.when((pl.program_id(0) == 0) & (pl.program_id(1) == 0))
    def _():
        bf16 = jnp.bfloat16
        q_scale = HEAD_DIM ** -0.5 * LOG2E
        src = np.cumsum([0, D_FOX, D_FOX, D_FOX, N_FOX_HEADS, D_SWA])
        src_fq, src_fk, src_fv, src_ff, src_sq, src_skv = (int(c) for c in src)
        w_ref[C_FQ:C_FQ + D_FOX] = (wt_ref[src_fq:src_fq + D_FOX] * q_scale).astype(bf16)
        w_ref[C_FK:C_FK + D_FOX] = wt_ref[src_fk:src_fk + D_FOX].astype(bf16)
        w_ref[C_FV:C_FV + D_FOX] = wt_ref[src_fv:src_fv + D_FOX].astype(bf16)
        w_ref[C_SQ:C_SQ + D_SWA] = (wt_ref[src_sq:src_sq + D_SWA] * q_scale).astype(bf16)
        w_ref[C_SK:C_SK + 2 * D_SWA_KV] = wt_ref[src_skv:src_skv + 2 * D_SWA_KV].astype(bf16)
        gate = wt_ref[src_ff:src_ff + N_FOX_HEADS]
        w_ref[C_FF:C_FF + LANES] = jnp.tile(gate, (LANES // N_FOX_HEADS, 1)).astype(bf16)

    @pl.when(pl.program_id(1) == 0)
    def _():
        carry_ref[...] = jnp.zeros_like(carry_ref)

    a = (_rms(x_ref[...]) * g_ref[...]).astype(jnp.bfloat16)

    def proj(lo, width):
        return _dot_nt(a, w_ref[lo:lo + width])

    fq_ref[...] = proj(C_FQ, D_FOX).astype(jnp.bfloat16)
    fk_ref[...] = proj(C_FK, D_FOX).astype(jnp.bfloat16)
    fv_ref[...] = proj(C_FV, D_FOX).astype(jnp.bfloat16)
    sq_ref[...] = proj(C_SQ, D_SWA).astype(jnp.bfloat16)

    lane = lax.broadcasted_iota(jnp.int32, (PROJ_TM, LANES), 1)
    lower = lane < HEAD_DIM
    kv = proj(C_SK, 2 * D_SWA_KV)
    for half, out_ref in enumerate((sk_ref, sv_ref)):
        z = kv[:, half * LANES:(half + 1) * LANES]
        zr = pltpu.roll(z, HEAD_DIM, 1)
        out_ref[:, :LANES] = jnp.where(lower, z, zr).astype(jnp.bfloat16)
        out_ref[:, LANES:] = jnp.where(lower, zr, z).astype(jnp.bfloat16)

    term = (lane // N_FOX_HEADS) % N_TERMS
    v = proj(C_FF, LANES) + bf_ref[...]
    logf = (jnp.minimum(v, 0.0) - jnp.log1p(jnp.exp(-jnp.abs(v)))) * LOG2E
    x = _pick_piece(_split_bf16(logf), term, TERM_SRC)
    c = carry_ref[...] + _dot(tri_ref[...], x)
    carry_ref[...] = c[PROJ_TM - 1:PROJ_TM, :]
    terms = _pick_piece(_split_bf16(c), term, TERM_CUT)
    one = jnp.ones_like(terms)
    qa_ref[...] = jnp.where(lower, terms, one)
    ka_ref[...] = jnp.where(lower, one, -terms)


def _split_bf16(v):
    pieces = []
    r = v
    for _ in range(N_SPLIT):
        t = r.astype(jnp.bfloat16)
        pieces.append(t)
        r = r - t.astype(jnp.float32)
    return pieces


def _pick_piece(pieces, term, piece_of_term):
    out = pieces[piece_of_term[-1]]
    for t in range(N_TERMS - 2, -1, -1):
        if piece_of_term[t] != piece_of_term[t + 1]:
            out = jnp.where(term <= t, pieces[piece_of_term[t]], out)
    return out


def _proj_call(x2, g_pre, w_in_t, b_ff, batch, seq):
    n = batch * seq
    steps = seq // PROJ_TM
    tri = np.tril(np.ones((PROJ_TM, PROJ_TM), np.float32))
    row = lambda b, s: (b * steps + s, 0)
    const = lambda b, s: (0, 0)
    bf16 = jnp.bfloat16

    def out(width):
        return (jax.ShapeDtypeStruct((n, width), bf16), pl.BlockSpec((PROJ_TM, width), row))

    outs = [out(D_FOX), out(D_FOX), out(D_FOX), out(LANES), out(LANES),
            out(D_SWA), out(2 * LANES), out(2 * LANES)]
    return pl.pallas_call(
        _proj_kernel,
        grid=(batch, steps),
        in_specs=[
            pl.BlockSpec((PROJ_TM, D_MODEL), row),
            pl.BlockSpec((1, D_MODEL), const),
            pl.BlockSpec(w_in_t.shape, const, pipeline_mode=pl.Buffered(1)),
            pl.BlockSpec((1, LANES), const),
            pl.BlockSpec((PROJ_TM, PROJ_TM), const),
        ],
        out_specs=[o[1] for o in outs],
        out_shape=[o[0] for o in outs],
        scratch_shapes=[pltpu.VMEM((1, LANES), jnp.float32),
                        pltpu.VMEM((D_PROJ, D_MODEL), bf16)],
        compiler_params=pltpu.CompilerParams(
            dimension_semantics=("arbitrary", "arbitrary"),
            vmem_limit_bytes=VMEM_LIMIT),
        name="proj",
    )(x2, g_pre, w_in_t, b_ff, jnp.asarray(tri, bf16))


def _fox_kernel(q_ref, qa_ref, k_ref, ka_ref, v_ref, o_ref, qm_ref, s_ref, m_ref, acc_ref, *, seq):
    t = FOX_T
    lane = lax.broadcasted_iota(jnp.int32, (t, LANES), 1)
    lower = lane < HEAD_DIM
    head_of_lane = lane % N_FOX_HEADS
    first_head = 2 * pl.program_id(1)
    d = FOX_DIAG
    causal = (lax.broadcasted_iota(jnp.int32, (d, d), 0)
              >= lax.broadcasted_iota(jnp.int32, (d, d), 1))

    def tile_rows(i):
        return slice(i * t, (i + 1) * t)

    def parts(i, j):
        if j < i:
            return [(0, t, t)]
        return [(r, r + d, r + d) for r in range(0, t, d)]

    def mask_queries(i):
        q = q_ref[0, tile_rows(i), :]
        qa = qa_ref[0, tile_rows(i), :]
        zero = jnp.zeros_like(q)
        for e in range(2):
            qm_ref[i, e] = jnp.concatenate(
                [jnp.where(lower, q, zero) if e == 0 else jnp.where(lower, zero, q),
                 jnp.where(head_of_lane == first_head + e, qa, zero)], axis=-1)

    def scores(i, j, slot):
        for r0, r1, nk in parts(i, j):
            keys = slice(j * t, j * t + nk)
            kk = jnp.concatenate([k_ref[0, keys, :], ka_ref[0, keys, :]], axis=-1)
            for e in range(2):
                s_ref[slot, e, r0:r1, :nk] = _dot_nt(qm_ref[i, e, r0:r1], kk)

    def update(i, j, slot):
        for r0, r1, nk in parts(i, j):
            v = v_ref[0, j * t:j * t + nk, :]
            one = jnp.ones_like(v)
            for e in range(2):
                s = s_ref[slot, e, r0:r1, :nk]
                if j == i:
                    last = jnp.where(causal, s[:, nk - d:], NEG_BIG)
                    s = last if nk == d else jnp.concatenate([s[:, :nk - d], last], axis=-1)
                row_max = jnp.max(s, axis=-1, keepdims=True)
                m_new = (jnp.broadcast_to(row_max, (r1 - r0, LANES)) if j == 0
                         else jnp.maximum(m_ref[e, r0:r1], row_max))
                p = jnp.concatenate(
                    [jnp.exp2(s[:, c * LANES:(c + 1) * LANES] - m_new)
                     for c in range(nk // LANES)], axis=-1).astype(jnp.bfloat16)
                v_lower = lax.broadcasted_iota(jnp.int32, v.shape, 1) < HEAD_DIM
                ve = jnp.where(v_lower, v, one) if e == 0 else jnp.where(v_lower, one, v)
                pv = _dot(p, ve)
                if j > 0:
                    pv = jnp.exp2(m_ref[e, r0:r1] - m_new) * acc_ref[e, r0:r1] + pv
                acc_ref[e, r0:r1] = pv
                m_ref[e, r0:r1] = m_new

    def finish(i):
        outs = []
        for e in range(2):
            acc = acc_ref[e]
            outs.append(acc / pltpu.roll(acc, HEAD_DIM, 1))
        o_ref[0, tile_rows(i), :] = jnp.where(lower, outs[0], outs[1]).astype(o_ref.dtype)

    n_tiles = seq // t
    for i in range(n_tiles):
        mask_queries(i)
    pairs = [(i, j) for i in range(n_tiles) for j in range(i + 1)]
    scores(*pairs[0], 0)
    for idx, (i, j) in enumerate(pairs):
        if idx + 1 < len(pairs):
            scores(*pairs[idx + 1], (idx + 1) % 2)
        update(i, j, idx % 2)
        if j == i:
            finish(i)


def _fox_call(fq, qa, fk, ka, fv, batch, seq):
    t = FOX_T
    shape3 = (batch, seq, N_PAIRS * LANES)
    aug3 = (batch, seq, LANES)
    spec = pl.BlockSpec((1, seq, LANES), lambda b, p: (b, 0, p))
    aug_spec = pl.BlockSpec((1, seq, LANES), lambda b, p: (b, 0, 0))
    return pl.pallas_call(
        functools.partial(_fox_kernel, seq=seq),
        grid=(batch, N_PAIRS),
        in_specs=[spec, aug_spec, spec, aug_spec, spec],
        out_specs=spec,
        out_shape=jax.ShapeDtypeStruct(shape3, jnp.bfloat16),
        scratch_shapes=[pltpu.VMEM((seq // t, 2, t, 2 * LANES), jnp.bfloat16),
                        pltpu.VMEM((2, 2, t, t), jnp.float32),
                        pltpu.VMEM((2, t, LANES), jnp.float32),
                        pltpu.VMEM((2, t, LANES), jnp.float32)],
        compiler_params=pltpu.CompilerParams(
            dimension_semantics=("parallel", "parallel"),
            vmem_limit_bytes=VMEM_LIMIT),
        name="fox",
    )(fq.reshape(shape3), qa.reshape(aug3), fk.reshape(shape3), ka.reshape(aug3),
      fv.reshape(shape3))


def _t5_bucket(n):
    max_exact = N_BUCKETS // 2
    large = max_exact + (np.log(np.maximum(n, 1) / max_exact)
                         / np.log(MAX_DISTANCE / max_exact)
                         * (N_BUCKETS - max_exact)).astype(np.int32)
    large = np.minimum(large, N_BUCKETS - 1)
    return np.where(n < max_exact, n, large).astype(np.int32)


def _band_buckets():
    i = np.arange(Q_BLOCK)[:, None]
    j = np.arange(2 * Q_BLOCK)[None, :]
    dist = i + Q_BLOCK - j
    in_window = (dist >= 0) & (dist < WINDOW)
    return np.where(in_window, _t5_bucket(np.clip(dist, 0, None)), -1).astype(np.int32)


def _swa_kernel(bucket_ref, rel_ref, sink_ref, q_ref, k_ref, v_ref, o_ref, bias_ref, sinkc_ref,
                s_ref, *, seq):
    qb = Q_BLOCK
    kv = pl.program_id(1)

    @pl.when((pl.program_id(0) == 0) & (kv == 0))
    def _():
        bucket = bucket_ref[...]
        for h in range(N_SWA_HEADS):
            bias = jnp.full(bucket.shape, NEG_BIG, jnp.float32)
            for b in range(N_BUCKETS):
                bias = jnp.where(bucket == b, rel_ref[b, h] * LOG2E, bias)
            group, g = divmod(h, SWA_GROUP)
            bias_ref[group, g * qb:(g + 1) * qb, :] = bias
            sinkc_ref[group, g * qb:(g + 1) * qb, :] = jnp.full((qb, LANES), sink_ref[h] * LOG2E)

    lower = lax.broadcasted_iota(jnp.int32, (qb, LANES), 1) < HEAD_DIM

    def band_of(n):
        if n == 0:
            return slice(0, qb), qb, slice(qb, 2 * qb)
        return slice((n - 1) * qb, (n + 1) * qb), 2 * qb, slice(0, 2 * qb)

    def scores(n, slot):
        rows = slice(n * qb, (n + 1) * qb)
        band, width, cols = band_of(n)
        parts = []
        for pair in range(SWA_GROUP // 2):
            qp = q_ref[0, rows, pair * LANES:(pair + 1) * LANES]
            zero = jnp.zeros_like(qp)
            parts += [jnp.where(lower, qp, zero), jnp.where(lower, zero, qp)]
        s_ref[slot, :, :width] = (_dot_nt(jnp.concatenate(parts, axis=0), k_ref[0, band, :])
                                  + bias_ref[kv, :, cols])

    def update(n, slot):
        rows = slice(n * qb, (n + 1) * qb)
        band, width, _ = band_of(n)
        vb = v_ref[0, band, :]
        vb = jnp.concatenate([vb, jnp.ones_like(vb)], axis=-1)
        s = s_ref[slot, :, :width]
        sink = sinkc_ref[kv]
        m = jnp.maximum(jnp.max(s, axis=-1, keepdims=True), sink)
        p = jnp.concatenate(
            [jnp.exp2(s[:, c * LANES:(c + 1) * LANES] - m) for c in range(width // LANES)],
            axis=-1).astype(jnp.bfloat16)
        acc = _dot(p, vb)
        out = acc[:, :LANES] / (acc[:, LANES:] + jnp.exp2(sink - m))
        for pair in range(SWA_GROUP // 2):
            even = slice(2 * pair * qb, (2 * pair + 1) * qb)
            odd = slice((2 * pair + 1) * qb, (2 * pair + 2) * qb)
            o_ref[0, rows, pair * LANES:(pair + 1) * LANES] = jnp.where(
                lower, out[even], out[odd]).astype(o_ref.dtype)

    n_blocks = seq // qb
    scores(0, 0)
    for n in range(n_blocks):
        if n + 1 < n_blocks:
            scores(n + 1, (n + 1) % 2)
        update(n, n % 2)


def _swa_call(sq, sk, sv, rel_bias, sinks, batch, seq):
    bucket = _band_buckets()
    width = SWA_GROUP * HEAD_DIM
    smem = pl.BlockSpec(memory_space=pltpu.SMEM)
    q_spec = pl.BlockSpec((1, seq, width), lambda b, kv: (b, 0, kv))
    kv_spec = pl.BlockSpec((1, seq, LANES), lambda b, kv: (b, 0, kv))
    return pl.pallas_call(
        functools.partial(_swa_kernel, seq=seq),
        grid=(batch, N_SWA_KV_HEADS),
        in_specs=[pl.BlockSpec(bucket.shape, lambda b, kv: (0, 0)), smem, smem,
                  q_spec, kv_spec, kv_spec],
        out_specs=q_spec,
        out_shape=jax.ShapeDtypeStruct((batch, seq, D_SWA), jnp.bfloat16),
        scratch_shapes=[
            pltpu.VMEM((N_SWA_KV_HEADS, SWA_GROUP * Q_BLOCK, 2 * Q_BLOCK), jnp.float32),
            pltpu.VMEM((N_SWA_KV_HEADS, SWA_GROUP * Q_BLOCK, LANES), jnp.float32),
            pltpu.VMEM((2, SWA_GROUP * Q_BLOCK, 2 * Q_BLOCK), jnp.float32)],
        compiler_params=pltpu.CompilerParams(
            dimension_semantics=("arbitrary", "arbitrary"),
            vmem_limit_bytes=VMEM_LIMIT),
        name="swa",
    )(jnp.asarray(bucket), rel_bias, sinks, sq.reshape(batch, seq, D_SWA),
      sk.reshape(batch, seq, 2 * LANES), sv.reshape(batch, seq, 2 * LANES))


def _post_kernel(fox_ref, swa_ref, x_ref, p_ref, wo_ref, w1_ref, w2_ref, wg_ref, wp_ref,
                 gains_ref, o_ref):
    bf16 = jnp.bfloat16
    mix = _dot(jnp.concatenate([fox_ref[...], swa_ref[...]], axis=-1), wo_ref[...])
    h = x_ref[...] + _rms(mix) * gains_ref[0:1, :]

    m = (_rms(h) * gains_ref[1:2, :]).astype(bf16)
    y = jnp.zeros((POST_TM, D_MODEL), jnp.float32)
    for c in range(D_FF // FF_CHUNK):
        cols = slice(c * FF_CHUNK, (c + 1) * FF_CHUNK)
        hid = jnp.square(jnp.maximum(_dot(m, w1_ref[:, cols]), 0.0))
        y = y + _dot(hid.astype(bf16), w2_ref[cols, :])
    h = h + _rms(y) * gains_ref[2:3, :]

    gate = jax.nn.sigmoid(_dot(h.astype(bf16), wg_ref[...]))
    e = _dot(p_ref[...].astype(bf16), wp_ref[...]) * gate
    o_ref[...] = h + _rms(e) * gains_ref[3:4, :]


def _post_call(fox, swa, x2, p2, wo, w1, w2, wg, wp, gains):
    n = x2.shape[0]
    row = lambda i: (i, 0)
    const = lambda i: (0, 0)

    def resident(shape):
        return pl.BlockSpec(shape, const, pipeline_mode=pl.Buffered(1))

    return pl.pallas_call(
        _post_kernel,
        grid=(n // POST_TM,),
        in_specs=[
            pl.BlockSpec((POST_TM, D_FOX), row),
            pl.BlockSpec((POST_TM, D_SWA), row),
            pl.BlockSpec((POST_TM, D_MODEL), row),
            pl.BlockSpec((POST_TM, D_PLE), row),
            resident(wo.shape), resident(w1.shape), resident(w2.shape),
            resident(wg.shape), resident(wp.shape), resident(gains.shape),
        ],
        out_specs=pl.BlockSpec((POST_TM, D_MODEL), row),
        out_shape=jax.ShapeDtypeStruct((n, D_MODEL), jnp.float32),
        compiler_params=pltpu.CompilerParams(
            dimension_semantics=("parallel",),
            vmem_limit_bytes=VMEM_LIMIT),
        name="post",
    )(fox, swa, x2, p2, wo, w1, w2, wg, wp, gains)


def _gate_lanes(v):
    return jnp.tile(v, (1, LANES // N_FOX_HEADS))


def kernel(x, p, w_in, b_forget, w_out, rel_bias, swa_sinks, g_attn_pre, g_attn_post,
           w_ff1, w_ff2, g_ff_pre, g_ff_post, w_ple, w_ple_gate, g_ple_post):
    batch, seq, _ = x.shape
    n = batch * seq
    bf16 = jnp.bfloat16
    h = x.reshape(n, D_MODEL)
    for i in range(p.shape[0]):
        b_ff = _gate_lanes(b_forget[i].reshape(1, N_FOX_HEADS))
        fq, fk, fv, qa, ka, sq, sk, sv = _proj_call(
            h, g_attn_pre[i].reshape(1, D_MODEL), w_in[i].T, b_ff, batch, seq)
        fox = _fox_call(fq, qa, fk, ka, fv, batch, seq).reshape(n, D_FOX)
        swa = _swa_call(sq, sk, sv, rel_bias, swa_sinks[i], batch, seq).reshape(n, D_SWA)
        gains = jnp.stack([g_attn_post[i], g_ff_pre[i], g_ff_post[i], g_ple_post[i]])
        h = _post_call(fox, swa, h, p[i].reshape(n, D_PLE),
                       w_out[i].astype(bf16), w_ff1[i].astype(bf16), w_ff2[i].astype(bf16),
                       w_ple_gate[i].astype(bf16), w_ple[i].astype(bf16), gains)
    return h.reshape(batch, seq, D_MODEL)
```

```python
import functools

import jax
import jax.numpy as jnp
import numpy as np
from jax import lax
from jax.experimental import pallas as pl
from jax.experimental.pallas import tpu as pltpu

D_MODEL = 1024
HEAD_DIM = 64
N_FOX_HEADS = 8
N_SWA_HEADS = 8
N_SWA_KV_HEADS = 2
SWA_GROUP = N_SWA_HEADS // N_SWA_KV_HEADS
D_FOX = N_FOX_HEADS * HEAD_DIM
D_SWA = N_SWA_HEADS * HEAD_DIM
D_SWA_KV = N_SWA_KV_HEADS * HEAD_DIM
D_FF = 4 * D_MODEL
D_PLE = 256
WINDOW = 128
Q_BLOCK = 128
N_BUCKETS = 32
MAX_DISTANCE = 128
RMS_EPS = 1e-6

LANES = 128
N_PAIRS = N_FOX_HEADS // 2
N_SPLIT = 3
N_TERMS = HEAD_DIM // N_FOX_HEADS
TERM_SRC = (0, 0, 0, 1, 1, 1, 2, 2)
TERM_CUT = (0, 1, 2, 0, 1, 2, 0, 1)
NEG_BIG = -1e30
LOG2E = float(np.log2(np.e))

PROJ_TM = 512
FOX_T = 512
FOX_DIAG = 256
POST_TM = 512
POST_STRIP = 256
FF_CHUNK = 1024
VMEM_LIMIT = 56 * 1024 * 1024

C_FQ, C_FK, C_FV = 0, D_FOX, 2 * D_FOX
C_SQ = 3 * D_FOX
C_SK = C_SQ + D_SWA
C_SV = C_SK + D_SWA_KV
C_FF = C_SV + D_SWA_KV
D_PROJ = C_FF + LANES


def _rms(v):
    return v * lax.rsqrt(jnp.mean(v * v, axis=-1, keepdims=True) + RMS_EPS)


def _dot(a, b):
    return jnp.dot(a, b, preferred_element_type=jnp.float32)


def _dot_nt(a, b):
    return lax.dot_general(a, b, (((1,), (1,)), ((), ())),
                           preferred_element_type=jnp.float32)


def _proj_kernel(x_ref, g_ref, wt_ref, bf_ref, tri_ref,
                 fq_ref, fk_ref, fv_ref, qa_ref, ka_ref, sq_ref, sk_ref, sv_ref,
                 carry_ref, w_ref):
    @pl.when((pl.program_id(0) == 0) & (pl.program_id(1) == 0))
    def _():
        bf16 = jnp.bfloat16
        q_scale = HEAD_DIM ** -0.5 * LOG2E
        src = np.cumsum([0, D_FOX, D_FOX, D_FOX, N_FOX_HEADS, D_SWA])
        src_fq, src_fk, src_fv, src_ff, src_sq, src_skv = (int(c) for c in src)
        w_ref[C_FQ:C_FQ + D_FOX] = (wt_ref[src_fq:src_fq + D_FOX] * q_scale).astype(bf16)
        w_ref[C_FK:C_FK + D_FOX] = wt_ref[src_fk:src_fk + D_FOX].astype(bf16)
        w_ref[C_FV:C_FV + D_FOX] = wt_ref[src_fv:src_fv + D_FOX].astype(bf16)
        w_ref[C_SQ:C_SQ + D_SWA] = (wt_ref[src_sq:src_sq + D_SWA] * q_scale).astype(bf16)
        w_ref[C_SK:C_SK + 2 * D_SWA_KV] = wt_ref[src_skv:src_skv + 2 * D_SWA_KV].astype(bf16)
        gate = wt_ref[src_ff:src_ff + N_FOX_HEADS]
        w_ref[C_FF:C_FF + LANES] = jnp.tile(gate, (LANES // N_FOX_HEADS, 1)).astype(bf16)

    @pl.when(pl.program_id(1) == 0)
    def _():
        carry_ref[...] = jnp.zeros_like(carry_ref)

    a = (_rms(x_ref[...]) * g_ref[...]).astype(jnp.bfloat16)

    def proj(lo, width):
        return _dot_nt(a, w_ref[lo:lo + width])

    fq_ref[...] = proj(C_FQ, D_FOX).astype(jnp.bfloat16)
    fk_ref[...] = proj(C_FK, D_FOX).astype(jnp.bfloat16)
    fv_ref[...] = proj(C_FV, D_FOX).astype(jnp.bfloat16)
    sq_ref[...] = proj(C_SQ, D_SWA).astype(jnp.bfloat16)

    lane = lax.broadcasted_iota(jnp.int32, (PROJ_TM, LANES), 1)
    lower = lane < HEAD_DIM
    kv = proj(C_SK, 2 * D_SWA_KV)
    for half, out_ref in enumerate((sk_ref, sv_ref)):
        z = kv[:, half * LANES:(half + 1) * LANES]
        zr = pltpu.roll(z, HEAD_DIM, 1)
        out_ref[:, :LANES] = jnp.where(lower, z, zr).astype(jnp.bfloat16)
        out_ref[:, LANES:] = jnp.where(lower, zr, z).astype(jnp.bfloat16)

    term = (lane // N_FOX_HEADS) % N_TERMS
    v = proj(C_FF, LANES) + bf_ref[...]
    logf = (jnp.minimum(v, 0.0) - jnp.log1p(jnp.exp(-jnp.abs(v)))) * LOG2E
    x = _pick_piece(_split_bf16(logf), term, TERM_SRC)
    c = carry_ref[...] + _dot(tri_ref[...], x)
    carry_ref[...] = c[PROJ_TM - 1:PROJ_TM, :]
    terms = _pick_piece(_split_bf16(c), term, TERM_CUT)
    one = jnp.ones_like(terms)
    qa_ref[...] = jnp.where(lower, terms, one)
    ka_ref[...] = jnp.where(lower, one, -terms)


def _split_bf16(v):
    pieces = []
    r = v
    for _ in range(N_SPLIT):
        t = r.astype(jnp.bfloat16)
        pieces.append(t)
        r = r - t.astype(jnp.float32)
    return pieces


def _pick_piece(pieces, term, piece_of_term):
    out = pieces[piece_of_term[-1]]
    for t in range(N_TERMS - 2, -1, -1):
        if piece_of_term[t] != piece_of_term[t + 1]:
            out = jnp.where(term <= t, pieces[piece_of_term[t]], out)
    return out


def _proj_call(x2, g_pre, w_in_t, b_ff, batch, seq):
    n = batch * seq
    steps = seq // PROJ_TM
    tri = np.tril(np.ones((PROJ_TM, PROJ_TM), np.float32))
    row = lambda b, s: (b * steps + s, 0)
    const = lambda b, s: (0, 0)
    bf16 = jnp.bfloat16

    def out(width):
        return (jax.ShapeDtypeStruct((n, width), bf16), pl.BlockSpec((PROJ_TM, width), row))

    outs = [out(D_FOX), out(D_FOX), out(D_FOX), out(LANES), out(LANES),
            out(D_SWA), out(2 * LANES), out(2 * LANES)]
    return pl.pallas_call(
        _proj_kernel,
        grid=(batch, steps),
        in_specs=[
            pl.BlockSpec((PROJ_TM, D_MODEL), row),
            pl.BlockSpec((1, D_MODEL), const),
            pl.BlockSpec(w_in_t.shape, const, pipeline_mode=pl.Buffered(1)),
            pl.BlockSpec((1, LANES), const),
            pl.BlockSpec((PROJ_TM, PROJ_TM), const),
        ],
        out_specs=[o[1] for o in outs],
        out_shape=[o[0] for o in outs],
        scratch_shapes=[pltpu.VMEM((1, LANES), jnp.float32),
                        pltpu.VMEM((D_PROJ, D_MODEL), bf16)],
        compiler_params=pltpu.CompilerParams(
            dimension_semantics=("arbitrary", "arbitrary"),
            vmem_limit_bytes=VMEM_LIMIT),
        name="proj",
    )(x2, g_pre, w_in_t, b_ff, jnp.asarray(tri, bf16))


def _fox_kernel(q_ref, qa_ref, k_ref, ka_ref, v_ref, o_ref, qm_ref, s_ref, m_ref, acc_ref, *, seq):
    t = FOX_T
    lane = lax.broadcasted_iota(jnp.int32, (t, LANES), 1)
    lower = lane < HEAD_DIM
    head_of_lane = lane % N_FOX_HEADS
    first_head = 2 * pl.program_id(1)
    d = FOX_DIAG
    causal = (lax.broadcasted_iota(jnp.int32, (d, d), 0)
              >= lax.broadcasted_iota(jnp.int32, (d, d), 1))

    def tile_rows(i):
        return slice(i * t, (i + 1) * t)

    def parts(i, j):
        if j < i:
            return [(0, t, t)]
        return [(r, r + d, r + d) for r in range(0, t, d)]

    def mask_queries(i):
        q = q_ref[0, tile_rows(i), :]
        qa = qa_ref[0, tile_rows(i), :]
        zero = jnp.zeros_like(q)
        for e in range(2):
            qm_ref[i, e] = jnp.concatenate(
                [jnp.where(lower, q, zero) if e == 0 else jnp.where(lower, zero, q),
                 jnp.where(head_of_lane == first_head + e, qa, zero)], axis=-1)

    def scores(i, j, slot):
        for r0, r1, nk in parts(i, j):
            keys = slice(j * t, j * t + nk)
            kk = jnp.concatenate([k_ref[0, keys, :], ka_ref[0, keys, :]], axis=-1)
            for e in range(2):
                s_ref[slot, e, r0:r1, :nk] = _dot_nt(qm_ref[i, e, r0:r1], kk)

    def update(i, j, slot):
        for r0, r1, nk in parts(i, j):
            v = v_ref[0, j * t:j * t + nk, :]
            one = jnp.ones_like(v)
            for e in range(2):
                s = s_ref[slot, e, r0:r1, :nk]
                if j == i:
                    last = jnp.where(causal, s[:, nk - d:], NEG_BIG)
                    s = last if nk == d else jnp.concatenate([s[:, :nk - d], last], axis=-1)
                row_max = jnp.max(s, axis=-1, keepdims=True)
                m_new = (jnp.broadcast_to(row_max, (r1 - r0, LANES)) if j == 0
                         else jnp.maximum(m_ref[e, r0:r1], row_max))
                p = jnp.concatenate(
                    [jnp.exp2(s[:, c * LANES:(c + 1) * LANES] - m_new)
                     for c in range(nk // LANES)], axis=-1).astype(jnp.bfloat16)
                v_lower = lax.broadcasted_iota(jnp.int32, v.shape, 1) < HEAD_DIM
                ve = jnp.where(v_lower, v, one) if e == 0 else jnp.where(v_lower, one, v)
                pv = _dot(p, ve)
                if j > 0:
                    pv = jnp.exp2(m_ref[e, r0:r1] - m_new) * acc_ref[e, r0:r1] + pv
                acc_ref[e, r0:r1] = pv
                m_ref[e, r0:r1] = m_new

    def finish(i):
        outs = []
        for e in range(2):
            acc = acc_ref[e]
            outs.append(acc / pltpu.roll(acc, HEAD_DIM, 1))
        o_ref[0, tile_rows(i), :] = jnp.where(lower, outs[0], outs[1]).astype(o_ref.dtype)

    n_tiles = seq // t
    for i in range(n_tiles):
        mask_queries(i)
    pairs = [(i, j) for i in range(n_tiles) for j in range(i + 1)]
    scores(*pairs[0], 0)
    for idx, (i, j) in enumerate(pairs):
        if idx + 1 < len(pairs):
            scores(*pairs[idx + 1], (idx + 1) % 2)
        update(i, j, idx % 2)
        if j == i:
            finish(i)


def _fox_call(fq, qa, fk, ka, fv, batch, seq):
    t = FOX_T
    shape3 = (batch, seq, N_PAIRS * LANES)
    aug3 = (batch, seq, LANES)
    spec = pl.BlockSpec((1, seq, LANES), lambda b, p: (b, 0, p))
    aug_spec = pl.BlockSpec((1, seq, LANES), lambda b, p: (b, 0, 0))
    return pl.pallas_call(
        functools.partial(_fox_kernel, seq=seq),
        grid=(batch, N_PAIRS),
        in_specs=[spec, aug_spec, spec, aug_spec, spec],
        out_specs=spec,
        out_shape=jax.ShapeDtypeStruct(shape3, jnp.bfloat16),
        scratch_shapes=[pltpu.VMEM((seq // t, 2, t, 2 * LANES), jnp.bfloat16),
                        pltpu.VMEM((2, 2, t, t), jnp.float32),
                        pltpu.VMEM((2, t, LANES), jnp.float32),
                        pltpu.VMEM((2, t, LANES), jnp.float32)],
        compiler_params=pltpu.CompilerParams(
            dimension_semantics=("parallel", "parallel"),
            vmem_limit_bytes=VMEM_LIMIT),
        name="fox",
    )(fq.reshape(shape3), qa.reshape(aug3), fk.reshape(shape3), ka.reshape(aug3),
      fv.reshape(shape3))


def _t5_bucket(n):
    max_exact = N_BUCKETS // 2
    large = max_exact + (np.log(np.maximum(n, 1) / max_exact)
                         / np.log(MAX_DISTANCE / max_exact)
                         * (N_BUCKETS - max_exact)).astype(np.int32)
    large = np.minimum(large, N_BUCKETS - 1)
    return np.where(n < max_exact, n, large).astype(np.int32)


def _band_buckets():
    i = np.arange(Q_BLOCK)[:, None]
    j = np.arange(2 * Q_BLOCK)[None, :]
    dist = i + Q_BLOCK - j
    in_window = (dist >= 0) & (dist < WINDOW)
    return np.where(in_window, _t5_bucket(np.clip(dist, 0, None)), -1).astype(np.int32)


def _swa_kernel(bucket_ref, rel_ref, sink_ref, q_ref, k_ref, v_ref, o_ref, bias_ref, sinkc_ref,
                s_ref, *, seq):
    qb = Q_BLOCK
    kv = pl.program_id(1)

    @pl.when((pl.program_id(0) == 0) & (kv == 0))
    def _():
        bucket = bucket_ref[...]
        for h in range(N_SWA_HEADS):
            bias = jnp.full(bucket.shape, NEG_BIG, jnp.float32)
            for b in range(N_BUCKETS):
                bias = jnp.where(bucket == b, rel_ref[b, h] * LOG2E, bias)
            group, g = divmod(h, SWA_GROUP)
            bias_ref[group, g * qb:(g + 1) * qb, :] = bias
            sinkc_ref[group, g * qb:(g + 1) * qb, :] = jnp.full((qb, LANES), sink_ref[h] * LOG2E)

    lower = lax.broadcasted_iota(jnp.int32, (qb, LANES), 1) < HEAD_DIM

    def band_of(n):
        if n == 0:
            return slice(0, qb), qb, slice(qb, 2 * qb)
        return slice((n - 1) * qb, (n + 1) * qb), 2 * qb, slice(0, 2 * qb)

    def scores(n, slot):
        rows = slice(n * qb, (n + 1) * qb)
        band, width, cols = band_of(n)
        parts = []
        for pair in range(SWA_GROUP // 2):
            qp = q_ref[0, rows, pair * LANES:(pair + 1) * LANES]
            zero = jnp.zeros_like(qp)
            parts += [jnp.where(lower, qp, zero), jnp.where(lower, zero, qp)]
        s_ref[slot, :, :width] = (_dot_nt(jnp.concatenate(parts, axis=0), k_ref[0, band, :])
                                  + bias_ref[kv, :, cols])

    def update(n, slot):
        rows = slice(n * qb, (n + 1) * qb)
        band, width, _ = band_of(n)
        vb = v_ref[0, band, :]
        vb = jnp.concatenate([vb, jnp.ones_like(vb)], axis=-1)
        s = s_ref[slot, :, :width]
        sink = sinkc_ref[kv]
        m = jnp.maximum(jnp.max(s, axis=-1, keepdims=True), sink)
        p = jnp.concatenate(
            [jnp.exp2(s[:, c * LANES:(c + 1) * LANES] - m) for c in range(width // LANES)],
            axis=-1).astype(jnp.bfloat16)
        acc = _dot(p, vb)
        out = acc[:, :LANES] / (acc[:, LANES:] + jnp.exp2(sink - m))
        for pair in range(SWA_GROUP // 2):
            even = slice(2 * pair * qb, (2 * pair + 1) * qb)
            odd = slice((2 * pair + 1) * qb, (2 * pair + 2) * qb)
            o_ref[0, rows, pair * LANES:(pair + 1) * LANES] = jnp.where(
                lower, out[even], out[odd]).astype(o_ref.dtype)

    n_blocks = seq // qb
    scores(0, 0)
    for n in range(n_blocks):
        if n + 1 < n_blocks:
            scores(n + 1, (n + 1) % 2)
        update(n, n % 2)


def _swa_call(sq, sk, sv, rel_bias, sinks, batch, seq):
    bucket = _band_buckets()
    width = SWA_GROUP * HEAD_DIM
    smem = pl.BlockSpec(memory_space=pltpu.SMEM)
    q_spec = pl.BlockSpec((1, seq, width), lambda b, kv: (b, 0, kv))
    kv_spec = pl.BlockSpec((1, seq, LANES), lambda b, kv: (b, 0, kv))
    return pl.pallas_call(
        functools.partial(_swa_kernel, seq=seq),
        grid=(batch, N_SWA_KV_HEADS),
        in_specs=[pl.BlockSpec(bucket.shape, lambda b, kv: (0, 0)), smem, smem,
                  q_spec, kv_spec, kv_spec],
        out_specs=q_spec,
        out_shape=jax.ShapeDtypeStruct((batch, seq, D_SWA), jnp.bfloat16),
        scratch_shapes=[
            pltpu.VMEM((N_SWA_KV_HEADS, SWA_GROUP * Q_BLOCK, 2 * Q_BLOCK), jnp.float32),
            pltpu.VMEM((N_SWA_KV_HEADS, SWA_GROUP * Q_BLOCK, LANES), jnp.float32),
            pltpu.VMEM((2, SWA_GROUP * Q_BLOCK, 2 * Q_BLOCK), jnp.float32)],
        compiler_params=pltpu.CompilerParams(
            dimension_semantics=("arbitrary", "arbitrary"),
            vmem_limit_bytes=VMEM_LIMIT),
        name="swa",
    )(jnp.asarray(bucket), rel_bias, sinks, sq.reshape(batch, seq, D_SWA),
      sk.reshape(batch, seq, 2 * LANES), sv.reshape(batch, seq, 2 * LANES))


def _post_kernel(fox_ref, swa_ref, x_ref, p_ref, wo_ref, w1_ref, w2_ref, wg_ref, wp_ref,
                 gains_ref, o_ref):
    bf16 = jnp.bfloat16
    h_parts, m_parts = [], []
    for r in range(0, POST_TM, POST_STRIP):
        rows = slice(r, r + POST_STRIP)
        mix = _dot(jnp.concatenate([fox_ref[rows, :], swa_ref[rows, :]], axis=-1), wo_ref[...])
        hh = x_ref[rows, :] + _rms(mix) * gains_ref[0:1, :]
        h_parts.append(hh)
        m_parts.append((_rms(hh) * gains_ref[1:2, :]).astype(bf16))
    h = jnp.concatenate(h_parts, axis=0)
    m = jnp.concatenate(m_parts, axis=0)
    ple = _dot(p_ref[...].astype(bf16), wp_ref[...])

    y = jnp.zeros((POST_TM, D_MODEL), jnp.float32)
    for c in range(D_FF // FF_CHUNK):
        cols = slice(c * FF_CHUNK, (c + 1) * FF_CHUNK)
        hid = jnp.square(jnp.maximum(_dot(m, w1_ref[:, cols]), 0.0))
        y = y + _dot(hid.astype(bf16), w2_ref[cols, :])

    for r in range(0, POST_TM, POST_STRIP):
        rows = slice(r, r + POST_STRIP)
        hh = h[rows] + _rms(y[rows]) * gains_ref[2:3, :]
        gate = jax.nn.sigmoid(_dot(hh.astype(bf16), wg_ref[...]))
        o_ref[rows, :] = hh + _rms(ple[rows] * gate) * gains_ref[3:4, :]


def _post_call(fox, swa, x2, p2, wo, w1, w2, wg, wp, gains):
    n = x2.shape[0]
    row = lambda i: (i, 0)
    const = lambda i: (0, 0)

    def resident(shape):
        return pl.BlockSpec(shape, const, pipeline_mode=pl.Buffered(1))

    return pl.pallas_call(
        _post_kernel,
        grid=(n // POST_TM,),
        in_specs=[
            pl.BlockSpec((POST_TM, D_FOX), row),
            pl.BlockSpec((POST_TM, D_SWA), row),
            pl.BlockSpec((POST_TM, D_MODEL), row),
            pl.BlockSpec((POST_TM, D_PLE), row),
            resident(wo.shape), resident(w1.shape), resident(w2.shape),
            resident(wg.shape), resident(wp.shape), resident(gains.shape),
        ],
        out_specs=pl.BlockSpec((POST_TM, D_MODEL), row),
        out_shape=jax.ShapeDtypeStruct((n, D_MODEL), jnp.float32),
        compiler_params=pltpu.CompilerParams(
            dimension_semantics=("parallel",),
            vmem_limit_bytes=VMEM_LIMIT),
        name="post",
    )(fox, swa, x2, p2, wo, w1, w2, wg, wp, gains)


def _gate_lanes(v):
    return jnp.tile(v, (1, LANES // N_FOX_HEADS))


def kernel(x, p, w_in, b_forget, w_out, rel_bias, swa_sinks, g_attn_pre, g_attn_post,
           w_ff1, w_ff2, g_ff_pre, g_ff_post, w_ple, w_ple_gate, g_ple_post):
    batch, seq, _ = x.shape
    n = batch * seq
    bf16 = jnp.bfloat16
    h = x.reshape(n, D_MODEL)
    for i in range(p.shape[0]):
        b_ff = _gate_lanes(b_forget[i].reshape(1, N_FOX_HEADS))
        fq, fk, fv, qa, ka, sq, sk, sv = _proj_call(
            h, g_attn_pre[i].reshape(1, D_MODEL), w_in[i].T, b_ff, batch, seq)
        fox = _fox_call(fq, qa, fk, ka, fv, batch, seq).reshape(n, D_FOX)
        swa = _swa_call(sq, sk, sv, rel_bias, swa_sinks[i], batch, seq).reshape(n, D_SWA)
        gains = jnp.stack([g_attn_post[i], g_ff_pre[i], g_ff_post[i], g_ple_post[i]])
        h = _post_call(fox, swa, h, p[i].reshape(n, D_PLE),
                       w_out[i].astype(bf16), w_ff1[i].astype(bf16), w_ff2[i].astype(bf16),
                       w_ple_gate[i].astype(bf16), w_ple[i].astype(bf16), gains)
    return h.reshape(batch, seq, D_MODEL)
```

```python
import functools

import jax
import jax.numpy as jnp
import numpy as np
from jax import lax
from jax.experimental import pallas as pl
from jax.experimental.pallas import tpu as pltpu

D_MODEL = 1024
HEAD_DIM = 64
N_FOX_HEADS = 8
N_SWA_HEADS = 8
N_SWA_KV_HEADS = 2
SWA_GROUP = N_SWA_HEADS // N_SWA_KV_HEADS
D_FOX = N_FOX_HEADS * HEAD_DIM
D_SWA = N_SWA_HEADS * HEAD_DIM
D_SWA_KV = N_SWA_KV_HEADS * HEAD_DIM
D_FF = 4 * D_MODEL
D_PLE = 256
WINDOW = 128
Q_BLOCK = 128
N_BUCKETS = 32
MAX_DISTANCE = 128
RMS_EPS = 1e-6

LANES = 128
N_PAIRS = N_FOX_HEADS // 2
N_SPLIT = 3
N_TERMS = HEAD_DIM // N_FOX_HEADS
TERM_SRC = (0, 0, 0, 1, 1, 1, 2, 2)
TERM_CUT = (0, 1, 2, 0, 1, 2, 0, 1)
NEG_BIG = -1e30
LOG2E = float(np.log2(np.e))

PROJ_TM = 512
FOX_T = 512
FOX_DIAG = 256
POST_TM = 512
POST_STRIP = 256
FF_CHUNK = 1024
VMEM_LIMIT = 56 * 1024 * 1024

C_FQ, C_FK, C_FV = 0, D_FOX, 2 * D_FOX
C_SQ = 3 * D_FOX
C_SK = C_SQ + D_SWA
C_SV = C_SK + D_SWA_KV
C_FF = C_SV + D_SWA_KV
D_PROJ = C_FF + LANES


def _rms(v):
    return v * lax.rsqrt(jnp.mean(v * v, axis=-1, keepdims=True) + RMS_EPS)


def _dot(a, b):
    return jnp.dot(a, b, preferred_element_type=jnp.float32)


def _dot_nt(a, b):
    return lax.dot_general(a, b, (((1,), (1,)), ((), ())),
                           preferred_element_type=jnp.float32)


def _proj_kernel(x_ref, g_ref, wt_ref, bf_ref, tri_ref,
                 fq_ref, fk_ref, fv_ref, qa_ref, ka_ref, sq_ref, sk_ref, sv_ref,
                 carry_ref, w_ref):
    @pl.when((pl.program_id(0) == 0) & (pl.program_id(1) == 0))
    def _():
        bf16 = jnp.bfloat16
        q_scale = HEAD_DIM ** -0.5 * LOG2E
        src = np.cumsum([0, D_FOX, D_FOX, D_FOX, N_FOX_HEADS, D_SWA])
        src_fq, src_fk, src_fv, src_ff, src_sq, src_skv = (int(c) for c in src)
        w_ref[C_FQ:C_FQ + D_FOX] = (wt_ref[src_fq:src_fq + D_FOX] * q_scale).astype(bf16)
        w_ref[C_FK:C_FK + D_FOX] = wt_ref[src_fk:src_fk + D_FOX].astype(bf16)
        w_ref[C_FV:C_FV + D_FOX] = wt_ref[src_fv:src_fv + D_FOX].astype(bf16)
        w_ref[C_SQ:C_SQ + D_SWA] = (wt_ref[src_sq:src_sq + D_SWA] * q_scale).astype(bf16)
        w_ref[C_SK:C_SK + 2 * D_SWA_KV] = wt_ref[src_skv:src_skv + 2 * D_SWA_KV].astype(bf16)
        gate = wt_ref[src_ff:src_ff + N_FOX_HEADS]
        w_ref[C_FF:C_FF + LANES] = jnp.tile(gate, (LANES // N_FOX_HEADS, 1)).astype(bf16)

    @pl.when(pl.program_id(1) == 0)
    def _():
        carry_ref[...] = jnp.zeros_like(carry_ref)

    a = (_rms(x_ref[...]) * g_ref[...]).astype(jnp.bfloat16)

    def proj(lo, width):
        return _dot_nt(a, w_ref[lo:lo + width])

    fq_ref[...] = proj(C_FQ, D_FOX).astype(jnp.bfloat16)
    fk_ref[...] = proj(C_FK, D_FOX).astype(jnp.bfloat16)
    fv_ref[...] = proj(C_FV, D_FOX).astype(jnp.bfloat16)
    sq_ref[...] = proj(C_SQ, D_SWA).astype(jnp.bfloat16)

    lane = lax.broadcasted_iota(jnp.int32, (PROJ_TM, LANES), 1)
    lower = lane < HEAD_DIM
    kv = proj(C_SK, 2 * D_SWA_KV)
    for half, out_ref in enumerate((sk_ref, sv_ref)):
        z = kv[:, half * LANES:(half + 1) * LANES]
        zr = pltpu.roll(z, HEAD_DIM, 1)
        out_ref[:, :LANES] = jnp.where(lower, z, zr).astype(jnp.bfloat16)
        out_ref[:, LANES:] = jnp.where(lower, zr, z).astype(jnp.bfloat16)

    term = (lane // N_FOX_HEADS) % N_TERMS
    v = proj(C_FF, LANES) + bf_ref[...]
    logf = (jnp.minimum(v, 0.0) - jnp.log1p(jnp.exp(-jnp.abs(v)))) * LOG2E
    x = _pick_piece(_split_bf16(logf), term, TERM_SRC)
    c = carry_ref[...] + _dot(tri_ref[...], x)
    carry_ref[...] = c[PROJ_TM - 1:PROJ_TM, :]
    terms = _pick_piece(_split_bf16(c), term, TERM_CUT)
    one = jnp.ones_like(terms)
    qa_ref[...] = jnp.where(lower, terms, one)
    ka_ref[...] = jnp.where(lower, one, -terms)


def _split_bf16(v):
    pieces = []
    r = v
    for _ in range(N_SPLIT):
        t = r.astype(jnp.bfloat16)
        pieces.append(t)
        r = r - t.astype(jnp.float32)
    return pieces


def _pick_piece(pieces, term, piece_of_term):
    out = pieces[piece_of_term[-1]]
    for t in range(N_TERMS - 2, -1, -1):
        if piece_of_term[t] != piece_of_term[t + 1]:
            out = jnp.where(term <= t, pieces[piece_of_term[t]], out)
    return out


def _proj_call(x2, g_pre, w_in_t, b_ff, batch, seq):
    n = batch * seq
    steps = seq // PROJ_TM
    tri = np.tril(np.ones((PROJ_TM, PROJ_TM), np.float32))
    row = lambda b, s: (b * steps + s, 0)
    const = lambda b, s: (0, 0)
    bf16 = jnp.bfloat16

    def out(width):
        return (jax.ShapeDtypeStruct((n, width), bf16), pl.BlockSpec((PROJ_TM, width), row))

    outs = [out(D_FOX), out(D_FOX), out(D_FOX), out(LANES), out(LANES),
            out(D_SWA), out(2 * LANES), out(2 * LANES)]
    return pl.pallas_call(
        _proj_kernel,
        grid=(batch, steps),
        in_specs=[
            pl.BlockSpec((PROJ_TM, D_MODEL), row),
            pl.BlockSpec((1, D_MODEL), const),
            pl.BlockSpec(w_in_t.shape, const, pipeline_mode=pl.Buffered(1)),
            pl.BlockSpec((1, LANES), const),
            pl.BlockSpec((PROJ_TM, PROJ_TM), const),
        ],
        out_specs=[o[1] for o in outs],
        out_shape=[o[0] for o in outs],
        scratch_shapes=[pltpu.VMEM((1, LANES), jnp.float32),
                        pltpu.VMEM((D_PROJ, D_MODEL), bf16)],
        compiler_params=pltpu.CompilerParams(
            dimension_semantics=("arbitrary", "arbitrary"),
            vmem_limit_bytes=VMEM_LIMIT),
        name="proj",
    )(x2, g_pre, w_in_t, b_ff, jnp.asarray(tri, bf16))


def _fox_kernel(*refs, seq, n_cast):
    q_ref, qa_ref, k_ref, ka_ref, v_ref = refs[:5]
    w_refs = refs[5:5 + n_cast]
    o_ref = refs[5 + n_cast]
    wb_refs = refs[6 + n_cast:6 + 2 * n_cast]
    qm_ref, s_ref, m_ref, acc_ref = refs[6 + 2 * n_cast:]
    for w_ref, wb_ref in zip(w_refs, wb_refs):
        wb_ref[...] = w_ref[...].astype(wb_ref.dtype)

    t = FOX_T
    lane = lax.broadcasted_iota(jnp.int32, (t, LANES), 1)
    lower = lane < HEAD_DIM
    head_of_lane = lane % N_FOX_HEADS
    first_head = 2 * pl.program_id(1)
    d = FOX_DIAG
    causal = (lax.broadcasted_iota(jnp.int32, (d, d), 0)
              >= lax.broadcasted_iota(jnp.int32, (d, d), 1))

    def tile_rows(i):
        return slice(i * t, (i + 1) * t)

    def parts(i, j):
        if j < i:
            return [(0, t, t)]
        return [(r, r + d, r + d) for r in range(0, t, d)]

    def mask_queries(i):
        q = q_ref[0, tile_rows(i), :]
        qa = qa_ref[0, tile_rows(i), :]
        zero = jnp.zeros_like(q)
        for e in range(2):
            qm_ref[i, e] = jnp.concatenate(
                [jnp.where(lower, q, zero) if e == 0 else jnp.where(lower, zero, q),
                 jnp.where(head_of_lane == first_head + e, qa, zero)], axis=-1)

    def scores(i, j, slot):
        for r0, r1, nk in parts(i, j):
            keys = slice(j * t, j * t + nk)
            kk = jnp.concatenate([k_ref[0, keys, :], ka_ref[0, keys, :]], axis=-1)
            for e in range(2):
                s_ref[slot, e, r0:r1, :nk] = _dot_nt(qm_ref[i, e, r0:r1], kk)

    def update(i, j, slot):
        for r0, r1, nk in parts(i, j):
            v = v_ref[0, j * t:j * t + nk, :]
            one = jnp.ones_like(v)
            for e in range(2):
                s = s_ref[slot, e, r0:r1, :nk]
                if j == i:
                    last = jnp.where(causal, s[:, nk - d:], NEG_BIG)
                    s = last if nk == d else jnp.concatenate([s[:, :nk - d], last], axis=-1)
                row_max = jnp.max(s, axis=-1, keepdims=True)
                m_new = (jnp.broadcast_to(row_max, (r1 - r0, LANES)) if j == 0
                         else jnp.maximum(m_ref[e, r0:r1], row_max))
                p = jnp.concatenate(
                    [jnp.exp2(s[:, c * LANES:(c + 1) * LANES] - m_new)
                     for c in range(nk // LANES)], axis=-1).astype(jnp.bfloat16)
                v_lower = lax.broadcasted_iota(jnp.int32, v.shape, 1) < HEAD_DIM
                ve = jnp.where(v_lower, v, one) if e == 0 else jnp.where(v_lower, one, v)
                pv = _dot(p, ve)
                if j > 0:
                    pv = jnp.exp2(m_ref[e, r0:r1] - m_new) * acc_ref[e, r0:r1] + pv
                acc_ref[e, r0:r1] = pv
                m_ref[e, r0:r1] = m_new

    def finish(i):
        outs = []
        for e in range(2):
            acc = acc_ref[e]
            outs.append(acc / pltpu.roll(acc, HEAD_DIM, 1))
        o_ref[0, tile_rows(i), :] = jnp.where(lower, outs[0], outs[1]).astype(o_ref.dtype)

    n_tiles = seq // t
    for i in range(n_tiles):
        mask_queries(i)
    pairs = [(i, j) for i in range(n_tiles) for j in range(i + 1)]
    scores(*pairs[0], 0)
    for idx, (i, j) in enumerate(pairs):
        if idx + 1 < len(pairs):
            scores(*pairs[idx + 1], (idx + 1) % 2)
        update(i, j, idx % 2)
        if j == i:
            finish(i)


def _fox_call(fq, qa, fk, ka, fv, weights, batch, seq):
    t = FOX_T
    bf16 = jnp.bfloat16
    shape3 = (batch, seq, N_PAIRS * LANES)
    aug3 = (batch, seq, LANES)
    spec = pl.BlockSpec((1, seq, LANES), lambda b, p: (b, 0, p))
    aug_spec = pl.BlockSpec((1, seq, LANES), lambda b, p: (b, 0, 0))
    steps = batch * N_PAIRS
    w_specs = [pl.BlockSpec((w.shape[0] // steps, w.shape[1]), lambda b, p: (b * N_PAIRS + p, 0))
               for w in weights]
    return pl.pallas_call(
        functools.partial(_fox_kernel, seq=seq, n_cast=len(weights)),
        grid=(batch, N_PAIRS),
        in_specs=[spec, aug_spec, spec, aug_spec, spec] + w_specs,
        out_specs=[spec] + w_specs,
        out_shape=[jax.ShapeDtypeStruct(shape3, bf16)]
        + [jax.ShapeDtypeStruct(w.shape, bf16) for w in weights],
        scratch_shapes=[pltpu.VMEM((seq // t, 2, t, 2 * LANES), jnp.bfloat16),
                        pltpu.VMEM((2, 2, t, t), jnp.float32),
                        pltpu.VMEM((2, t, LANES), jnp.float32),
                        pltpu.VMEM((2, t, LANES), jnp.float32)],
        compiler_params=pltpu.CompilerParams(
            dimension_semantics=("parallel", "parallel"),
            vmem_limit_bytes=VMEM_LIMIT),
        name="fox",
    )(fq.reshape(shape3), qa.reshape(aug3), fk.reshape(shape3), ka.reshape(aug3),
      fv.reshape(shape3), *weights)


def _t5_bucket(n):
    max_exact = N_BUCKETS // 2
    large = max_exact + (np.log(np.maximum(n, 1) / max_exact)
                         / np.log(MAX_DISTANCE / max_exact)
                         * (N_BUCKETS - max_exact)).astype(np.int32)
    large = np.minimum(large, N_BUCKETS - 1)
    return np.where(n < max_exact, n, large).astype(np.int32)


def _band_buckets():
    i = np.arange(Q_BLOCK)[:, None]
    j = np.arange(2 * Q_BLOCK)[None, :]
    dist = i + Q_BLOCK - j
    in_window = (dist >= 0) & (dist < WINDOW)
    return np.where(in_window, _t5_bucket(np.clip(dist, 0, None)), -1).astype(np.int32)


def _swa_kernel(bucket_ref, rel_ref, sink_ref, q_ref, k_ref, v_ref, o_ref, bias_ref, sinkc_ref,
                s_ref, *, seq):
    qb = Q_BLOCK
    kv = pl.program_id(1)

    @pl.when((pl.program_id(0) == 0) & (kv == 0))
    def _():
        bucket = bucket_ref[...]
        for h in range(N_SWA_HEADS):
            bias = jnp.full(bucket.shape, NEG_BIG, jnp.float32)
            for b in range(N_BUCKETS):
                bias = jnp.where(bucket == b, rel_ref[b, h] * LOG2E, bias)
            group, g = divmod(h, SWA_GROUP)
            bias_ref[group, g * qb:(g + 1) * qb, :] = bias
            sinkc_ref[group, g * qb:(g + 1) * qb, :] = jnp.full((qb, LANES), sink_ref[h] * LOG2E)

    lower = lax.broadcasted_iota(jnp.int32, (qb, LANES), 1) < HEAD_DIM

    def band_of(n):
        if n == 0:
            return slice(0, qb), qb, slice(qb, 2 * qb)
        return slice((n - 1) * qb, (n + 1) * qb), 2 * qb, slice(0, 2 * qb)

    def scores(n, slot):
        rows = slice(n * qb, (n + 1) * qb)
        band, width, cols = band_of(n)
        parts = []
        for pair in range(SWA_GROUP // 2):
            qp = q_ref[0, rows, pair * LANES:(pair + 1) * LANES]
            zero = jnp.zeros_like(qp)
            parts += [jnp.where(lower, qp, zero), jnp.where(lower, zero, qp)]
        s_ref[slot, :, :width] = (_dot_nt(jnp.concatenate(parts, axis=0), k_ref[0, band, :])
                                  + bias_ref[kv, :, cols])

    def update(n, slot):
        rows = slice(n * qb, (n + 1) * qb)
        band, width, _ = band_of(n)
        vb = v_ref[0, band, :]
        vb = jnp.concatenate([vb, jnp.ones_like(vb)], axis=-1)
        s = s_ref[slot, :, :width]
        sink = sinkc_ref[kv]
        m = jnp.maximum(jnp.max(s, axis=-1, keepdims=True), sink)
        p = jnp.concatenate(
            [jnp.exp2(s[:, c * LANES:(c + 1) * LANES] - m) for c in range(width // LANES)],
            axis=-1).astype(jnp.bfloat16)
        acc = _dot(p, vb)
        out = acc[:, :LANES] / (acc[:, LANES:] + jnp.exp2(sink - m))
        for pair in range(SWA_GROUP // 2):
            even = slice(2 * pair * qb, (2 * pair + 1) * qb)
            odd = slice((2 * pair + 1) * qb, (2 * pair + 2) * qb)
            o_ref[0, rows, pair * LANES:(pair + 1) * LANES] = jnp.where(
                lower, out[even], out[odd]).astype(o_ref.dtype)

    n_blocks = seq // qb
    scores(0, 0)
    for n in range(n_blocks):
        if n + 1 < n_blocks:
            scores(n + 1, (n + 1) % 2)
        update(n, n % 2)


def _swa_call(sq, sk, sv, rel_bias, sinks, batch, seq):
    bucket = _band_buckets()
    width = SWA_GROUP * HEAD_DIM
    smem = pl.BlockSpec(memory_space=pltpu.SMEM)
    q_spec = pl.BlockSpec((1, seq, width), lambda b, kv: (b, 0, kv))
    kv_spec = pl.BlockSpec((1, seq, LANES), lambda b, kv: (b, 0, kv))
    return pl.pallas_call(
        functools.partial(_swa_kernel, seq=seq),
        grid=(batch, N_SWA_KV_HEADS),
        in_specs=[pl.BlockSpec(bucket.shape, lambda b, kv: (0, 0)), smem, smem,
                  q_spec, kv_spec, kv_spec],
        out_specs=q_spec,
        out_shape=jax.ShapeDtypeStruct((batch, seq, D_SWA), jnp.bfloat16),
        scratch_shapes=[
            pltpu.VMEM((N_SWA_KV_HEADS, SWA_GROUP * Q_BLOCK, 2 * Q_BLOCK), jnp.float32),
            pltpu.VMEM((N_SWA_KV_HEADS, SWA_GROUP * Q_BLOCK, LANES), jnp.float32),
            pltpu.VMEM((2, SWA_GROUP * Q_BLOCK, 2 * Q_BLOCK), jnp.float32)],
        compiler_params=pltpu.CompilerParams(
            dimension_semantics=("arbitrary", "arbitrary"),
            vmem_limit_bytes=VMEM_LIMIT),
        name="swa",
    )(jnp.asarray(bucket), rel_bias, sinks, sq.reshape(batch, seq, D_SWA),
      sk.reshape(batch, seq, 2 * LANES), sv.reshape(batch, seq, 2 * LANES))


def _post_kernel(fox_ref, swa_ref, x_ref, p_ref, wo_ref, w1_ref, w2_ref, wg_ref, wp_ref,
                 gains_ref, o_ref):
    bf16 = jnp.bfloat16
    h_parts, m_parts = [], []
    for r in range(0, POST_TM, POST_STRIP):
        rows = slice(r, r + POST_STRIP)
        mix = _dot(jnp.concatenate([fox_ref[rows, :], swa_ref[rows, :]], axis=-1), wo_ref[...])
        hh = x_ref[rows, :] + _rms(mix) * gains_ref[0:1, :]
        h_parts.append(hh)
        m_parts.append((_rms(hh) * gains_ref[1:2, :]).astype(bf16))
    h = jnp.concatenate(h_parts, axis=0)
    m = jnp.concatenate(m_parts, axis=0)
    ple = _dot(p_ref[...].astype(bf16), wp_ref[...])

    y = jnp.zeros((POST_TM, D_MODEL), jnp.float32)
    for c in range(D_FF // FF_CHUNK):
        cols = slice(c * FF_CHUNK, (c + 1) * FF_CHUNK)
        hid = jnp.square(jnp.maximum(_dot(m, w1_ref[:, cols]), 0.0))
        y = y + _dot(hid.astype(bf16), w2_ref[cols, :])

    for r in range(0, POST_TM, POST_STRIP):
        rows = slice(r, r + POST_STRIP)
        hh = h[rows] + _rms(y[rows]) * gains_ref[2:3, :]
        gate = jax.nn.sigmoid(_dot(hh.astype(bf16), wg_ref[...]))
        o_ref[rows, :] = hh + _rms(ple[rows] * gate) * gains_ref[3:4, :]


def _post_call(fox, swa, x2, p2, wo, w1, w2, wg, wp, gains):
    n = x2.shape[0]
    row = lambda i: (i, 0)
    const = lambda i: (0, 0)

    def resident(shape):
        return pl.BlockSpec(shape, const, pipeline_mode=pl.Buffered(1))

    return pl.pallas_call(
        _post_kernel,
        grid=(n // POST_TM,),
        in_specs=[
            pl.BlockSpec((POST_TM, D_FOX), row),
            pl.BlockSpec((POST_TM, D_SWA), row),
            pl.BlockSpec((POST_TM, D_MODEL), row),
            pl.BlockSpec((POST_TM, D_PLE), row),
            resident(wo.shape), resident(w1.shape), resident(w2.shape),
            resident(wg.shape), resident(wp.shape), resident(gains.shape),
        ],
        out_specs=pl.BlockSpec((POST_TM, D_MODEL), row),
        out_shape=jax.ShapeDtypeStruct((n, D_MODEL), jnp.float32),
        compiler_params=pltpu.CompilerParams(
            dimension_semantics=("parallel",),
            vmem_limit_bytes=VMEM_LIMIT),
        name="post",
    )(fox, swa, x2, p2, wo, w1, w2, wg, wp, gains)


def _gate_lanes(v):
    return jnp.tile(v, (1, LANES // N_FOX_HEADS))


def kernel(x, p, w_in, b_forget, w_out, rel_bias, swa_sinks, g_attn_pre, g_attn_post,
           w_ff1, w_ff2, g_ff_pre, g_ff_post, w_ple, w_ple_gate, g_ple_post):
    batch, seq, _ = x.shape
    n = batch * seq
    bf16 = jnp.bfloat16
    h = x.reshape(n, D_MODEL)
    for i in range(p.shape[0]):
        b_ff = _gate_lanes(b_forget[i].reshape(1, N_FOX_HEADS))
        fq, fk, fv, qa, ka, sq, sk, sv = _proj_call(
            h, g_attn_pre[i].reshape(1, D_MODEL), w_in[i].T, b_ff, batch, seq)
        fox, wo, w1, w2, wg = _fox_call(
            fq, qa, fk, ka, fv, [w_out[i], w_ff1[i], w_ff2[i], w_ple_gate[i]], batch, seq)
        swa = _swa_call(sq, sk, sv, rel_bias, swa_sinks[i], batch, seq).reshape(n, D_SWA)
        gains = jnp.stack([g_attn_post[i], g_ff_pre[i], g_ff_post[i], g_ple_post[i]])
        h = _post_call(fox.reshape(n, D_FOX), swa, h, p[i].reshape(n, D_PLE),
                       wo, w1, w2, wg, w_ple[i].astype(bf16), gains)
    return h.reshape(batch, seq, D_MODEL)
```

```python
import functools

import jax
import jax.numpy as jnp
import numpy as np
from jax import lax
from jax.experimental import pallas as pl
from jax.experimental.pallas import tpu as pltpu

D_MODEL = 1024
HEAD_DIM = 64
N_FOX_HEADS = 8
N_SWA_HEADS = 8
N_SWA_KV_HEADS = 2
SWA_GROUP = N_SWA_HEADS // N_SWA_KV_HEADS
D_FOX = N_FOX_HEADS * HEAD_DIM
D_SWA = N_SWA_HEADS * HEAD_DIM
D_SWA_KV = N_SWA_KV_HEADS * HEAD_DIM
D_FF = 4 * D_MODEL
D_PLE = 256
WINDOW = 128
Q_BLOCK = 128
N_BUCKETS = 32
MAX_DISTANCE = 128
RMS_EPS = 1e-6

LANES = 128
N_PAIRS = N_FOX_HEADS // 2
N_SPLIT = 3
N_TERMS = HEAD_DIM // N_FOX_HEADS
TERM_SRC = (0, 0, 0, 1, 1, 1, 2, 2)
TERM_CUT = (0, 1, 2, 0, 1, 2, 0, 1)
NEG_BIG = -1e30
LOG2E = float(np.log2(np.e))

PROJ_TM = 1024
PROJ_STRIP = 512
FOX_T = 512
FOX_DIAG = 256
POST_TM = 512
POST_STRIP = 256
FF_CHUNK = 1024
VMEM_LIMIT = 56 * 1024 * 1024

C_FQ, C_FK, C_FV = 0, D_FOX, 2 * D_FOX
C_SQ = 3 * D_FOX
C_SK = C_SQ + D_SWA
C_SV = C_SK + D_SWA_KV
C_FF = C_SV + D_SWA_KV
D_PROJ = C_FF + LANES


def _rms(v):
    return v * lax.rsqrt(jnp.mean(v * v, axis=-1, keepdims=True) + RMS_EPS)


def _dot(a, b):
    return jnp.dot(a, b, preferred_element_type=jnp.float32)


def _dot_nt(a, b):
    return lax.dot_general(a, b, (((1,), (1,)), ((), ())),
                           preferred_element_type=jnp.float32)


def _proj_kernel(x_ref, g_ref, wt_ref, bf_ref, tri_ref,
                 fq_ref, fk_ref, fv_ref, qa_ref, ka_ref, sq_ref, sk_ref, sv_ref,
                 carry_ref, w_ref):
    @pl.when((pl.program_id(0) == 0) & (pl.program_id(1) == 0))
    def _():
        bf16 = jnp.bfloat16
        q_scale = HEAD_DIM ** -0.5 * LOG2E
        src = np.cumsum([0, D_FOX, D_FOX, D_FOX, N_FOX_HEADS, D_SWA])
        src_fq, src_fk, src_fv, src_ff, src_sq, src_skv = (int(c) for c in src)
        w_ref[C_FQ:C_FQ + D_FOX] = (wt_ref[src_fq:src_fq + D_FOX] * q_scale).astype(bf16)
        w_ref[C_FK:C_FK + D_FOX] = wt_ref[src_fk:src_fk + D_FOX].astype(bf16)
        w_ref[C_FV:C_FV + D_FOX] = wt_ref[src_fv:src_fv + D_FOX].astype(bf16)
        w_ref[C_SQ:C_SQ + D_SWA] = (wt_ref[src_sq:src_sq + D_SWA] * q_scale).astype(bf16)
        w_ref[C_SK:C_SK + 2 * D_SWA_KV] = wt_ref[src_skv:src_skv + 2 * D_SWA_KV].astype(bf16)
        gate = wt_ref[src_ff:src_ff + N_FOX_HEADS]
        w_ref[C_FF:C_FF + LANES] = jnp.tile(gate, (LANES // N_FOX_HEADS, 1)).astype(bf16)

    @pl.when(pl.program_id(1) == 0)
    def _():
        carry_ref[...] = jnp.zeros_like(carry_ref)

    lane = lax.broadcasted_iota(jnp.int32, (PROJ_STRIP, LANES), 1)
    lower = lane < HEAD_DIM
    term = (lane // N_FOX_HEADS) % N_TERMS

    for r in range(0, PROJ_TM, PROJ_STRIP):
        rows = slice(r, r + PROJ_STRIP)
        a = (_rms(x_ref[rows, :]) * g_ref[...]).astype(jnp.bfloat16)

        def proj(lo, width):
            return _dot_nt(a, w_ref[lo:lo + width])

        fq_ref[rows, :] = proj(C_FQ, D_FOX).astype(jnp.bfloat16)
        fk_ref[rows, :] = proj(C_FK, D_FOX).astype(jnp.bfloat16)
        fv_ref[rows, :] = proj(C_FV, D_FOX).astype(jnp.bfloat16)
        sq_ref[rows, :] = proj(C_SQ, D_SWA).astype(jnp.bfloat16)

        kv = proj(C_SK, 2 * D_SWA_KV)
        for half, out_ref in enumerate((sk_ref, sv_ref)):
            z = kv[:, half * LANES:(half + 1) * LANES]
            zr = pltpu.roll(z, HEAD_DIM, 1)
            out_ref[rows, :LANES] = jnp.where(lower, z, zr).astype(jnp.bfloat16)
            out_ref[rows, LANES:] = jnp.where(lower, zr, z).astype(jnp.bfloat16)

        v = proj(C_FF, LANES) + bf_ref[...]
        logf = (jnp.minimum(v, 0.0) - jnp.log1p(jnp.exp(-jnp.abs(v)))) * LOG2E
        x = _pick_piece(_split_bf16(logf), term, TERM_SRC)
        c = carry_ref[...] + _dot(tri_ref[...], x)
        carry_ref[...] = c[PROJ_STRIP - 1:PROJ_STRIP, :]
        terms = _pick_piece(_split_bf16(c), term, TERM_CUT)
        one = jnp.ones_like(terms)
        qa_ref[rows, :] = jnp.where(lower, terms, one)
        ka_ref[rows, :] = jnp.where(lower, one, -terms)


def _split_bf16(v):
    pieces = []
    r = v
    for _ in range(N_SPLIT):
        t = r.astype(jnp.bfloat16)
        pieces.append(t)
        r = r - t.astype(jnp.float32)
    return pieces


def _pick_piece(pieces, term, piece_of_term):
    out = pieces[piece_of_term[-1]]
    for t in range(N_TERMS - 2, -1, -1):
        if piece_of_term[t] != piece_of_term[t + 1]:
            out = jnp.where(term <= t, pieces[piece_of_term[t]], out)
    return out


def _proj_call(x2, g_pre, w_in_t, b_ff, batch, seq):
    n = batch * seq
    steps = seq // PROJ_TM
    tri = np.tril(np.ones((PROJ_STRIP, PROJ_STRIP), np.float32))
    row = lambda b, s: (b * steps + s, 0)
    const = lambda b, s: (0, 0)
    bf16 = jnp.bfloat16

    def out(width):
        return (jax.ShapeDtypeStruct((n, width), bf16), pl.BlockSpec((PROJ_TM, width), row))

    outs = [out(D_FOX), out(D_FOX), out(D_FOX), out(LANES), out(LANES),
            out(D_SWA), out(2 * LANES), out(2 * LANES)]
    return pl.pallas_call(
        _proj_kernel,
        grid=(batch, steps),
        in_specs=[
            pl.BlockSpec((PROJ_TM, D_MODEL), row),
            pl.BlockSpec((1, D_MODEL), const),
            pl.BlockSpec(w_in_t.shape, const, pipeline_mode=pl.Buffered(1)),
            pl.BlockSpec((1, LANES), const),
            pl.BlockSpec((PROJ_STRIP, PROJ_STRIP), const),
        ],
        out_specs=[o[1] for o in outs],
        out_shape=[o[0] for o in outs],
        scratch_shapes=[pltpu.VMEM((1, LANES), jnp.float32),
                        pltpu.VMEM((D_PROJ, D_MODEL), bf16)],
        compiler_params=pltpu.CompilerParams(
            dimension_semantics=("arbitrary", "arbitrary"),
            vmem_limit_bytes=VMEM_LIMIT),
        name="proj",
    )(x2, g_pre, w_in_t, b_ff, jnp.asarray(tri, bf16))


def _fox_kernel(*refs, seq, n_cast):
    q_ref, qa_ref, k_ref, ka_ref, v_ref = refs[:5]
    w_refs = refs[5:5 + n_cast]
    o_ref = refs[5 + n_cast]
    wb_refs = refs[6 + n_cast:6 + 2 * n_cast]
    qm_ref, s_ref, m_ref, acc_ref = refs[6 + 2 * n_cast:]
    for w_ref, wb_ref in zip(w_refs, wb_refs):
        wb_ref[...] = w_ref[...].astype(wb_ref.dtype)

    t = FOX_T
    lane = lax.broadcasted_iota(jnp.int32, (t, LANES), 1)
    lower = lane < HEAD_DIM
    head_of_lane = lane % N_FOX_HEADS
    first_head = 2 * pl.program_id(1)
    d = FOX_DIAG
    causal = (lax.broadcasted_iota(jnp.int32, (d, d), 0)
              >= lax.broadcasted_iota(jnp.int32, (d, d), 1))

    def tile_rows(i):
        return slice(i * t, (i + 1) * t)

    def parts(i, j):
        if j < i:
            return [(0, t, t)]
        return [(r, r + d, r + d) for r in range(0, t, d)]

    def mask_queries(i):
        q = q_ref[0, tile_rows(i), :]
        qa = qa_ref[0, tile_rows(i), :]
        zero = jnp.zeros_like(q)
        for e in range(2):
            qm_ref[i, e] = jnp.concatenate(
                [jnp.where(lower, q, zero) if e == 0 else jnp.where(lower, zero, q),
                 jnp.where(head_of_lane == first_head + e, qa, zero)], axis=-1)

    def scores(i, j, slot):
        for r0, r1, nk in parts(i, j):
            keys = slice(j * t, j * t + nk)
            kk = jnp.concatenate([k_ref[0, keys, :], ka_ref[0, keys, :]], axis=-1)
            for e in range(2):
                s_ref[slot, e, r0:r1, :nk] = _dot_nt(qm_ref[i, e, r0:r1], kk)

    def update(i, j, slot):
        for r0, r1, nk in parts(i, j):
            v = v_ref[0, j * t:j * t + nk, :]
            one = jnp.ones_like(v)
            for e in range(2):
                s = s_ref[slot, e, r0:r1, :nk]
                if j == i:
                    last = jnp.where(causal, s[:, nk - d:], NEG_BIG)
                    s = last if nk == d else jnp.concatenate([s[:, :nk - d], last], axis=-1)
                row_max = jnp.max(s, axis=-1, keepdims=True)
                m_new = (jnp.broadcast_to(row_max, (r1 - r0, LANES)) if j == 0
                         else jnp.maximum(m_ref[e, r0:r1], row_max))
                p = jnp.concatenate(
                    [jnp.exp2(s[:, c * LANES:(c + 1) * LANES] - m_new)
                     for c in range(nk // LANES)], axis=-1).astype(jnp.bfloat16)
                v_lower = lax.broadcasted_iota(jnp.int32, v.shape, 1) < HEAD_DIM
                ve = jnp.where(v_lower, v, one) if e == 0 else jnp.where(v_lower, one, v)
                pv = _dot(p, ve)
                if j > 0:
                    pv = jnp.exp2(m_ref[e, r0:r1] - m_new) * acc_ref[e, r0:r1] + pv
                acc_ref[e, r0:r1] = pv
                m_ref[e, r0:r1] = m_new

    def finish(i):
        outs = []
        for e in range(2):
            acc = acc_ref[e]
            outs.append(acc / pltpu.roll(acc, HEAD_DIM, 1))
        o_ref[0, tile_rows(i), :] = jnp.where(lower, outs[0], outs[1]).astype(o_ref.dtype)

    n_tiles = seq // t
    for i in range(n_tiles):
        mask_queries(i)
    pairs = [(i, j) for i in range(n_tiles) for j in range(i + 1)]
    scores(*pairs[0], 0)
    for idx, (i, j) in enumerate(pairs):
        if idx + 1 < len(pairs):
            scores(*pairs[idx + 1], (idx + 1) % 2)
        update(i, j, idx % 2)
        if j == i:
            finish(i)


def _fox_call(fq, qa, fk, ka, fv, weights, batch, seq):
    t = FOX_T
    bf16 = jnp.bfloat16
    shape3 = (batch, seq, N_PAIRS * LANES)
    aug3 = (batch, seq, LANES)
    spec = pl.BlockSpec((1, seq, LANES), lambda b, p: (b, 0, p))
    aug_spec = pl.BlockSpec((1, seq, LANES), lambda b, p: (b, 0, 0))
    steps = batch * N_PAIRS
    w_specs = [pl.BlockSpec((w.shape[0] // steps, w.shape[1]), lambda b, p: (b * N_PAIRS + p, 0))
               for w in weights]
    return pl.pallas_call(
        functools.partial(_fox_kernel, seq=seq, n_cast=len(weights)),
        grid=(batch, N_PAIRS),
        in_specs=[spec, aug_spec, spec, aug_spec, spec] + w_specs,
        out_specs=[spec] + w_specs,
        out_shape=[jax.ShapeDtypeStruct(shape3, bf16)]
        + [jax.ShapeDtypeStruct(w.shape, bf16) for w in weights],
        scratch_shapes=[pltpu.VMEM((seq // t, 2, t, 2 * LANES), jnp.bfloat16),
                        pltpu.VMEM((2, 2, t, t), jnp.float32),
                        pltpu.VMEM((2, t, LANES), jnp.float32),
                        pltpu.VMEM((2, t, LANES), jnp.float32)],
        compiler_params=pltpu.CompilerParams(
            dimension_semantics=("parallel", "parallel"),
            vmem_limit_bytes=VMEM_LIMIT),
        name="fox",
    )(fq.reshape(shape3), qa.reshape(aug3), fk.reshape(shape3), ka.reshape(aug3),
      fv.reshape(shape3), *weights)


def _t5_bucket(n):
    max_exact = N_BUCKETS // 2
    large = max_exact + (np.log(np.maximum(n, 1) / max_exact)
                         / np.log(MAX_DISTANCE / max_exact)
                         * (N_BUCKETS - max_exact)).astype(np.int32)
    large = np.minimum(large, N_BUCKETS - 1)
    return np.where(n < max_exact, n, large).astype(np.int32)


def _band_buckets():
    i = np.arange(Q_BLOCK)[:, None]
    j = np.arange(2 * Q_BLOCK)[None, :]
    dist = i + Q_BLOCK - j
    in_window = (dist >= 0) & (dist < WINDOW)
    return np.where(in_window, _t5_bucket(np.clip(dist, 0, None)), -1).astype(np.int32)


def _swa_kernel(bucket_ref, rel_ref, sink_ref, q_ref, k_ref, v_ref, o_ref, bias_ref, sinkc_ref,
                s_ref, *, seq):
    qb = Q_BLOCK
    kv = pl.program_id(1)

    @pl.when((pl.program_id(0) == 0) & (kv == 0))
    def _():
        bucket = bucket_ref[...]
        for h in range(N_SWA_HEADS):
            bias = jnp.full(bucket.shape, NEG_BIG, jnp.float32)
            for b in range(N_BUCKETS):
                bias = jnp.where(bucket == b, rel_ref[b, h] * LOG2E, bias)
            group, g = divmod(h, SWA_GROUP)
            bias_ref[group, g * qb:(g + 1) * qb, :] = bias
            sinkc_ref[group, g * qb:(g + 1) * qb, :] = jnp.full((qb, LANES), sink_ref[h] * LOG2E)

    lower = lax.broadcasted_iota(jnp.int32, (qb, LANES), 1) < HEAD_DIM

    def band_of(n):
        if n == 0:
            return slice(0, qb), qb, slice(qb, 2 * qb)
        return slice((n - 1) * qb, (n + 1) * qb), 2 * qb, slice(0, 2 * qb)

    def scores(n, slot):
        rows = slice(n * qb, (n + 1) * qb)
        band, width, cols = band_of(n)
        parts = []
        for pair in range(SWA_GROUP // 2):
            qp = q_ref[0, rows, pair * LANES:(pair + 1) * LANES]
            zero = jnp.zeros_like(qp)
            parts += [jnp.where(lower, qp, zero), jnp.where(lower, zero, qp)]
        s_ref[slot, :, :width] = (_dot_nt(jnp.concatenate(parts, axis=0), k_ref[0, band, :])
                                  + bias_ref[kv, :, cols])

    def update(n, slot):
        rows = slice(n * qb, (n + 1) * qb)
        band, width, _ = band_of(n)
        vb = v_ref[0, band, :]
        vb = jnp.concatenate([vb, jnp.ones_like(vb)], axis=-1)
        s = s_ref[slot, :, :width]
        sink = sinkc_ref[kv]
        m = jnp.maximum(jnp.max(s, axis=-1, keepdims=True), sink)
        p = jnp.concatenate(
            [jnp.exp2(s[:, c * LANES:(c + 1) * LANES] - m) for c in range(width // LANES)],
            axis=-1).astype(jnp.bfloat16)
        acc = _dot(p, vb)
        out = acc[:, :LANES] / (acc[:, LANES:] + jnp.exp2(sink - m))
        for pair in range(SWA_GROUP // 2):
            even = slice(2 * pair * qb, (2 * pair + 1) * qb)
            odd = slice((2 * pair + 1) * qb, (2 * pair + 2) * qb)
            o_ref[0, rows, pair * LANES:(pair + 1) * LANES] = jnp.where(
                lower, out[even], out[odd]).astype(o_ref.dtype)

    n_blocks = seq // qb
    scores(0, 0)
    for n in range(n_blocks):
        if n + 1 < n_blocks:
            scores(n + 1, (n + 1) % 2)
        update(n, n % 2)


def _swa_call(sq, sk, sv, rel_bias, sinks, batch, seq):
    bucket = _band_buckets()
    width = SWA_GROUP * HEAD_DIM
    smem = pl.BlockSpec(memory_space=pltpu.SMEM)
    q_spec = pl.BlockSpec((1, seq, width), lambda b, kv: (b, 0, kv))
    kv_spec = pl.BlockSpec((1, seq, LANES), lambda b, kv: (b, 0, kv))
    return pl.pallas_call(
        functools.partial(_swa_kernel, seq=seq),
        grid=(batch, N_SWA_KV_HEADS),
        in_specs=[pl.BlockSpec(bucket.shape, lambda b, kv: (0, 0)), smem, smem,
                  q_spec, kv_spec, kv_spec],
        out_specs=q_spec,
        out_shape=jax.ShapeDtypeStruct((batch, seq, D_SWA), jnp.bfloat16),
        scratch_shapes=[
            pltpu.VMEM((N_SWA_KV_HEADS, SWA_GROUP * Q_BLOCK, 2 * Q_BLOCK), jnp.float32),
            pltpu.VMEM((N_SWA_KV_HEADS, SWA_GROUP * Q_BLOCK, LANES), jnp.float32),
            pltpu.VMEM((2, SWA_GROUP * Q_BLOCK, 2 * Q_BLOCK), jnp.float32)],
        compiler_params=pltpu.CompilerParams(
            dimension_semantics=("arbitrary", "arbitrary"),
            vmem_limit_bytes=VMEM_LIMIT),
        name="swa",
    )(jnp.asarray(bucket), rel_bias, sinks, sq.reshape(batch, seq, D_SWA),
      sk.reshape(batch, seq, 2 * LANES), sv.reshape(batch, seq, 2 * LANES))


def _post_kernel(fox_ref, swa_ref, x_ref, p_ref, wo_ref, w1_ref, w2_ref, wg_ref, wp_ref,
                 gains_ref, o_ref):
    bf16 = jnp.bfloat16
    h_parts, m_parts = [], []
    for r in range(0, POST_TM, POST_STRIP):
        rows = slice(r, r + POST_STRIP)
        mix = _dot(jnp.concatenate([fox_ref[rows, :], swa_ref[rows, :]], axis=-1), wo_ref[...])
        hh = x_ref[rows, :] + _rms(mix) * gains_ref[0:1, :]
        h_parts.append(hh)
        m_parts.append((_rms(hh) * gains_ref[1:2, :]).astype(bf16))
    h = jnp.concatenate(h_parts, axis=0)
    m = jnp.concatenate(m_parts, axis=0)
    ple = _dot(p_ref[...].astype(bf16), wp_ref[...])

    y = jnp.zeros((POST_TM, D_MODEL), jnp.float32)
    for c in range(D_FF // FF_CHUNK):
        cols = slice(c * FF_CHUNK, (c + 1) * FF_CHUNK)
        hid = jnp.square(jnp.maximum(_dot(m, w1_ref[:, cols]), 0.0))
        y = y + _dot(hid.astype(bf16), w2_ref[cols, :])

    for r in range(0, POST_TM, POST_STRIP):
        rows = slice(r, r + POST_STRIP)
        hh = h[rows] + _rms(y[rows]) * gains_ref[2:3, :]
        gate = jax.nn.sigmoid(_dot(hh.astype(bf16), wg_ref[...]))
        o_ref[rows, :] = hh + _rms(ple[rows] * gate) * gains_ref[3:4, :]


def _post_call(fox, swa, x2, p2, wo, w1, w2, wg, wp, gains):
    n = x2.shape[0]
    row = lambda i: (i, 0)
    const = lambda i: (0, 0)

    def resident(shape):
        return pl.BlockSpec(shape, const, pipeline_mode=pl.Buffered(1))

    return pl.pallas_call(
        _post_kernel,
        grid=(n // POST_TM,),
        in_specs=[
            pl.BlockSpec((POST_TM, D_FOX), row),
            pl.BlockSpec((POST_TM, D_SWA), row),
            pl.BlockSpec((POST_TM, D_MODEL), row),
            pl.BlockSpec((POST_TM, D_PLE), row),
            resident(wo.shape), resident(w1.shape), resident(w2.shape),
            resident(wg.shape), resident(wp.shape), resident(gains.shape),
        ],
        out_specs=pl.BlockSpec((POST_TM, D_MODEL), row),
        out_shape=jax.ShapeDtypeStruct((n, D_MODEL), jnp.float32),
        compiler_params=pltpu.CompilerParams(
            dimension_semantics=("parallel",),
            vmem_limit_bytes=VMEM_LIMIT),
        name="post",
    )(fox, swa, x2, p2, wo, w1, w2, wg, wp, gains)


def _gate_lanes(v):
    return jnp.tile(v, (1, LANES // N_FOX_HEADS))


def kernel(x, p, w_in, b_forget, w_out, rel_bias, swa_sinks, g_attn_pre, g_attn_post,
           w_ff1, w_ff2, g_ff_pre, g_ff_post, w_ple, w_ple_gate, g_ple_post):
    batch, seq, _ = x.shape
    n = batch * seq
    bf16 = jnp.bfloat16
    h = x.reshape(n, D_MODEL)
    for i in range(p.shape[0]):
        b_ff = _gate_lanes(b_forget[i].reshape(1, N_FOX_HEADS))
        fq, fk, fv, qa, ka, sq, sk, sv = _proj_call(
            h, g_attn_pre[i].reshape(1, D_MODEL), w_in[i].T, b_ff, batch, seq)
        fox, wo, w1, w2, wg = _fox_call(
            fq, qa, fk, ka, fv, [w_out[i], w_ff1[i], w_ff2[i], w_ple_gate[i]], batch, seq)
        swa = _swa_call(sq, sk, sv, rel_bias, swa_sinks[i], batch, seq).reshape(n, D_SWA)
        gains = jnp.stack([g_attn_post[i], g_ff_pre[i], g_ff_post[i], g_ple_post[i]])
        h = _post_call(fox.reshape(n, D_FOX), swa, h, p[i].reshape(n, D_PLE),
                       wo, w1, w2, wg, w_ple[i].astype(bf16), gains)
    return h.reshape(batch, seq, D_MODEL)
```

```python
import functools

import jax
import jax.numpy as jnp
import numpy as np
from jax import lax
from jax.experimental import pallas as pl
from jax.experimental.pallas import tpu as pltpu

D_MODEL = 1024
HEAD_DIM = 64
N_FOX_HEADS = 8
N_SWA_HEADS = 8
N_SWA_KV_HEADS = 2
SWA_GROUP = N_SWA_HEADS // N_SWA_KV_HEADS
D_FOX = N_FOX_HEADS * HEAD_DIM
D_SWA = N_SWA_HEADS * HEAD_DIM
D_SWA_KV = N_SWA_KV_HEADS * HEAD_DIM
D_FF = 4 * D_MODEL
D_PLE = 256
WINDOW = 128
Q_BLOCK = 128
N_BUCKETS = 32
MAX_DISTANCE = 128
RMS_EPS = 1e-6

LANES = 128
N_PAIRS = N_FOX_HEADS // 2
N_SPLIT = 3
N_TERMS = HEAD_DIM // N_FOX_HEADS
TERM_SRC = (0, 0, 0, 1, 1, 1, 2, 2)
TERM_CUT = (0, 1, 2, 0, 1, 2, 0, 1)
NEG_BIG = -1e30
LOG2E = float(np.log2(np.e))

PROJ_TM = 1024
PROJ_STRIP = 512
FOX_T = 512
FOX_DIAG = 256
FOX_PAIRS = 2
POST_TM = 512
POST_STRIP = 256
FF_CHUNK = 1024
VMEM_LIMIT = 56 * 1024 * 1024

C_FQ, C_FK, C_FV = 0, D_FOX, 2 * D_FOX
C_SQ = 3 * D_FOX
C_SK = C_SQ + D_SWA
C_SV = C_SK + D_SWA_KV
C_FF = C_SV + D_SWA_KV
D_PROJ = C_FF + LANES


def _rms(v):
    return v * lax.rsqrt(jnp.mean(v * v, axis=-1, keepdims=True) + RMS_EPS)


def _dot(a, b):
    return jnp.dot(a, b, preferred_element_type=jnp.float32)


def _dot_nt(a, b):
    return lax.dot_general(a, b, (((1,), (1,)), ((), ())),
                           preferred_element_type=jnp.float32)


def _proj_kernel(x_ref, g_ref, wt_ref, bf_ref, tri_ref,
                 fq_ref, fk_ref, fv_ref, qa_ref, ka_ref, sq_ref, sk_ref, sv_ref,
                 carry_ref, w_ref):
    @pl.when((pl.program_id(0) == 0) & (pl.program_id(1) == 0))
    def _():
        bf16 = jnp.bfloat16
        q_scale = HEAD_DIM ** -0.5 * LOG2E
        src = np.cumsum([0, D_FOX, D_FOX, D_FOX, N_FOX_HEADS, D_SWA])
        src_fq, src_fk, src_fv, src_ff, src_sq, src_skv = (int(c) for c in src)
        w_ref[C_FQ:C_FQ + D_FOX] = (wt_ref[src_fq:src_fq + D_FOX] * q_scale).astype(bf16)
        w_ref[C_FK:C_FK + D_FOX] = wt_ref[src_fk:src_fk + D_FOX].astype(bf16)
        w_ref[C_FV:C_FV + D_FOX] = wt_ref[src_fv:src_fv + D_FOX].astype(bf16)
        w_ref[C_SQ:C_SQ + D_SWA] = (wt_ref[src_sq:src_sq + D_SWA] * q_scale).astype(bf16)
        w_ref[C_SK:C_SK + 2 * D_SWA_KV] = wt_ref[src_skv:src_skv + 2 * D_SWA_KV].astype(bf16)
        gate = wt_ref[src_ff:src_ff + N_FOX_HEADS]
        w_ref[C_FF:C_FF + LANES] = jnp.tile(gate, (LANES // N_FOX_HEADS, 1)).astype(bf16)

    @pl.when(pl.program_id(1) == 0)
    def _():
        carry_ref[...] = jnp.zeros_like(carry_ref)

    lane = lax.broadcasted_iota(jnp.int32, (PROJ_STRIP, LANES), 1)
    lower = lane < HEAD_DIM
    term = (lane // N_FOX_HEADS) % N_TERMS

    for r in range(0, PROJ_TM, PROJ_STRIP):
        rows = slice(r, r + PROJ_STRIP)
        a = (_rms(x_ref[rows, :]) * g_ref[...]).astype(jnp.bfloat16)

        def proj(lo, width):
            return _dot_nt(a, w_ref[lo:lo + width])

        fq_ref[rows, :] = proj(C_FQ, D_FOX).astype(jnp.bfloat16)
        fk_ref[rows, :] = proj(C_FK, D_FOX).astype(jnp.bfloat16)
        fv_ref[rows, :] = proj(C_FV, D_FOX).astype(jnp.bfloat16)
        sq_ref[rows, :] = proj(C_SQ, D_SWA).astype(jnp.bfloat16)

        kv = proj(C_SK, 2 * D_SWA_KV)
        for half, out_ref in enumerate((sk_ref, sv_ref)):
            z = kv[:, half * LANES:(half + 1) * LANES]
            zr = pltpu.roll(z, HEAD_DIM, 1)
            out_ref[rows, :LANES] = jnp.where(lower, z, zr).astype(jnp.bfloat16)
            out_ref[rows, LANES:] = jnp.where(lower, zr, z).astype(jnp.bfloat16)

        v = proj(C_FF, LANES) + bf_ref[...]
        logf = (jnp.minimum(v, 0.0) - jnp.log1p(jnp.exp(-jnp.abs(v)))) * LOG2E
        x = _pick_piece(_split_bf16(logf), term, TERM_SRC)
        c = carry_ref[...] + _dot(tri_ref[...], x)
        carry_ref[...] = c[PROJ_STRIP - 1:PROJ_STRIP, :]
        terms = _pick_piece(_split_bf16(c), term, TERM_CUT)
        one = jnp.ones_like(terms)
        qa_ref[rows, :] = jnp.where(lower, terms, one)
        ka_ref[rows, :] = jnp.where(lower, one, -terms)


def _split_bf16(v):
    pieces = []
    r = v
    for _ in range(N_SPLIT):
        t = r.astype(jnp.bfloat16)
        pieces.append(t)
        r = r - t.astype(jnp.float32)
    return pieces


def _pick_piece(pieces, term, piece_of_term):
    out = pieces[piece_of_term[-1]]
    for t in range(N_TERMS - 2, -1, -1):
        if piece_of_term[t] != piece_of_term[t + 1]:
            out = jnp.where(term <= t, pieces[piece_of_term[t]], out)
    return out


def _proj_call(x2, g_pre, w_in_t, b_ff, batch, seq):
    n = batch * seq
    steps = seq // PROJ_TM
    tri = np.tril(np.ones((PROJ_STRIP, PROJ_STRIP), np.float32))
    row = lambda b, s: (b * steps + s, 0)
    const = lambda b, s: (0, 0)
    bf16 = jnp.bfloat16

    def out(width):
        return (jax.ShapeDtypeStruct((n, width), bf16), pl.BlockSpec((PROJ_TM, width), row))

    outs = [out(D_FOX), out(D_FOX), out(D_FOX), out(LANES), out(LANES),
            out(D_SWA), out(2 * LANES), out(2 * LANES)]
    return pl.pallas_call(
        _proj_kernel,
        grid=(batch, steps),
        in_specs=[
            pl.BlockSpec((PROJ_TM, D_MODEL), row),
            pl.BlockSpec((1, D_MODEL), const),
            pl.BlockSpec(w_in_t.shape, const, pipeline_mode=pl.Buffered(1)),
            pl.BlockSpec((1, LANES), const),
            pl.BlockSpec((PROJ_STRIP, PROJ_STRIP), const),
        ],
        out_specs=[o[1] for o in outs],
        out_shape=[o[0] for o in outs],
        scratch_shapes=[pltpu.VMEM((1, LANES), jnp.float32),
                        pltpu.VMEM((D_PROJ, D_MODEL), bf16)],
        compiler_params=pltpu.CompilerParams(
            dimension_semantics=("arbitrary", "arbitrary"),
            vmem_limit_bytes=VMEM_LIMIT),
        name="proj",
    )(x2, g_pre, w_in_t, b_ff, jnp.asarray(tri, bf16))


def _fox_kernel(*refs, seq, n_cast):
    q_ref, qa_ref, k_ref, ka_ref, v_ref = refs[:5]
    w_refs = refs[5:5 + n_cast]
    o_ref = refs[5 + n_cast]
    wb_refs = refs[6 + n_cast:6 + 2 * n_cast]
    qm_ref, s_ref, m_ref, acc_ref = refs[6 + 2 * n_cast:]
    for w_ref, wb_ref in zip(w_refs, wb_refs):
        wb_ref[...] = w_ref[...].astype(wb_ref.dtype)

    t = FOX_T
    lane = lax.broadcasted_iota(jnp.int32, (t, LANES), 1)
    lower = lane < HEAD_DIM
    head_of_lane = lane % N_FOX_HEADS
    d = FOX_DIAG
    causal = (lax.broadcasted_iota(jnp.int32, (d, d), 0)
              >= lax.broadcasted_iota(jnp.int32, (d, d), 1))

    def tile_rows(i):
        return slice(i * t, (i + 1) * t)

    def pair_lanes(pp):
        return slice(pp * LANES, (pp + 1) * LANES)

    def parts(i, j):
        if j < i:
            return [(0, t, t)]
        return [(r, r + d, r + d) for r in range(0, t, d)]

    def mask_queries(pp, i):
        q = q_ref[0, tile_rows(i), pair_lanes(pp)]
        qa = qa_ref[0, tile_rows(i), :]
        zero = jnp.zeros_like(q)
        first_head = 2 * (FOX_PAIRS * pl.program_id(1) + pp)
        for e in range(2):
            qm_ref[pp, i, e] = jnp.concatenate(
                [jnp.where(lower, q, zero) if e == 0 else jnp.where(lower, zero, q),
                 jnp.where(head_of_lane == first_head + e, qa, zero)], axis=-1)

    def scores(pp, i, j, slot):
        for r0, r1, nk in parts(i, j):
            keys = slice(j * t, j * t + nk)
            kk = jnp.concatenate([k_ref[0, keys, pair_lanes(pp)], ka_ref[0, keys, :]], axis=-1)
            for e in range(2):
                s_ref[slot, e, r0:r1, :nk] = _dot_nt(qm_ref[pp, i, e, r0:r1], kk)

    def update(pp, i, j, slot):
        for r0, r1, nk in parts(i, j):
            v = v_ref[0, j * t:j * t + nk, pair_lanes(pp)]
            one = jnp.ones_like(v)
            for e in range(2):
                s = s_ref[slot, e, r0:r1, :nk]
                if j == i:
                    last = jnp.where(causal, s[:, nk - d:], NEG_BIG)
                    s = last if nk == d else jnp.concatenate([s[:, :nk - d], last], axis=-1)
                row_max = jnp.max(s, axis=-1, keepdims=True)
                m_new = (jnp.broadcast_to(row_max, (r1 - r0, LANES)) if j == 0
                         else jnp.maximum(m_ref[pp, e, r0:r1], row_max))
                p = jnp.concatenate(
                    [jnp.exp2(s[:, c * LANES:(c + 1) * LANES] - m_new)
                     for c in range(nk // LANES)], axis=-1).astype(jnp.bfloat16)
                v_lower = lax.broadcasted_iota(jnp.int32, v.shape, 1) < HEAD_DIM
                ve = jnp.where(v_lower, v, one) if e == 0 else jnp.where(v_lower, one, v)
                pv = _dot(p, ve)
                if j > 0:
                    pv = jnp.exp2(m_ref[pp, e, r0:r1] - m_new) * acc_ref[pp, e, r0:r1] + pv
                acc_ref[pp, e, r0:r1] = pv
                m_ref[pp, e, r0:r1] = m_new

    def finish(pp, i):
        outs = []
        for e in range(2):
            acc = acc_ref[pp, e]
            outs.append(acc / pltpu.roll(acc, HEAD_DIM, 1))
        o_ref[0, tile_rows(i), pair_lanes(pp)] = (
            jnp.where(lower, outs[0], outs[1]).astype(o_ref.dtype))

    n_tiles = seq // t
    for pp in range(FOX_PAIRS):
        for i in range(n_tiles):
            mask_queries(pp, i)
    work = [(pp, i, j) for pp in range(FOX_PAIRS) for i in range(n_tiles) for j in range(i + 1)]
    scores(*work[0], 0)
    for idx, (pp, i, j) in enumerate(work):
        if idx + 1 < len(work):
            scores(*work[idx + 1], (idx + 1) % 2)
        update(pp, i, j, idx % 2)
        if j == i:
            finish(pp, i)


def _fox_call(fq, qa, fk, ka, fv, weights, batch, seq):
    t = FOX_T
    bf16 = jnp.bfloat16
    groups = N_PAIRS // FOX_PAIRS
    shape3 = (batch, seq, N_PAIRS * LANES)
    aug3 = (batch, seq, LANES)
    spec = pl.BlockSpec((1, seq, FOX_PAIRS * LANES), lambda b, g: (b, 0, g))
    aug_spec = pl.BlockSpec((1, seq, LANES), lambda b, g: (b, 0, 0))
    steps = batch * groups
    w_specs = [pl.BlockSpec((w.shape[0] // steps, w.shape[1]), lambda b, g: (b * groups + g, 0))
               for w in weights]
    return pl.pallas_call(
        functools.partial(_fox_kernel, seq=seq, n_cast=len(weights)),
        grid=(batch, groups),
        in_specs=[spec, aug_spec, spec, aug_spec, spec] + w_specs,
        out_specs=[spec] + w_specs,
        out_shape=[jax.ShapeDtypeStruct(shape3, bf16)]
        + [jax.ShapeDtypeStruct(w.shape, bf16) for w in weights],
        scratch_shapes=[pltpu.VMEM((FOX_PAIRS, seq // t, 2, t, 2 * LANES), jnp.bfloat16),
                        pltpu.VMEM((2, 2, t, t), jnp.float32),
                        pltpu.VMEM((FOX_PAIRS, 2, t, LANES), jnp.float32),
                        pltpu.VMEM((FOX_PAIRS, 2, t, LANES), jnp.float32)],
        compiler_params=pltpu.CompilerParams(
            dimension_semantics=("parallel", "parallel"),
            vmem_limit_bytes=VMEM_LIMIT),
        name="fox",
    )(fq.reshape(shape3), qa.reshape(aug3), fk.reshape(shape3), ka.reshape(aug3),
      fv.reshape(shape3), *weights)


def _t5_bucket(n):
    max_exact = N_BUCKETS // 2
    large = max_exact + (np.log(np.maximum(n, 1) / max_exact)
                         / np.log(MAX_DISTANCE / max_exact)
                         * (N_BUCKETS - max_exact)).astype(np.int32)
    large = np.minimum(large, N_BUCKETS - 1)
    return np.where(n < max_exact, n, large).astype(np.int32)


def _band_buckets():
    i = np.arange(Q_BLOCK)[:, None]
    j = np.arange(2 * Q_BLOCK)[None, :]
    dist = i + Q_BLOCK - j
    in_window = (dist >= 0) & (dist < WINDOW)
    return np.where(in_window, _t5_bucket(np.clip(dist, 0, None)), -1).astype(np.int32)


def _swa_kernel(bucket_ref, rel_ref, sink_ref, q_ref, k_ref, v_ref, o_ref, bias_ref, sinkc_ref,
                s_ref, *, seq):
    qb = Q_BLOCK
    kv = pl.program_id(1)

    @pl.when((pl.program_id(0) == 0) & (kv == 0))
    def _():
        bucket = bucket_ref[...]
        for h in range(N_SWA_HEADS):
            bias = jnp.full(bucket.shape, NEG_BIG, jnp.float32)
            for b in range(N_BUCKETS):
                bias = jnp.where(bucket == b, rel_ref[b, h] * LOG2E, bias)
            group, g = divmod(h, SWA_GROUP)
            bias_ref[group, g * qb:(g + 1) * qb, :] = bias
            sinkc_ref[group, g * qb:(g + 1) * qb, :] = jnp.full((qb, LANES), sink_ref[h] * LOG2E)

    lower = lax.broadcasted_iota(jnp.int32, (qb, LANES), 1) < HEAD_DIM

    def band_of(n):
        if n == 0:
            return slice(0, qb), qb, slice(qb, 2 * qb)
        return slice((n - 1) * qb, (n + 1) * qb), 2 * qb, slice(0, 2 * qb)

    def scores(n, slot):
        rows = slice(n * qb, (n + 1) * qb)
        band, width, cols = band_of(n)
        parts = []
        for pair in range(SWA_GROUP // 2):
            qp = q_ref[0, rows, pair * LANES:(pair + 1) * LANES]
            zero = jnp.zeros_like(qp)
            parts += [jnp.where(lower, qp, zero), jnp.where(lower, zero, qp)]
        s_ref[slot, :, :width] = (_dot_nt(jnp.concatenate(parts, axis=0), k_ref[0, band, :])
                                  + bias_ref[kv, :, cols])

    def update(n, slot):
        rows = slice(n * qb, (n + 1) * qb)
        band, width, _ = band_of(n)
        vb = v_ref[0, band, :]
        vb = jnp.concatenate([vb, jnp.ones_like(vb)], axis=-1)
        s = s_ref[slot, :, :width]
        sink = sinkc_ref[kv]
        m = jnp.maximum(jnp.max(s, axis=-1, keepdims=True), sink)
        p = jnp.concatenate(
            [jnp.exp2(s[:, c * LANES:(c + 1) * LANES] - m) for c in range(width // LANES)],
            axis=-1).astype(jnp.bfloat16)
        acc = _dot(p, vb)
        out = acc[:, :LANES] / (acc[:, LANES:] + jnp.exp2(sink - m))
        for pair in range(SWA_GROUP // 2):
            even = slice(2 * pair * qb, (2 * pair + 1) * qb)
            odd = slice((2 * pair + 1) * qb, (2 * pair + 2) * qb)
            o_ref[0, rows, pair * LANES:(pair + 1) * LANES] = jnp.where(
                lower, out[even], out[odd]).astype(o_ref.dtype)

    n_blocks = seq // qb
    scores(0, 0)
    for n in range(n_blocks):
        if n + 1 < n_blocks:
            scores(n + 1, (n + 1) % 2)
        update(n, n % 2)


def _swa_call(sq, sk, sv, rel_bias, sinks, batch, seq):
    bucket = _band_buckets()
    width = SWA_GROUP * HEAD_DIM
    smem = pl.BlockSpec(memory_space=pltpu.SMEM)
    q_spec = pl.BlockSpec((1, seq, width), lambda b, kv: (b, 0, kv))
    kv_spec = pl.BlockSpec((1, seq, LANES), lambda b, kv: (b, 0, kv))
    return pl.pallas_call(
        functools.partial(_swa_kernel, seq=seq),
        grid=(batch, N_SWA_KV_HEADS),
        in_specs=[pl.BlockSpec(bucket.shape, lambda b, kv: (0, 0)), smem, smem,
                  q_spec, kv_spec, kv_spec],
        out_specs=q_spec,
        out_shape=jax.ShapeDtypeStruct((batch, seq, D_SWA), jnp.bfloat16),
        scratch_shapes=[
            pltpu.VMEM((N_SWA_KV_HEADS, SWA_GROUP * Q_BLOCK, 2 * Q_BLOCK), jnp.float32),
            pltpu.VMEM((N_SWA_KV_HEADS, SWA_GROUP * Q_BLOCK, LANES), jnp.float32),
            pltpu.VMEM((2, SWA_GROUP * Q_BLOCK, 2 * Q_BLOCK), jnp.float32)],
        compiler_params=pltpu.CompilerParams(
            dimension_semantics=("arbitrary", "arbitrary"),
            vmem_limit_bytes=VMEM_LIMIT),
        name="swa",
    )(jnp.asarray(bucket), rel_bias, sinks, sq.reshape(batch, seq, D_SWA),
      sk.reshape(batch, seq, 2 * LANES), sv.reshape(batch, seq, 2 * LANES))


def _post_kernel(fox_ref, swa_ref, x_ref, p_ref, wo_ref, w1_ref, w2_ref, wg_ref, wp_ref,
                 gains_ref, o_ref):
    bf16 = jnp.bfloat16
    h_parts, m_parts = [], []
    for r in range(0, POST_TM, POST_STRIP):
        rows = slice(r, r + POST_STRIP)
        mix = _dot(jnp.concatenate([fox_ref[rows, :], swa_ref[rows, :]], axis=-1), wo_ref[...])
        hh = x_ref[rows, :] + _rms(mix) * gains_ref[0:1, :]
        h_parts.append(hh)
        m_parts.append((_rms(hh) * gains_ref[1:2, :]).astype(bf16))
    h = jnp.concatenate(h_parts, axis=0)
    m = jnp.concatenate(m_parts, axis=0)
    ple = _dot(p_ref[...].astype(bf16), wp_ref[...])

    y = jnp.zeros((POST_TM, D_MODEL), jnp.float32)
    for c in range(D_FF // FF_CHUNK):
        cols = slice(c * FF_CHUNK, (c + 1) * FF_CHUNK)
        hid = jnp.square(jnp.maximum(_dot(m, w1_ref[:, cols]), 0.0))
        y = y + _dot(hid.astype(bf16), w2_ref[cols, :])

    for r in range(0, POST_TM, POST_STRIP):
        rows = slice(r, r + POST_STRIP)
        hh = h[rows] + _rms(y[rows]) * gains_ref[2:3, :]
        gate = jax.nn.sigmoid(_dot(hh.astype(bf16), wg_ref[...]))
        o_ref[rows, :] = hh + _rms(ple[rows] * gate) * gains_ref[3:4, :]


def _post_call(fox, swa, x2, p2, wo, w1, w2, wg, wp, gains):
    n = x2.shape[0]
    row = lambda i: (i, 0)
    const = lambda i: (0, 0)

    def resident(shape):
        return pl.BlockSpec(shape, const, pipeline_mode=pl.Buffered(1))

    return pl.pallas_call(
        _post_kernel,
        grid=(n // POST_TM,),
        in_specs=[
            pl.BlockSpec((POST_TM, D_FOX), row),
            pl.BlockSpec((POST_TM, D_SWA), row),
            pl.BlockSpec((POST_TM, D_MODEL), row),
            pl.BlockSpec((POST_TM, D_PLE), row),
            resident(wo.shape), resident(w1.shape), resident(w2.shape),
            resident(wg.shape), resident(wp.shape), resident(gains.shape),
        ],
        out_specs=pl.BlockSpec((POST_TM, D_MODEL), row),
        out_shape=jax.ShapeDtypeStruct((n, D_MODEL), jnp.float32),
        compiler_params=pltpu.CompilerParams(
            dimension_semantics=("parallel",),
            vmem_limit_bytes=VMEM_LIMIT),
        name="post",
    )(fox, swa, x2, p2, wo, w1, w2, wg, wp, gains)


def _gate_lanes(v):
    return jnp.tile(v, (1, LANES // N_FOX_HEADS))


def kernel(x, p, w_in, b_forget, w_out, rel_bias, swa_sinks, g_attn_pre, g_attn_post,
           w_ff1, w_ff2, g_ff_pre, g_ff_post, w_ple, w_ple_gate, g_ple_post):
    batch, seq, _ = x.shape
    n = batch * seq
    bf16 = jnp.bfloat16
    h = x.reshape(n, D_MODEL)
    for i in range(p.shape[0]):
        b_ff = _gate_lanes(b_forget[i].reshape(1, N_FOX_HEADS))
        fq, fk, fv, qa, ka, sq, sk, sv = _proj_call(
            h, g_attn_pre[i].reshape(1, D_MODEL), w_in[i].T, b_ff, batch, seq)
        fox, wo, w1, w2, wg = _fox_call(
            fq, qa, fk, ka, fv, [w_out[i], w_ff1[i], w_ff2[i], w_ple_gate[i]], batch, seq)
        swa = _swa_call(sq, sk, sv, rel_bias, swa_sinks[i], batch, seq).reshape(n, D_SWA)
        gains = jnp.stack([g_attn_post[i], g_ff_pre[i], g_ff_post[i], g_ple_post[i]])
        h = _post_call(fox.reshape(n, D_FOX), swa, h, p[i].reshape(n, D_PLE),
                       wo, w1, w2, wg, w_ple[i].astype(bf16), gains)
    return h.reshape(batch, seq, D_MODEL)
```

```python
import functools

import jax
import jax.numpy as jnp
import numpy as np
from jax import lax
from jax.experimental import pallas as pl
from jax.experimental.pallas import tpu as pltpu

D_MODEL = 1024
HEAD_DIM = 64
N_FOX_HEADS = 8
N_SWA_HEADS = 8
N_SWA_KV_HEADS = 2
SWA_GROUP = N_SWA_HEADS // N_SWA_KV_HEADS
D_FOX = N_FOX_HEADS * HEAD_DIM
D_SWA = N_SWA_HEADS * HEAD_DIM
D_SWA_KV = N_SWA_KV_HEADS * HEAD_DIM
D_FF = 4 * D_MODEL
D_PLE = 256
WINDOW = 128
Q_BLOCK = 128
N_BUCKETS = 32
MAX_DISTANCE = 128
RMS_EPS = 1e-6

LANES = 128
N_PAIRS = N_FOX_HEADS // 2
N_SPLIT = 3
N_TERMS = HEAD_DIM // N_FOX_HEADS
TERM_SRC = (0, 0, 0, 1, 1, 1, 2, 2)
TERM_CUT = (0, 1, 2, 0, 1, 2, 0, 1)
NEG_BIG = -1e30
LOG2E = float(np.log2(np.e))

PROJ_TM = 1024
PROJ_STRIP = 512
FOX_T = 512
FOX_DIAG = 256
FOX_PAIRS = 1
POST_TM = 512
POST_STRIP = 256
FF_CHUNK = 1024
VMEM_LIMIT = 56 * 1024 * 1024

C_FQ, C_FK, C_FV = 0, D_FOX, 2 * D_FOX
C_SQ = 3 * D_FOX
C_SK = C_SQ + D_SWA
C_SV = C_SK + D_SWA_KV
C_FF = C_SV + D_SWA_KV
D_PROJ = C_FF + LANES


def _rms(v):
    return v * lax.rsqrt(jnp.mean(v * v, axis=-1, keepdims=True) + RMS_EPS)


def _dot(a, b):
    return jnp.dot(a, b, preferred_element_type=jnp.float32)


def _dot_nt(a, b):
    return lax.dot_general(a, b, (((1,), (1,)), ((), ())),
                           preferred_element_type=jnp.float32)


def _proj_kernel(x_ref, g_ref, wt_ref, bf_ref, tri_ref,
                 fq_ref, fk_ref, fv_ref, qa_ref, ka_ref, sq_ref, sk_ref, sv_ref,
                 carry_ref, w_ref):
    @pl.when((pl.program_id(0) == 0) & (pl.program_id(1) == 0))
    def _():
        bf16 = jnp.bfloat16
        q_scale = HEAD_DIM ** -0.5 * LOG2E
        src = np.cumsum([0, D_FOX, D_FOX, D_FOX, N_FOX_HEADS, D_SWA])
        src_fq, src_fk, src_fv, src_ff, src_sq, src_skv = (int(c) for c in src)
        w_ref[C_FQ:C_FQ + D_FOX] = (wt_ref[src_fq:src_fq + D_FOX] * q_scale).astype(bf16)
        w_ref[C_FK:C_FK + D_FOX] = wt_ref[src_fk:src_fk + D_FOX].astype(bf16)
        w_ref[C_FV:C_FV + D_FOX] = wt_ref[src_fv:src_fv + D_FOX].astype(bf16)
        w_ref[C_SQ:C_SQ + D_SWA] = (wt_ref[src_sq:src_sq + D_SWA] * q_scale).astype(bf16)
        w_ref[C_SK:C_SK + 2 * D_SWA_KV] = wt_ref[src_skv:src_skv + 2 * D_SWA_KV].astype(bf16)
        gate = wt_ref[src_ff:src_ff + N_FOX_HEADS]
        w_ref[C_FF:C_FF + LANES] = jnp.tile(gate, (LANES // N_FOX_HEADS, 1)).astype(bf16)

    @pl.when(pl.program_id(1) == 0)
    def _():
        carry_ref[...] = jnp.zeros_like(carry_ref)

    lane = lax.broadcasted_iota(jnp.int32, (PROJ_STRIP, LANES), 1)
    lower = lane < HEAD_DIM
    term = (lane // N_FOX_HEADS) % N_TERMS

    for r in range(0, PROJ_TM, PROJ_STRIP):
        rows = slice(r, r + PROJ_STRIP)
        a = (_rms(x_ref[rows, :]) * g_ref[...]).astype(jnp.bfloat16)

        def proj(lo, width):
            return _dot_nt(a, w_ref[lo:lo + width])

        v = proj(C_FF, LANES) + bf_ref[...]
        logf = (jnp.minimum(v, 0.0) - jnp.log1p(jnp.exp(-jnp.abs(v)))) * LOG2E
        x = _pick_piece(_split_bf16(logf), term, TERM_SRC)
        c = carry_ref[...] + _dot(tri_ref[...], x)
        carry_ref[...] = c[PROJ_STRIP - 1:PROJ_STRIP, :]
        terms = _pick_piece(_split_bf16(c), term, TERM_CUT)
        one = jnp.ones_like(terms)
        qa_ref[rows, :] = jnp.where(lower, terms, one)
        ka_ref[rows, :] = jnp.where(lower, one, -terms)

        kv = proj(C_SK, 2 * D_SWA_KV)
        for half, out_ref in enumerate((sk_ref, sv_ref)):
            z = kv[:, half * LANES:(half + 1) * LANES]
            zr = pltpu.roll(z, HEAD_DIM, 1)
            out_ref[rows, :LANES] = jnp.where(lower, z, zr).astype(jnp.bfloat16)
            out_ref[rows, LANES:] = jnp.where(lower, zr, z).astype(jnp.bfloat16)

        fq_ref[rows, :] = proj(C_FQ, D_FOX).astype(jnp.bfloat16)
        fk_ref[rows, :] = proj(C_FK, D_FOX).astype(jnp.bfloat16)
        fv_ref[rows, :] = proj(C_FV, D_FOX).astype(jnp.bfloat16)
        sq_ref[rows, :] = proj(C_SQ, D_SWA).astype(jnp.bfloat16)


def _split_bf16(v):
    pieces = []
    r = v
    for _ in range(N_SPLIT):
        t = r.astype(jnp.bfloat16)
        pieces.append(t)
        r = r - t.astype(jnp.float32)
    return pieces


def _pick_piece(pieces, term, piece_of_term):
    out = pieces[piece_of_term[-1]]
    for t in range(N_TERMS - 2, -1, -1):
        if piece_of_term[t] != piece_of_term[t + 1]:
            out = jnp.where(term <= t, pieces[piece_of_term[t]], out)
    return out


def _proj_call(x2, g_pre, w_in_t, b_ff, batch, seq):
    n = batch * seq
    steps = seq // PROJ_TM
    tri = np.tril(np.ones((PROJ_STRIP, PROJ_STRIP), np.float32))
    row = lambda b, s: (b * steps + s, 0)
    const = lambda b, s: (0, 0)
    bf16 = jnp.bfloat16

    def out(width):
        return (jax.ShapeDtypeStruct((n, width), bf16), pl.BlockSpec((PROJ_TM, width), row))

    outs = [out(D_FOX), out(D_FOX), out(D_FOX), out(LANES), out(LANES),
            out(D_SWA), out(2 * LANES), out(2 * LANES)]
    return pl.pallas_call(
        _proj_kernel,
        grid=(batch, steps),
        in_specs=[
            pl.BlockSpec((PROJ_TM, D_MODEL), row),
            pl.BlockSpec((1, D_MODEL), const),
            pl.BlockSpec(w_in_t.shape, const, pipeline_mode=pl.Buffered(1)),
            pl.BlockSpec((1, LANES), const),
            pl.BlockSpec((PROJ_STRIP, PROJ_STRIP), const),
        ],
        out_specs=[o[1] for o in outs],
        out_shape=[o[0] for o in outs],
        scratch_shapes=[pltpu.VMEM((1, LANES), jnp.float32),
                        pltpu.VMEM((D_PROJ, D_MODEL), bf16)],
        compiler_params=pltpu.CompilerParams(
            dimension_semantics=("arbitrary", "arbitrary"),
            vmem_limit_bytes=VMEM_LIMIT),
        name="proj",
    )(x2, g_pre, w_in_t, b_ff, jnp.asarray(tri, bf16))


def _fox_kernel(*refs, seq, n_cast):
    q_ref, qa_ref, k_ref, ka_ref, v_ref = refs[:5]
    w_refs = refs[5:5 + n_cast]
    o_ref = refs[5 + n_cast]
    wb_refs = refs[6 + n_cast:6 + 2 * n_cast]
    qm_ref, s_ref, m_ref, acc_ref = refs[6 + 2 * n_cast:]
    for w_ref, wb_ref in zip(w_refs, wb_refs):
        wb_ref[...] = w_ref[...].astype(wb_ref.dtype)

    t = FOX_T
    lane = lax.broadcasted_iota(jnp.int32, (t, LANES), 1)
    lower = lane < HEAD_DIM
    head_of_lane = lane % N_FOX_HEADS
    d = FOX_DIAG
    causal = (lax.broadcasted_iota(jnp.int32, (d, d), 0)
              >= lax.broadcasted_iota(jnp.int32, (d, d), 1))

    def tile_rows(i):
        return slice(i * t, (i + 1) * t)

    def pair_lanes(pp):
        return slice(pp * LANES, (pp + 1) * LANES)

    def parts(i, j):
        if j < i:
            return [(0, t, t)]
        return [(r, r + d, r + d) for r in range(0, t, d)]

    def mask_queries(pp, i):
        q = q_ref[0, tile_rows(i), pair_lanes(pp)]
        qa = qa_ref[0, tile_rows(i), :]
        zero = jnp.zeros_like(q)
        first_head = 2 * (FOX_PAIRS * pl.program_id(1) + pp)
        for e in range(2):
            qm_ref[pp, i, e] = jnp.concatenate(
                [jnp.where(lower, q, zero) if e == 0 else jnp.where(lower, zero, q),
                 jnp.where(head_of_lane == first_head + e, qa, zero)], axis=-1)

    def scores(pp, i, j, slot):
        for r0, r1, nk in parts(i, j):
            keys = slice(j * t, j * t + nk)
            kk = jnp.concatenate([k_ref[0, keys, pair_lanes(pp)], ka_ref[0, keys, :]], axis=-1)
            for e in range(2):
                s_ref[slot, e, r0:r1, :nk] = _dot_nt(qm_ref[pp, i, e, r0:r1], kk)

    def update(pp, i, j, slot):
        for r0, r1, nk in parts(i, j):
            v = v_ref[0, j * t:j * t + nk, pair_lanes(pp)]
            one = jnp.ones_like(v)
            for e in range(2):
                s = s_ref[slot, e, r0:r1, :nk]
                if j == i:
                    last = jnp.where(causal, s[:, nk - d:], NEG_BIG)
                    s = last if nk == d else jnp.concatenate([s[:, :nk - d], last], axis=-1)
                row_max = jnp.max(s, axis=-1, keepdims=True)
                m_new = (jnp.broadcast_to(row_max, (r1 - r0, LANES)) if j == 0
                         else jnp.maximum(m_ref[pp, e, r0:r1], row_max))
                p = jnp.concatenate(
                    [jnp.exp2(s[:, c * LANES:(c + 1) * LANES] - m_new)
                     for c in range(nk // LANES)], axis=-1).astype(jnp.bfloat16)
                v_lower = lax.broadcasted_iota(jnp.int32, v.shape, 1) < HEAD_DIM
                ve = jnp.where(v_lower, v, one) if e == 0 else jnp.where(v_lower, one, v)
                pv = _dot(p, ve)
                if j > 0:
                    pv = jnp.exp2(m_ref[pp, e, r0:r1] - m_new) * acc_ref[pp, e, r0:r1] + pv
                acc_ref[pp, e, r0:r1] = pv
                m_ref[pp, e, r0:r1] = m_new

    def finish(pp, i):
        outs = []
        for e in range(2):
            acc = acc_ref[pp, e]
            outs.append(acc / pltpu.roll(acc, HEAD_DIM, 1))
        o_ref[0, tile_rows(i), pair_lanes(pp)] = (
            jnp.where(lower, outs[0], outs[1]).astype(o_ref.dtype))

    n_tiles = seq // t
    work = [(pp, i, j) for pp in range(FOX_PAIRS) for i in range(n_tiles) for j in range(i + 1)]

    def issue_scores(idx):
        pp, i, j = work[idx]
        if j == 0:
            mask_queries(pp, i)
        scores(pp, i, j, idx % 2)

    issue_scores(0)
    for idx, (pp, i, j) in enumerate(work):
        if idx + 1 < len(work):
            issue_scores(idx + 1)
        update(pp, i, j, idx % 2)
        if j == i:
            finish(pp, i)


def _fox_call(fq, qa, fk, ka, fv, weights, batch, seq):
    t = FOX_T
    bf16 = jnp.bfloat16
    groups = N_PAIRS // FOX_PAIRS
    shape3 = (batch, seq, N_PAIRS * LANES)
    aug3 = (batch, seq, LANES)
    spec = pl.BlockSpec((1, seq, FOX_PAIRS * LANES), lambda b, g: (b, 0, g))
    aug_spec = pl.BlockSpec((1, seq, LANES), lambda b, g: (b, 0, 0))
    steps = batch * groups
    w_specs = [pl.BlockSpec((w.shape[0] // steps, w.shape[1]), lambda b, g: (b * groups + g, 0))
               for w in weights]
    return pl.pallas_call(
        functools.partial(_fox_kernel, seq=seq, n_cast=len(weights)),
        grid=(batch, groups),
        in_specs=[spec, aug_spec, spec, aug_spec, spec] + w_specs,
        out_specs=[spec] + w_specs,
        out_shape=[jax.ShapeDtypeStruct(shape3, bf16)]
        + [jax.ShapeDtypeStruct(w.shape, bf16) for w in weights],
        scratch_shapes=[pltpu.VMEM((FOX_PAIRS, seq // t, 2, t, 2 * LANES), jnp.bfloat16),
                        pltpu.VMEM((2, 2, t, t), jnp.float32),
                        pltpu.VMEM((FOX_PAIRS, 2, t, LANES), jnp.float32),
                        pltpu.VMEM((FOX_PAIRS, 2, t, LANES), jnp.float32)],
        compiler_params=pltpu.CompilerParams(
            dimension_semantics=("parallel", "parallel"),
            vmem_limit_bytes=VMEM_LIMIT),
        name="fox",
    )(fq.reshape(shape3), qa.reshape(aug3), fk.reshape(shape3), ka.reshape(aug3),
      fv.reshape(shape3), *weights)


def _t5_bucket(n):
    max_exact = N_BUCKETS // 2
    large = max_exact + (np.log(np.maximum(n, 1) / max_exact)
                         / np.log(MAX_DISTANCE / max_exact)
                         * (N_BUCKETS - max_exact)).astype(np.int32)
    large = np.minimum(large, N_BUCKETS - 1)
    return np.where(n < max_exact, n, large).astype(np.int32)


def _band_buckets():
    i = np.arange(Q_BLOCK)[:, None]
    j = np.arange(2 * Q_BLOCK)[None, :]
    dist = i + Q_BLOCK - j
    in_window = (dist >= 0) & (dist < WINDOW)
    return np.where(in_window, _t5_bucket(np.clip(dist, 0, None)), -1).astype(np.int32)


def _swa_kernel(bucket_ref, rel_ref, sink_ref, q_ref, k_ref, v_ref, o_ref, bias_ref, sinkc_ref,
                s_ref, *, seq):
    qb = Q_BLOCK
    kv = pl.program_id(1)

    @pl.when((pl.program_id(0) == 0) & (kv == 0))
    def _():
        bucket = bucket_ref[...]
        for h in range(N_SWA_HEADS):
            bias = jnp.full(bucket.shape, NEG_BIG, jnp.float32)
            for b in range(N_BUCKETS):
                bias = jnp.where(bucket == b, rel_ref[b, h] * LOG2E, bias)
            group, g = divmod(h, SWA_GROUP)
            bias_ref[group, g * qb:(g + 1) * qb, :] = bias
            sinkc_ref[group, g * qb:(g + 1) * qb, :] = jnp.full((qb, LANES), sink_ref[h] * LOG2E)

    lower = lax.broadcasted_iota(jnp.int32, (qb, LANES), 1) < HEAD_DIM

    def band_of(n):
        if n == 0:
            return slice(0, qb), qb, slice(qb, 2 * qb)
        return slice((n - 1) * qb, (n + 1) * qb), 2 * qb, slice(0, 2 * qb)

    def scores(n, slot):
        rows = slice(n * qb, (n + 1) * qb)
        band, width, cols = band_of(n)
        parts = []
        for pair in range(SWA_GROUP // 2):
            qp = q_ref[0, rows, pair * LANES:(pair + 1) * LANES]
            zero = jnp.zeros_like(qp)
            parts += [jnp.where(lower, qp, zero), jnp.where(lower, zero, qp)]
        s_ref[slot, :, :width] = (_dot_nt(jnp.concatenate(parts, axis=0), k_ref[0, band, :])
                                  + bias_ref[kv, :, cols])

    def update(n, slot):
        rows = slice(n * qb, (n + 1) * qb)
        band, width, _ = band_of(n)
        vb = v_ref[0, band, :]
        vb = jnp.concatenate([vb, jnp.ones_like(vb)], axis=-1)
        s = s_ref[slot, :, :width]
        sink = sinkc_ref[kv]
        m = jnp.maximum(jnp.max(s, axis=-1, keepdims=True), sink)
        p = jnp.concatenate(
            [jnp.exp2(s[:, c * LANES:(c + 1) * LANES] - m) for c in range(width // LANES)],
            axis=-1).astype(jnp.bfloat16)
        acc = _dot(p, vb)
        out = acc[:, :LANES] / (acc[:, LANES:] + jnp.exp2(sink - m))
        for pair in range(SWA_GROUP // 2):
            even = slice(2 * pair * qb, (2 * pair + 1) * qb)
            odd = slice((2 * pair + 1) * qb, (2 * pair + 2) * qb)
            o_ref[0, rows, pair * LANES:(pair + 1) * LANES] = jnp.where(
                lower, out[even], out[odd]).astype(o_ref.dtype)

    n_blocks = seq // qb
    scores(0, 0)
    for n in range(n_blocks):
        if n + 1 < n_blocks:
            scores(n + 1, (n + 1) % 2)
        update(n, n % 2)


def _swa_call(sq, sk, sv, rel_bias, sinks, batch, seq):
    bucket = _band_buckets()
    width = SWA_GROUP * HEAD_DIM
    smem = pl.BlockSpec(memory_space=pltpu.SMEM)
    q_spec = pl.BlockSpec((1, seq, width), lambda b, kv: (b, 0, kv))
    kv_spec = pl.BlockSpec((1, seq, LANES), lambda b, kv: (b, 0, kv))
    return pl.pallas_call(
        functools.partial(_swa_kernel, seq=seq),
        grid=(batch, N_SWA_KV_HEADS),
        in_specs=[pl.BlockSpec(bucket.shape, lambda b, kv: (0, 0)), smem, smem,
                  q_spec, kv_spec, kv_spec],
        out_specs=q_spec,
        out_shape=jax.ShapeDtypeStruct((batch, seq, D_SWA), jnp.bfloat16),
        scratch_shapes=[
            pltpu.VMEM((N_SWA_KV_HEADS, SWA_GROUP * Q_BLOCK, 2 * Q_BLOCK), jnp.float32),
            pltpu.VMEM((N_SWA_KV_HEADS, SWA_GROUP * Q_BLOCK, LANES), jnp.float32),
            pltpu.VMEM((2, SWA_GROUP * Q_BLOCK, 2 * Q_BLOCK), jnp.float32)],
        compiler_params=pltpu.CompilerParams(
            dimension_semantics=("arbitrary", "arbitrary"),
            vmem_limit_bytes=VMEM_LIMIT),
        name="swa",
    )(jnp.asarray(bucket), rel_bias, sinks, sq.reshape(batch, seq, D_SWA),
      sk.reshape(batch, seq, 2 * LANES), sv.reshape(batch, seq, 2 * LANES))


def _post_kernel(fox_ref, swa_ref, x_ref, p_ref, wo_ref, w1_ref, w2_ref, wg_ref, wp_ref,
                 gains_ref, o_ref):
    bf16 = jnp.bfloat16
    h_parts, m_parts = [], []
    for r in range(0, POST_TM, POST_STRIP):
        rows = slice(r, r + POST_STRIP)
        mix = _dot(jnp.concatenate([fox_ref[rows, :], swa_ref[rows, :]], axis=-1), wo_ref[...])
        hh = x_ref[rows, :] + _rms(mix) * gains_ref[0:1, :]
        h_parts.append(hh)
        m_parts.append((_rms(hh) * gains_ref[1:2, :]).astype(bf16))
    h = jnp.concatenate(h_parts, axis=0)
    m = jnp.concatenate(m_parts, axis=0)
    ple = _dot(p_ref[...].astype(bf16), wp_ref[...])

    y = jnp.zeros((POST_TM, D_MODEL), jnp.float32)
    for c in range(D_FF // FF_CHUNK):
        cols = slice(c * FF_CHUNK, (c + 1) * FF_CHUNK)
        hid = jnp.square(jnp.maximum(_dot(m, w1_ref[:, cols]), 0.0))
        y = y + _dot(hid.astype(bf16), w2_ref[cols, :])

    for r in range(0, POST_TM, POST_STRIP):
        rows = slice(r, r + POST_STRIP)
        hh = h[rows] + _rms(y[rows]) * gains_ref[2:3, :]
        gate = jax.nn.sigmoid(_dot(hh.astype(bf16), wg_ref[...]))
        o_ref[rows, :] = hh + _rms(ple[rows] * gate) * gains_ref[3:4, :]


def _post_call(fox, swa, x2, p2, wo, w1, w2, wg, wp, gains):
    n = x2.shape[0]
    row = lambda i: (i, 0)
    const = lambda i: (0, 0)

    def resident(shape):
        return pl.BlockSpec(shape, const, pipeline_mode=pl.Buffered(1))

    return pl.pallas_call(
        _post_kernel,
        grid=(n // POST_TM,),
        in_specs=[
            pl.BlockSpec((POST_TM, D_FOX), row),
            pl.BlockSpec((POST_TM, D_SWA), row),
            pl.BlockSpec((POST_TM, D_MODEL), row),
            pl.BlockSpec((POST_TM, D_PLE), row),
            resident(wo.shape), resident(w1.shape), resident(w2.shape),
            resident(wg.shape), resident(wp.shape), resident(gains.shape),
        ],
        out_specs=pl.BlockSpec((POST_TM, D_MODEL), row),
        out_shape=jax.ShapeDtypeStruct((n, D_MODEL), jnp.float32),
        compiler_params=pltpu.CompilerParams(
            dimension_semantics=("parallel",),
            vmem_limit_bytes=VMEM_LIMIT),
        name="post",
    )(fox, swa, x2, p2, wo, w1, w2, wg, wp, gains)


def _gate_lanes(v):
    return jnp.tile(v, (1, LANES // N_FOX_HEADS))


def kernel(x, p, w_in, b_forget, w_out, rel_bias, swa_sinks, g_attn_pre, g_attn_post,
           w_ff1, w_ff2, g_ff_pre, g_ff_post, w_ple, w_ple_gate, g_ple_post):
    batch, seq, _ = x.shape
    n = batch * seq
    bf16 = jnp.bfloat16
    h = x.reshape(n, D_MODEL)
    for i in range(p.shape[0]):
        b_ff = _gate_lanes(b_forget[i].reshape(1, N_FOX_HEADS))
        fq, fk, fv, qa, ka, sq, sk, sv = _proj_call(
            h, g_attn_pre[i].reshape(1, D_MODEL), w_in[i].T, b_ff, batch, seq)
        fox, wo, w1, w2, wg = _fox_call(
            fq, qa, fk, ka, fv, [w_out[i], w_ff1[i], w_ff2[i], w_ple_gate[i]], batch, seq)
        swa = _swa_call(sq, sk, sv, rel_bias, swa_sinks[i], batch, seq).reshape(n, D_SWA)
        gains = jnp.stack([g_attn_post[i], g_ff_pre[i], g_ff_post[i], g_ple_post[i]])
        h = _post_call(fox.reshape(n, D_FOX), swa, h, p[i].reshape(n, D_PLE),
                       wo, w1, w2, wg, w_ple[i].astype(bf16), gains)
    return h.reshape(batch, seq, D_MODEL)
```

```python
import functools

import jax
import jax.numpy as jnp
import numpy as np
from jax import lax
from jax.experimental import pallas as pl
from jax.experimental.pallas import tpu as pltpu

D_MODEL = 1024
HEAD_DIM = 64
N_FOX_HEADS = 8
N_SWA_HEADS = 8
N_SWA_KV_HEADS = 2
SWA_GROUP = N_SWA_HEADS // N_SWA_KV_HEADS
D_FOX = N_FOX_HEADS * HEAD_DIM
D_SWA = N_SWA_HEADS * HEAD_DIM
D_SWA_KV = N_SWA_KV_HEADS * HEAD_DIM
D_FF = 4 * D_MODEL
D_PLE = 256
WINDOW = 128
Q_BLOCK = 128
N_BUCKETS = 32
MAX_DISTANCE = 128
RMS_EPS = 1e-6

LANES = 128
N_PAIRS = N_FOX_HEADS // 2
N_SPLIT = 3
N_TERMS = HEAD_DIM // N_FOX_HEADS
TERM_SRC = (0, 0, 0, 1, 1, 1, 2, 2)
TERM_CUT = (0, 1, 2, 0, 1, 2, 0, 1)
NEG_BIG = -1e30
LOG2E = float(np.log2(np.e))

PROJ_TM = 1024
PROJ_STRIP = 512
FOX_T = 512
FOX_DIAG = 256
FOX_PAIRS = 1
POST_TM = 1024
POST_STRIP = 256
FF_CHUNK = 512
VMEM_LIMIT = 60000 * 1024

C_FQ, C_FK, C_FV = 0, D_FOX, 2 * D_FOX
C_SQ = 3 * D_FOX
C_SK = C_SQ + D_SWA
C_SV = C_SK + D_SWA_KV
C_FF = C_SV + D_SWA_KV
D_PROJ = C_FF + LANES


def _rms(v):
    return v * lax.rsqrt(jnp.mean(v * v, axis=-1, keepdims=True) + RMS_EPS)


def _dot(a, b):
    return jnp.dot(a, b, preferred_element_type=jnp.float32)


def _dot_nt(a, b):
    return lax.dot_general(a, b, (((1,), (1,)), ((), ())),
                           preferred_element_type=jnp.float32)


def _proj_kernel(x_ref, g_ref, wt_ref, bf_ref, tri_ref,
                 fq_ref, fk_ref, fv_ref, qa_ref, ka_ref, sq_ref, sk_ref, sv_ref,
                 carry_ref, w_ref):
    @pl.when((pl.program_id(0) == 0) & (pl.program_id(1) == 0))
    def _():
        bf16 = jnp.bfloat16
        q_scale = HEAD_DIM ** -0.5 * LOG2E
        src = np.cumsum([0, D_FOX, D_FOX, D_FOX, N_FOX_HEADS, D_SWA])
        src_fq, src_fk, src_fv, src_ff, src_sq, src_skv = (int(c) for c in src)
        w_ref[C_FQ:C_FQ + D_FOX] = (wt_ref[src_fq:src_fq + D_FOX] * q_scale).astype(bf16)
        w_ref[C_FK:C_FK + D_FOX] = wt_ref[src_fk:src_fk + D_FOX].astype(bf16)
        w_ref[C_FV:C_FV + D_FOX] = wt_ref[src_fv:src_fv + D_FOX].astype(bf16)
        w_ref[C_SQ:C_SQ + D_SWA] = (wt_ref[src_sq:src_sq + D_SWA] * q_scale).astype(bf16)
        w_ref[C_SK:C_SK + 2 * D_SWA_KV] = wt_ref[src_skv:src_skv + 2 * D_SWA_KV].astype(bf16)
        gate = wt_ref[src_ff:src_ff + N_FOX_HEADS]
        w_ref[C_FF:C_FF + LANES] = jnp.tile(gate, (LANES // N_FOX_HEADS, 1)).astype(bf16)

    @pl.when(pl.program_id(1) == 0)
    def _():
        carry_ref[...] = jnp.zeros_like(carry_ref)

    lane = lax.broadcasted_iota(jnp.int32, (PROJ_STRIP, LANES), 1)
    lower = lane < HEAD_DIM
    term = (lane // N_FOX_HEADS) % N_TERMS

    for r in range(0, PROJ_TM, PROJ_STRIP):
        rows = slice(r, r + PROJ_STRIP)
        a = (_rms(x_ref[rows, :]) * g_ref[...]).astype(jnp.bfloat16)

        def proj(lo, width):
            return _dot_nt(a, w_ref[lo:lo + width])

        v = proj(C_FF, LANES) + bf_ref[...]
        logf = (jnp.minimum(v, 0.0) - jnp.log1p(jnp.exp(-jnp.abs(v)))) * LOG2E
        x = _pick_piece(_split_bf16(logf), term, TERM_SRC)
        c = carry_ref[...] + _dot(tri_ref[...], x)
        carry_ref[...] = c[PROJ_STRIP - 1:PROJ_STRIP, :]
        terms = _pick_piece(_split_bf16(c), term, TERM_CUT)
        one = jnp.ones_like(terms)
        qa_ref[rows, :] = jnp.where(lower, terms, one)
        ka_ref[rows, :] = jnp.where(lower, one, -terms)

        kv = proj(C_SK, 2 * D_SWA_KV)
        for half, out_ref in enumerate((sk_ref, sv_ref)):
            z = kv[:, half * LANES:(half + 1) * LANES]
            zr = pltpu.roll(z, HEAD_DIM, 1)
            out_ref[rows, :LANES] = jnp.where(lower, z, zr).astype(jnp.bfloat16)
            out_ref[rows, LANES:] = jnp.where(lower, zr, z).astype(jnp.bfloat16)

        fq_ref[rows, :] = proj(C_FQ, D_FOX).astype(jnp.bfloat16)
        fk_ref[rows, :] = proj(C_FK, D_FOX).astype(jnp.bfloat16)
        fv_ref[rows, :] = proj(C_FV, D_FOX).astype(jnp.bfloat16)
        sq_ref[rows, :] = proj(C_SQ, D_SWA).astype(jnp.bfloat16)


def _split_bf16(v):
    pieces = []
    r = v
    for _ in range(N_SPLIT):
        t = r.astype(jnp.bfloat16)
        pieces.append(t)
        r = r - t.astype(jnp.float32)
    return pieces


def _pick_piece(pieces, term, piece_of_term):
    out = pieces[piece_of_term[-1]]
    for t in range(N_TERMS - 2, -1, -1):
        if piece_of_term[t] != piece_of_term[t + 1]:
            out = jnp.where(term <= t, pieces[piece_of_term[t]], out)
    return out


def _proj_call(x2, g_pre, w_in_t, b_ff, batch, seq):
    n = batch * seq
    steps = seq // PROJ_TM
    tri = np.tril(np.ones((PROJ_STRIP, PROJ_STRIP), np.float32))
    row = lambda b, s: (b * steps + s, 0)
    const = lambda b, s: (0, 0)
    bf16 = jnp.bfloat16

    def out(width):
        return (jax.ShapeDtypeStruct((n, width), bf16), pl.BlockSpec((PROJ_TM, width), row))

    outs = [out(D_FOX), out(D_FOX), out(D_FOX), out(LANES), out(LANES),
            out(D_SWA), out(2 * LANES), out(2 * LANES)]
    return pl.pallas_call(
        _proj_kernel,
        grid=(batch, steps),
        in_specs=[
            pl.BlockSpec((PROJ_TM, D_MODEL), row),
            pl.BlockSpec((1, D_MODEL), const),
            pl.BlockSpec(w_in_t.shape, const, pipeline_mode=pl.Buffered(1)),
            pl.BlockSpec((1, LANES), const),
            pl.BlockSpec((PROJ_STRIP, PROJ_STRIP), const),
        ],
        out_specs=[o[1] for o in outs],
        out_shape=[o[0] for o in outs],
        scratch_shapes=[pltpu.VMEM((1, LANES), jnp.float32),
                        pltpu.VMEM((D_PROJ, D_MODEL), bf16)],
        compiler_params=pltpu.CompilerParams(
            dimension_semantics=("arbitrary", "arbitrary"),
            vmem_limit_bytes=VMEM_LIMIT),
        name="proj",
    )(x2, g_pre, w_in_t, b_ff, jnp.asarray(tri, bf16))


def _fox_kernel(*refs, seq, n_cast):
    q_ref, qa_ref, k_ref, ka_ref, v_ref = refs[:5]
    w_refs = refs[5:5 + n_cast]
    o_ref = refs[5 + n_cast]
    wb_refs = refs[6 + n_cast:6 + 2 * n_cast]
    qm_ref, s_ref, m_ref, acc_ref = refs[6 + 2 * n_cast:]
    for w_ref, wb_ref in zip(w_refs, wb_refs):
        wb_ref[...] = w_ref[...].astype(wb_ref.dtype)

    t = FOX_T
    lane = lax.broadcasted_iota(jnp.int32, (t, LANES), 1)
    lower = lane < HEAD_DIM
    head_of_lane = lane % N_FOX_HEADS
    d = FOX_DIAG
    causal = (lax.broadcasted_iota(jnp.int32, (d, d), 0)
              >= lax.broadcasted_iota(jnp.int32, (d, d), 1))

    def tile_rows(i):
        return slice(i * t, (i + 1) * t)

    def pair_lanes(pp):
        return slice(pp * LANES, (pp + 1) * LANES)

    def parts(i, j):
        if j < i:
            return [(0, t, t)]
        return [(r, r + d, r + d) for r in range(0, t, d)]

    def mask_queries(pp, i):
        q = q_ref[0, tile_rows(i), pair_lanes(pp)]
        qa = qa_ref[0, tile_rows(i), :]
        zero = jnp.zeros_like(q)
        first_head = 2 * (FOX_PAIRS * pl.program_id(1) + pp)
        for e in range(2):
            qm_ref[pp, i, e] = jnp.concatenate(
                [jnp.where(lower, q, zero) if e == 0 else jnp.where(lower, zero, q),
                 jnp.where(head_of_lane == first_head + e, qa, zero)], axis=-1)

    def scores(pp, i, j, slot):
        for r0, r1, nk in parts(i, j):
            keys = slice(j * t, j * t + nk)
            kk = jnp.concatenate([k_ref[0, keys, pair_lanes(pp)], ka_ref[0, keys, :]], axis=-1)
            for e in range(2):
                s_ref[slot, e, r0:r1, :nk] = _dot_nt(qm_ref[pp, i, e, r0:r1], kk)

    def update(pp, i, j, slot):
        for r0, r1, nk in parts(i, j):
            v = v_ref[0, j * t:j * t + nk, pair_lanes(pp)]
            one = jnp.ones_like(v)
            for e in range(2):
                s = s_ref[slot, e, r0:r1, :nk]
                if j == i:
                    last = jnp.where(causal, s[:, nk - d:], NEG_BIG)
                    s = last if nk == d else jnp.concatenate([s[:, :nk - d], last], axis=-1)
                row_max = jnp.max(s, axis=-1, keepdims=True)
                m_new = (jnp.broadcast_to(row_max, (r1 - r0, LANES)) if j == 0
                         else jnp.maximum(m_ref[pp, e, r0:r1], row_max))
                p = jnp.concatenate(
                    [jnp.exp2(s[:, c * LANES:(c + 1) * LANES] - m_new)
                     for c in range(nk // LANES)], axis=-1).astype(jnp.bfloat16)
                v_lower = lax.broadcasted_iota(jnp.int32, v.shape, 1) < HEAD_DIM
                ve = jnp.where(v_lower, v, one) if e == 0 else jnp.where(v_lower, one, v)
                pv = _dot(p, ve)
                if j > 0:
                    pv = jnp.exp2(m_ref[pp, e, r0:r1] - m_new) * acc_ref[pp, e, r0:r1] + pv
                acc_ref[pp, e, r0:r1] = pv
                m_ref[pp, e, r0:r1] = m_new

    def finish(pp, i):
        outs = []
        for e in range(2):
            acc = acc_ref[pp, e]
            outs.append(acc / pltpu.roll(acc, HEAD_DIM, 1))
        o_ref[0, tile_rows(i), pair_lanes(pp)] = (
            jnp.where(lower, outs[0], outs[1]).astype(o_ref.dtype))

    n_tiles = seq // t
    work = [(pp, i, j) for pp in range(FOX_PAIRS) for i in range(n_tiles) for j in range(i + 1)]

    def issue_scores(idx):
        pp, i, j = work[idx]
        if j == 0:
            mask_queries(pp, i)
        scores(pp, i, j, idx % 2)

    issue_scores(0)
    for idx, (pp, i, j) in enumerate(work):
        if idx + 1 < len(work):
            issue_scores(idx + 1)
        update(pp, i, j, idx % 2)
        if j == i:
            finish(pp, i)


def _fox_call(fq, qa, fk, ka, fv, weights, batch, seq):
    t = FOX_T
    bf16 = jnp.bfloat16
    groups = N_PAIRS // FOX_PAIRS
    shape3 = (batch, seq, N_PAIRS * LANES)
    aug3 = (batch, seq, LANES)
    spec = pl.BlockSpec((1, seq, FOX_PAIRS * LANES), lambda b, g: (b, 0, g))
    aug_spec = pl.BlockSpec((1, seq, LANES), lambda b, g: (b, 0, 0))
    steps = batch * groups
    w_specs = [pl.BlockSpec((w.shape[0] // steps, w.shape[1]), lambda b, g: (b * groups + g, 0))
               for w in weights]
    return pl.pallas_call(
        functools.partial(_fox_kernel, seq=seq, n_cast=len(weights)),
        grid=(batch, groups),
        in_specs=[spec, aug_spec, spec, aug_spec, spec] + w_specs,
        out_specs=[spec] + w_specs,
        out_shape=[jax.ShapeDtypeStruct(shape3, bf16)]
        + [jax.ShapeDtypeStruct(w.shape, bf16) for w in weights],
        scratch_shapes=[pltpu.VMEM((FOX_PAIRS, seq // t, 2, t, 2 * LANES), jnp.bfloat16),
                        pltpu.VMEM((2, 2, t, t), jnp.float32),
                        pltpu.VMEM((FOX_PAIRS, 2, t, LANES), jnp.float32),
                        pltpu.VMEM((FOX_PAIRS, 2, t, LANES), jnp.float32)],
        compiler_params=pltpu.CompilerParams(
            dimension_semantics=("parallel", "parallel"),
            vmem_limit_bytes=VMEM_LIMIT),
        name="fox",
    )(fq.reshape(shape3), qa.reshape(aug3), fk.reshape(shape3), ka.reshape(aug3),
      fv.reshape(shape3), *weights)


def _t5_bucket(n):
    max_exact = N_BUCKETS // 2
    large = max_exact + (np.log(np.maximum(n, 1) / max_exact)
                         / np.log(MAX_DISTANCE / max_exact)
                         * (N_BUCKETS - max_exact)).astype(np.int32)
    large = np.minimum(large, N_BUCKETS - 1)
    return np.where(n < max_exact, n, large).astype(np.int32)


def _band_buckets():
    i = np.arange(Q_BLOCK)[:, None]
    j = np.arange(2 * Q_BLOCK)[None, :]
    dist = i + Q_BLOCK - j
    in_window = (dist >= 0) & (dist < WINDOW)
    return np.where(in_window, _t5_bucket(np.clip(dist, 0, None)), -1).astype(np.int32)


def _swa_kernel(bucket_ref, rel_ref, sink_ref, q_ref, k_ref, v_ref, o_ref, bias_ref, sinkc_ref,
                s_ref, *, seq):
    qb = Q_BLOCK
    kv = pl.program_id(1)

    @pl.when((pl.program_id(0) == 0) & (kv == 0))
    def _():
        bucket = bucket_ref[...]
        for h in range(N_SWA_HEADS):
            bias = jnp.full(bucket.shape, NEG_BIG, jnp.float32)
            for b in range(N_BUCKETS):
                bias = jnp.where(bucket == b, rel_ref[b, h] * LOG2E, bias)
            group, g = divmod(h, SWA_GROUP)
            bias_ref[group, g * qb:(g + 1) * qb, :] = bias
            sinkc_ref[group, g * qb:(g + 1) * qb, :] = jnp.full((qb, LANES), sink_ref[h] * LOG2E)

    lower = lax.broadcasted_iota(jnp.int32, (qb, LANES), 1) < HEAD_DIM

    def band_of(n):
        if n == 0:
            return slice(0, qb), qb, slice(qb, 2 * qb)
        return slice((n - 1) * qb, (n + 1) * qb), 2 * qb, slice(0, 2 * qb)

    def scores(n, slot):
        rows = slice(n * qb, (n + 1) * qb)
        band, width, cols = band_of(n)
        parts = []
        for pair in range(SWA_GROUP // 2):
            qp = q_ref[0, rows, pair * LANES:(pair + 1) * LANES]
            zero = jnp.zeros_like(qp)
            parts += [jnp.where(lower, qp, zero), jnp.where(lower, zero, qp)]
        s_ref[slot, :, :width] = (_dot_nt(jnp.concatenate(parts, axis=0), k_ref[0, band, :])
                                  + bias_ref[kv, :, cols])

    def update(n, slot):
        rows = slice(n * qb, (n + 1) * qb)
        band, width, _ = band_of(n)
        vb = v_ref[0, band, :]
        vb = jnp.concatenate([vb, jnp.ones_like(vb)], axis=-1)
        s = s_ref[slot, :, :width]
        sink = sinkc_ref[kv]
        m = jnp.maximum(jnp.max(s, axis=-1, keepdims=True), sink)
        p = jnp.concatenate(
            [jnp.exp2(s[:, c * LANES:(c + 1) * LANES] - m) for c in range(width // LANES)],
            axis=-1).astype(jnp.bfloat16)
        acc = _dot(p, vb)
        out = acc[:, :LANES] / (acc[:, LANES:] + jnp.exp2(sink - m))
        for pair in range(SWA_GROUP // 2):
            even = slice(2 * pair * qb, (2 * pair + 1) * qb)
            odd = slice((2 * pair + 1) * qb, (2 * pair + 2) * qb)
            o_ref[0, rows, pair * LANES:(pair + 1) * LANES] = jnp.where(
                lower, out[even], out[odd]).astype(o_ref.dtype)

    n_blocks = seq // qb
    scores(0, 0)
    for n in range(n_blocks):
        if n + 1 < n_blocks:
            scores(n + 1, (n + 1) % 2)
        update(n, n % 2)


def _swa_call(sq, sk, sv, rel_bias, sinks, batch, seq):
    bucket = _band_buckets()
    width = SWA_GROUP * HEAD_DIM
    smem = pl.BlockSpec(memory_space=pltpu.SMEM)
    q_spec = pl.BlockSpec((1, seq, width), lambda b, kv: (b, 0, kv))
    kv_spec = pl.BlockSpec((1, seq, LANES), lambda b, kv: (b, 0, kv))
    return pl.pallas_call(
        functools.partial(_swa_kernel, seq=seq),
        grid=(batch, N_SWA_KV_HEADS),
        in_specs=[pl.BlockSpec(bucket.shape, lambda b, kv: (0, 0)), smem, smem,
                  q_spec, kv_spec, kv_spec],
        out_specs=q_spec,
        out_shape=jax.ShapeDtypeStruct((batch, seq, D_SWA), jnp.bfloat16),
        scratch_shapes=[
            pltpu.VMEM((N_SWA_KV_HEADS, SWA_GROUP * Q_BLOCK, 2 * Q_BLOCK), jnp.float32),
            pltpu.VMEM((N_SWA_KV_HEADS, SWA_GROUP * Q_BLOCK, LANES), jnp.float32),
            pltpu.VMEM((2, SWA_GROUP * Q_BLOCK, 2 * Q_BLOCK), jnp.float32)],
        compiler_params=pltpu.CompilerParams(
            dimension_semantics=("arbitrary", "arbitrary"),
            vmem_limit_bytes=VMEM_LIMIT),
        name="swa",
    )(jnp.asarray(bucket), rel_bias, sinks, sq.reshape(batch, seq, D_SWA),
      sk.reshape(batch, seq, 2 * LANES), sv.reshape(batch, seq, 2 * LANES))


def _post_kernel(fox_ref, swa_ref, x_ref, p_ref, wo_ref, w1_ref, w2_ref, wg_ref, wp_ref,
                 gains_ref, o_ref):
    bf16 = jnp.bfloat16
    h_parts, m_parts = [], []
    for r in range(0, POST_TM, POST_STRIP):
        rows = slice(r, r + POST_STRIP)
        mix = _dot(jnp.concatenate([fox_ref[rows, :], swa_ref[rows, :]], axis=-1), wo_ref[...])
        hh = x_ref[rows, :] + _rms(mix) * gains_ref[0:1, :]
        h_parts.append(hh)
        m_parts.append((_rms(hh) * gains_ref[1:2, :]).astype(bf16))
    h = jnp.concatenate(h_parts, axis=0)
    m = jnp.concatenate(m_parts, axis=0)
    ple = _dot(p_ref[...].astype(bf16), wp_ref[...])

    y = jnp.zeros((POST_TM, D_MODEL), jnp.float32)
    for c in range(D_FF // FF_CHUNK):
        cols = slice(c * FF_CHUNK, (c + 1) * FF_CHUNK)
        hid = jnp.square(jnp.maximum(_dot(m, w1_ref[:, cols]), 0.0))
        y = y + _dot(hid.astype(bf16), w2_ref[cols, :])

    for r in range(0, POST_TM, POST_STRIP):
        rows = slice(r, r + POST_STRIP)
        hh = h[rows] + _rms(y[rows]) * gains_ref[2:3, :]
        gate = jax.nn.sigmoid(_dot(hh.astype(bf16), wg_ref[...]))
        o_ref[rows, :] = hh + _rms(ple[rows] * gate) * gains_ref[3:4, :]


def _post_call(fox, swa, x2, p2, wo, w1, w2, wg, wp, gains):
    n = x2.shape[0]
    row = lambda i: (i, 0)
    const = lambda i: (0, 0)

    def resident(shape):
        return pl.BlockSpec(shape, const, pipeline_mode=pl.Buffered(1))

    return pl.pallas_call(
        _post_kernel,
        grid=(n // POST_TM,),
        in_specs=[
            pl.BlockSpec((POST_TM, D_FOX), row),
            pl.BlockSpec((POST_TM, D_SWA), row),
            pl.BlockSpec((POST_TM, D_MODEL), row),
            pl.BlockSpec((POST_TM, D_PLE), row),
            resident(wo.shape), resident(w1.shape), resident(w2.shape),
            resident(wg.shape), resident(wp.shape), resident(gains.shape),
        ],
        out_specs=pl.BlockSpec((POST_TM, D_MODEL), row),
        out_shape=jax.ShapeDtypeStruct((n, D_MODEL), jnp.float32),
        compiler_params=pltpu.CompilerParams(
            dimension_semantics=("parallel",),
            vmem_limit_bytes=VMEM_LIMIT),
        name="post",
    )(fox, swa, x2, p2, wo, w1, w2, wg, wp, gains)


def _gate_lanes(v):
    return jnp.tile(v, (1, LANES // N_FOX_HEADS))


def kernel(x, p, w_in, b_forget, w_out, rel_bias, swa_sinks, g_attn_pre, g_attn_post,
           w_ff1, w_ff2, g_ff_pre, g_ff_post, w_ple, w_ple_gate, g_ple_post):
    batch, seq, _ = x.shape
    n = batch * seq
    bf16 = jnp.bfloat16
    h = x.reshape(n, D_MODEL)
    for i in range(p.shape[0]):
        b_ff = _gate_lanes(b_forget[i].reshape(1, N_FOX_HEADS))
        fq, fk, fv, qa, ka, sq, sk, sv = _proj_call(
            h, g_attn_pre[i].reshape(1, D_MODEL), w_in[i].T, b_ff, batch, seq)
        fox, wo, w1, w2, wg = _fox_call(
            fq, qa, fk, ka, fv, [w_out[i], w_ff1[i], w_ff2[i], w_ple_gate[i]], batch, seq)
        swa = _swa_call(sq, sk, sv, rel_bias, swa_sinks[i], batch, seq).reshape(n, D_SWA)
        gains = jnp.stack([g_attn_post[i], g_ff_pre[i], g_ff_post[i], g_ple_post[i]])
        h = _post_call(fox.reshape(n, D_FOX), swa, h, p[i].reshape(n, D_PLE),
                       wo, w1, w2, wg, w_ple[i].astype(bf16), gains)
    return h.reshape(batch, seq, D_MODEL)
```

```python
import functools

import jax
import jax.numpy as jnp
import numpy as np
from jax import lax
from jax.experimental import pallas as pl
from jax.experimental.pallas import tpu as pltpu

D_MODEL = 1024
HEAD_DIM = 64
N_FOX_HEADS = 8
N_SWA_HEADS = 8
N_SWA_KV_HEADS = 2
SWA_GROUP = N_SWA_HEADS // N_SWA_KV_HEADS
D_FOX = N_FOX_HEADS * HEAD_DIM
D_SWA = N_SWA_HEADS * HEAD_DIM
D_SWA_KV = N_SWA_KV_HEADS * HEAD_DIM
D_FF = 4 * D_MODEL
D_PLE = 256
WINDOW = 128
Q_BLOCK = 128
N_BUCKETS = 32
MAX_DISTANCE = 128
RMS_EPS = 1e-6

LANES = 128
N_PAIRS = N_FOX_HEADS // 2
N_SPLIT = 3
N_TERMS = HEAD_DIM // N_FOX_HEADS
TERM_SRC = (0, 0, 0, 1, 1, 1, 2, 2)
TERM_CUT = (0, 1, 2, 0, 1, 2, 0, 1)
NEG_BIG = -1e30
LOG2E = float(np.log2(np.e))

PROJ_TM = 1024
PROJ_STRIP = 512
FOX_T = 512
FOX_DIAG = 256
FOX_PAIRS = 1
POST_TM = 512
POST_STRIP = 256
FF_CHUNK = 1024
VMEM_LIMIT = 56 * 1024 * 1024

C_FQ, C_FK, C_FV = 0, D_FOX, 2 * D_FOX
C_SQ = 3 * D_FOX
C_SK = C_SQ + D_SWA
C_SV = C_SK + D_SWA_KV
C_FF = C_SV + D_SWA_KV
D_PROJ = C_FF + LANES


def _rms(v):
    return v * lax.rsqrt(jnp.mean(v * v, axis=-1, keepdims=True) + RMS_EPS)


def _dot(a, b):
    return jnp.dot(a, b, preferred_element_type=jnp.float32)


def _dot_nt(a, b):
    return lax.dot_general(a, b, (((1,), (1,)), ((), ())),
                           preferred_element_type=jnp.float32)


def _proj_kernel(x_ref, g_ref, wt_ref, bf_ref, tri_ref,
                 fq_ref, fk_ref, fv_ref, qa_ref, ka_ref, sq_ref, sk_ref, sv_ref,
                 carry_ref, w_ref):
    @pl.when((pl.program_id(0) == 0) & (pl.program_id(1) == 0))
    def _():
        bf16 = jnp.bfloat16
        q_scale = HEAD_DIM ** -0.5 * LOG2E
        src = np.cumsum([0, D_FOX, D_FOX, D_FOX, N_FOX_HEADS, D_SWA])
        src_fq, src_fk, src_fv, src_ff, src_sq, src_skv = (int(c) for c in src)
        w_ref[C_FQ:C_FQ + D_FOX] = (wt_ref[src_fq:src_fq + D_FOX] * q_scale).astype(bf16)
        w_ref[C_FK:C_FK + D_FOX] = wt_ref[src_fk:src_fk + D_FOX].astype(bf16)
        w_ref[C_FV:C_FV + D_FOX] = wt_ref[src_fv:src_fv + D_FOX].astype(bf16)
        w_ref[C_SQ:C_SQ + D_SWA] = (wt_ref[src_sq:src_sq + D_SWA] * q_scale).astype(bf16)
        w_ref[C_SK:C_SK + 2 * D_SWA_KV] = wt_ref[src_skv:src_skv + 2 * D_SWA_KV].astype(bf16)
        gate = wt_ref[src_ff:src_ff + N_FOX_HEADS]
        w_ref[C_FF:C_FF + LANES] = jnp.tile(gate, (LANES // N_FOX_HEADS, 1)).astype(bf16)

    @pl.when(pl.program_id(1) == 0)
    def _():
        carry_ref[...] = jnp.zeros_like(carry_ref)

    lane = lax.broadcasted_iota(jnp.int32, (PROJ_STRIP, LANES), 1)
    lower = lane < HEAD_DIM
    term = (lane // N_FOX_HEADS) % N_TERMS

    for r in range(0, PROJ_TM, PROJ_STRIP):
        rows = slice(r, r + PROJ_STRIP)
        a = (_rms(x_ref[rows, :]) * g_ref[...]).astype(jnp.bfloat16)

        def proj(lo, width):
            return _dot_nt(a, w_ref[lo:lo + width])

        v = proj(C_FF, LANES) + bf_ref[...]
        logf = (jnp.minimum(v, 0.0) - jnp.log1p(jnp.exp(-jnp.abs(v)))) * LOG2E
        x = _pick_piece(_split_bf16(logf), term, TERM_SRC)
        c = carry_ref[...] + _dot(tri_ref[...], x)
        carry_ref[...] = c[PROJ_STRIP - 1:PROJ_STRIP, :]
        terms = _pick_piece(_split_bf16(c), term, TERM_CUT)
        one = jnp.ones_like(terms)
        qa_ref[rows, :] = jnp.where(lower, terms, one)
        ka_ref[rows, :] = jnp.where(lower, one, -terms)

        kv = proj(C_SK, 2 * D_SWA_KV)
        for half, out_ref in enumerate((sk_ref, sv_ref)):
            z = kv[:, half * LANES:(half + 1) * LANES]
            zr = pltpu.roll(z, HEAD_DIM, 1)
            out_ref[rows, :LANES] = jnp.where(lower, z, zr).astype(jnp.bfloat16)
            out_ref[rows, LANES:] = jnp.where(lower, zr, z).astype(jnp.bfloat16)

        fq_ref[rows, :] = proj(C_FQ, D_FOX).astype(jnp.bfloat16)
        fk_ref[rows, :] = proj(C_FK, D_FOX).astype(jnp.bfloat16)
        fv_ref[rows, :] = proj(C_FV, D_FOX).astype(jnp.bfloat16)
        sq_ref[rows, :] = proj(C_SQ, D_SWA).astype(jnp.bfloat16)


def _split_bf16(v):
    pieces = []
    r = v
    for _ in range(N_SPLIT):
        t = r.astype(jnp.bfloat16)
        pieces.append(t)
        r = r - t.astype(jnp.float32)
    return pieces


def _pick_piece(pieces, term, piece_of_term):
    out = pieces[piece_of_term[-1]]
    for t in range(N_TERMS - 2, -1, -1):
        if piece_of_term[t] != piece_of_term[t + 1]:
            out = jnp.where(term <= t, pieces[piece_of_term[t]], out)
    return out


def _proj_call(x2, g_pre, w_in_t, b_ff, batch, seq):
    n = batch * seq
    steps = seq // PROJ_TM
    tri = np.tril(np.ones((PROJ_STRIP, PROJ_STRIP), np.float32))
    row = lambda b, s: (b * steps + s, 0)
    const = lambda b, s: (0, 0)
    bf16 = jnp.bfloat16

    def out(width):
        return (jax.ShapeDtypeStruct((n, width), bf16), pl.BlockSpec((PROJ_TM, width), row))

    outs = [out(D_FOX), out(D_FOX), out(D_FOX), out(LANES), out(LANES),
            out(D_SWA), out(2 * LANES), out(2 * LANES)]
    return pl.pallas_call(
        _proj_kernel,
        grid=(batch, steps),
        in_specs=[
            pl.BlockSpec((PROJ_TM, D_MODEL), row),
            pl.BlockSpec((1, D_MODEL), const),
            pl.BlockSpec(w_in_t.shape, const, pipeline_mode=pl.Buffered(1)),
            pl.BlockSpec((1, LANES), const),
            pl.BlockSpec((PROJ_STRIP, PROJ_STRIP), const),
        ],
        out_specs=[o[1] for o in outs],
        out_shape=[o[0] for o in outs],
        scratch_shapes=[pltpu.VMEM((1, LANES), jnp.float32),
                        pltpu.VMEM((D_PROJ, D_MODEL), bf16)],
        compiler_params=pltpu.CompilerParams(
            dimension_semantics=("arbitrary", "arbitrary"),
            vmem_limit_bytes=VMEM_LIMIT),
        name="proj",
    )(x2, g_pre, w_in_t, b_ff, jnp.asarray(tri, bf16))


def _fox_kernel(*refs, seq, n_cast):
    q_ref, qa_ref, k_ref, ka_ref, v_ref = refs[:5]
    w_refs = refs[5:5 + n_cast]
    o_ref = refs[5 + n_cast]
    wb_refs = refs[6 + n_cast:6 + 2 * n_cast]
    qm_ref, s_ref, m_ref, acc_ref = refs[6 + 2 * n_cast:]
    for w_ref, wb_ref in zip(w_refs, wb_refs):
        wb_ref[...] = w_ref[...].astype(wb_ref.dtype)

    t = FOX_T
    lane = lax.broadcasted_iota(jnp.int32, (t, LANES), 1)
    lower = lane < HEAD_DIM
    head_of_lane = lane % N_FOX_HEADS
    d = FOX_DIAG
    causal = (lax.broadcasted_iota(jnp.int32, (d, d), 0)
              >= lax.broadcasted_iota(jnp.int32, (d, d), 1))

    def tile_rows(i):
        return slice(i * t, (i + 1) * t)

    def pair_lanes(pp):
        return slice(pp * LANES, (pp + 1) * LANES)

    def parts(i, j):
        if j < i:
            return [(0, t, t)]
        return [(r, r + d, r + d) for r in range(0, t, d)]

    def mask_queries(pp, i):
        q = q_ref[0, tile_rows(i), pair_lanes(pp)]
        qa = qa_ref[0, tile_rows(i), :]
        zero = jnp.zeros_like(q)
        first_head = 2 * (FOX_PAIRS * pl.program_id(1) + pp)
        for e in range(2):
            qm_ref[pp, i, e] = jnp.concatenate(
                [jnp.where(lower, q, zero) if e == 0 else jnp.where(lower, zero, q),
                 jnp.where(head_of_lane == first_head + e, qa, zero)], axis=-1)

    def scores(pp, i, j, slot):
        for r0, r1, nk in parts(i, j):
            keys = slice(j * t, j * t + nk)
            kk = jnp.concatenate([k_ref[0, keys, pair_lanes(pp)], ka_ref[0, keys, :]], axis=-1)
            for e in range(2):
                s_ref[slot, e, r0:r1, :nk] = _dot_nt(qm_ref[pp, i, e, r0:r1], kk)

    def update(pp, i, j, slot):
        for r0, r1, nk in parts(i, j):
            v = v_ref[0, j * t:j * t + nk, pair_lanes(pp)]
            one = jnp.ones_like(v)
            for e in range(2):
                s = s_ref[slot, e, r0:r1, :nk]
                if j == i:
                    last = jnp.where(causal, s[:, nk - d:], NEG_BIG)
                    s = last if nk == d else jnp.concatenate([s[:, :nk - d], last], axis=-1)
                row_max = jnp.max(s, axis=-1, keepdims=True)
                m_new = (jnp.broadcast_to(row_max, (r1 - r0, LANES)) if j == 0
                         else jnp.maximum(m_ref[pp, e, r0:r1], row_max))
                p = jnp.concatenate(
                    [jnp.exp2(s[:, c * LANES:(c + 1) * LANES] - m_new)
                     for c in range(nk // LANES)], axis=-1).astype(jnp.bfloat16)
                v_lower = lax.broadcasted_iota(jnp.int32, v.shape, 1) < HEAD_DIM
                ve = jnp.where(v_lower, v, one) if e == 0 else jnp.where(v_lower, one, v)
                pv = _dot(p, ve)
                if j > 0:
                    pv = jnp.exp2(m_ref[pp, e, r0:r1] - m_new) * acc_ref[pp, e, r0:r1] + pv
                acc_ref[pp, e, r0:r1] = pv
                m_ref[pp, e, r0:r1] = m_new

    def finish(pp, i):
        outs = []
        for e in range(2):
            acc = acc_ref[pp, e]
            outs.append(acc / pltpu.roll(acc, HEAD_DIM, 1))
        o_ref[0, tile_rows(i), pair_lanes(pp)] = (
            jnp.where(lower, outs[0], outs[1]).astype(o_ref.dtype))

    n_tiles = seq // t
    work = [(pp, i, j) for pp in range(FOX_PAIRS) for i in range(n_tiles) for j in range(i + 1)]

    def issue_scores(idx):
        pp, i, j = work[idx]
        if j == 0:
            mask_queries(pp, i)
        scores(pp, i, j, idx % 2)

    issue_scores(0)
    for idx, (pp, i, j) in enumerate(work):
        if idx + 1 < len(work):
            issue_scores(idx + 1)
        update(pp, i, j, idx % 2)
        if j == i:
            finish(pp, i)


def _fox_call(fq, qa, fk, ka, fv, weights, batch, seq):
    t = FOX_T
    bf16 = jnp.bfloat16
    groups = N_PAIRS // FOX_PAIRS
    shape3 = (batch, seq, N_PAIRS * LANES)
    aug3 = (batch, seq, LANES)
    spec = pl.BlockSpec((1, seq, FOX_PAIRS * LANES), lambda b, g: (b, 0, g))
    aug_spec = pl.BlockSpec((1, seq, LANES), lambda b, g: (b, 0, 0))
    steps = batch * groups
    w_specs = [pl.BlockSpec((w.shape[0] // steps, w.shape[1]), lambda b, g: (b * groups + g, 0))
               for w in weights]
    return pl.pallas_call(
        functools.partial(_fox_kernel, seq=seq, n_cast=len(weights)),
        grid=(batch, groups),
        in_specs=[spec, aug_spec, spec, aug_spec, spec] + w_specs,
        out_specs=[spec] + w_specs,
        out_shape=[jax.ShapeDtypeStruct(shape3, bf16)]
        + [jax.ShapeDtypeStruct(w.shape, bf16) for w in weights],
        scratch_shapes=[pltpu.VMEM((FOX_PAIRS, seq // t, 2, t, 2 * LANES), jnp.bfloat16),
                        pltpu.VMEM((2, 2, t, t), jnp.float32),
                        pltpu.VMEM((FOX_PAIRS, 2, t, LANES), jnp.float32),
                        pltpu.VMEM((FOX_PAIRS, 2, t, LANES), jnp.float32)],
        compiler_params=pltpu.CompilerParams(
            dimension_semantics=("parallel", "parallel"),
            vmem_limit_bytes=VMEM_LIMIT),
        name="fox",
    )(fq.reshape(shape3), qa.reshape(aug3), fk.reshape(shape3), ka.reshape(aug3),
      fv.reshape(shape3), *weights)


def _t5_bucket(n):
    max_exact = N_BUCKETS // 2
    large = max_exact + (np.log(np.maximum(n, 1) / max_exact)
                         / np.log(MAX_DISTANCE / max_exact)
                         * (N_BUCKETS - max_exact)).astype(np.int32)
    large = np.minimum(large, N_BUCKETS - 1)
    return np.where(n < max_exact, n, large).astype(np.int32)


def _band_buckets():
    i = np.arange(Q_BLOCK)[:, None]
    j = np.arange(2 * Q_BLOCK)[None, :]
    dist = i + Q_BLOCK - j
    in_window = (dist >= 0) & (dist < WINDOW)
    return np.where(in_window, _t5_bucket(np.clip(dist, 0, None)), -1).astype(np.int32)


def _swa_kernel(bucket_ref, rel_ref, sink_ref, q_ref, k_ref, v_ref, o_ref, bias_ref, sinkc_ref,
                s_ref, *, seq):
    qb = Q_BLOCK
    kv = pl.program_id(1)

    @pl.when((pl.program_id(0) == 0) & (kv == 0))
    def _():
        bucket = bucket_ref[...]
        for h in range(N_SWA_HEADS):
            bias = jnp.full(bucket.shape, NEG_BIG, jnp.float32)
            for b in range(N_BUCKETS):
                bias = jnp.where(bucket == b, rel_ref[b, h] * LOG2E, bias)
            bias = jnp.where(lax.broadcasted_iota(jnp.int32, bucket.shape, 1) == 0,
                             sink_ref[h] * LOG2E, bias)
            group, g = divmod(h, SWA_GROUP)
            bias_ref[group, g * qb:(g + 1) * qb, :] = bias
            sinkc_ref[group, g * qb:(g + 1) * qb, :] = jnp.full((qb, LANES), sink_ref[h] * LOG2E)

    lower = lax.broadcasted_iota(jnp.int32, (qb, LANES), 1) < HEAD_DIM

    def band_of(n):
        if n == 0:
            return slice(0, qb), qb, slice(qb, 2 * qb)
        return slice((n - 1) * qb, (n + 1) * qb), 2 * qb, slice(0, 2 * qb)

    def scores(n, slot):
        rows = slice(n * qb, (n + 1) * qb)
        band, width, cols = band_of(n)
        parts = []
        for pair in range(SWA_GROUP // 2):
            qp = q_ref[0, rows, pair * LANES:(pair + 1) * LANES]
            zero = jnp.zeros_like(qp)
            parts += [jnp.where(lower, qp, zero), jnp.where(lower, zero, qp)]
        s_ref[slot, :, :width] = (_dot_nt(jnp.concatenate(parts, axis=0), sink_slot(n, k_ref))
                                  + bias_ref[kv, :, cols])

    def sink_slot(n, ref):
        band, _, _ = band_of(n)
        x = ref[0, band, :]
        if n == 0:
            return x
        return jnp.where(lax.broadcasted_iota(jnp.int32, x.shape, 0) == 0, jnp.zeros_like(x), x)

    def update(n, slot):
        rows = slice(n * qb, (n + 1) * qb)
        _, width, _ = band_of(n)
        vb = sink_slot(n, v_ref)
        vb = jnp.concatenate([vb, jnp.ones_like(vb)], axis=-1)
        s = s_ref[slot, :, :width]
        m = jnp.max(s, axis=-1, keepdims=True)
        if n == 0:
            sink = sinkc_ref[kv]
            m = jnp.maximum(m, sink)
        p = jnp.concatenate(
            [jnp.exp2(s[:, c * LANES:(c + 1) * LANES] - m) for c in range(width // LANES)],
            axis=-1).astype(jnp.bfloat16)
        acc = _dot(p, vb)
        denom = acc[:, LANES:] + jnp.exp2(sink - m) if n == 0 else acc[:, LANES:]
        out = acc[:, :LANES] / denom
        for pair in range(SWA_GROUP // 2):
            even = slice(2 * pair * qb, (2 * pair + 1) * qb)
            odd = slice((2 * pair + 1) * qb, (2 * pair + 2) * qb)
            o_ref[0, rows, pair * LANES:(pair + 1) * LANES] = jnp.where(
                lower, out[even], out[odd]).astype(o_ref.dtype)

    n_blocks = seq // qb
    scores(0, 0)
    for n in range(n_blocks):
        if n + 1 < n_blocks:
            scores(n + 1, (n + 1) % 2)
        update(n, n % 2)


def _swa_call(sq, sk, sv, rel_bias, sinks, batch, seq):
    bucket = _band_buckets()
    width = SWA_GROUP * HEAD_DIM
    smem = pl.BlockSpec(memory_space=pltpu.SMEM)
    q_spec = pl.BlockSpec((1, seq, width), lambda b, kv: (b, 0, kv))
    kv_spec = pl.BlockSpec((1, seq, LANES), lambda b, kv: (b, 0, kv))
    return pl.pallas_call(
        functools.partial(_swa_kernel, seq=seq),
        grid=(batch, N_SWA_KV_HEADS),
        in_specs=[pl.BlockSpec(bucket.shape, lambda b, kv: (0, 0)), smem, smem,
                  q_spec, kv_spec, kv_spec],
        out_specs=q_spec,
        out_shape=jax.ShapeDtypeStruct((batch, seq, D_SWA), jnp.bfloat16),
        scratch_shapes=[
            pltpu.VMEM((N_SWA_KV_HEADS, SWA_GROUP * Q_BLOCK, 2 * Q_BLOCK), jnp.float32),
            pltpu.VMEM((N_SWA_KV_HEADS, SWA_GROUP * Q_BLOCK, LANES), jnp.float32),
            pltpu.VMEM((2, SWA_GROUP * Q_BLOCK, 2 * Q_BLOCK), jnp.float32)],
        compiler_params=pltpu.CompilerParams(
            dimension_semantics=("arbitrary", "arbitrary"),
            vmem_limit_bytes=VMEM_LIMIT),
        name="swa",
    )(jnp.asarray(bucket), rel_bias, sinks, sq.reshape(batch, seq, D_SWA),
      sk.reshape(batch, seq, 2 * LANES), sv.reshape(batch, seq, 2 * LANES))


def _post_kernel(fox_ref, swa_ref, x_ref, p_ref, wo_ref, w1_ref, w2_ref, wg_ref, wp_ref,
                 gains_ref, o_ref):
    bf16 = jnp.bfloat16
    h_parts, m_parts = [], []
    for r in range(0, POST_TM, POST_STRIP):
        rows = slice(r, r + POST_STRIP)
        mix = _dot(jnp.concatenate([fox_ref[rows, :], swa_ref[rows, :]], axis=-1), wo_ref[...])
        hh = x_ref[rows, :] + _rms(mix) * gains_ref[0:1, :]
        h_parts.append(hh)
        m_parts.append((_rms(hh) * gains_ref[1:2, :]).astype(bf16))
    h = jnp.concatenate(h_parts, axis=0)
    m = jnp.concatenate(m_parts, axis=0)
    ple = _dot(p_ref[...].astype(bf16), wp_ref[...])

    y = jnp.zeros((POST_TM, D_MODEL), jnp.float32)
    for c in range(D_FF // FF_CHUNK):
        cols = slice(c * FF_CHUNK, (c + 1) * FF_CHUNK)
        hid = jnp.square(jnp.maximum(_dot(m, w1_ref[:, cols]), 0.0))
        y = y + _dot(hid.astype(bf16), w2_ref[cols, :])

    for r in range(0, POST_TM, POST_STRIP):
        rows = slice(r, r + POST_STRIP)
        hh = h[rows] + _rms(y[rows]) * gains_ref[2:3, :]
        gate = jax.nn.sigmoid(_dot(hh.astype(bf16), wg_ref[...]))
        o_ref[rows, :] = hh + _rms(ple[rows] * gate) * gains_ref[3:4, :]


def _post_call(fox, swa, x2, p2, wo, w1, w2, wg, wp, gains):
    n = x2.shape[0]
    row = lambda i: (i, 0)
    const = lambda i: (0, 0)

    def resident(shape):
        return pl.BlockSpec(shape, const, pipeline_mode=pl.Buffered(1))

    return pl.pallas_call(
        _post_kernel,
        grid=(n // POST_TM,),
        in_specs=[
            pl.BlockSpec((POST_TM, D_FOX), row),
            pl.BlockSpec((POST_TM, D_SWA), row),
            pl.BlockSpec((POST_TM, D_MODEL), row),
            pl.BlockSpec((POST_TM, D_PLE), row),
            resident(wo.shape), resident(w1.shape), resident(w2.shape),
            resident(wg.shape), resident(wp.shape), resident(gains.shape),
        ],
        out_specs=pl.BlockSpec((POST_TM, D_MODEL), row),
        out_shape=jax.ShapeDtypeStruct((n, D_MODEL), jnp.float32),
        compiler_params=pltpu.CompilerParams(
            dimension_semantics=("parallel",),
            vmem_limit_bytes=VMEM_LIMIT),
        name="post",
    )(fox, swa, x2, p2, wo, w1, w2, wg, wp, gains)


def _gate_lanes(v):
    return jnp.tile(v, (1, LANES // N_FOX_HEADS))


def kernel(x, p, w_in, b_forget, w_out, rel_bias, swa_sinks, g_attn_pre, g_attn_post,
           w_ff1, w_ff2, g_ff_pre, g_ff_post, w_ple, w_ple_gate, g_ple_post):
    batch, seq, _ = x.shape
    n = batch * seq
    bf16 = jnp.bfloat16
    h = x.reshape(n, D_MODEL)
    for i in range(p.shape[0]):
        b_ff = _gate_lanes(b_forget[i].reshape(1, N_FOX_HEADS))
        fq, fk, fv, qa, ka, sq, sk, sv = _proj_call(
            h, g_attn_pre[i].reshape(1, D_MODEL), w_in[i].T, b_ff, batch, seq)
        fox, wo, w1, w2, wg = _fox_call(
            fq, qa, fk, ka, fv, [w_out[i], w_ff1[i], w_ff2[i], w_ple_gate[i]], batch, seq)
        swa = _swa_call(sq, sk, sv, rel_bias, swa_sinks[i], batch, seq).reshape(n, D_SWA)
        gains = jnp.stack([g_attn_post[i], g_ff_pre[i], g_ff_post[i], g_ple_post[i]])
        h = _post_call(fox.reshape(n, D_FOX), swa, h, p[i].reshape(n, D_PLE),
                       wo, w1, w2, wg, w_ple[i].astype(bf16), gains)
    return h.reshape(batch, seq, D_MODEL)
```

```python
import functools

import jax
import jax.numpy as jnp
import numpy as np
from jax import lax
from jax.experimental import pallas as pl
from jax.experimental.pallas import tpu as pltpu

D_MODEL = 1024
HEAD_DIM = 64
N_FOX_HEADS = 8
N_SWA_HEADS = 8
N_SWA_KV_HEADS = 2
SWA_GROUP = N_SWA_HEADS // N_SWA_KV_HEADS
D_FOX = N_FOX_HEADS * HEAD_DIM
D_SWA = N_SWA_HEADS * HEAD_DIM
D_SWA_KV = N_SWA_KV_HEADS * HEAD_DIM
D_FF = 4 * D_MODEL
D_PLE = 256
WINDOW = 128
Q_BLOCK = 128
N_BUCKETS = 32
MAX_DISTANCE = 128
RMS_EPS = 1e-6

LANES = 128
N_PAIRS = N_FOX_HEADS // 2
N_SPLIT = 3
N_TERMS = HEAD_DIM // N_FOX_HEADS
TERM_SRC = (0, 0, 0, 1, 1, 1, 2, 2)
TERM_CUT = (0, 1, 2, 0, 1, 2, 0, 1)
NEG_BIG = -1e30
LOG2E = float(np.log2(np.e))

PROJ_TM = 1024
PROJ_STRIP = 512
FOX_T = 512
FOX_DIAG = 256
FOX_PAIRS = 1
POST_TM = 512
POST_STRIP = 256
FF_CHUNK = 1024
VMEM_LIMIT = 56 * 1024 * 1024

C_FQ, C_FK, C_FV = 0, D_FOX, 2 * D_FOX
C_SQ = 3 * D_FOX
C_SK = C_SQ + D_SWA
C_SV = C_SK + D_SWA_KV
C_FF = C_SV + D_SWA_KV
D_PROJ = C_FF + LANES


def _rms(v):
    return v * lax.rsqrt(jnp.mean(v * v, axis=-1, keepdims=True) + RMS_EPS)


def _dot(a, b):
    return jnp.dot(a, b, preferred_element_type=jnp.float32)


def _dot_nt(a, b):
    return lax.dot_general(a, b, (((1,), (1,)), ((), ())),
                           preferred_element_type=jnp.float32)


def _proj_kernel(x_ref, g_ref, wt_ref, bf_ref, tri_ref,
                 fq_ref, fk_ref, fv_ref, qa_ref, ka_ref, sq_ref, sk_ref, sv_ref,
                 carry_ref, w_ref):
    @pl.when((pl.program_id(0) == 0) & (pl.program_id(1) == 0))
    def _():
        bf16 = jnp.bfloat16
        q_scale = HEAD_DIM ** -0.5 * LOG2E
        src = np.cumsum([0, D_FOX, D_FOX, D_FOX, N_FOX_HEADS, D_SWA])
        src_fq, src_fk, src_fv, src_ff, src_sq, src_skv = (int(c) for c in src)
        w_ref[C_FQ:C_FQ + D_FOX] = (wt_ref[src_fq:src_fq + D_FOX] * q_scale).astype(bf16)
        w_ref[C_FK:C_FK + D_FOX] = wt_ref[src_fk:src_fk + D_FOX].astype(bf16)
        w_ref[C_FV:C_FV + D_FOX] = wt_ref[src_fv:src_fv + D_FOX].astype(bf16)
        w_ref[C_SQ:C_SQ + D_SWA] = (wt_ref[src_sq:src_sq + D_SWA] * q_scale).astype(bf16)
        w_ref[C_SK:C_SK + 2 * D_SWA_KV] = wt_ref[src_skv:src_skv + 2 * D_SWA_KV].astype(bf16)
        gate = wt_ref[src_ff:src_ff + N_FOX_HEADS]
        w_ref[C_FF:C_FF + LANES] = jnp.tile(gate, (LANES // N_FOX_HEADS, 1)).astype(bf16)

    @pl.when(pl.program_id(1) == 0)
    def _():
        carry_ref[...] = jnp.zeros_like(carry_ref)

    lane = lax.broadcasted_iota(jnp.int32, (PROJ_STRIP, LANES), 1)
    lower = lane < HEAD_DIM
    term = (lane // N_FOX_HEADS) % N_TERMS

    for r in range(0, PROJ_TM, PROJ_STRIP):
        rows = slice(r, r + PROJ_STRIP)
        a = (_rms(x_ref[rows, :]) * g_ref[...]).astype(jnp.bfloat16)

        def proj(lo, width):
            return _dot_nt(a, w_ref[lo:lo + width])

        v = proj(C_FF, LANES) + bf_ref[...]
        logf = (jnp.minimum(v, 0.0) - jnp.log1p(jnp.exp(-jnp.abs(v)))) * LOG2E
        x = _pick_piece(_split_bf16(logf), term, TERM_SRC)
        c = carry_ref[...] + _dot(tri_ref[...], x)
        carry_ref[...] = c[PROJ_STRIP - 1:PROJ_STRIP, :]
        terms = _pick_piece(_split_bf16(c), term, TERM_CUT)
        one = jnp.ones_like(terms)
        qa_ref[rows, :] = jnp.where(lower, terms, one)
        ka_ref[rows, :] = jnp.where(lower, one, -terms)

        kv = proj(C_SK, 2 * D_SWA_KV)
        for half, out_ref in enumerate((sk_ref, sv_ref)):
            z = kv[:, half * LANES:(half + 1) * LANES]
            zr = pltpu.roll(z, HEAD_DIM, 1)
            out_ref[rows, :LANES] = jnp.where(lower, z, zr).astype(jnp.bfloat16)
            out_ref[rows, LANES:] = jnp.where(lower, zr, z).astype(jnp.bfloat16)

        fq_ref[rows, :] = proj(C_FQ, D_FOX).astype(jnp.bfloat16)
        fk_ref[rows, :] = proj(C_FK, D_FOX).astype(jnp.bfloat16)
        fv_ref[rows, :] = proj(C_FV, D_FOX).astype(jnp.bfloat16)
        sq_ref[rows, :] = proj(C_SQ, D_SWA).astype(jnp.bfloat16)


def _split_bf16(v):
    pieces = []
    r = v
    for _ in range(N_SPLIT):
        t = r.astype(jnp.bfloat16)
        pieces.append(t)
        r = r - t.astype(jnp.float32)
    return pieces


def _pick_piece(pieces, term, piece_of_term):
    out = pieces[piece_of_term[-1]]
    for t in range(N_TERMS - 2, -1, -1):
        if piece_of_term[t] != piece_of_term[t + 1]:
            out = jnp.where(term <= t, pieces[piece_of_term[t]], out)
    return out


def _proj_call(x2, g_pre, w_in_t, b_ff, batch, seq):
    n = batch * seq
    steps = seq // PROJ_TM
    tri = np.tril(np.ones((PROJ_STRIP, PROJ_STRIP), np.float32))
    row = lambda b, s: (b * steps + s, 0)
    const = lambda b, s: (0, 0)
    bf16 = jnp.bfloat16

    def out(width):
        return (jax.ShapeDtypeStruct((n, width), bf16), pl.BlockSpec((PROJ_TM, width), row))

    outs = [out(D_FOX), out(D_FOX), out(D_FOX), out(LANES), out(LANES),
            out(D_SWA), out(2 * LANES), out(2 * LANES)]
    return pl.pallas_call(
        _proj_kernel,
        grid=(batch, steps),
        in_specs=[
            pl.BlockSpec((PROJ_TM, D_MODEL), row),
            pl.BlockSpec((1, D_MODEL), const),
            pl.BlockSpec(w_in_t.shape, const, pipeline_mode=pl.Buffered(1)),
            pl.BlockSpec((1, LANES), const),
            pl.BlockSpec((PROJ_STRIP, PROJ_STRIP), const),
        ],
        out_specs=[o[1] for o in outs],
        out_shape=[o[0] for o in outs],
        scratch_shapes=[pltpu.VMEM((1, LANES), jnp.float32),
                        pltpu.VMEM((D_PROJ, D_MODEL), bf16)],
        compiler_params=pltpu.CompilerParams(
            dimension_semantics=("arbitrary", "arbitrary"),
            vmem_limit_bytes=VMEM_LIMIT),
        name="proj",
    )(x2, g_pre, w_in_t, b_ff, jnp.asarray(tri, bf16))


def _fox_kernel(*refs, seq, n_cast):
    q_ref, qa_ref, k_ref, ka_ref, v_ref = refs[:5]
    w_refs = refs[5:5 + n_cast]
    o_ref = refs[5 + n_cast]
    wb_refs = refs[6 + n_cast:6 + 2 * n_cast]
    qm_ref, s_ref, m_ref, acc_ref = refs[6 + 2 * n_cast:]

    t = FOX_T
    lane = lax.broadcasted_iota(jnp.int32, (t, LANES), 1)
    lower = lane < HEAD_DIM
    head_of_lane = lane % N_FOX_HEADS
    d = FOX_DIAG
    causal = (lax.broadcasted_iota(jnp.int32, (d, d), 0)
              >= lax.broadcasted_iota(jnp.int32, (d, d), 1))

    def tile_rows(i):
        return slice(i * t, (i + 1) * t)

    def pair_lanes(pp):
        return slice(pp * LANES, (pp + 1) * LANES)

    def parts(i, j):
        if j < i:
            return [(0, t, t)]
        return [(r, r + d, r + d) for r in range(0, t, d)]

    def mask_queries(pp, i):
        q = q_ref[0, tile_rows(i), pair_lanes(pp)]
        qa = qa_ref[0, tile_rows(i), :]
        zero = jnp.zeros_like(q)
        first_head = 2 * (FOX_PAIRS * pl.program_id(1) + pp)
        for e in range(2):
            qm_ref[pp, i, e] = jnp.concatenate(
                [jnp.where(lower, q, zero) if e == 0 else jnp.where(lower, zero, q),
                 jnp.where(head_of_lane == first_head + e, qa, zero)], axis=-1)

    def scores(pp, i, j, slot):
        for r0, r1, nk in parts(i, j):
            keys = slice(j * t, j * t + nk)
            kk = jnp.concatenate([k_ref[0, keys, pair_lanes(pp)], ka_ref[0, keys, :]], axis=-1)
            for e in range(2):
                s_ref[slot, e, r0:r1, :nk] = _dot_nt(qm_ref[pp, i, e, r0:r1], kk)

    def update(pp, i, j, slot):
        for r0, r1, nk in parts(i, j):
            v = v_ref[0, j * t:j * t + nk, pair_lanes(pp)]
            one = jnp.ones_like(v)
            for e in range(2):
                s = s_ref[slot, e, r0:r1, :nk]
                if j == i:
                    last = jnp.where(causal, s[:, nk - d:], NEG_BIG)
                    s = last if nk == d else jnp.concatenate([s[:, :nk - d], last], axis=-1)
                row_max = jnp.max(s, axis=-1, keepdims=True)
                m_new = (jnp.broadcast_to(row_max, (r1 - r0, LANES)) if j == 0
                         else jnp.maximum(m_ref[pp, e, r0:r1], row_max))
                p = jnp.concatenate(
                    [jnp.exp2(s[:, c * LANES:(c + 1) * LANES] - m_new)
                     for c in range(nk // LANES)], axis=-1).astype(jnp.bfloat16)
                v_lower = lax.broadcasted_iota(jnp.int32, v.shape, 1) < HEAD_DIM
                ve = jnp.where(v_lower, v, one) if e == 0 else jnp.where(v_lower, one, v)
                pv = _dot(p, ve)
                if j > 0:
                    pv = jnp.exp2(m_ref[pp, e, r0:r1] - m_new) * acc_ref[pp, e, r0:r1] + pv
                acc_ref[pp, e, r0:r1] = pv
                m_ref[pp, e, r0:r1] = m_new

    def finish(pp, i):
        outs = []
        for e in range(2):
            acc = acc_ref[pp, e]
            outs.append(acc / pltpu.roll(acc, HEAD_DIM, 1))
        o_ref[0, tile_rows(i), pair_lanes(pp)] = (
            jnp.where(lower, outs[0], outs[1]).astype(o_ref.dtype))

    n_tiles = seq // t
    work = [(pp, i, j) for pp in range(FOX_PAIRS) for i in range(n_tiles) for j in range(i + 1)]

    def issue_scores(idx):
        pp, i, j = work[idx]
        if j == 0:
            mask_queries(pp, i)
        scores(pp, i, j, idx % 2)

    issue_scores(0)
    for idx, (pp, i, j) in enumerate(work):
        if idx + 1 < len(work):
            issue_scores(idx + 1)
        update(pp, i, j, idx % 2)
        if j == i:
            finish(pp, i)

    for w_ref, wb_ref in zip(w_refs, wb_refs):
        wb_ref[...] = w_ref[...].astype(wb_ref.dtype)


def _fox_call(fq, qa, fk, ka, fv, weights, batch, seq):
    t = FOX_T
    bf16 = jnp.bfloat16
    groups = N_PAIRS // FOX_PAIRS
    shape3 = (batch, seq, N_PAIRS * LANES)
    aug3 = (batch, seq, LANES)
    spec = pl.BlockSpec((1, seq, FOX_PAIRS * LANES), lambda b, g: (b, 0, g))
    aug_spec = pl.BlockSpec((1, seq, LANES), lambda b, g: (b, 0, 0))
    steps = batch * groups
    w_specs = [pl.BlockSpec((w.shape[0] // steps, w.shape[1]), lambda b, g: (b * groups + g, 0))
               for w in weights]
    return pl.pallas_call(
        functools.partial(_fox_kernel, seq=seq, n_cast=len(weights)),
        grid=(batch, groups),
        in_specs=[spec, aug_spec, spec, aug_spec, spec] + w_specs,
        out_specs=[spec] + w_specs,
        out_shape=[jax.ShapeDtypeStruct(shape3, bf16)]
        + [jax.ShapeDtypeStruct(w.shape, bf16) for w in weights],
        scratch_shapes=[pltpu.VMEM((FOX_PAIRS, seq // t, 2, t, 2 * LANES), jnp.bfloat16),
                        pltpu.VMEM((2, 2, t, t), jnp.float32),
                        pltpu.VMEM((FOX_PAIRS, 2, t, LANES), jnp.float32),
                        pltpu.VMEM((FOX_PAIRS, 2, t, LANES), jnp.float32)],
        compiler_params=pltpu.CompilerParams(
            dimension_semantics=("parallel", "parallel"),
            vmem_limit_bytes=VMEM_LIMIT),
        name="fox",
    )(fq.reshape(shape3), qa.reshape(aug3), fk.reshape(shape3), ka.reshape(aug3),
      fv.reshape(shape3), *weights)


def _t5_bucket(n):
    max_exact = N_BUCKETS // 2
    large = max_exact + (np.log(np.maximum(n, 1) / max_exact)
                         / np.log(MAX_DISTANCE / max_exact)
                         * (N_BUCKETS - max_exact)).astype(np.int32)
    large = np.minimum(large, N_BUCKETS - 1)
    return np.where(n < max_exact, n, large).astype(np.int32)


def _band_buckets():
    i = np.arange(Q_BLOCK)[:, None]
    j = np.arange(2 * Q_BLOCK)[None, :]
    dist = i + Q_BLOCK - j
    in_window = (dist >= 0) & (dist < WINDOW)
    return np.where(in_window, _t5_bucket(np.clip(dist, 0, None)), -1).astype(np.int32)


def _swa_kernel(bucket_ref, rel_ref, sink_ref, q_ref, k_ref, v_ref, o_ref, bias_ref, sinkc_ref,
                s_ref, *, seq):
    qb = Q_BLOCK
    kv = pl.program_id(1)

    @pl.when((pl.program_id(0) == 0) & (kv == 0))
    def _():
        bucket = bucket_ref[...]
        for h in range(N_SWA_HEADS):
            bias = jnp.full(bucket.shape, NEG_BIG, jnp.float32)
            for b in range(N_BUCKETS):
                bias = jnp.where(bucket == b, rel_ref[b, h] * LOG2E, bias)
            bias = jnp.where(lax.broadcasted_iota(jnp.int32, bucket.shape, 1) == 0,
                             sink_ref[h] * LOG2E, bias)
            group, g = divmod(h, SWA_GROUP)
            bias_ref[group, g * qb:(g + 1) * qb, :] = bias
            sinkc_ref[group, g * qb:(g + 1) * qb, :] = jnp.full((qb, LANES), sink_ref[h] * LOG2E)

    lower = lax.broadcasted_iota(jnp.int32, (qb, LANES), 1) < HEAD_DIM

    def band_of(n):
        if n == 0:
            return slice(0, qb), qb, slice(qb, 2 * qb)
        return slice((n - 1) * qb, (n + 1) * qb), 2 * qb, slice(0, 2 * qb)

    def scores(n, slot):
        rows = slice(n * qb, (n + 1) * qb)
        band, width, cols = band_of(n)
        parts = []
        for pair in range(SWA_GROUP // 2):
            qp = q_ref[0, rows, pair * LANES:(pair + 1) * LANES]
            zero = jnp.zeros_like(qp)
            parts += [jnp.where(lower, qp, zero), jnp.where(lower, zero, qp)]
        s_ref[slot, :, :width] = (_dot_nt(jnp.concatenate(parts, axis=0), sink_slot(n, k_ref))
                                  + bias_ref[kv, :, cols])

    def sink_slot(n, ref):
        band, _, _ = band_of(n)
        x = ref[0, band, :]
        if n == 0:
            return x
        return jnp.where(lax.broadcasted_iota(jnp.int32, x.shape, 0) == 0, jnp.zeros_like(x), x)

    def update(n, slot):
        rows = slice(n * qb, (n + 1) * qb)
        _, width, _ = band_of(n)
        vb = sink_slot(n, v_ref)
        vb = jnp.concatenate([vb, jnp.ones_like(vb)], axis=-1)
        s = s_ref[slot, :, :width]
        m = jnp.max(s, axis=-1, keepdims=True)
        if n == 0:
            sink = sinkc_ref[kv]
            m = jnp.maximum(m, sink)
        p = jnp.concatenate(
            [jnp.exp2(s[:, c * LANES:(c + 1) * LANES] - m) for c in range(width // LANES)],
            axis=-1).astype(jnp.bfloat16)
        acc = _dot(p, vb)
        denom = acc[:, LANES:] + jnp.exp2(sink - m) if n == 0 else acc[:, LANES:]
        out = acc[:, :LANES] / denom
        for pair in range(SWA_GROUP // 2):
            even = slice(2 * pair * qb, (2 * pair + 1) * qb)
            odd = slice((2 * pair + 1) * qb, (2 * pair + 2) * qb)
            o_ref[0, rows, pair * LANES:(pair + 1) * LANES] = jnp.where(
                lower, out[even], out[odd]).astype(o_ref.dtype)

    n_blocks = seq // qb
    scores(0, 0)
    for n in range(n_blocks):
        if n + 1 < n_blocks:
            scores(n + 1, (n + 1) % 2)
        update(n, n % 2)


def _swa_call(sq, sk, sv, rel_bias, sinks, batch, seq):
    bucket = _band_buckets()
    width = SWA_GROUP * HEAD_DIM
    smem = pl.BlockSpec(memory_space=pltpu.SMEM)
    q_spec = pl.BlockSpec((1, seq, width), lambda b, kv: (b, 0, kv))
    kv_spec = pl.BlockSpec((1, seq, LANES), lambda b, kv: (b, 0, kv))
    return pl.pallas_call(
        functools.partial(_swa_kernel, seq=seq),
        grid=(batch, N_SWA_KV_HEADS),
        in_specs=[pl.BlockSpec(bucket.shape, lambda b, kv: (0, 0)), smem, smem,
                  q_spec, kv_spec, kv_spec],
        out_specs=q_spec,
        out_shape=jax.ShapeDtypeStruct((batch, seq, D_SWA), jnp.bfloat16),
        scratch_shapes=[
            pltpu.VMEM((N_SWA_KV_HEADS, SWA_GROUP * Q_BLOCK, 2 * Q_BLOCK), jnp.float32),
            pltpu.VMEM((N_SWA_KV_HEADS, SWA_GROUP * Q_BLOCK, LANES), jnp.float32),
            pltpu.VMEM((2, SWA_GROUP * Q_BLOCK, 2 * Q_BLOCK), jnp.float32)],
        compiler_params=pltpu.CompilerParams(
            dimension_semantics=("arbitrary", "arbitrary"),
            vmem_limit_bytes=VMEM_LIMIT),
        name="swa",
    )(jnp.asarray(bucket), rel_bias, sinks, sq.reshape(batch, seq, D_SWA),
      sk.reshape(batch, seq, 2 * LANES), sv.reshape(batch, seq, 2 * LANES))


def _post_kernel(fox_ref, swa_ref, x_ref, p_ref, wo_ref, w1_ref, w2_ref, wg_ref, wp_ref,
                 gains_ref, o_ref, h_ref, y_ref, *, n_tiles):
    bf16 = jnp.bfloat16
    i = pl.program_id(0)
    strips = [slice(r, r + POST_STRIP) for r in range(0, POST_TM, POST_STRIP)]

    def tail_norm():
        return [h_ref[rows, :] + _rms(y_ref[rows, :]) * gains_ref[2:3, :] for rows in strips]

    def tail_strip(k, hh_prev, ple):
        rows, hh = strips[k], hh_prev[k]
        gate = jax.nn.sigmoid(_dot(hh.astype(bf16), wg_ref[...]))
        o_ref[rows, :] = hh + _rms(ple[rows] * gate) * gains_ref[3:4, :]

    def head():
        m_parts = []
        for rows in strips:
            mix = _dot(jnp.concatenate([fox_ref[rows, :], swa_ref[rows, :]], axis=-1), wo_ref[...])
            hh = x_ref[rows, :] + _rms(mix) * gains_ref[0:1, :]
            h_ref[rows, :] = hh
            m_parts.append((_rms(hh) * gains_ref[1:2, :]).astype(bf16))
        return jnp.concatenate(m_parts, axis=0)

    def mlp_chunk(m, c):
        cols = slice(c * FF_CHUNK, (c + 1) * FF_CHUNK)
        hid = jnp.square(jnp.maximum(_dot(m, w1_ref[:, cols]), 0.0))
        return _dot(hid.astype(bf16), w2_ref[cols, :])

    @pl.when(i == 0)
    def _():
        h_ref[...] = jnp.zeros_like(h_ref)
        y_ref[...] = jnp.zeros_like(y_ref)

    @pl.when(i < n_tiles)
    def _():
        hh_prev = tail_norm()
        m = head()
        ple = _dot(p_ref[...].astype(bf16), wp_ref[...])
        y = None
        for c in range(D_FF // FF_CHUNK):
            if c < len(strips):
                tail_strip(c, hh_prev, ple)
            part = mlp_chunk(m, c)
            y = part if y is None else y + part
        y_ref[...] = y

    @pl.when(i == n_tiles)
    def _():
        hh_prev = tail_norm()
        ple = _dot(p_ref[...].astype(bf16), wp_ref[...])
        for k in range(len(strips)):
            tail_strip(k, hh_prev, ple)


def _post_call(fox, swa, x2, p2, wo, w1, w2, wg, wp, gains):
    n = x2.shape[0]
    n_tiles = n // POST_TM
    cur = lambda i: (jnp.minimum(i, n_tiles - 1), 0)
    prev = lambda i: (jnp.maximum(i - 1, 0), 0)
    const = lambda i: (0, 0)

    def resident(shape):
        return pl.BlockSpec(shape, const, pipeline_mode=pl.Buffered(1))

    return pl.pallas_call(
        functools.partial(_post_kernel, n_tiles=n_tiles),
        grid=(n_tiles + 1,),
        in_specs=[
            pl.BlockSpec((POST_TM, D_FOX), cur),
            pl.BlockSpec((POST_TM, D_SWA), cur),
            pl.BlockSpec((POST_TM, D_MODEL), cur),
            pl.BlockSpec((POST_TM, D_PLE), prev),
            resident(wo.shape), resident(w1.shape), resident(w2.shape),
            resident(wg.shape), resident(wp.shape), resident(gains.shape),
        ],
        out_specs=pl.BlockSpec((POST_TM, D_MODEL), prev),
        out_shape=jax.ShapeDtypeStruct((n, D_MODEL), jnp.float32),
        scratch_shapes=[pltpu.VMEM((POST_TM, D_MODEL), jnp.float32),
                        pltpu.VMEM((POST_TM, D_MODEL), jnp.float32)],
        compiler_params=pltpu.CompilerParams(
            dimension_semantics=("arbitrary",),
            vmem_limit_bytes=VMEM_LIMIT),
        name="post",
    )(fox, swa, x2, p2, wo, w1, w2, wg, wp, gains)


def _gate_lanes(v):
    return jnp.tile(v, (1, LANES // N_FOX_HEADS))


def kernel(x, p, w_in, b_forget, w_out, rel_bias, swa_sinks, g_attn_pre, g_attn_post,
           w_ff1, w_ff2, g_ff_pre, g_ff_post, w_ple, w_ple_gate, g_ple_post):
    batch, seq, _ = x.shape
    n = batch * seq
    bf16 = jnp.bfloat16
    h = x.reshape(n, D_MODEL)
    for i in range(p.shape[0]):
        b_ff = _gate_lanes(b_forget[i].reshape(1, N_FOX_HEADS))
        fq, fk, fv, qa, ka, sq, sk, sv = _proj_call(
            h, g_attn_pre[i].reshape(1, D_MODEL), w_in[i].T, b_ff, batch, seq)
        fox, wo, w1, w2, wg = _fox_call(
            fq, qa, fk, ka, fv, [w_out[i], w_ff1[i], w_ff2[i], w_ple_gate[i]], batch, seq)
        swa = _swa_call(sq, sk, sv, rel_bias, swa_sinks[i], batch, seq).reshape(n, D_SWA)
        gains = jnp.stack([g_attn_post[i], g_ff_pre[i], g_ff_post[i], g_ple_post[i]])
        h = _post_call(fox.reshape(n, D_FOX), swa, h, p[i].reshape(n, D_PLE),
                       wo, w1, w2, wg, w_ple[i].astype(bf16), gains)
    return h.reshape(batch, seq, D_MODEL)
```

```python
import functools

import jax
import jax.numpy as jnp
import numpy as np
from jax import lax
from jax.experimental import pallas as pl
from jax.experimental.pallas import tpu as pltpu

D_MODEL = 1024
HEAD_DIM = 64
N_FOX_HEADS = 8
N_SWA_HEADS = 8
N_SWA_KV_HEADS = 2
SWA_GROUP = N_SWA_HEADS // N_SWA_KV_HEADS
D_FOX = N_FOX_HEADS * HEAD_DIM
D_SWA = N_SWA_HEADS * HEAD_DIM
D_SWA_KV = N_SWA_KV_HEADS * HEAD_DIM
D_FF = 4 * D_MODEL
D_PLE = 256
WINDOW = 128
Q_BLOCK = 128
N_BUCKETS = 32
MAX_DISTANCE = 128
RMS_EPS = 1e-6

LANES = 128
N_PAIRS = N_FOX_HEADS // 2
N_SPLIT = 3
N_TERMS = HEAD_DIM // N_FOX_HEADS
TERM_SRC = (0, 0, 0, 1, 1, 1, 2, 2)
TERM_CUT = (0, 1, 2, 0, 1, 2, 0, 1)
NEG_BIG = -1e30
LOG2E = float(np.log2(np.e))

PROJ_TM = 1024
PROJ_STRIP = 512
FOX_T = 512
FOX_DIAG = 256
FOX_PAIRS = 1
FOX_KEYS = 1024
POST_TM = 512
POST_STRIP = 256
FF_CHUNK = 1024
VMEM_LIMIT = 56 * 1024 * 1024

C_FQ, C_FK, C_FV = 0, D_FOX, 2 * D_FOX
C_SQ = 3 * D_FOX
C_SK = C_SQ + D_SWA
C_SV = C_SK + D_SWA_KV
C_FF = C_SV + D_SWA_KV
D_PROJ = C_FF + LANES


def _rms(v):
    return v * lax.rsqrt(jnp.mean(v * v, axis=-1, keepdims=True) + RMS_EPS)


def _dot(a, b):
    return jnp.dot(a, b, preferred_element_type=jnp.float32)


def _dot_nt(a, b):
    return lax.dot_general(a, b, (((1,), (1,)), ((), ())),
                           preferred_element_type=jnp.float32)


def _proj_kernel(x_ref, g_ref, wt_ref, bf_ref, tri_ref,
                 fq_ref, fk_ref, fv_ref, qa_ref, ka_ref, sq_ref, sk_ref, sv_ref,
                 carry_ref, w_ref):
    @pl.when((pl.program_id(0) == 0) & (pl.program_id(1) == 0))
    def _():
        bf16 = jnp.bfloat16
        q_scale = HEAD_DIM ** -0.5 * LOG2E
        src = np.cumsum([0, D_FOX, D_FOX, D_FOX, N_FOX_HEADS, D_SWA])
        src_fq, src_fk, src_fv, src_ff, src_sq, src_skv = (int(c) for c in src)
        w_ref[C_FQ:C_FQ + D_FOX] = (wt_ref[src_fq:src_fq + D_FOX] * q_scale).astype(bf16)
        w_ref[C_FK:C_FK + D_FOX] = wt_ref[src_fk:src_fk + D_FOX].astype(bf16)
        w_ref[C_FV:C_FV + D_FOX] = wt_ref[src_fv:src_fv + D_FOX].astype(bf16)
        w_ref[C_SQ:C_SQ + D_SWA] = (wt_ref[src_sq:src_sq + D_SWA] * q_scale).astype(bf16)
        w_ref[C_SK:C_SK + 2 * D_SWA_KV] = wt_ref[src_skv:src_skv + 2 * D_SWA_KV].astype(bf16)
        gate = wt_ref[src_ff:src_ff + N_FOX_HEADS]
        w_ref[C_FF:C_FF + LANES] = jnp.tile(gate, (LANES // N_FOX_HEADS, 1)).astype(bf16)

    @pl.when(pl.program_id(1) == 0)
    def _():
        carry_ref[...] = jnp.zeros_like(carry_ref)

    lane = lax.broadcasted_iota(jnp.int32, (PROJ_STRIP, LANES), 1)
    lower = lane < HEAD_DIM
    term = (lane // N_FOX_HEADS) % N_TERMS

    for r in range(0, PROJ_TM, PROJ_STRIP):
        rows = slice(r, r + PROJ_STRIP)
        a = (_rms(x_ref[rows, :]) * g_ref[...]).astype(jnp.bfloat16)

        def proj(lo, width):
            return _dot_nt(a, w_ref[lo:lo + width])

        v = proj(C_FF, LANES) + bf_ref[...]
        logf = (jnp.minimum(v, 0.0) - jnp.log1p(jnp.exp(-jnp.abs(v)))) * LOG2E
        x = _pick_piece(_split_bf16(logf), term, TERM_SRC)

        kv = proj(C_SK, 2 * D_SWA_KV)
        for half, out_ref in enumerate((sk_ref, sv_ref)):
            z = kv[:, half * LANES:(half + 1) * LANES]
            zr = pltpu.roll(z, HEAD_DIM, 1)
            out_ref[rows, :LANES] = jnp.where(lower, z, zr).astype(jnp.bfloat16)
            out_ref[rows, LANES:] = jnp.where(lower, zr, z).astype(jnp.bfloat16)
        fq_ref[rows, :] = proj(C_FQ, D_FOX).astype(jnp.bfloat16)

        c = carry_ref[...] + _dot(tri_ref[...], x)
        carry_ref[...] = c[PROJ_STRIP - 1:PROJ_STRIP, :]
        terms = _pick_piece(_split_bf16(c), term, TERM_CUT)
        one = jnp.ones_like(terms)
        qa_ref[rows, :] = jnp.where(lower, terms, one)
        ka_ref[rows, :] = jnp.where(lower, one, -terms)

        fk_ref[rows, :] = proj(C_FK, D_FOX).astype(jnp.bfloat16)
        fv_ref[rows, :] = proj(C_FV, D_FOX).astype(jnp.bfloat16)
        sq_ref[rows, :] = proj(C_SQ, D_SWA).astype(jnp.bfloat16)


def _split_bf16(v):
    pieces = []
    r = v
    for _ in range(N_SPLIT):
        t = r.astype(jnp.bfloat16)
        pieces.append(t)
        r = r - t.astype(jnp.float32)
    return pieces


def _pick_piece(pieces, term, piece_of_term):
    out = pieces[piece_of_term[-1]]
    for t in range(N_TERMS - 2, -1, -1):
        if piece_of_term[t] != piece_of_term[t + 1]:
            out = jnp.where(term <= t, pieces[piece_of_term[t]], out)
    return out


def _proj_call(x2, g_pre, w_in_t, b_ff, batch, seq):
    n = batch * seq
    steps = seq // PROJ_TM
    tri = np.tril(np.ones((PROJ_STRIP, PROJ_STRIP), np.float32))
    row = lambda b, s: (b * steps + s, 0)
    const = lambda b, s: (0, 0)
    bf16 = jnp.bfloat16

    def out(width):
        return (jax.ShapeDtypeStruct((n, width), bf16), pl.BlockSpec((PROJ_TM, width), row))

    outs = [out(D_FOX), out(D_FOX), out(D_FOX), out(LANES), out(LANES),
            out(D_SWA), out(2 * LANES), out(2 * LANES)]
    return pl.pallas_call(
        _proj_kernel,
        grid=(batch, steps),
        in_specs=[
            pl.BlockSpec((PROJ_TM, D_MODEL), row),
            pl.BlockSpec((1, D_MODEL), const),
            pl.BlockSpec(w_in_t.shape, const, pipeline_mode=pl.Buffered(1)),
            pl.BlockSpec((1, LANES), const),
            pl.BlockSpec((PROJ_STRIP, PROJ_STRIP), const),
        ],
        out_specs=[o[1] for o in outs],
        out_shape=[o[0] for o in outs],
        scratch_shapes=[pltpu.VMEM((1, LANES), jnp.float32),
                        pltpu.VMEM((D_PROJ, D_MODEL), bf16)],
        compiler_params=pltpu.CompilerParams(
            dimension_semantics=("arbitrary", "arbitrary"),
            vmem_limit_bytes=VMEM_LIMIT),
        name="proj",
    )(x2, g_pre, w_in_t, b_ff, jnp.asarray(tri, bf16))


def _fox_kernel(*refs, seq, n_cast):
    q_ref, qa_ref, k_ref, ka_ref, v_ref = refs[:5]
    w_refs = refs[5:5 + n_cast]
    o_ref = refs[5 + n_cast]
    wb_refs = refs[6 + n_cast:6 + 2 * n_cast]
    qm_ref, s_ref, m_ref, acc_ref = refs[6 + 2 * n_cast:]
    for w_ref, wb_ref in zip(w_refs, wb_refs):
        wb_ref[...] = w_ref[...].astype(wb_ref.dtype)

    t = FOX_T
    lane = lax.broadcasted_iota(jnp.int32, (t, LANES), 1)
    lower = lane < HEAD_DIM
    head_of_lane = lane % N_FOX_HEADS
    d = FOX_DIAG
    causal = (lax.broadcasted_iota(jnp.int32, (d, d), 0)
              >= lax.broadcasted_iota(jnp.int32, (d, d), 1))

    def tile_rows(i):
        return slice(i * t, (i + 1) * t)

    def pair_lanes(pp):
        return slice(pp * LANES, (pp + 1) * LANES)

    def parts(i, k0, nk):
        if k0 + nk < (i + 1) * t:
            return [(0, t, nk)]
        return [(r, r + d, nk - t + r + d) for r in range(0, t, d)]

    def mask_queries(pp, i):
        q = q_ref[0, tile_rows(i), pair_lanes(pp)]
        qa = qa_ref[0, tile_rows(i), :]
        zero = jnp.zeros_like(q)
        first_head = 2 * (FOX_PAIRS * pl.program_id(1) + pp)
        for e in range(2):
            qm_ref[pp, i, e] = jnp.concatenate(
                [jnp.where(lower, q, zero) if e == 0 else jnp.where(lower, zero, q),
                 jnp.where(head_of_lane == first_head + e, qa, zero)], axis=-1)

    def scores(pp, i, k0, nk, slot):
        for r0, r1, n in parts(i, k0, nk):
            keys = slice(k0, k0 + n)
            kk = jnp.concatenate([k_ref[0, keys, pair_lanes(pp)], ka_ref[0, keys, :]], axis=-1)
            for e in range(2):
                s_ref[slot, e, r0:r1, :n] = _dot_nt(qm_ref[pp, i, e, r0:r1], kk)

    def update(pp, i, k0, nk, slot):
        diag = k0 + nk == (i + 1) * t
        for r0, r1, n in parts(i, k0, nk):
            v = v_ref[0, k0:k0 + n, pair_lanes(pp)]
            one = jnp.ones_like(v)
            for e in range(2):
                s = s_ref[slot, e, r0:r1, :n]
                if diag:
                    last = jnp.where(causal, s[:, n - d:], NEG_BIG)
                    s = last if n == d else jnp.concatenate([s[:, :n - d], last], axis=-1)
                row_max = jnp.max(s, axis=-1, keepdims=True)
                m_new = (jnp.broadcast_to(row_max, (r1 - r0, LANES)) if k0 == 0
                         else jnp.maximum(m_ref[pp, e, r0:r1], row_max))
                p = jnp.concatenate(
                    [jnp.exp2(s[:, c * LANES:(c + 1) * LANES] - m_new)
                     for c in range(n // LANES)], axis=-1).astype(jnp.bfloat16)
                v_lower = lax.broadcasted_iota(jnp.int32, v.shape, 1) < HEAD_DIM
                ve = jnp.where(v_lower, v, one) if e == 0 else jnp.where(v_lower, one, v)
                pv = _dot(p, ve)
                if k0 > 0:
                    pv = jnp.exp2(m_ref[pp, e, r0:r1] - m_new) * acc_ref[pp, e, r0:r1] + pv
                acc_ref[pp, e, r0:r1] = pv
                m_ref[pp, e, r0:r1] = m_new

    def finish(pp, i):
        outs = []
        for e in range(2):
            acc = acc_ref[pp, e]
            outs.append(acc / pltpu.roll(acc, HEAD_DIM, 1))
        o_ref[0, tile_rows(i), pair_lanes(pp)] = (
            jnp.where(lower, outs[0], outs[1]).astype(o_ref.dtype))

    work = [(pp, i, k0, min(FOX_KEYS, (i + 1) * t - k0))
            for pp in range(FOX_PAIRS) for i in range(seq // t)
            for k0 in range(0, (i + 1) * t, FOX_KEYS)]

    def issue_scores(idx):
        pp, i, k0, nk = work[idx]
        if k0 == 0:
            mask_queries(pp, i)
        scores(pp, i, k0, nk, idx % 2)

    issue_scores(0)
    for idx, (pp, i, k0, nk) in enumerate(work):
        if idx + 1 < len(work):
            issue_scores(idx + 1)
        update(pp, i, k0, nk, idx % 2)
        if k0 + nk == (i + 1) * t:
            finish(pp, i)


def _fox_call(fq, qa, fk, ka, fv, weights, batch, seq):
    t = FOX_T
    bf16 = jnp.bfloat16
    groups = N_PAIRS // FOX_PAIRS
    shape3 = (batch, seq, N_PAIRS * LANES)
    aug3 = (batch, seq, LANES)
    spec = pl.BlockSpec((1, seq, FOX_PAIRS * LANES), lambda b, g: (b, 0, g))
    aug_spec = pl.BlockSpec((1, seq, LANES), lambda b, g: (b, 0, 0))
    steps = batch * groups
    w_specs = [pl.BlockSpec((w.shape[0] // steps, w.shape[1]), lambda b, g: (b * groups + g, 0))
               for w in weights]
    return pl.pallas_call(
        functools.partial(_fox_kernel, seq=seq, n_cast=len(weights)),
        grid=(batch, groups),
        in_specs=[spec, aug_spec, spec, aug_spec, spec] + w_specs,
        out_specs=[spec] + w_specs,
        out_shape=[jax.ShapeDtypeStruct(shape3, bf16)]
        + [jax.ShapeDtypeStruct(w.shape, bf16) for w in weights],
        scratch_shapes=[pltpu.VMEM((FOX_PAIRS, seq // t, 2, t, 2 * LANES), jnp.bfloat16),
                        pltpu.VMEM((2, 2, t, FOX_KEYS), jnp.float32),
                        pltpu.VMEM((FOX_PAIRS, 2, t, LANES), jnp.float32),
                        pltpu.VMEM((FOX_PAIRS, 2, t, LANES), jnp.float32)],
        compiler_params=pltpu.CompilerParams(
            dimension_semantics=("parallel", "parallel"),
            vmem_limit_bytes=VMEM_LIMIT),
        name="fox",
    )(fq.reshape(shape3), qa.reshape(aug3), fk.reshape(shape3), ka.reshape(aug3),
      fv.reshape(shape3), *weights)


def _t5_bucket(n):
    max_exact = N_BUCKETS // 2
    large = max_exact + (np.log(np.maximum(n, 1) / max_exact)
                         / np.log(MAX_DISTANCE / max_exact)
                         * (N_BUCKETS - max_exact)).astype(np.int32)
    large = np.minimum(large, N_BUCKETS - 1)
    return np.where(n < max_exact, n, large).astype(np.int32)


def _band_buckets():
    i = np.arange(Q_BLOCK)[:, None]
    j = np.arange(2 * Q_BLOCK)[None, :]
    dist = i + Q_BLOCK - j
    in_window = (dist >= 0) & (dist < WINDOW)
    return np.where(in_window, _t5_bucket(np.clip(dist, 0, None)), -1).astype(np.int32)


def _swa_kernel(bucket_ref, rel_ref, sink_ref, q_ref, k_ref, v_ref, o_ref, bias_ref, sinkc_ref,
                s_ref, *, seq):
    qb = Q_BLOCK
    kv = pl.program_id(1)

    @pl.when((pl.program_id(0) == 0) & (kv == 0))
    def _():
        bucket = bucket_ref[...]
        for h in range(N_SWA_HEADS):
            bias = jnp.full(bucket.shape, NEG_BIG, jnp.float32)
            for b in range(N_BUCKETS):
                bias = jnp.where(bucket == b, rel_ref[b, h] * LOG2E, bias)
            bias = jnp.where(lax.broadcasted_iota(jnp.int32, bucket.shape, 1) == 0,
                             sink_ref[h] * LOG2E, bias)
            group, g = divmod(h, SWA_GROUP)
            bias_ref[group, g * qb:(g + 1) * qb, :] = bias
            sinkc_ref[group, g * qb:(g + 1) * qb, :] = jnp.full((qb, LANES), sink_ref[h] * LOG2E)

    lower = lax.broadcasted_iota(jnp.int32, (qb, LANES), 1) < HEAD_DIM

    def band_of(n):
        if n == 0:
            return slice(0, qb), qb, slice(qb, 2 * qb)
        return slice((n - 1) * qb, (n + 1) * qb), 2 * qb, slice(0, 2 * qb)

    def scores(n, slot):
        rows = slice(n * qb, (n + 1) * qb)
        band, width, cols = band_of(n)
        parts = []
        for pair in range(SWA_GROUP // 2):
            qp = q_ref[0, rows, pair * LANES:(pair + 1) * LANES]
            zero = jnp.zeros_like(qp)
            parts += [jnp.where(lower, qp, zero), jnp.where(lower, zero, qp)]
        s_ref[slot, :, :width] = (_dot_nt(jnp.concatenate(parts, axis=0), sink_slot(n, k_ref))
                                  + bias_ref[kv, :, cols])

    def sink_slot(n, ref):
        band, _, _ = band_of(n)
        x = ref[0, band, :]
        if n == 0:
            return x
        return jnp.where(lax.broadcasted_iota(jnp.int32, x.shape, 0) == 0, jnp.zeros_like(x), x)

    def update(n, slot):
        rows = slice(n * qb, (n + 1) * qb)
        _, width, _ = band_of(n)
        vb = sink_slot(n, v_ref)
        vb = jnp.concatenate([vb, jnp.ones_like(vb)], axis=-1)
        s = s_ref[slot, :, :width]
        m = jnp.max(s, axis=-1, keepdims=True)
        if n == 0:
            sink = sinkc_ref[kv]
            m = jnp.maximum(m, sink)
        p = jnp.concatenate(
            [jnp.exp2(s[:, c * LANES:(c + 1) * LANES] - m) for c in range(width // LANES)],
            axis=-1).astype(jnp.bfloat16)
        acc = _dot(p, vb)
        denom = acc[:, LANES:] + jnp.exp2(sink - m) if n == 0 else acc[:, LANES:]
        out = acc[:, :LANES] / denom
        for pair in range(SWA_GROUP // 2):
            even = slice(2 * pair * qb, (2 * pair + 1) * qb)
            odd = slice((2 * pair + 1) * qb, (2 * pair + 2) * qb)
            o_ref[0, rows, pair * LANES:(pair + 1) * LANES] = jnp.where(
                lower, out[even], out[odd]).astype(o_ref.dtype)

    n_blocks = seq // qb
    scores(0, 0)
    for n in range(n_blocks):
        if n + 1 < n_blocks:
            scores(n + 1, (n + 1) % 2)
        update(n, n % 2)


def _swa_call(sq, sk, sv, rel_bias, sinks, batch, seq):
    bucket = _band_buckets()
    width = SWA_GROUP * HEAD_DIM
    smem = pl.BlockSpec(memory_space=pltpu.SMEM)
    q_spec = pl.BlockSpec((1, seq, width), lambda b, kv: (b, 0, kv))
    kv_spec = pl.BlockSpec((1, seq, LANES), lambda b, kv: (b, 0, kv))
    return pl.pallas_call(
        functools.partial(_swa_kernel, seq=seq),
        grid=(batch, N_SWA_KV_HEADS),
        in_specs=[pl.BlockSpec(bucket.shape, lambda b, kv: (0, 0)), smem, smem,
                  q_spec, kv_spec, kv_spec],
        out_specs=q_spec,
        out_shape=jax.ShapeDtypeStruct((batch, seq, D_SWA), jnp.bfloat16),
        scratch_shapes=[
            pltpu.VMEM((N_SWA_KV_HEADS, SWA_GROUP * Q_BLOCK, 2 * Q_BLOCK), jnp.float32),
            pltpu.VMEM((N_SWA_KV_HEADS, SWA_GROUP * Q_BLOCK, LANES), jnp.float32),
            pltpu.VMEM((2, SWA_GROUP * Q_BLOCK, 2 * Q_BLOCK), jnp.float32)],
        compiler_params=pltpu.CompilerParams(
            dimension_semantics=("arbitrary", "arbitrary"),
            vmem_limit_bytes=VMEM_LIMIT),
        name="swa",
    )(jnp.asarray(bucket), rel_bias, sinks, sq.reshape(batch, seq, D_SWA),
      sk.reshape(batch, seq, 2 * LANES), sv.reshape(batch, seq, 2 * LANES))


def _post_kernel(fox_ref, swa_ref, x_ref, p_ref, wo_ref, w1_ref, w2_ref, wg_ref, wp_ref,
                 gains_ref, o_ref):
    bf16 = jnp.bfloat16
    h_parts, m_parts = [], []
    for r in range(0, POST_TM, POST_STRIP):
        rows = slice(r, r + POST_STRIP)
        mix = _dot(jnp.concatenate([fox_ref[rows, :], swa_ref[rows, :]], axis=-1), wo_ref[...])
        hh = x_ref[rows, :] + _rms(mix) * gains_ref[0:1, :]
        h_parts.append(hh)
        m_parts.append((_rms(hh) * gains_ref[1:2, :]).astype(bf16))
    h = jnp.concatenate(h_parts, axis=0)
    m = jnp.concatenate(m_parts, axis=0)
    ple = _dot(p_ref[...].astype(bf16), wp_ref[...])

    y = jnp.zeros((POST_TM, D_MODEL), jnp.float32)
    for c in range(D_FF // FF_CHUNK):
        cols = slice(c * FF_CHUNK, (c + 1) * FF_CHUNK)
        hid = jnp.square(jnp.maximum(_dot(m, w1_ref[:, cols]), 0.0))
        y = y + _dot(hid.astype(bf16), w2_ref[cols, :])

    for r in range(0, POST_TM, POST_STRIP):
        rows = slice(r, r + POST_STRIP)
        hh = h[rows] + _rms(y[rows]) * gains_ref[2:3, :]
        gate = jax.nn.sigmoid(_dot(hh.astype(bf16), wg_ref[...]))
        o_ref[rows, :] = hh + _rms(ple[rows] * gate) * gains_ref[3:4, :]


def _post_call(fox, swa, x2, p2, wo, w1, w2, wg, wp, gains):
    n = x2.shape[0]
    row = lambda i: (i, 0)
    const = lambda i: (0, 0)

    def resident(shape):
        return pl.BlockSpec(shape, const, pipeline_mode=pl.Buffered(1))

    return pl.pallas_call(
        _post_kernel,
        grid=(n // POST_TM,),
        in_specs=[
            pl.BlockSpec((POST_TM, D_FOX), row),
            pl.BlockSpec((POST_TM, D_SWA), row),
            pl.BlockSpec((POST_TM, D_MODEL), row),
            pl.BlockSpec((POST_TM, D_PLE), row),
            resident(wo.shape), resident(w1.shape), resident(w2.shape),
            resident(wg.shape), resident(wp.shape), resident(gains.shape),
        ],
        out_specs=pl.BlockSpec((POST_TM, D_MODEL), row),
        out_shape=jax.ShapeDtypeStruct((n, D_MODEL), jnp.float32),
        compiler_params=pltpu.CompilerParams(
            dimension_semantics=("parallel",),
            vmem_limit_bytes=VMEM_LIMIT),
        name="post",
    )(fox, swa, x2, p2, wo, w1, w2, wg, wp, gains)


def _gate_lanes(v):
    return jnp.tile(v, (1, LANES // N_FOX_HEADS))


def kernel(x, p, w_in, b_forget, w_out, rel_bias, swa_sinks, g_attn_pre, g_attn_post,
           w_ff1, w_ff2, g_ff_pre, g_ff_post, w_ple, w_ple_gate, g_ple_post):
    batch, seq, _ = x.shape
    n = batch * seq
    bf16 = jnp.bfloat16
    h = x.reshape(n, D_MODEL)
    for i in range(p.shape[0]):
        b_ff = _gate_lanes(b_forget[i].reshape(1, N_FOX_HEADS))
        fq, fk, fv, qa, ka, sq, sk, sv = _proj_call(
            h, g_attn_pre[i].reshape(1, D_MODEL), w_in[i].T, b_ff, batch, seq)
        fox, wo, w1, w2, wg = _fox_call(
            fq, qa, fk, ka, fv, [w_out[i], w_ff1[i], w_ff2[i], w_ple_gate[i]], batch, seq)
        swa = _swa_call(sq, sk, sv, rel_bias, swa_sinks[i], batch, seq).reshape(n, D_SWA)
        gains = jnp.stack([g_attn_post[i], g_ff_pre[i], g_ff_post[i], g_ple_post[i]])
        h = _post_call(fox.reshape(n, D_FOX), swa, h, p[i].reshape(n, D_PLE),
                       wo, w1, w2, wg, w_ple[i].astype(bf16), gains)
    return h.reshape(batch, seq, D_MODEL)
```

```python
import functools

import jax
import jax.numpy as jnp
import numpy as np
from jax import lax
from jax.experimental import pallas as pl
from jax.experimental.pallas import tpu as pltpu

D_MODEL = 1024
HEAD_DIM = 64
N_FOX_HEADS = 8
N_SWA_HEADS = 8
N_SWA_KV_HEADS = 2
SWA_GROUP = N_SWA_HEADS // N_SWA_KV_HEADS
D_FOX = N_FOX_HEADS * HEAD_DIM
D_SWA = N_SWA_HEADS * HEAD_DIM
D_SWA_KV = N_SWA_KV_HEADS * HEAD_DIM
D_FF = 4 * D_MODEL
D_PLE = 256
WINDOW = 128
Q_BLOCK = 128
N_BUCKETS = 32
MAX_DISTANCE = 128
RMS_EPS = 1e-6

LANES = 128
N_PAIRS = N_FOX_HEADS // 2
N_SPLIT = 3
N_TERMS = HEAD_DIM // N_FOX_HEADS
TERM_SRC = (0, 0, 0, 1, 1, 1, 2, 2)
TERM_CUT = (0, 1, 2, 0, 1, 2, 0, 1)
NEG_BIG = -1e30
LOG2E = float(np.log2(np.e))

PROJ_TM = 1024
PROJ_STRIP = 512
FOX_T = 512
FOX_DIAG = 256
FOX_PAIRS = 1
FOX_KEYS = 1024
POST_TM = 512
POST_STRIP = 256
FF_CHUNK = 1024
VMEM_LIMIT = 56 * 1024 * 1024

C_FQ, C_FK, C_FV = 0, D_FOX, 2 * D_FOX
C_SQ = 3 * D_FOX
C_SK = C_SQ + D_SWA
C_SV = C_SK + D_SWA_KV
C_FF = C_SV + D_SWA_KV
D_PROJ = C_FF + LANES


def _rms(v):
    return v * lax.rsqrt(jnp.mean(v * v, axis=-1, keepdims=True) + RMS_EPS)


def _dot(a, b):
    return jnp.dot(a, b, preferred_element_type=jnp.float32)


def _dot_nt(a, b):
    return lax.dot_general(a, b, (((1,), (1,)), ((), ())),
                           preferred_element_type=jnp.float32)


def _proj_kernel(x_ref, g_ref, wt_ref, bf_ref, tri_ref,
                 fq_ref, fk_ref, fv_ref, qa_ref, ka_ref, sq_ref, sk_ref, sv_ref,
                 carry_ref, w_ref):
    @pl.when((pl.program_id(0) == 0) & (pl.program_id(1) == 0))
    def _():
        bf16 = jnp.bfloat16
        q_scale = HEAD_DIM ** -0.5 * LOG2E
        src = np.cumsum([0, D_FOX, D_FOX, D_FOX, N_FOX_HEADS, D_SWA])
        src_fq, src_fk, src_fv, src_ff, src_sq, src_skv = (int(c) for c in src)
        w_ref[C_FQ:C_FQ + D_FOX] = (wt_ref[src_fq:src_fq + D_FOX] * q_scale).astype(bf16)
        w_ref[C_FK:C_FK + D_FOX] = wt_ref[src_fk:src_fk + D_FOX].astype(bf16)
        w_ref[C_FV:C_FV + D_FOX] = wt_ref[src_fv:src_fv + D_FOX].astype(bf16)
        w_ref[C_SQ:C_SQ + D_SWA] = (wt_ref[src_sq:src_sq + D_SWA] * q_scale).astype(bf16)
        w_ref[C_SK:C_SK + 2 * D_SWA_KV] = wt_ref[src_skv:src_skv + 2 * D_SWA_KV].astype(bf16)
        gate = wt_ref[src_ff:src_ff + N_FOX_HEADS]
        w_ref[C_FF:C_FF + LANES] = jnp.tile(gate, (LANES // N_FOX_HEADS, 1)).astype(bf16)

    @pl.when(pl.program_id(1) == 0)
    def _():
        carry_ref[...] = jnp.zeros_like(carry_ref)

    lane = lax.broadcasted_iota(jnp.int32, (PROJ_STRIP, LANES), 1)
    lower = lane < HEAD_DIM
    term = (lane // N_FOX_HEADS) % N_TERMS

    for r in range(0, PROJ_TM, PROJ_STRIP):
        rows = slice(r, r + PROJ_STRIP)
        a = (_rms(x_ref[rows, :]) * g_ref[...]).astype(jnp.bfloat16)

        def proj(lo, width):
            return _dot_nt(a, w_ref[lo:lo + width])

        v = proj(C_FF, LANES) + bf_ref[...]
        logf = (jnp.minimum(v, 0.0) - jnp.log1p(jnp.exp(-jnp.abs(v)))) * LOG2E
        x = _pick_piece(_split_bf16(logf), term, TERM_SRC)
        c = carry_ref[...] + _dot(tri_ref[...], x)
        carry_ref[...] = c[PROJ_STRIP - 1:PROJ_STRIP, :]
        terms = _pick_piece(_split_bf16(c), term, TERM_CUT)
        one = jnp.ones_like(terms)
        qa_ref[rows, :] = jnp.where(lower, terms, one)
        ka_ref[rows, :] = jnp.where(lower, one, -terms)

        kv = proj(C_SK, 2 * D_SWA_KV)
        for half, out_ref in enumerate((sk_ref, sv_ref)):
            z = kv[:, half * LANES:(half + 1) * LANES]
            zr = pltpu.roll(z, HEAD_DIM, 1)
            out_ref[rows, :LANES] = jnp.where(lower, z, zr).astype(jnp.bfloat16)
            out_ref[rows, LANES:] = jnp.where(lower, zr, z).astype(jnp.bfloat16)

        fq_ref[rows, :] = proj(C_FQ, D_FOX).astype(jnp.bfloat16)
        fk_ref[rows, :] = proj(C_FK, D_FOX).astype(jnp.bfloat16)
        fv_ref[rows, :] = proj(C_FV, D_FOX).astype(jnp.bfloat16)
        sq_ref[rows, :] = proj(C_SQ, D_SWA).astype(jnp.bfloat16)


def _split_bf16(v):
    pieces = []
    r = v
    for _ in range(N_SPLIT):
        t = r.astype(jnp.bfloat16)
        pieces.append(t)
        r = r - t.astype(jnp.float32)
    return pieces


def _pick_piece(pieces, term, piece_of_term):
    out = pieces[piece_of_term[-1]]
    for t in range(N_TERMS - 2, -1, -1):
        if piece_of_term[t] != piece_of_term[t + 1]:
            out = jnp.where(term <= t, pieces[piece_of_term[t]], out)
    return out


def _proj_call(x2, g_pre, w_in_t, b_ff, batch, seq):
    n = batch * seq
    steps = seq // PROJ_TM
    tri = np.tril(np.ones((PROJ_STRIP, PROJ_STRIP), np.float32))
    row = lambda b, s: (b * steps + s, 0)
    const = lambda b, s: (0, 0)
    bf16 = jnp.bfloat16

    def out(width):
        return (jax.ShapeDtypeStruct((n, width), bf16), pl.BlockSpec((PROJ_TM, width), row))

    outs = [out(D_FOX), out(D_FOX), out(D_FOX), out(LANES), out(LANES),
            out(D_SWA), out(2 * LANES), out(2 * LANES)]
    return pl.pallas_call(
        _proj_kernel,
        grid=(batch, steps),
        in_specs=[
            pl.BlockSpec((PROJ_TM, D_MODEL), row),
            pl.BlockSpec((1, D_MODEL), const),
            pl.BlockSpec(w_in_t.shape, const, pipeline_mode=pl.Buffered(1)),
            pl.BlockSpec((1, LANES), const),
            pl.BlockSpec((PROJ_STRIP, PROJ_STRIP), const),
        ],
        out_specs=[o[1] for o in outs],
        out_shape=[o[0] for o in outs],
        scratch_shapes=[pltpu.VMEM((1, LANES), jnp.float32),
                        pltpu.VMEM((D_PROJ, D_MODEL), bf16)],
        compiler_params=pltpu.CompilerParams(
            dimension_semantics=("arbitrary", "arbitrary"),
            vmem_limit_bytes=VMEM_LIMIT),
        name="proj",
    )(x2, g_pre, w_in_t, b_ff, jnp.asarray(tri, bf16))


def _fox_kernel(*refs, seq, n_cast):
    q_ref, qa_ref, k_ref, ka_ref, v_ref = refs[:5]
    w_refs = refs[5:5 + n_cast]
    o_ref = refs[5 + n_cast]
    wb_refs = refs[6 + n_cast:6 + 2 * n_cast]
    qm_ref, s_ref, m_ref, acc_ref = refs[6 + 2 * n_cast:]
    for w_ref, wb_ref in zip(w_refs, wb_refs):
        wb_ref[...] = w_ref[...].astype(wb_ref.dtype)

    t = FOX_T
    lane = lax.broadcasted_iota(jnp.int32, (t, LANES), 1)
    lower = lane < HEAD_DIM
    head_of_lane = lane % N_FOX_HEADS
    d = FOX_DIAG
    causal = (lax.broadcasted_iota(jnp.int32, (d, d), 0)
              >= lax.broadcasted_iota(jnp.int32, (d, d), 1))

    def tile_rows(i):
        return slice(i * t, (i + 1) * t)

    def pair_lanes(pp):
        return slice(pp * LANES, (pp + 1) * LANES)

    def parts(i, k0, nk):
        if k0 + nk < (i + 1) * t:
            return [(0, t, nk)]
        return [(r, r + d, nk - t + r + d) for r in range(0, t, d)]

    def mask_queries(pp, i):
        q = q_ref[0, tile_rows(i), pair_lanes(pp)]
        qa = qa_ref[0, tile_rows(i), :]
        zero = jnp.zeros_like(q)
        first_head = 2 * (FOX_PAIRS * pl.program_id(1) + pp)
        for e in range(2):
            qm_ref[pp, i, e] = jnp.concatenate(
                [jnp.where(lower, q, zero) if e == 0 else jnp.where(lower, zero, q),
                 jnp.where(head_of_lane == first_head + e, qa, zero)], axis=-1)

    def scores(pp, i, k0, nk, slot):
        for r0, r1, n in parts(i, k0, nk):
            keys = slice(k0, k0 + n)
            kk = jnp.concatenate([k_ref[0, keys, pair_lanes(pp)], ka_ref[0, keys, :]], axis=-1)
            for e in range(2):
                s_ref[slot, e, r0:r1, :n] = _dot_nt(qm_ref[pp, i, e, r0:r1], kk)

    def update(pp, i, k0, nk, slot):
        diag = k0 + nk == (i + 1) * t
        for r0, r1, n in parts(i, k0, nk):
            v = v_ref[0, k0:k0 + n, pair_lanes(pp)]
            one = jnp.ones_like(v)
            for e in range(2):
                s = s_ref[slot, e, r0:r1, :n]
                if diag:
                    last = jnp.where(causal, s[:, n - d:], NEG_BIG)
                    s = last if n == d else jnp.concatenate([s[:, :n - d], last], axis=-1)
                row_max = jnp.max(s, axis=-1, keepdims=True)
                m_new = (jnp.broadcast_to(row_max, (r1 - r0, LANES)) if k0 == 0
                         else jnp.maximum(m_ref[pp, e, r0:r1], row_max))
                p = jnp.concatenate(
                    [jnp.exp2(s[:, c * LANES:(c + 1) * LANES] - m_new)
                     for c in range(n // LANES)], axis=-1).astype(jnp.bfloat16)
                v_lower = lax.broadcasted_iota(jnp.int32, v.shape, 1) < HEAD_DIM
                ve = jnp.where(v_lower, v, one) if e == 0 else jnp.where(v_lower, one, v)
                pv = _dot(p, ve)
                if k0 > 0:
                    pv = jnp.exp2(m_ref[pp, e, r0:r1] - m_new) * acc_ref[pp, e, r0:r1] + pv
                acc_ref[pp, e, r0:r1] = pv
                m_ref[pp, e, r0:r1] = m_new

    def finish(pp, i):
        outs = []
        for e in range(2):
            acc = acc_ref[pp, e]
            outs.append(acc / pltpu.roll(acc, HEAD_DIM, 1))
        o_ref[0, tile_rows(i), pair_lanes(pp)] = (
            jnp.where(lower, outs[0], outs[1]).astype(o_ref.dtype))

    work = [(pp, i, k0, min(FOX_KEYS, (i + 1) * t - k0))
            for pp in range(FOX_PAIRS) for i in range(seq // t)
            for k0 in range(0, (i + 1) * t, FOX_KEYS)]

    def issue_scores(idx):
        pp, i, k0, nk = work[idx]
        if k0 == 0:
            mask_queries(pp, i)
        scores(pp, i, k0, nk, idx % 2)

    issue_scores(0)
    for idx, (pp, i, k0, nk) in enumerate(work):
        if idx + 1 < len(work):
            issue_scores(idx + 1)
        update(pp, i, k0, nk, idx % 2)
        if k0 + nk == (i + 1) * t:
            finish(pp, i)


def _fox_call(fq, qa, fk, ka, fv, weights, batch, seq):
    t = FOX_T
    bf16 = jnp.bfloat16
    groups = N_PAIRS // FOX_PAIRS
    shape3 = (batch, seq, N_PAIRS * LANES)
    aug3 = (batch, seq, LANES)
    spec = pl.BlockSpec((1, seq, FOX_PAIRS * LANES), lambda b, g: (b, 0, g))
    aug_spec = pl.BlockSpec((1, seq, LANES), lambda b, g: (b, 0, 0))
    steps = batch * groups
    w_specs = [pl.BlockSpec((w.shape[0] // steps, w.shape[1]), lambda b, g: (b * groups + g, 0))
               for w in weights]
    return pl.pallas_call(
        functools.partial(_fox_kernel, seq=seq, n_cast=len(weights)),
        grid=(batch, groups),
        in_specs=[spec, aug_spec, spec, aug_spec, spec] + w_specs,
        out_specs=[spec] + w_specs,
        out_shape=[jax.ShapeDtypeStruct(shape3, bf16)]
        + [jax.ShapeDtypeStruct(w.shape, bf16) for w in weights],
        scratch_shapes=[pltpu.VMEM((FOX_PAIRS, seq // t, 2, t, 2 * LANES), jnp.bfloat16),
                        pltpu.VMEM((2, 2, t, FOX_KEYS), jnp.float32),
                        pltpu.VMEM((FOX_PAIRS, 2, t, LANES), jnp.float32),
                        pltpu.VMEM((FOX_PAIRS, 2, t, LANES), jnp.float32)],
        compiler_params=pltpu.CompilerParams(
            dimension_semantics=("parallel", "parallel"),
            vmem_limit_bytes=VMEM_LIMIT),
        name="fox",
    )(fq.reshape(shape3), qa.reshape(aug3), fk.reshape(shape3), ka.reshape(aug3),
      fv.reshape(shape3), *weights)


def _t5_bucket(n):
    max_exact = N_BUCKETS // 2
    large = max_exact + (np.log(np.maximum(n, 1) / max_exact)
                         / np.log(MAX_DISTANCE / max_exact)
                         * (N_BUCKETS - max_exact)).astype(np.int32)
    large = np.minimum(large, N_BUCKETS - 1)
    return np.where(n < max_exact, n, large).astype(np.int32)


def _band_buckets():
    i = np.arange(Q_BLOCK)[:, None]
    j = np.arange(2 * Q_BLOCK)[None, :]
    dist = i + Q_BLOCK - j
    in_window = (dist >= 0) & (dist < WINDOW)
    return np.where(in_window, _t5_bucket(np.clip(dist, 0, None)), -1).astype(np.int32)


def _swa_kernel(bucket_ref, rel_ref, sink_ref, q_ref, k_ref, v_ref, o_ref, bias_ref, sinkc_ref,
                s_ref, *, seq):
    qb = Q_BLOCK
    kv = pl.program_id(1)

    @pl.when((pl.program_id(0) == 0) & (kv == 0))
    def _():
        bucket = bucket_ref[...]
        for h in range(N_SWA_HEADS):
            bias = jnp.full(bucket.shape, NEG_BIG, jnp.float32)
            for b in range(N_BUCKETS):
                bias = jnp.where(bucket == b, rel_ref[b, h] * LOG2E, bias)
            bias = jnp.where(lax.broadcasted_iota(jnp.int32, bucket.shape, 1) == 0,
                             sink_ref[h] * LOG2E, bias)
            group, g = divmod(h, SWA_GROUP)
            bias_ref[group, g * qb:(g + 1) * qb, :] = bias
            sinkc_ref[group, g * qb:(g + 1) * qb, :] = jnp.full((qb, LANES), sink_ref[h] * LOG2E)

    lower = lax.broadcasted_iota(jnp.int32, (qb, LANES), 1) < HEAD_DIM

    def band_of(n):
        if n == 0:
            return slice(0, qb), qb, slice(qb, 2 * qb)
        return slice((n - 1) * qb, (n + 1) * qb), 2 * qb, slice(0, 2 * qb)

    def scores(n, slot):
        rows = slice(n * qb, (n + 1) * qb)
        band, width, cols = band_of(n)
        parts = []
        for pair in range(SWA_GROUP // 2):
            qp = q_ref[0, rows, pair * LANES:(pair + 1) * LANES]
            zero = jnp.zeros_like(qp)
            parts += [jnp.where(lower, qp, zero), jnp.where(lower, zero, qp)]
        s_ref[slot, :, :width] = (_dot_nt(jnp.concatenate(parts, axis=0), sink_slot(n, k_ref))
                                  + bias_ref[kv, :, cols])

    def sink_slot(n, ref):
        band, _, _ = band_of(n)
        x = ref[0, band, :]
        if n == 0:
            return x
        return jnp.where(lax.broadcasted_iota(jnp.int32, x.shape, 0) == 0, jnp.zeros_like(x), x)

    def update(n, slot):
        rows = slice(n * qb, (n + 1) * qb)
        _, width, _ = band_of(n)
        vb = sink_slot(n, v_ref)
        vb = jnp.concatenate([vb, jnp.ones_like(vb)], axis=-1)
        s = s_ref[slot, :, :width]
        m = jnp.max(s, axis=-1, keepdims=True)
        if n == 0:
            sink = sinkc_ref[kv]
            m = jnp.maximum(m, sink)
        p = jnp.concatenate(
            [jnp.exp2(s[:, c * LANES:(c + 1) * LANES] - m) for c in range(width // LANES)],
            axis=-1).astype(jnp.bfloat16)
        acc = _dot(p, vb)
        denom = acc[:, LANES:] + jnp.exp2(sink - m) if n == 0 else acc[:, LANES:]
        out = acc[:, :LANES] / denom
        for pair in range(SWA_GROUP // 2):
            even = slice(2 * pair * qb, (2 * pair + 1) * qb)
            odd = slice((2 * pair + 1) * qb, (2 * pair + 2) * qb)
            o_ref[0, rows, pair * LANES:(pair + 1) * LANES] = jnp.where(
                lower, out[even], out[odd]).astype(o_ref.dtype)

    n_blocks = seq // qb
    scores(0, 0)
    for n in range(n_blocks):
        if n + 1 < n_blocks:
            scores(n + 1, (n + 1) % 2)
        update(n, n % 2)


def _swa_call(sq, sk, sv, rel_bias, sinks, batch, seq):
    bucket = _band_buckets()
    width = SWA_GROUP * HEAD_DIM
    smem = pl.BlockSpec(memory_space=pltpu.SMEM)
    q_spec = pl.BlockSpec((1, seq, width), lambda b, kv: (b, 0, kv))
    kv_spec = pl.BlockSpec((1, seq, LANES), lambda b, kv: (b, 0, kv))
    return pl.pallas_call(
        functools.partial(_swa_kernel, seq=seq),
        grid=(batch, N_SWA_KV_HEADS),
        in_specs=[pl.BlockSpec(bucket.shape, lambda b, kv: (0, 0)), smem, smem,
                  q_spec, kv_spec, kv_spec],
        out_specs=q_spec,
        out_shape=jax.ShapeDtypeStruct((batch, seq, D_SWA), jnp.bfloat16),
        scratch_shapes=[
            pltpu.VMEM((N_SWA_KV_HEADS, SWA_GROUP * Q_BLOCK, 2 * Q_BLOCK), jnp.float32),
            pltpu.VMEM((N_SWA_KV_HEADS, SWA_GROUP * Q_BLOCK, LANES), jnp.float32),
            pltpu.VMEM((2, SWA_GROUP * Q_BLOCK, 2 * Q_BLOCK), jnp.float32)],
        compiler_params=pltpu.CompilerParams(
            dimension_semantics=("arbitrary", "arbitrary"),
            vmem_limit_bytes=VMEM_LIMIT),
        name="swa",
    )(jnp.asarray(bucket), rel_bias, sinks, sq.reshape(batch, seq, D_SWA),
      sk.reshape(batch, seq, 2 * LANES), sv.reshape(batch, seq, 2 * LANES))


def _post_kernel(fox_ref, swa_ref, x_ref, p_ref, wo_ref, w1_ref, w2_ref, wg_ref, wp_ref,
                 gains_ref, o_ref):
    bf16 = jnp.bfloat16
    h_parts, m_parts = [], []
    for r in range(0, POST_TM, POST_STRIP):
        rows = slice(r, r + POST_STRIP)
        mix = _dot(jnp.concatenate([fox_ref[rows, :], swa_ref[rows, :]], axis=-1), wo_ref[...])
        hh = x_ref[rows, :] + _rms(mix) * gains_ref[0:1, :]
        h_parts.append(hh)
        m_parts.append((_rms(hh) * gains_ref[1:2, :]).astype(bf16))
    h = jnp.concatenate(h_parts, axis=0)
    m = jnp.concatenate(m_parts, axis=0)
    ple = _dot(p_ref[...].astype(bf16), wp_ref[...])

    y = jnp.zeros((POST_TM, D_MODEL), jnp.float32)
    for c in range(D_FF // FF_CHUNK):
        cols = slice(c * FF_CHUNK, (c + 1) * FF_CHUNK)
        hid = jnp.square(jnp.maximum(_dot(m, w1_ref[:, cols]), 0.0))
        y = y + _dot(hid.astype(bf16), w2_ref[cols, :])

    for r in range(0, POST_TM, POST_STRIP):
        rows = slice(r, r + POST_STRIP)
        hh = h[rows] + _rms(y[rows]) * gains_ref[2:3, :]
        gate = jax.nn.sigmoid(_dot(hh.astype(bf16), wg_ref[...]))
        o_ref[rows, :] = hh + _rms(ple[rows] * gate) * gains_ref[3:4, :]


def _post_call(fox, swa, x2, p2, wo, w1, w2, wg, wp, gains):
    n = x2.shape[0]
    row = lambda i: (i, 0)
    const = lambda i: (0, 0)

    def resident(shape):
        return pl.BlockSpec(shape, const, pipeline_mode=pl.Buffered(1))

    return pl.pallas_call(
        _post_kernel,
        grid=(n // POST_TM,),
        in_specs=[
            pl.BlockSpec((POST_TM, D_FOX), row),
            pl.BlockSpec((POST_TM, D_SWA), row),
            pl.BlockSpec((POST_TM, D_MODEL), row),
            pl.BlockSpec((POST_TM, D_PLE), row),
            resident(wo.shape), resident(w1.shape), resident(w2.shape),
            resident(wg.shape), resident(wp.shape), resident(gains.shape),
        ],
        out_specs=pl.BlockSpec((POST_TM, D_MODEL), row),
        out_shape=jax.ShapeDtypeStruct((n, D_MODEL), jnp.float32),
        compiler_params=pltpu.CompilerParams(
            dimension_semantics=("parallel",),
            vmem_limit_bytes=VMEM_LIMIT),
        name="post",
    )(fox, swa, x2, p2, wo, w1, w2, wg, wp, gains)


def _gate_lanes(v):
    return jnp.tile(v, (1, LANES // N_FOX_HEADS))


def kernel(x, p, w_in, b_forget, w_out, rel_bias, swa_sinks, g_attn_pre, g_attn_post,
           w_ff1, w_ff2, g_ff_pre, g_ff_post, w_ple, w_ple_gate, g_ple_post):
    batch, seq, _ = x.shape
    n = batch * seq
    bf16 = jnp.bfloat16
    h = x.reshape(n, D_MODEL)
    for i in range(p.shape[0]):
        b_ff = _gate_lanes(b_forget[i].reshape(1, N_FOX_HEADS))
        fq, fk, fv, qa, ka, sq, sk, sv = _proj_call(
            h, g_attn_pre[i].reshape(1, D_MODEL), w_in[i].T, b_ff, batch, seq)
        fox, wo, w1, w2, wg = _fox_call(
            fq, qa, fk, ka, fv, [w_out[i], w_ff1[i], w_ff2[i], w_ple_gate[i]], batch, seq)
        swa = _swa_call(sq, sk, sv, rel_bias, swa_sinks[i], batch, seq).reshape(n, D_SWA)
        gains = jnp.stack([g_attn_post[i], g_ff_pre[i], g_ff_post[i], g_ple_post[i]])
        h = _post_call(fox.reshape(n, D_FOX), swa, h, p[i].reshape(n, D_PLE),
                       wo, w1, w2, wg, w_ple[i].astype(bf16), gains)
    return h.reshape(batch, seq, D_MODEL)
```

```python
import functools

import jax
import jax.numpy as jnp
import numpy as np
from jax import lax
from jax.experimental import pallas as pl
from jax.experimental.pallas import tpu as pltpu

D_MODEL = 1024
HEAD_DIM = 64
N_FOX_HEADS = 8
N_SWA_HEADS = 8
N_SWA_KV_HEADS = 2
SWA_GROUP = N_SWA_HEADS // N_SWA_KV_HEADS
D_FOX = N_FOX_HEADS * HEAD_DIM
D_SWA = N_SWA_HEADS * HEAD_DIM
D_SWA_KV = N_SWA_KV_HEADS * HEAD_DIM
D_FF = 4 * D_MODEL
D_PLE = 256
WINDOW = 128
Q_BLOCK = 128
N_BUCKETS = 32
MAX_DISTANCE = 128
RMS_EPS = 1e-6

LANES = 128
N_PAIRS = N_FOX_HEADS // 2
N_SPLIT = 3
N_TERMS = HEAD_DIM // N_FOX_HEADS
TERM_SRC = (0, 0, 0, 1, 1, 1, 2, 2)
TERM_CUT = (0, 1, 2, 0, 1, 2, 0, 1)
NEG_BIG = -1e30
LOG2E = float(np.log2(np.e))

PROJ_TM = 1024
PROJ_STRIP = 512
FOX_T = 512
FOX_DIAG = 256
FOX_PAIRS = 1
FOX_KEYS = 1024
POST_TM = 512
POST_STRIP = 256
FF_CHUNK = 1024
VMEM_LIMIT = 56 * 1024 * 1024

C_FQ, C_FK, C_FV = 0, D_FOX, 2 * D_FOX
C_SQ = 3 * D_FOX
C_SK = C_SQ + D_SWA
C_SV = C_SK + D_SWA_KV
C_FF = C_SV + D_SWA_KV
D_PROJ = C_FF + LANES


def _rms(v):
    return v * lax.rsqrt(jnp.mean(v * v, axis=-1, keepdims=True) + RMS_EPS)


def _dot(a, b):
    return jnp.dot(a, b, preferred_element_type=jnp.float32)


def _dot_nt(a, b):
    return lax.dot_general(a, b, (((1,), (1,)), ((), ())),
                           preferred_element_type=jnp.float32)


def _proj_kernel(x_ref, g_ref, wt_ref, bf_ref, tri_ref,
                 fq_ref, fk_ref, fv_ref, qa_ref, ka_ref, sq_ref, sk_ref, sv_ref,
                 carry_ref, w_ref):
    @pl.when((pl.program_id(0) == 0) & (pl.program_id(1) == 0))
    def _():
        bf16 = jnp.bfloat16
        q_scale = HEAD_DIM ** -0.5 * LOG2E
        src = np.cumsum([0, D_FOX, D_FOX, D_FOX, N_FOX_HEADS, D_SWA])
        src_fq, src_fk, src_fv, src_ff, src_sq, src_skv = (int(c) for c in src)
        w_ref[C_FQ:C_FQ + D_FOX] = (wt_ref[src_fq:src_fq + D_FOX] * q_scale).astype(bf16)
        w_ref[C_FK:C_FK + D_FOX] = wt_ref[src_fk:src_fk + D_FOX].astype(bf16)
        w_ref[C_FV:C_FV + D_FOX] = wt_ref[src_fv:src_fv + D_FOX].astype(bf16)
        w_ref[C_SQ:C_SQ + D_SWA] = (wt_ref[src_sq:src_sq + D_SWA] * q_scale).astype(bf16)
        w_ref[C_SK:C_SK + 2 * D_SWA_KV] = wt_ref[src_skv:src_skv + 2 * D_SWA_KV].astype(bf16)
        gate = wt_ref[src_ff:src_ff + N_FOX_HEADS]
        w_ref[C_FF:C_FF + LANES] = jnp.tile(gate, (LANES // N_FOX_HEADS, 1)).astype(bf16)

    @pl.when(pl.program_id(1) == 0)
    def _():
        carry_ref[...] = jnp.zeros_like(carry_ref)

    lane = lax.broadcasted_iota(jnp.int32, (PROJ_STRIP, LANES), 1)
    lower = lane < HEAD_DIM
    term = (lane // N_FOX_HEADS) % N_TERMS

    for r in range(0, PROJ_TM, PROJ_STRIP):
        rows = slice(r, r + PROJ_STRIP)
        a = (_rms(x_ref[rows, :]) * g_ref[...]).astype(jnp.bfloat16)

        def proj(lo, width):
            return _dot_nt(a, w_ref[lo:lo + width])

        v = proj(C_FF, LANES) + bf_ref[...]
        logf = (jnp.minimum(v, 0.0) - jnp.log1p(jnp.exp(-jnp.abs(v)))) * LOG2E
        x = _pick_piece(_split_bf16(logf), term, TERM_SRC)
        c = carry_ref[...] + _dot(tri_ref[...], x)
        carry_ref[...] = c[PROJ_STRIP - 1:PROJ_STRIP, :]
        terms = _pick_piece(_split_bf16(c), term, TERM_CUT)
        one = jnp.ones_like(terms)
        qa_ref[rows, :] = jnp.where(lower, terms, one)
        ka_ref[rows, :] = jnp.where(lower, one, -terms)

        kv = proj(C_SK, 2 * D_SWA_KV)
        for half, out_ref in enumerate((sk_ref, sv_ref)):
            z = kv[:, half * LANES:(half + 1) * LANES]
            zr = pltpu.roll(z, HEAD_DIM, 1)
            out_ref[rows, :LANES] = jnp.where(lower, z, zr).astype(jnp.bfloat16)
            out_ref[rows, LANES:] = jnp.where(lower, zr, z).astype(jnp.bfloat16)

        fq_ref[rows, :] = proj(C_FQ, D_FOX).astype(jnp.bfloat16)
        fk_ref[rows, :] = proj(C_FK, D_FOX).astype(jnp.bfloat16)
        fv_ref[rows, :] = proj(C_FV, D_FOX).astype(jnp.bfloat16)
        sq_ref[rows, :] = proj(C_SQ, D_SWA).astype(jnp.bfloat16)


def _split_bf16(v):
    pieces = []
    r = v
    for _ in range(N_SPLIT):
        t = r.astype(jnp.bfloat16)
        pieces.append(t)
        r = r - t.astype(jnp.float32)
    return pieces


def _pick_piece(pieces, term, piece_of_term):
    out = pieces[piece_of_term[-1]]
    for t in range(N_TERMS - 2, -1, -1):
        if piece_of_term[t] != piece_of_term[t + 1]:
            out = jnp.where(term <= t, pieces[piece_of_term[t]], out)
    return out


def _proj_call(x2, g_pre, w_in_t, b_ff, batch, seq):
    n = batch * seq
    steps = seq // PROJ_TM
    tri = np.tril(np.ones((PROJ_STRIP, PROJ_STRIP), np.float32))
    row = lambda b, s: (b * steps + s, 0)
    const = lambda b, s: (0, 0)
    bf16 = jnp.bfloat16

    def out(width):
        return (jax.ShapeDtypeStruct((n, width), bf16), pl.BlockSpec((PROJ_TM, width), row))

    outs = [out(D_FOX), out(D_FOX), out(D_FOX), out(LANES), out(LANES),
            out(D_SWA), out(2 * LANES), out(2 * LANES)]
    return pl.pallas_call(
        _proj_kernel,
        grid=(batch, steps),
        in_specs=[
            pl.BlockSpec((PROJ_TM, D_MODEL), row),
            pl.BlockSpec((1, D_MODEL), const),
            pl.BlockSpec(w_in_t.shape, const, pipeline_mode=pl.Buffered(1)),
            pl.BlockSpec((1, LANES), const),
            pl.BlockSpec((PROJ_STRIP, PROJ_STRIP), const),
        ],
        out_specs=[o[1] for o in outs],
        out_shape=[o[0] for o in outs],
        scratch_shapes=[pltpu.VMEM((1, LANES), jnp.float32),
                        pltpu.VMEM((D_PROJ, D_MODEL), bf16)],
        compiler_params=pltpu.CompilerParams(
            dimension_semantics=("arbitrary", "arbitrary"),
            vmem_limit_bytes=VMEM_LIMIT),
        name="proj",
    )(x2, g_pre, w_in_t, b_ff, jnp.asarray(tri, bf16))


def _fox_kernel(*refs, seq, n_cast):
    q_ref, qa_ref, k_ref, ka_ref, v_ref = refs[:5]
    w_refs = refs[5:5 + n_cast]
    o_ref = refs[5 + n_cast]
    wb_refs = refs[6 + n_cast:6 + 2 * n_cast]
    qm_ref, s_ref, m_ref, acc_ref = refs[6 + 2 * n_cast:]
    for w_ref, wb_ref in zip(w_refs, wb_refs):
        wb_ref[...] = w_ref[...].astype(wb_ref.dtype)

    t = FOX_T
    lane = lax.broadcasted_iota(jnp.int32, (t, LANES), 1)
    lower = lane < HEAD_DIM
    head_of_lane = lane % N_FOX_HEADS
    d = FOX_DIAG
    causal = (lax.broadcasted_iota(jnp.int32, (d, d), 0)
              >= lax.broadcasted_iota(jnp.int32, (d, d), 1))

    def tile_rows(i):
        return slice(i * t, (i + 1) * t)

    def pair_lanes(pp):
        return slice(pp * LANES, (pp + 1) * LANES)

    def parts(i, k0, nk):
        if k0 + nk < (i + 1) * t:
            return [(0, t, nk)]
        return [(r, r + d, nk - t + r + d) for r in range(0, t, d)]

    def mask_queries(pp, i):
        q = q_ref[0, tile_rows(i), pair_lanes(pp)]
        qa = qa_ref[0, tile_rows(i), :]
        zero = jnp.zeros_like(q)
        first_head = 2 * (FOX_PAIRS * pl.program_id(1) + pp)
        for e in range(2):
            qm_ref[pp, i, e] = jnp.concatenate(
                [jnp.where(lower, q, zero) if e == 0 else jnp.where(lower, zero, q),
                 jnp.where(head_of_lane == first_head + e, qa, zero)], axis=-1)

    def scores(pp, i, k0, nk, slot):
        for r0, r1, n in parts(i, k0, nk):
            keys = slice(k0, k0 + n)
            kk = jnp.concatenate([k_ref[0, keys, pair_lanes(pp)], ka_ref[0, keys, :]], axis=-1)
            for e in range(2):
                s_ref[slot, e, r0:r1, :n] = _dot_nt(qm_ref[pp, i, e, r0:r1], kk)

    def update(pp, i, k0, nk, slot):
        diag = k0 + nk == (i + 1) * t
        for r0, r1, n in parts(i, k0, nk):
            v = v_ref[0, k0:k0 + n, pair_lanes(pp)]
            one = jnp.ones_like(v)
            for e in range(2):
                s = s_ref[slot, e, r0:r1, :n]
                if diag:
                    last = jnp.where(causal, s[:, n - d:], NEG_BIG)
                    s = last if n == d else jnp.concatenate([s[:, :n - d], last], axis=-1)
                row_max = jnp.max(s, axis=-1, keepdims=True)
                m_new = (jnp.broadcast_to(row_max, (r1 - r0, LANES)) if k0 == 0
                         else jnp.maximum(m_ref[pp, e, r0:r1], row_max))
                p = jnp.concatenate(
                    [jnp.exp2(s[:, c * LANES:(c + 1) * LANES] - m_new)
                     for c in range(n // LANES)], axis=-1).astype(jnp.bfloat16)
                v_lower = lax.broadcasted_iota(jnp.int32, v.shape, 1) < HEAD_DIM
                ve = jnp.where(v_lower, v, one) if e == 0 else jnp.where(v_lower, one, v)
                pv = _dot(p, ve)
                if k0 > 0:
                    pv = jnp.exp2(m_ref[pp, e, r0:r1] - m_new) * acc_ref[pp, e, r0:r1] + pv
                acc_ref[pp, e, r0:r1] = pv
                m_ref[pp, e, r0:r1] = m_new

    def finish(pp, i):
        outs = []
        for e in range(2):
            acc = acc_ref[pp, e]
            outs.append(acc / pltpu.roll(acc, HEAD_DIM, 1))
        o_ref[0, tile_rows(i), pair_lanes(pp)] = (
            jnp.where(lower, outs[0], outs[1]).astype(o_ref.dtype))

    work = [(pp, i, k0, min(FOX_KEYS, (i + 1) * t - k0))
            for pp in range(FOX_PAIRS) for i in range(seq // t)
            for k0 in range(0, (i + 1) * t, FOX_KEYS)]

    def issue_scores(idx):
        pp, i, k0, nk = work[idx]
        if k0 == 0:
            mask_queries(pp, i)
        scores(pp, i, k0, nk, idx % 2)

    issue_scores(0)
    for idx, (pp, i, k0, nk) in enumerate(work):
        if idx + 1 < len(work):
            issue_scores(idx + 1)
        update(pp, i, k0, nk, idx % 2)
        if k0 + nk == (i + 1) * t:
            finish(pp, i)


def _fox_call(fq, qa, fk, ka, fv, weights, batch, seq):
    t = FOX_T
    bf16 = jnp.bfloat16
    groups = N_PAIRS // FOX_PAIRS
    shape3 = (batch, seq, N_PAIRS * LANES)
    aug3 = (batch, seq, LANES)
    spec = pl.BlockSpec((1, seq, FOX_PAIRS * LANES), lambda b, g: (b, 0, g))
    aug_spec = pl.BlockSpec((1, seq, LANES), lambda b, g: (b, 0, 0))
    steps = batch * groups
    w_specs = [pl.BlockSpec((w.shape[0] // steps, w.shape[1]), lambda b, g: (b * groups + g, 0))
               for w in weights]
    return pl.pallas_call(
        functools.partial(_fox_kernel, seq=seq, n_cast=len(weights)),
        grid=(batch, groups),
        in_specs=[spec, aug_spec, spec, aug_spec, spec] + w_specs,
        out_specs=[spec] + w_specs,
        out_shape=[jax.ShapeDtypeStruct(shape3, bf16)]
        + [jax.ShapeDtypeStruct(w.shape, bf16) for w in weights],
        scratch_shapes=[pltpu.VMEM((FOX_PAIRS, seq // t, 2, t, 2 * LANES), jnp.bfloat16),
                        pltpu.VMEM((2, 2, t, FOX_KEYS), jnp.float32),
                        pltpu.VMEM((FOX_PAIRS, 2, t, LANES), jnp.float32),
                        pltpu.VMEM((FOX_PAIRS, 2, t, LANES), jnp.float32)],
        compiler_params=pltpu.CompilerParams(
            dimension_semantics=("parallel", "parallel"),
            vmem_limit_bytes=VMEM_LIMIT),
        name="fox",
    )(fq.reshape(shape3), qa.reshape(aug3), fk.reshape(shape3), ka.reshape(aug3),
      fv.reshape(shape3), *weights)


def _t5_bucket(n):
    max_exact = N_BUCKETS // 2
    large = max_exact + (np.log(np.maximum(n, 1) / max_exact)
                         / np.log(MAX_DISTANCE / max_exact)
                         * (N_BUCKETS - max_exact)).astype(np.int32)
    large = np.minimum(large, N_BUCKETS - 1)
    return np.where(n < max_exact, n, large).astype(np.int32)


def _band_buckets():
    i = np.arange(Q_BLOCK)[:, None]
    j = np.arange(2 * Q_BLOCK)[None, :]
    dist = i + Q_BLOCK - j
    in_window = (dist >= 0) & (dist < WINDOW)
    return np.where(in_window, _t5_bucket(np.clip(dist, 0, None)), -1).astype(np.int32)


def _swa_kernel(bucket_ref, rel_ref, sink_ref, q_ref, k_ref, v_ref, o_ref, bias_ref, sinkc_ref,
                s_ref, *, seq):
    qb = Q_BLOCK
    qw = SWA_GROUP * HEAD_DIM

    @pl.when(pl.program_id(0) == 0)
    def _():
        bucket = bucket_ref[...]
        for h in range(N_SWA_HEADS):
            bias = jnp.full(bucket.shape, NEG_BIG, jnp.float32)
            for b in range(N_BUCKETS):
                bias = jnp.where(bucket == b, rel_ref[b, h] * LOG2E, bias)
            bias = jnp.where(lax.broadcasted_iota(jnp.int32, bucket.shape, 1) == 0,
                             sink_ref[h] * LOG2E, bias)
            group, g = divmod(h, SWA_GROUP)
            bias_ref[group, g * qb:(g + 1) * qb, :] = bias
            sinkc_ref[group, g * qb:(g + 1) * qb, :] = jnp.full((qb, LANES), sink_ref[h] * LOG2E)

    lower = lax.broadcasted_iota(jnp.int32, (qb, LANES), 1) < HEAD_DIM

    def band_of(n):
        if n == 0:
            return slice(0, qb), qb, slice(qb, 2 * qb)
        return slice((n - 1) * qb, (n + 1) * qb), 2 * qb, slice(0, 2 * qb)

    def scores(kv, n, slot):
        rows = slice(n * qb, (n + 1) * qb)
        band, width, cols = band_of(n)
        parts = []
        for pair in range(SWA_GROUP // 2):
            qp = q_ref[0, rows, kv * qw + pair * LANES:kv * qw + (pair + 1) * LANES]
            zero = jnp.zeros_like(qp)
            parts += [jnp.where(lower, qp, zero), jnp.where(lower, zero, qp)]
        s_ref[slot, :, :width] = (_dot_nt(jnp.concatenate(parts, axis=0), sink_slot(kv, n, k_ref))
                                  + bias_ref[kv, :, cols])

    def sink_slot(kv, n, ref):
        band, _, _ = band_of(n)
        x = ref[0, band, kv * LANES:(kv + 1) * LANES]
        if n == 0:
            return x
        return jnp.where(lax.broadcasted_iota(jnp.int32, x.shape, 0) == 0, jnp.zeros_like(x), x)

    def update(kv, n, slot):
        rows = slice(n * qb, (n + 1) * qb)
        _, width, _ = band_of(n)
        vb = sink_slot(kv, n, v_ref)
        vb = jnp.concatenate([vb, jnp.ones_like(vb)], axis=-1)
        s = s_ref[slot, :, :width]
        m = jnp.max(s, axis=-1, keepdims=True)
        if n == 0:
            sink = sinkc_ref[kv]
            m = jnp.maximum(m, sink)
        p = jnp.concatenate(
            [jnp.exp2(s[:, c * LANES:(c + 1) * LANES] - m) for c in range(width // LANES)],
            axis=-1).astype(jnp.bfloat16)
        acc = _dot(p, vb)
        denom = acc[:, LANES:] + jnp.exp2(sink - m) if n == 0 else acc[:, LANES:]
        out = acc[:, :LANES] / denom
        for pair in range(SWA_GROUP // 2):
            even = slice(2 * pair * qb, (2 * pair + 1) * qb)
            odd = slice((2 * pair + 1) * qb, (2 * pair + 2) * qb)
            o_ref[0, rows, kv * qw + pair * LANES:kv * qw + (pair + 1) * LANES] = jnp.where(
                lower, out[even], out[odd]).astype(o_ref.dtype)

    work = [(kv, n) for kv in range(N_SWA_KV_HEADS) for n in range(seq // qb)]
    scores(*work[0], 0)
    for idx, item in enumerate(work):
        if idx + 1 < len(work):
            scores(*work[idx + 1], (idx + 1) % 2)
        update(*item, idx % 2)


def _swa_call(sq, sk, sv, rel_bias, sinks, batch, seq):
    bucket = _band_buckets()
    smem = pl.BlockSpec(memory_space=pltpu.SMEM)
    q_spec = pl.BlockSpec((1, seq, D_SWA), lambda b: (b, 0, 0))
    kv_spec = pl.BlockSpec((1, seq, N_SWA_KV_HEADS * LANES), lambda b: (b, 0, 0))
    return pl.pallas_call(
        functools.partial(_swa_kernel, seq=seq),
        grid=(batch,),
        in_specs=[pl.BlockSpec(bucket.shape, lambda b: (0, 0)), smem, smem,
                  q_spec, kv_spec, kv_spec],
        out_specs=q_spec,
        out_shape=jax.ShapeDtypeStruct((batch, seq, D_SWA), jnp.bfloat16),
        scratch_shapes=[
            pltpu.VMEM((N_SWA_KV_HEADS, SWA_GROUP * Q_BLOCK, 2 * Q_BLOCK), jnp.float32),
            pltpu.VMEM((N_SWA_KV_HEADS, SWA_GROUP * Q_BLOCK, LANES), jnp.float32),
            pltpu.VMEM((2, SWA_GROUP * Q_BLOCK, 2 * Q_BLOCK), jnp.float32)],
        compiler_params=pltpu.CompilerParams(
            dimension_semantics=("arbitrary",),
            vmem_limit_bytes=VMEM_LIMIT),
        name="swa",
    )(jnp.asarray(bucket), rel_bias, sinks, sq.reshape(batch, seq, D_SWA),
      sk.reshape(batch, seq, 2 * LANES), sv.reshape(batch, seq, 2 * LANES))


def _post_kernel(fox_ref, swa_ref, x_ref, p_ref, wo_ref, w1_ref, w2_ref, wg_ref, wp_ref,
                 gains_ref, o_ref):
    bf16 = jnp.bfloat16
    h_parts, m_parts = [], []
    for r in range(0, POST_TM, POST_STRIP):
        rows = slice(r, r + POST_STRIP)
        mix = _dot(jnp.concatenate([fox_ref[rows, :], swa_ref[rows, :]], axis=-1), wo_ref[...])
        hh = x_ref[rows, :] + _rms(mix) * gains_ref[0:1, :]
        h_parts.append(hh)
        m_parts.append((_rms(hh) * gains_ref[1:2, :]).astype(bf16))
    h = jnp.concatenate(h_parts, axis=0)
    m = jnp.concatenate(m_parts, axis=0)
    ple = _dot(p_ref[...].astype(bf16), wp_ref[...])

    y = jnp.zeros((POST_TM, D_MODEL), jnp.float32)
    for c in range(D_FF // FF_CHUNK):
        cols = slice(c * FF_CHUNK, (c + 1) * FF_CHUNK)
        hid = jnp.square(jnp.maximum(_dot(m, w1_ref[:, cols]), 0.0))
        y = y + _dot(hid.astype(bf16), w2_ref[cols, :])

    for r in range(0, POST_TM, POST_STRIP):
        rows = slice(r, r + POST_STRIP)
        hh = h[rows] + _rms(y[rows]) * gains_ref[2:3, :]
        gate = jax.nn.sigmoid(_dot(hh.astype(bf16), wg_ref[...]))
        o_ref[rows, :] = hh + _rms(ple[rows] * gate) * gains_ref[3:4, :]


def _post_call(fox, swa, x2, p2, wo, w1, w2, wg, wp, gains):
    n = x2.shape[0]
    row = lambda i: (i, 0)
    const = lambda i: (0, 0)

    def resident(shape):
        return pl.BlockSpec(shape, const, pipeline_mode=pl.Buffered(1))

    return pl.pallas_call(
        _post_kernel,
        grid=(n // POST_TM,),
        in_specs=[
            pl.BlockSpec((POST_TM, D_FOX), row),
            pl.BlockSpec((POST_TM, D_SWA), row),
            pl.BlockSpec((POST_TM, D_MODEL), row),
            pl.BlockSpec((POST_TM, D_PLE), row),
            resident(wo.shape), resident(w1.shape), resident(w2.shape),
            resident(wg.shape), resident(wp.shape), resident(gains.shape),
        ],
        out_specs=pl.BlockSpec((POST_TM, D_MODEL), row),
        out_shape=jax.ShapeDtypeStruct((n, D_MODEL), jnp.float32),
        compiler_params=pltpu.CompilerParams(
            dimension_semantics=("parallel",),
            vmem_limit_bytes=VMEM_LIMIT),
        name="post",
    )(fox, swa, x2, p2, wo, w1, w2, wg, wp, gains)


def _gate_lanes(v):
    return jnp.tile(v, (1, LANES // N_FOX_HEADS))


def kernel(x, p, w_in, b_forget, w_out, rel_bias, swa_sinks, g_attn_pre, g_attn_post,
           w_ff1, w_ff2, g_ff_pre, g_ff_post, w_ple, w_ple_gate, g_ple_post):
    batch, seq, _ = x.shape
    n = batch * seq
    bf16 = jnp.bfloat16
    h = x.reshape(n, D_MODEL)
    for i in range(p.shape[0]):
        b_ff = _gate_lanes(b_forget[i].reshape(1, N_FOX_HEADS))
        fq, fk, fv, qa, ka, sq, sk, sv = _proj_call(
            h, g_attn_pre[i].reshape(1, D_MODEL), w_in[i].T, b_ff, batch, seq)
        fox, wo, w1, w2, wg = _fox_call(
            fq, qa, fk, ka, fv, [w_out[i], w_ff1[i], w_ff2[i], w_ple_gate[i]], batch, seq)
        swa = _swa_call(sq, sk, sv, rel_bias, swa_sinks[i], batch, seq).reshape(n, D_SWA)
        gains = jnp.stack([g_attn_post[i], g_ff_pre[i], g_ff_post[i], g_ple_post[i]])
        h = _post_call(fox.reshape(n, D_FOX), swa, h, p[i].reshape(n, D_PLE),
                       wo, w1, w2, wg, w_ple[i].astype(bf16), gains)
    return h.reshape(batch, seq, D_MODEL)
```

```python
import functools

import jax
import jax.numpy as jnp
import numpy as np
from jax import lax
from jax.experimental import pallas as pl
from jax.experimental.pallas import tpu as pltpu

D_MODEL = 1024
HEAD_DIM = 64
N_FOX_HEADS = 8
N_SWA_HEADS = 8
N_SWA_KV_HEADS = 2
SWA_GROUP = N_SWA_HEADS // N_SWA_KV_HEADS
D_FOX = N_FOX_HEADS * HEAD_DIM
D_SWA = N_SWA_HEADS * HEAD_DIM
D_SWA_KV = N_SWA_KV_HEADS * HEAD_DIM
D_FF = 4 * D_MODEL
D_PLE = 256
WINDOW = 128
Q_BLOCK = 128
N_BUCKETS = 32
MAX_DISTANCE = 128
RMS_EPS = 1e-6

LANES = 128
BF16_SUBLANES = 16
N_PAIRS = N_FOX_HEADS // 2
N_SPLIT = 3
N_TERMS = HEAD_DIM // N_FOX_HEADS
TERM_SRC = (0, 0, 0, 1, 1, 1, 2, 2)
TERM_CUT = (0, 1, 2, 0, 1, 2, 0, 1)
NEG_BIG = -1e30
LOG2E = float(np.log2(np.e))

PROJ_TM = 1024
PROJ_STRIP = 512
FOX_T = 512
FOX_DIAG = 256
FOX_PAIRS = 1
FOX_KEYS = 1024
POST_TM = 512
POST_STRIP = 256
FF_CHUNK = 1024
VMEM_LIMIT = 56 * 1024 * 1024

C_FQ, C_FK, C_FV = 0, D_FOX, 2 * D_FOX
C_SQ = 3 * D_FOX
C_SK = C_SQ + D_SWA
C_SV = C_SK + D_SWA_KV
C_FF = C_SV + D_SWA_KV
D_PROJ = C_FF + LANES


def _rms(v):
    return v * lax.rsqrt(jnp.mean(v * v, axis=-1, keepdims=True) + RMS_EPS)


def _dot(a, b):
    return jnp.dot(a, b, preferred_element_type=jnp.float32)


def _dot_nt(a, b):
    return lax.dot_general(a, b, (((1,), (1,)), ((), ())),
                           preferred_element_type=jnp.float32)


def _proj_kernel(x_ref, g_ref, wt_ref, bf_ref, tri_ref,
                 fq_ref, fk_ref, fv_ref, qa_ref, ka_ref, sq_ref, sk_ref, sv_ref,
                 carry_ref, w_ref):
    @pl.when((pl.program_id(0) == 0) & (pl.program_id(1) == 0))
    def _():
        bf16 = jnp.bfloat16
        q_scale = HEAD_DIM ** -0.5 * LOG2E
        src = np.cumsum([0, D_FOX, D_FOX, D_FOX, N_FOX_HEADS, D_SWA])
        src_fq, src_fk, src_fv, src_ff, src_sq, src_skv = (int(c) for c in src)
        w_ref[C_FQ:C_FQ + D_FOX] = (wt_ref[src_fq:src_fq + D_FOX] * q_scale).astype(bf16)
        w_ref[C_FK:C_FK + D_FOX] = wt_ref[src_fk:src_fk + D_FOX].astype(bf16)
        w_ref[C_FV:C_FV + D_FOX] = wt_ref[src_fv:src_fv + D_FOX].astype(bf16)
        w_ref[C_SQ:C_SQ + D_SWA] = (wt_ref[src_sq:src_sq + D_SWA] * q_scale).astype(bf16)
        w_ref[C_SK:C_SK + 2 * D_SWA_KV] = wt_ref[src_skv:src_skv + 2 * D_SWA_KV].astype(bf16)
        gate = wt_ref[src_ff:src_ff + N_FOX_HEADS]
        w_ref[C_FF:C_FF + LANES] = jnp.tile(gate, (LANES // N_FOX_HEADS, 1)).astype(bf16)

    @pl.when(pl.program_id(1) == 0)
    def _():
        carry_ref[...] = jnp.zeros_like(carry_ref)

    lane = lax.broadcasted_iota(jnp.int32, (PROJ_STRIP, LANES), 1)
    lower = lane < HEAD_DIM
    term = (lane // N_FOX_HEADS) % N_TERMS
    head_lane = lax.broadcasted_iota(jnp.int32, (1, LANES), 1) % N_FOX_HEADS
    gate_bias = jnp.zeros((1, LANES), jnp.float32)
    for head in range(N_FOX_HEADS):
        gate_bias = jnp.where(head_lane == head, bf_ref[head], gate_bias)

    for r in range(0, PROJ_TM, PROJ_STRIP):
        rows = slice(r, r + PROJ_STRIP)
        a = (_rms(x_ref[rows, :]) * g_ref[...]).astype(jnp.bfloat16)

        def proj(lo, width):
            return _dot_nt(a, w_ref[lo:lo + width])

        v = proj(C_FF, LANES) + gate_bias
        logf = (jnp.minimum(v, 0.0) - jnp.log1p(jnp.exp(-jnp.abs(v)))) * LOG2E
        x = _pick_piece(_split_bf16(logf), term, TERM_SRC)
        c = carry_ref[...] + _dot(tri_ref[...], x)
        carry_ref[...] = c[PROJ_STRIP - 1:PROJ_STRIP, :]
        terms = _pick_piece(_split_bf16(c), term, TERM_CUT)
        one = jnp.ones_like(terms)
        qa_ref[rows, :] = jnp.where(lower, terms, one)
        ka_ref[rows, :] = jnp.where(lower, one, -terms)

        kv = proj(C_SK, 2 * D_SWA_KV)
        for half, out_ref in enumerate((sk_ref, sv_ref)):
            z = kv[:, half * LANES:(half + 1) * LANES]
            zr = pltpu.roll(z, HEAD_DIM, 1)
            out_ref[rows, :LANES] = jnp.where(lower, z, zr).astype(jnp.bfloat16)
            out_ref[rows, LANES:] = jnp.where(lower, zr, z).astype(jnp.bfloat16)

        fq_ref[rows, :] = proj(C_FQ, D_FOX).astype(jnp.bfloat16)
        fk_ref[rows, :] = proj(C_FK, D_FOX).astype(jnp.bfloat16)
        fv_ref[rows, :] = proj(C_FV, D_FOX).astype(jnp.bfloat16)
        sq_ref[rows, :] = proj(C_SQ, D_SWA).astype(jnp.bfloat16)


def _split_bf16(v):
    pieces = []
    r = v
    for _ in range(N_SPLIT):
        t = r.astype(jnp.bfloat16)
        pieces.append(t)
        r = r - t.astype(jnp.float32)
    return pieces


def _pick_piece(pieces, term, piece_of_term):
    out = pieces[piece_of_term[-1]]
    for t in range(N_TERMS - 2, -1, -1):
        if piece_of_term[t] != piece_of_term[t + 1]:
            out = jnp.where(term <= t, pieces[piece_of_term[t]], out)
    return out


def _proj_call(x2, g_pre, w_in_t, b_forget, batch, seq):
    n = batch * seq
    steps = seq // PROJ_TM
    tri = np.tril(np.ones((PROJ_STRIP, PROJ_STRIP), np.float32))
    row = lambda b, s: (b * steps + s, 0)
    const = lambda b, s: (0, 0)
    bf16 = jnp.bfloat16

    def out(width):
        return (jax.ShapeDtypeStruct((n, width), bf16), pl.BlockSpec((PROJ_TM, width), row))

    outs = [out(D_FOX), out(D_FOX), out(D_FOX), out(LANES), out(LANES),
            out(D_SWA), out(2 * LANES), out(2 * LANES)]
    return pl.pallas_call(
        _proj_kernel,
        grid=(batch, steps),
        in_specs=[
            pl.BlockSpec((PROJ_TM, D_MODEL), row),
            pl.BlockSpec((1, D_MODEL), const),
            pl.BlockSpec(w_in_t.shape, const, pipeline_mode=pl.Buffered(1)),
            pl.BlockSpec(memory_space=pltpu.SMEM),
            pl.BlockSpec((PROJ_STRIP, PROJ_STRIP), const),
        ],
        out_specs=[o[1] for o in outs],
        out_shape=[o[0] for o in outs],
        scratch_shapes=[pltpu.VMEM((1, LANES), jnp.float32),
                        pltpu.VMEM((D_PROJ, D_MODEL), bf16)],
        compiler_params=pltpu.CompilerParams(
            dimension_semantics=("arbitrary", "arbitrary"),
            vmem_limit_bytes=VMEM_LIMIT),
        name="proj",
    )(x2, g_pre, w_in_t, b_forget, jnp.asarray(tri, bf16))


def _fox_kernel(*refs, seq, n_cast):
    q_ref, qa_ref, k_ref, ka_ref, v_ref = refs[:5]
    w_refs = refs[5:5 + n_cast]
    o_ref = refs[5 + n_cast]
    wb_refs = refs[6 + n_cast:6 + 2 * n_cast]
    qm_ref, s_ref, m_ref, acc_ref = refs[6 + 2 * n_cast:]
    for w_ref, wb_ref in zip(w_refs, wb_refs):
        wb_ref[...] = w_ref[...].astype(wb_ref.dtype)

    t = FOX_T
    lane = lax.broadcasted_iota(jnp.int32, (t, LANES), 1)
    lower = lane < HEAD_DIM
    head_of_lane = lane % N_FOX_HEADS
    d = FOX_DIAG
    causal = (lax.broadcasted_iota(jnp.int32, (d, d), 0)
              >= lax.broadcasted_iota(jnp.int32, (d, d), 1))

    def tile_rows(i):
        return slice(i * t, (i + 1) * t)

    def pair_lanes(pp):
        return slice(pp * LANES, (pp + 1) * LANES)

    def parts(i, k0, nk):
        if k0 + nk < (i + 1) * t:
            return [(0, t, nk)]
        return [(r, r + d, nk - t + r + d) for r in range(0, t, d)]

    def mask_queries(pp, i):
        q = q_ref[0, tile_rows(i), pair_lanes(pp)]
        qa = qa_ref[0, tile_rows(i), :]
        zero = jnp.zeros_like(q)
        first_head = 2 * (FOX_PAIRS * pl.program_id(1) + pp)
        for e in range(2):
            qm_ref[pp, i, e] = jnp.concatenate(
                [jnp.where(lower, q, zero) if e == 0 else jnp.where(lower, zero, q),
                 jnp.where(head_of_lane == first_head + e, qa, zero)], axis=-1)

    def scores(pp, i, k0, nk, slot):
        for r0, r1, n in parts(i, k0, nk):
            keys = slice(k0, k0 + n)
            kk = jnp.concatenate([k_ref[0, keys, pair_lanes(pp)], ka_ref[0, keys, :]], axis=-1)
            for e in range(2):
                s_ref[slot, e, r0:r1, :n] = _dot_nt(qm_ref[pp, i, e, r0:r1], kk)

    def update(pp, i, k0, nk, slot):
        diag = k0 + nk == (i + 1) * t
        for r0, r1, n in parts(i, k0, nk):
            v = v_ref[0, k0:k0 + n, pair_lanes(pp)]
            one = jnp.ones_like(v)
            for e in range(2):
                s = s_ref[slot, e, r0:r1, :n]
                if diag:
                    last = jnp.where(causal, s[:, n - d:], NEG_BIG)
                    s = last if n == d else jnp.concatenate([s[:, :n - d], last], axis=-1)
                row_max = jnp.max(s, axis=-1, keepdims=True)
                m_new = (jnp.broadcast_to(row_max, (r1 - r0, LANES)) if k0 == 0
                         else jnp.maximum(m_ref[pp, e, r0:r1], row_max))
                p = jnp.concatenate(
                    [jnp.exp2(s[:, c * LANES:(c + 1) * LANES] - m_new)
                     for c in range(n // LANES)], axis=-1).astype(jnp.bfloat16)
                v_lower = lax.broadcasted_iota(jnp.int32, v.shape, 1) < HEAD_DIM
                ve = jnp.where(v_lower, v, one) if e == 0 else jnp.where(v_lower, one, v)
                pv = _dot(p, ve)
                if k0 > 0:
                    pv = jnp.exp2(m_ref[pp, e, r0:r1] - m_new) * acc_ref[pp, e, r0:r1] + pv
                acc_ref[pp, e, r0:r1] = pv
                m_ref[pp, e, r0:r1] = m_new

    def finish(pp, i):
        outs = []
        for e in range(2):
            acc = acc_ref[pp, e]
            outs.append(acc / pltpu.roll(acc, HEAD_DIM, 1))
        o_ref[0, tile_rows(i), pair_lanes(pp)] = (
            jnp.where(lower, outs[0], outs[1]).astype(o_ref.dtype))

    work = [(pp, i, k0, min(FOX_KEYS, (i + 1) * t - k0))
            for pp in range(FOX_PAIRS) for i in range(seq // t)
            for k0 in range(0, (i + 1) * t, FOX_KEYS)]

    def issue_scores(idx):
        pp, i, k0, nk = work[idx]
        if k0 == 0:
            mask_queries(pp, i)
        scores(pp, i, k0, nk, idx % 2)

    issue_scores(0)
    for idx, (pp, i, k0, nk) in enumerate(work):
        if idx + 1 < len(work):
            issue_scores(idx + 1)
        update(pp, i, k0, nk, idx % 2)
        if k0 + nk == (i + 1) * t:
            finish(pp, i)


def _fox_call(fq, qa, fk, ka, fv, weights, batch, seq):
    t = FOX_T
    bf16 = jnp.bfloat16
    groups = N_PAIRS // FOX_PAIRS
    shape3 = (batch, seq, N_PAIRS * LANES)
    aug3 = (batch, seq, LANES)
    spec = pl.BlockSpec((1, seq, FOX_PAIRS * LANES), lambda b, g: (b, 0, g))
    aug_spec = pl.BlockSpec((1, seq, LANES), lambda b, g: (b, 0, 0))
    steps = batch * groups
    def w_spec(w):
        per_step = w.shape[0] // steps
        rows = max(per_step, BF16_SUBLANES)
        return pl.BlockSpec((rows, w.shape[1]),
                            lambda b, g: ((b * groups + g) * per_step // rows, 0))

    w_specs = [w_spec(w) for w in weights]
    return pl.pallas_call(
        functools.partial(_fox_kernel, seq=seq, n_cast=len(weights)),
        grid=(batch, groups),
        in_specs=[spec, aug_spec, spec, aug_spec, spec] + w_specs,
        out_specs=[spec] + w_specs,
        out_shape=[jax.ShapeDtypeStruct(shape3, bf16)]
        + [jax.ShapeDtypeStruct(w.shape, bf16) for w in weights],
        scratch_shapes=[pltpu.VMEM((FOX_PAIRS, seq // t, 2, t, 2 * LANES), jnp.bfloat16),
                        pltpu.VMEM((2, 2, t, FOX_KEYS), jnp.float32),
                        pltpu.VMEM((FOX_PAIRS, 2, t, LANES), jnp.float32),
                        pltpu.VMEM((FOX_PAIRS, 2, t, LANES), jnp.float32)],
        compiler_params=pltpu.CompilerParams(
            dimension_semantics=("arbitrary", "arbitrary"),
            vmem_limit_bytes=VMEM_LIMIT),
        name="fox",
    )(fq.reshape(shape3), qa.reshape(aug3), fk.reshape(shape3), ka.reshape(aug3),
      fv.reshape(shape3), *weights)


def _t5_bucket(n):
    max_exact = N_BUCKETS // 2
    large = max_exact + (np.log(np.maximum(n, 1) / max_exact)
                         / np.log(MAX_DISTANCE / max_exact)
                         * (N_BUCKETS - max_exact)).astype(np.int32)
    large = np.minimum(large, N_BUCKETS - 1)
    return np.where(n < max_exact, n, large).astype(np.int32)


def _band_buckets():
    i = np.arange(Q_BLOCK)[:, None]
    j = np.arange(2 * Q_BLOCK)[None, :]
    dist = i + Q_BLOCK - j
    in_window = (dist >= 0) & (dist < WINDOW)
    return np.where(in_window, _t5_bucket(np.clip(dist, 0, None)), -1).astype(np.int32)


def _swa_kernel(bucket_ref, rel_ref, sink_ref, q_ref, k_ref, v_ref, o_ref, bias_ref, sinkc_ref,
                s_ref, *, seq):
    qb = Q_BLOCK
    qw = SWA_GROUP * HEAD_DIM

    @pl.when(pl.program_id(0) == 0)
    def _():
        bucket = bucket_ref[...]
        for h in range(N_SWA_HEADS):
            bias = jnp.full(bucket.shape, NEG_BIG, jnp.float32)
            for b in range(N_BUCKETS):
                bias = jnp.where(bucket == b, rel_ref[h, b] * LOG2E, bias)
            bias = jnp.where(lax.broadcasted_iota(jnp.int32, bucket.shape, 1) == 0,
                             sink_ref[h] * LOG2E, bias)
            group, g = divmod(h, SWA_GROUP)
            bias_ref[group, g * qb:(g + 1) * qb, :] = bias
            sinkc_ref[group, g * qb:(g + 1) * qb, :] = jnp.full((qb, LANES), sink_ref[h] * LOG2E)

    lower = lax.broadcasted_iota(jnp.int32, (qb, LANES), 1) < HEAD_DIM

    def band_of(n):
        if n == 0:
            return slice(0, qb), qb, slice(qb, 2 * qb)
        return slice((n - 1) * qb, (n + 1) * qb), 2 * qb, slice(0, 2 * qb)

    def scores(kv, n, slot):
        rows = slice(n * qb, (n + 1) * qb)
        band, width, cols = band_of(n)
        parts = []
        for pair in range(SWA_GROUP // 2):
            qp = q_ref[0, rows, kv * qw + pair * LANES:kv * qw + (pair + 1) * LANES]
            zero = jnp.zeros_like(qp)
            parts += [jnp.where(lower, qp, zero), jnp.where(lower, zero, qp)]
        s_ref[slot, :, :width] = (_dot_nt(jnp.concatenate(parts, axis=0), sink_slot(kv, n, k_ref))
                                  + bias_ref[kv, :, cols])

    def sink_slot(kv, n, ref):
        band, _, _ = band_of(n)
        x = ref[0, band, kv * LANES:(kv + 1) * LANES]
        if n == 0:
            return x
        return jnp.where(lax.broadcasted_iota(jnp.int32, x.shape, 0) == 0, jnp.zeros_like(x), x)

    def update(kv, n, slot):
        rows = slice(n * qb, (n + 1) * qb)
        _, width, _ = band_of(n)
        vb = sink_slot(kv, n, v_ref)
        vb = jnp.concatenate([vb, jnp.ones_like(vb)], axis=-1)
        s = s_ref[slot, :, :width]
        m = jnp.max(s, axis=-1, keepdims=True)
        if n == 0:
            sink = sinkc_ref[kv]
            m = jnp.maximum(m, sink)
        p = jnp.concatenate(
            [jnp.exp2(s[:, c * LANES:(c + 1) * LANES] - m) for c in range(width // LANES)],
            axis=-1).astype(jnp.bfloat16)
        acc = _dot(p, vb)
        denom = acc[:, LANES:] + jnp.exp2(sink - m) if n == 0 else acc[:, LANES:]
        out = acc[:, :LANES] / denom
        for pair in range(SWA_GROUP // 2):
            even = slice(2 * pair * qb, (2 * pair + 1) * qb)
            odd = slice((2 * pair + 1) * qb, (2 * pair + 2) * qb)
            o_ref[0, rows, kv * qw + pair * LANES:kv * qw + (pair + 1) * LANES] = jnp.where(
                lower, out[even], out[odd]).astype(o_ref.dtype)

    work = [(kv, n) for kv in range(N_SWA_KV_HEADS) for n in range(seq // qb)]
    scores(*work[0], 0)
    for idx, item in enumerate(work):
        if idx + 1 < len(work):
            scores(*work[idx + 1], (idx + 1) % 2)
        update(*item, idx % 2)


def _swa_call(sq, sk, sv, rel_bias, sinks, batch, seq):
    bucket = _band_buckets()
    smem = pl.BlockSpec(memory_space=pltpu.SMEM)
    q_spec = pl.BlockSpec((1, seq, D_SWA), lambda b: (b, 0, 0))
    kv_spec = pl.BlockSpec((1, seq, N_SWA_KV_HEADS * LANES), lambda b: (b, 0, 0))
    return pl.pallas_call(
        functools.partial(_swa_kernel, seq=seq),
        grid=(batch,),
        in_specs=[pl.BlockSpec(bucket.shape, lambda b: (0, 0)), smem, smem,
                  q_spec, kv_spec, kv_spec],
        out_specs=q_spec,
        out_shape=jax.ShapeDtypeStruct((batch, seq, D_SWA), jnp.bfloat16),
        scratch_shapes=[
            pltpu.VMEM((N_SWA_KV_HEADS, SWA_GROUP * Q_BLOCK, 2 * Q_BLOCK), jnp.float32),
            pltpu.VMEM((N_SWA_KV_HEADS, SWA_GROUP * Q_BLOCK, LANES), jnp.float32),
            pltpu.VMEM((2, SWA_GROUP * Q_BLOCK, 2 * Q_BLOCK), jnp.float32)],
        compiler_params=pltpu.CompilerParams(
            dimension_semantics=("arbitrary",),
            vmem_limit_bytes=VMEM_LIMIT),
        name="swa",
    )(jnp.asarray(bucket), rel_bias.T, sinks, sq.reshape(batch, seq, D_SWA),
      sk.reshape(batch, seq, 2 * LANES), sv.reshape(batch, seq, 2 * LANES))


def _post_kernel(fox_ref, swa_ref, x_ref, p_ref, wo_ref, w1_ref, w2_ref, wg_ref, wp_ref,
                 g_attn_ref, g_pre_ref, g_ff_ref, g_ple_ref, o_ref):
    bf16 = jnp.bfloat16
    h_parts, m_parts = [], []
    for r in range(0, POST_TM, POST_STRIP):
        rows = slice(r, r + POST_STRIP)
        mix = _dot(jnp.concatenate([fox_ref[rows, :], swa_ref[rows, :]], axis=-1), wo_ref[...])
        hh = x_ref[rows, :] + _rms(mix) * g_attn_ref[...]
        h_parts.append(hh)
        m_parts.append((_rms(hh) * g_pre_ref[...]).astype(bf16))
    h = jnp.concatenate(h_parts, axis=0)
    m = jnp.concatenate(m_parts, axis=0)
    ple = _dot(p_ref[...].astype(bf16), wp_ref[...])

    y = jnp.zeros((POST_TM, D_MODEL), jnp.float32)
    for c in range(D_FF // FF_CHUNK):
        cols = slice(c * FF_CHUNK, (c + 1) * FF_CHUNK)
        hid = jnp.square(jnp.maximum(_dot(m, w1_ref[:, cols]), 0.0))
        y = y + _dot(hid.astype(bf16), w2_ref[cols, :])

    for r in range(0, POST_TM, POST_STRIP):
        rows = slice(r, r + POST_STRIP)
        hh = h[rows] + _rms(y[rows]) * g_ff_ref[...]
        gate = jax.nn.sigmoid(_dot(hh.astype(bf16), wg_ref[...]))
        o_ref[rows, :] = hh + _rms(ple[rows] * gate) * g_ple_ref[...]


def _post_call(fox, swa, x2, p2, wo, w1, w2, wg, wp, gains):
    n = x2.shape[0]
    row = lambda i: (i, 0)
    const = lambda i: (0, 0)

    def resident(shape):
        return pl.BlockSpec(shape, const, pipeline_mode=pl.Buffered(1))

    return pl.pallas_call(
        _post_kernel,
        grid=(n // POST_TM,),
        in_specs=[
            pl.BlockSpec((POST_TM, D_FOX), row),
            pl.BlockSpec((POST_TM, D_SWA), row),
            pl.BlockSpec((POST_TM, D_MODEL), row),
            pl.BlockSpec((POST_TM, D_PLE), row),
            resident(wo.shape), resident(w1.shape), resident(w2.shape),
            resident(wg.shape), resident(wp.shape), *[resident(g.shape) for g in gains],
        ],
        out_specs=pl.BlockSpec((POST_TM, D_MODEL), row),
        out_shape=jax.ShapeDtypeStruct((n, D_MODEL), jnp.float32),
        compiler_params=pltpu.CompilerParams(
            dimension_semantics=("parallel",),
            vmem_limit_bytes=VMEM_LIMIT),
        name="post",
    )(fox, swa, x2, p2, wo, w1, w2, wg, wp, *gains)


def kernel(x, p, w_in, b_forget, w_out, rel_bias, swa_sinks, g_attn_pre, g_attn_post,
           w_ff1, w_ff2, g_ff_pre, g_ff_post, w_ple, w_ple_gate, g_ple_post):
    batch, seq, _ = x.shape
    n = batch * seq
    h = x.reshape(n, D_MODEL)
    for i in range(p.shape[0]):
        fq, fk, fv, qa, ka, sq, sk, sv = _proj_call(
            h, g_attn_pre[i].reshape(1, D_MODEL), w_in[i].T, b_forget[i], batch, seq)
        fox, wo, w1, w2, wg, wp = _fox_call(
            fq, qa, fk, ka, fv, [w_out[i], w_ff1[i], w_ff2[i], w_ple_gate[i], w_ple[i]],
            batch, seq)
        swa = _swa_call(sq, sk, sv, rel_bias, swa_sinks[i], batch, seq).reshape(n, D_SWA)
        gains = [g[i].reshape(1, D_MODEL) for g in (g_attn_post, g_ff_pre, g_ff_post, g_ple_post)]
        h = _post_call(fox.reshape(n, D_FOX), swa, h, p[i].reshape(n, D_PLE),
                       wo, w1, w2, wg, wp, gains)
    return h.reshape(batch, seq, D_MODEL)
```

```python
import functools

import jax
import jax.numpy as jnp
import numpy as np
from jax import lax
from jax.experimental import pallas as pl
from jax.experimental.pallas import tpu as pltpu

D_MODEL = 1024
HEAD_DIM = 64
N_FOX_HEADS = 8
N_SWA_HEADS = 8
N_SWA_KV_HEADS = 2
SWA_GROUP = N_SWA_HEADS // N_SWA_KV_HEADS
D_FOX = N_FOX_HEADS * HEAD_DIM
D_SWA = N_SWA_HEADS * HEAD_DIM
D_SWA_KV = N_SWA_KV_HEADS * HEAD_DIM
D_FF = 4 * D_MODEL
D_PLE = 256
WINDOW = 128
Q_BLOCK = 128
N_BUCKETS = 32
MAX_DISTANCE = 128
RMS_EPS = 1e-6

LANES = 128
N_PAIRS = N_FOX_HEADS // 2
N_SPLIT = 3
N_TERMS = HEAD_DIM // N_FOX_HEADS
TERM_SRC = (0, 0, 0, 1, 1, 1, 2, 2)
TERM_CUT = (0, 1, 2, 0, 1, 2, 0, 1)
NEG_BIG = -1e30
LOG2E = float(np.log2(np.e))

PROJ_TM = 1024
PROJ_STRIP = 512
FOX_T = 512
FOX_DIAG = 256
FOX_PAIRS = 1
FOX_KEYS = 1024
POST_TM = 512
POST_STRIP = 256
FF_CHUNK = 1024
VMEM_LIMIT = 56 * 1024 * 1024

C_FQ, C_FK, C_FV = 0, D_FOX, 2 * D_FOX
C_SQ = 3 * D_FOX
C_SK = C_SQ + D_SWA
C_SV = C_SK + D_SWA_KV
C_FF = C_SV + D_SWA_KV
D_PROJ = C_FF + LANES


def _rms(v):
    return v * lax.rsqrt(jnp.mean(v * v, axis=-1, keepdims=True) + RMS_EPS)


def _dot(a, b):
    return jnp.dot(a, b, preferred_element_type=jnp.float32)


def _dot_nt(a, b):
    return lax.dot_general(a, b, (((1,), (1,)), ((), ())),
                           preferred_element_type=jnp.float32)


def _proj_kernel(x_ref, g_ref, wt_ref, bf_ref, tri_ref,
                 fq_ref, fk_ref, fv_ref, qa_ref, ka_ref, sq_ref, sk_ref, sv_ref,
                 carry_ref, w_ref):
    @pl.when((pl.program_id(0) == 0) & (pl.program_id(1) == 0))
    def _():
        bf16 = jnp.bfloat16
        q_scale = HEAD_DIM ** -0.5 * LOG2E
        src = np.cumsum([0, D_FOX, D_FOX, D_FOX, N_FOX_HEADS, D_SWA])
        src_fq, src_fk, src_fv, src_ff, src_sq, src_skv = (int(c) for c in src)
        w_ref[C_FQ:C_FQ + D_FOX] = (wt_ref[src_fq:src_fq + D_FOX] * q_scale).astype(bf16)
        w_ref[C_FK:C_FK + D_FOX] = wt_ref[src_fk:src_fk + D_FOX].astype(bf16)
        w_ref[C_FV:C_FV + D_FOX] = wt_ref[src_fv:src_fv + D_FOX].astype(bf16)
        w_ref[C_SQ:C_SQ + D_SWA] = (wt_ref[src_sq:src_sq + D_SWA] * q_scale).astype(bf16)
        w_ref[C_SK:C_SK + 2 * D_SWA_KV] = wt_ref[src_skv:src_skv + 2 * D_SWA_KV].astype(bf16)
        gate = wt_ref[src_ff:src_ff + N_FOX_HEADS]
        w_ref[C_FF:C_FF + LANES] = jnp.tile(gate, (LANES // N_FOX_HEADS, 1)).astype(bf16)

    @pl.when(pl.program_id(1) == 0)
    def _():
        carry_ref[...] = jnp.zeros_like(carry_ref)

    lane = lax.broadcasted_iota(jnp.int32, (PROJ_STRIP, LANES), 1)
    lower = lane < HEAD_DIM
    term = (lane // N_FOX_HEADS) % N_TERMS
    head_lane = lax.broadcasted_iota(jnp.int32, (1, LANES), 1) % N_FOX_HEADS
    gate_bias = jnp.zeros((1, LANES), jnp.float32)
    for head in range(N_FOX_HEADS):
        gate_bias = jnp.where(head_lane == head, bf_ref[head], gate_bias)

    for r in range(0, PROJ_TM, PROJ_STRIP):
        rows = slice(r, r + PROJ_STRIP)
        a = (_rms(x_ref[rows, :]) * g_ref[...]).astype(jnp.bfloat16)

        def proj(lo, width):
            return _dot_nt(a, w_ref[lo:lo + width])

        v = proj(C_FF, LANES) + gate_bias
        logf = (jnp.minimum(v, 0.0) - jnp.log1p(jnp.exp(-jnp.abs(v)))) * LOG2E
        x = _pick_piece(_split_bf16(logf), term, TERM_SRC)
        c = carry_ref[...] + _dot(tri_ref[...], x)
        carry_ref[...] = c[PROJ_STRIP - 1:PROJ_STRIP, :]
        terms = _pick_piece(_split_bf16(c), term, TERM_CUT)
        one = jnp.ones_like(terms)
        qa_ref[rows, :] = jnp.where(lower, terms, one)
        ka_ref[rows, :] = jnp.where(lower, one, -terms)

        kv = proj(C_SK, 2 * D_SWA_KV)
        for half, out_ref in enumerate((sk_ref, sv_ref)):
            z = kv[:, half * LANES:(half + 1) * LANES]
            zr = pltpu.roll(z, HEAD_DIM, 1)
            out_ref[rows, :LANES] = jnp.where(lower, z, zr).astype(jnp.bfloat16)
            out_ref[rows, LANES:] = jnp.where(lower, zr, z).astype(jnp.bfloat16)

        fq_ref[rows, :] = proj(C_FQ, D_FOX).astype(jnp.bfloat16)
        fk_ref[rows, :] = proj(C_FK, D_FOX).astype(jnp.bfloat16)
        fv_ref[rows, :] = proj(C_FV, D_FOX).astype(jnp.bfloat16)
        sq_ref[rows, :] = proj(C_SQ, D_SWA).astype(jnp.bfloat16)


def _split_bf16(v):
    pieces = []
    r = v
    for _ in range(N_SPLIT):
        t = r.astype(jnp.bfloat16)
        pieces.append(t)
        r = r - t.astype(jnp.float32)
    return pieces


def _pick_piece(pieces, term, piece_of_term):
    out = pieces[piece_of_term[-1]]
    for t in range(N_TERMS - 2, -1, -1):
        if piece_of_term[t] != piece_of_term[t + 1]:
            out = jnp.where(term <= t, pieces[piece_of_term[t]], out)
    return out


def _proj_call(x2, g_pre, w_in_t, b_forget, batch, seq):
    n = batch * seq
    steps = seq // PROJ_TM
    tri = np.tril(np.ones((PROJ_STRIP, PROJ_STRIP), np.float32))
    row = lambda b, s: (b * steps + s, 0)
    const = lambda b, s: (0, 0)
    bf16 = jnp.bfloat16

    def out(width):
        return (jax.ShapeDtypeStruct((n, width), bf16), pl.BlockSpec((PROJ_TM, width), row))

    outs = [out(D_FOX), out(D_FOX), out(D_FOX), out(LANES), out(LANES),
            out(D_SWA), out(2 * LANES), out(2 * LANES)]
    return pl.pallas_call(
        _proj_kernel,
        grid=(batch, steps),
        in_specs=[
            pl.BlockSpec((PROJ_TM, D_MODEL), row),
            pl.BlockSpec((1, D_MODEL), const),
            pl.BlockSpec(w_in_t.shape, const, pipeline_mode=pl.Buffered(1)),
            pl.BlockSpec(memory_space=pltpu.SMEM),
            pl.BlockSpec((PROJ_STRIP, PROJ_STRIP), const),
        ],
        out_specs=[o[1] for o in outs],
        out_shape=[o[0] for o in outs],
        scratch_shapes=[pltpu.VMEM((1, LANES), jnp.float32),
                        pltpu.VMEM((D_PROJ, D_MODEL), bf16)],
        compiler_params=pltpu.CompilerParams(
            dimension_semantics=("arbitrary", "arbitrary"),
            vmem_limit_bytes=VMEM_LIMIT),
        name="proj",
    )(x2, g_pre, w_in_t, b_forget, jnp.asarray(tri, bf16))


def _fox_kernel(*refs, seq, n_cast):
    q_ref, qa_ref, k_ref, ka_ref, v_ref = refs[:5]
    w_refs = refs[5:5 + n_cast]
    o_ref = refs[5 + n_cast]
    wb_refs = refs[6 + n_cast:6 + 2 * n_cast]
    qm_ref, s_ref, m_ref, acc_ref = refs[6 + 2 * n_cast:]
    for w_ref, wb_ref in zip(w_refs, wb_refs):
        wb_ref[...] = w_ref[...].astype(wb_ref.dtype)

    t = FOX_T
    lane = lax.broadcasted_iota(jnp.int32, (t, LANES), 1)
    lower = lane < HEAD_DIM
    head_of_lane = lane % N_FOX_HEADS
    d = FOX_DIAG
    causal = (lax.broadcasted_iota(jnp.int32, (d, d), 0)
              >= lax.broadcasted_iota(jnp.int32, (d, d), 1))

    def tile_rows(i):
        return slice(i * t, (i + 1) * t)

    def pair_lanes(pp):
        return slice(pp * LANES, (pp + 1) * LANES)

    def parts(i, k0, nk):
        if k0 + nk < (i + 1) * t:
            return [(0, t, nk)]
        return [(r, r + d, nk - t + r + d) for r in range(0, t, d)]

    def mask_queries(pp, i):
        q = q_ref[0, tile_rows(i), pair_lanes(pp)]
        qa = qa_ref[0, tile_rows(i), :]
        zero = jnp.zeros_like(q)
        first_head = 2 * (FOX_PAIRS * pl.program_id(1) + pp)
        for e in range(2):
            qm_ref[pp, i, e] = jnp.concatenate(
                [jnp.where(lower, q, zero) if e == 0 else jnp.where(lower, zero, q),
                 jnp.where(head_of_lane == first_head + e, qa, zero)], axis=-1)

    def scores(pp, i, k0, nk, slot):
        for r0, r1, n in parts(i, k0, nk):
            keys = slice(k0, k0 + n)
            kk = jnp.concatenate([k_ref[0, keys, pair_lanes(pp)], ka_ref[0, keys, :]], axis=-1)
            for e in range(2):
                s_ref[slot, e, r0:r1, :n] = _dot_nt(qm_ref[pp, i, e, r0:r1], kk)

    def update(pp, i, k0, nk, slot):
        diag = k0 + nk == (i + 1) * t
        for r0, r1, n in parts(i, k0, nk):
            v = v_ref[0, k0:k0 + n, pair_lanes(pp)]
            one = jnp.ones_like(v)
            for e in range(2):
                s = s_ref[slot, e, r0:r1, :n]
                if diag:
                    last = jnp.where(causal, s[:, n - d:], NEG_BIG)
                    s = last if n == d else jnp.concatenate([s[:, :n - d], last], axis=-1)
                row_max = jnp.max(s, axis=-1, keepdims=True)
                m_new = (jnp.broadcast_to(row_max, (r1 - r0, LANES)) if k0 == 0
                         else jnp.maximum(m_ref[pp, e, r0:r1], row_max))
                p = jnp.concatenate(
                    [jnp.exp2(s[:, c * LANES:(c + 1) * LANES] - m_new)
                     for c in range(n // LANES)], axis=-1).astype(jnp.bfloat16)
                v_lower = lax.broadcasted_iota(jnp.int32, v.shape, 1) < HEAD_DIM
                ve = jnp.where(v_lower, v, one) if e == 0 else jnp.where(v_lower, one, v)
                pv = _dot(p, ve)
                if k0 > 0:
                    pv = jnp.exp2(m_ref[pp, e, r0:r1] - m_new) * acc_ref[pp, e, r0:r1] + pv
                acc_ref[pp, e, r0:r1] = pv
                m_ref[pp, e, r0:r1] = m_new

    def finish(pp, i):
        outs = []
        for e in range(2):
            acc = acc_ref[pp, e]
            outs.append(acc / pltpu.roll(acc, HEAD_DIM, 1))
        o_ref[0, tile_rows(i), pair_lanes(pp)] = (
            jnp.where(lower, outs[0], outs[1]).astype(o_ref.dtype))

    work = [(pp, i, k0, min(FOX_KEYS, (i + 1) * t - k0))
            for pp in range(FOX_PAIRS) for i in range(seq // t)
            for k0 in range(0, (i + 1) * t, FOX_KEYS)]

    def issue_scores(idx):
        pp, i, k0, nk = work[idx]
        if k0 == 0:
            mask_queries(pp, i)
        scores(pp, i, k0, nk, idx % 2)

    issue_scores(0)
    for idx, (pp, i, k0, nk) in enumerate(work):
        if idx + 1 < len(work):
            issue_scores(idx + 1)
        update(pp, i, k0, nk, idx % 2)
        if k0 + nk == (i + 1) * t:
            finish(pp, i)


def _fox_call(fq, qa, fk, ka, fv, weights, batch, seq):
    t = FOX_T
    bf16 = jnp.bfloat16
    groups = N_PAIRS // FOX_PAIRS
    shape3 = (batch, seq, N_PAIRS * LANES)
    aug3 = (batch, seq, LANES)
    spec = pl.BlockSpec((1, seq, FOX_PAIRS * LANES), lambda b, g: (b, 0, g))
    aug_spec = pl.BlockSpec((1, seq, LANES), lambda b, g: (b, 0, 0))
    steps = batch * groups
    w_specs = [pl.BlockSpec((w.shape[0] // steps, w.shape[1]), lambda b, g: (b * groups + g, 0))
               for w in weights]
    return pl.pallas_call(
        functools.partial(_fox_kernel, seq=seq, n_cast=len(weights)),
        grid=(batch, groups),
        in_specs=[spec, aug_spec, spec, aug_spec, spec] + w_specs,
        out_specs=[spec] + w_specs,
        out_shape=[jax.ShapeDtypeStruct(shape3, bf16)]
        + [jax.ShapeDtypeStruct(w.shape, bf16) for w in weights],
        scratch_shapes=[pltpu.VMEM((FOX_PAIRS, seq // t, 2, t, 2 * LANES), jnp.bfloat16),
                        pltpu.VMEM((2, 2, t, FOX_KEYS), jnp.float32),
                        pltpu.VMEM((FOX_PAIRS, 2, t, LANES), jnp.float32),
                        pltpu.VMEM((FOX_PAIRS, 2, t, LANES), jnp.float32)],
        compiler_params=pltpu.CompilerParams(
            dimension_semantics=("parallel", "parallel"),
            vmem_limit_bytes=VMEM_LIMIT),
        name="fox",
    )(fq.reshape(shape3), qa.reshape(aug3), fk.reshape(shape3), ka.reshape(aug3),
      fv.reshape(shape3), *weights)


def _t5_bucket(n):
    max_exact = N_BUCKETS // 2
    large = max_exact + (np.log(np.maximum(n, 1) / max_exact)
                         / np.log(MAX_DISTANCE / max_exact)
                         * (N_BUCKETS - max_exact)).astype(np.int32)
    large = np.minimum(large, N_BUCKETS - 1)
    return np.where(n < max_exact, n, large).astype(np.int32)


def _band_buckets():
    i = np.arange(Q_BLOCK)[:, None]
    j = np.arange(2 * Q_BLOCK)[None, :]
    dist = i + Q_BLOCK - j
    in_window = (dist >= 0) & (dist < WINDOW)
    return np.where(in_window, _t5_bucket(np.clip(dist, 0, None)), -1).astype(np.int32)


def _swa_kernel(bucket_ref, rel_ref, sink_ref, q_ref, k_ref, v_ref, o_ref, bias_ref, sinkc_ref,
                s_ref, *, seq):
    qb = Q_BLOCK
    qw = SWA_GROUP * HEAD_DIM

    @pl.when(pl.program_id(0) == 0)
    def _():
        bucket = bucket_ref[...]
        for h in range(N_SWA_HEADS):
            bias = jnp.full(bucket.shape, NEG_BIG, jnp.float32)
            for b in range(N_BUCKETS):
                bias = jnp.where(bucket == b, rel_ref[h, b] * LOG2E, bias)
            bias = jnp.where(lax.broadcasted_iota(jnp.int32, bucket.shape, 1) == 0,
                             sink_ref[h] * LOG2E, bias)
            group, g = divmod(h, SWA_GROUP)
            bias_ref[group, g * qb:(g + 1) * qb, :] = bias
            sinkc_ref[group, g * qb:(g + 1) * qb, :] = jnp.full((qb, LANES), sink_ref[h] * LOG2E)

    lower = lax.broadcasted_iota(jnp.int32, (qb, LANES), 1) < HEAD_DIM

    def band_of(n):
        if n == 0:
            return slice(0, qb), qb, slice(qb, 2 * qb)
        return slice((n - 1) * qb, (n + 1) * qb), 2 * qb, slice(0, 2 * qb)

    def scores(kv, n, slot):
        rows = slice(n * qb, (n + 1) * qb)
        band, width, cols = band_of(n)
        parts = []
        for pair in range(SWA_GROUP // 2):
            qp = q_ref[0, rows, kv * qw + pair * LANES:kv * qw + (pair + 1) * LANES]
            zero = jnp.zeros_like(qp)
            parts += [jnp.where(lower, qp, zero), jnp.where(lower, zero, qp)]
        s_ref[slot, :, :width] = (_dot_nt(jnp.concatenate(parts, axis=0), sink_slot(kv, n, k_ref))
                                  + bias_ref[kv, :, cols])

    def sink_slot(kv, n, ref):
        band, _, _ = band_of(n)
        x = ref[0, band, kv * LANES:(kv + 1) * LANES]
        if n == 0:
            return x
        return jnp.where(lax.broadcasted_iota(jnp.int32, x.shape, 0) == 0, jnp.zeros_like(x), x)

    def update(kv, n, slot):
        rows = slice(n * qb, (n + 1) * qb)
        _, width, _ = band_of(n)
        vb = sink_slot(kv, n, v_ref)
        vb = jnp.concatenate([vb, jnp.ones_like(vb)], axis=-1)
        s = s_ref[slot, :, :width]
        m = jnp.max(s, axis=-1, keepdims=True)
        if n == 0:
            sink = sinkc_ref[kv]
            m = jnp.maximum(m, sink)
        p = jnp.concatenate(
            [jnp.exp2(s[:, c * LANES:(c + 1) * LANES] - m) for c in range(width // LANES)],
            axis=-1).astype(jnp.bfloat16)
        acc = _dot(p, vb)
        denom = acc[:, LANES:] + jnp.exp2(sink - m) if n == 0 else acc[:, LANES:]
        out = acc[:, :LANES] / denom
        for pair in range(SWA_GROUP // 2):
            even = slice(2 * pair * qb, (2 * pair + 1) * qb)
            odd = slice((2 * pair + 1) * qb, (2 * pair + 2) * qb)
            o_ref[0, rows, kv * qw + pair * LANES:kv * qw + (pair + 1) * LANES] = jnp.where(
                lower, out[even], out[odd]).astype(o_ref.dtype)

    work = [(kv, n) for kv in range(N_SWA_KV_HEADS) for n in range(seq // qb)]
    scores(*work[0], 0)
    for idx, item in enumerate(work):
        if idx + 1 < len(work):
            scores(*work[idx + 1], (idx + 1) % 2)
        update(*item, idx % 2)


def _swa_call(sq, sk, sv, rel_bias, sinks, batch, seq):
    bucket = _band_buckets()
    smem = pl.BlockSpec(memory_space=pltpu.SMEM)
    q_spec = pl.BlockSpec((1, seq, D_SWA), lambda b: (b, 0, 0))
    kv_spec = pl.BlockSpec((1, seq, N_SWA_KV_HEADS * LANES), lambda b: (b, 0, 0))
    return pl.pallas_call(
        functools.partial(_swa_kernel, seq=seq),
        grid=(batch,),
        in_specs=[pl.BlockSpec(bucket.shape, lambda b: (0, 0)), smem, smem,
                  q_spec, kv_spec, kv_spec],
        out_specs=q_spec,
        out_shape=jax.ShapeDtypeStruct((batch, seq, D_SWA), jnp.bfloat16),
        scratch_shapes=[
            pltpu.VMEM((N_SWA_KV_HEADS, SWA_GROUP * Q_BLOCK, 2 * Q_BLOCK), jnp.float32),
            pltpu.VMEM((N_SWA_KV_HEADS, SWA_GROUP * Q_BLOCK, LANES), jnp.float32),
            pltpu.VMEM((2, SWA_GROUP * Q_BLOCK, 2 * Q_BLOCK), jnp.float32)],
        compiler_params=pltpu.CompilerParams(
            dimension_semantics=("arbitrary",),
            vmem_limit_bytes=VMEM_LIMIT),
        name="swa",
    )(jnp.asarray(bucket), rel_bias.T, sinks, sq.reshape(batch, seq, D_SWA),
      sk.reshape(batch, seq, 2 * LANES), sv.reshape(batch, seq, 2 * LANES))


def _post_kernel(fox_ref, swa_ref, x_ref, p_ref, wo_ref, w1_ref, w2_ref, wg_ref, wp_ref,
                 g_attn_ref, g_pre_ref, g_ff_ref, g_ple_ref, o_ref):
    bf16 = jnp.bfloat16
    h_parts, m_parts = [], []
    for r in range(0, POST_TM, POST_STRIP):
        rows = slice(r, r + POST_STRIP)
        mix = _dot(jnp.concatenate([fox_ref[rows, :], swa_ref[rows, :]], axis=-1), wo_ref[...])
        hh = x_ref[rows, :] + _rms(mix) * g_attn_ref[...]
        h_parts.append(hh)
        m_parts.append((_rms(hh) * g_pre_ref[...]).astype(bf16))
    h = jnp.concatenate(h_parts, axis=0)
    m = jnp.concatenate(m_parts, axis=0)
    ple = _dot(p_ref[...].astype(bf16), wp_ref[...])

    y = jnp.zeros((POST_TM, D_MODEL), jnp.float32)
    for c in range(D_FF // FF_CHUNK):
        cols = slice(c * FF_CHUNK, (c + 1) * FF_CHUNK)
        hid = jnp.square(jnp.maximum(_dot(m, w1_ref[:, cols]), 0.0))
        y = y + _dot(hid.astype(bf16), w2_ref[cols, :])

    for r in range(0, POST_TM, POST_STRIP):
        rows = slice(r, r + POST_STRIP)
        hh = h[rows] + _rms(y[rows]) * g_ff_ref[...]
        gate = jax.nn.sigmoid(_dot(hh.astype(bf16), wg_ref[...]))
        o_ref[rows, :] = hh + _rms(ple[rows] * gate) * g_ple_ref[...]


def _post_call(fox, swa, x2, p2, wo, w1, w2, wg, wp, gains):
    n = x2.shape[0]
    row = lambda i: (i, 0)
    const = lambda i: (0, 0)

    def resident(shape):
        return pl.BlockSpec(shape, const, pipeline_mode=pl.Buffered(1))

    return pl.pallas_call(
        _post_kernel,
        grid=(n // POST_TM,),
        in_specs=[
            pl.BlockSpec((POST_TM, D_FOX), row),
            pl.BlockSpec((POST_TM, D_SWA), row),
            pl.BlockSpec((POST_TM, D_MODEL), row),
            pl.BlockSpec((POST_TM, D_PLE), row),
            resident(wo.shape), resident(w1.shape), resident(w2.shape),
            resident(wg.shape), resident(wp.shape), *[resident(g.shape) for g in gains],
        ],
        out_specs=pl.BlockSpec((POST_TM, D_MODEL), row),
        out_shape=jax.ShapeDtypeStruct((n, D_MODEL), jnp.float32),
        compiler_params=pltpu.CompilerParams(
            dimension_semantics=("parallel",),
            vmem_limit_bytes=VMEM_LIMIT),
        name="post",
    )(fox, swa, x2, p2, wo, w1, w2, wg, wp, *gains)


def kernel(x, p, w_in, b_forget, w_out, rel_bias, swa_sinks, g_attn_pre, g_attn_post,
           w_ff1, w_ff2, g_ff_pre, g_ff_post, w_ple, w_ple_gate, g_ple_post):
    batch, seq, _ = x.shape
    n = batch * seq
    h = x.reshape(n, D_MODEL)
    for i in range(p.shape[0]):
        fq, fk, fv, qa, ka, sq, sk, sv = _proj_call(
            h, g_attn_pre[i].reshape(1, D_MODEL), w_in[i].T, b_forget[i], batch, seq)
        fox, wo, w1, w2, wg = _fox_call(
            fq, qa, fk, ka, fv, [w_out[i], w_ff1[i], w_ff2[i], w_ple_gate[i]], batch, seq)
        wp = w_ple[i].astype(jnp.bfloat16)
        swa = _swa_call(sq, sk, sv, rel_bias, swa_sinks[i], batch, seq).reshape(n, D_SWA)
        gains = [g[i].reshape(1, D_MODEL) for g in (g_attn_post, g_ff_pre, g_ff_post, g_ple_post)]
        h = _post_call(fox.reshape(n, D_FOX), swa, h, p[i].reshape(n, D_PLE),
                       wo, w1, w2, wg, wp, gains)
    return h.reshape(batch, seq, D_MODEL)
```

```python
import functools

import jax
import jax.numpy as jnp
import numpy as np
from jax import lax
from jax.experimental import pallas as pl
from jax.experimental.pallas import tpu as pltpu

D_MODEL = 1024
HEAD_DIM = 64
N_FOX_HEADS = 8
N_SWA_HEADS = 8
N_SWA_KV_HEADS = 2
SWA_GROUP = N_SWA_HEADS // N_SWA_KV_HEADS
D_FOX = N_FOX_HEADS * HEAD_DIM
D_SWA = N_SWA_HEADS * HEAD_DIM
D_SWA_KV = N_SWA_KV_HEADS * HEAD_DIM
D_FF = 4 * D_MODEL
D_PLE = 256
WINDOW = 128
Q_BLOCK = 128
N_BUCKETS = 32
MAX_DISTANCE = 128
RMS_EPS = 1e-6

LANES = 128
N_PAIRS = N_FOX_HEADS // 2
N_SPLIT = 3
N_TERMS = HEAD_DIM // N_FOX_HEADS
TERM_SRC = (0, 0, 0, 1, 1, 1, 2, 2)
TERM_CUT = (0, 1, 2, 0, 1, 2, 0, 1)
assert len(TERM_SRC) == len(TERM_CUT) == N_TERMS and 2 * N_TERMS * N_FOX_HEADS == LANES
NEG_BIG = -1e30
LOG2E = float(np.log2(np.e))

PROJ_TM = 1024
PROJ_STRIP = 512
FOX_T = 512
FOX_DIAG = 256
FOX_KEYS = 1024
POST_TM = 512
POST_STRIP = 256
FF_CHUNK = 1024
VMEM_LIMIT = 56 * 1024 * 1024

C_FQ, C_FK, C_FV = 0, D_FOX, 2 * D_FOX
C_SQ = 3 * D_FOX
C_SK = C_SQ + D_SWA
C_SV = C_SK + D_SWA_KV
C_FF = C_SV + D_SWA_KV
D_PROJ = C_FF + LANES


def _rms(v):
    return v * lax.rsqrt(jnp.mean(v * v, axis=-1, keepdims=True) + RMS_EPS)


def _dot(a, b):
    return jnp.dot(a, b, preferred_element_type=jnp.float32)


def _dot_nt(a, b):
    return lax.dot_general(a, b, (((1,), (1,)), ((), ())),
                           preferred_element_type=jnp.float32)


def _proj_kernel(x_ref, g_ref, wt_ref, bf_ref, tri_ref,
                 fq_ref, fk_ref, fv_ref, qa_ref, ka_ref, sq_ref, sk_ref, sv_ref,
                 carry_ref, w_ref):
    @pl.when((pl.program_id(0) == 0) & (pl.program_id(1) == 0))
    def _():
        bf16 = jnp.bfloat16
        q_scale = HEAD_DIM ** -0.5 * LOG2E
        src = np.cumsum([0, D_FOX, D_FOX, D_FOX, N_FOX_HEADS, D_SWA])
        src_fq, src_fk, src_fv, src_ff, src_sq, src_skv = (int(c) for c in src)
        w_ref[C_FQ:C_FQ + D_FOX] = (wt_ref[src_fq:src_fq + D_FOX] * q_scale).astype(bf16)
        w_ref[C_FK:C_FK + D_FOX] = wt_ref[src_fk:src_fk + D_FOX].astype(bf16)
        w_ref[C_FV:C_FV + D_FOX] = wt_ref[src_fv:src_fv + D_FOX].astype(bf16)
        w_ref[C_SQ:C_SQ + D_SWA] = (wt_ref[src_sq:src_sq + D_SWA] * q_scale).astype(bf16)
        w_ref[C_SK:C_SK + 2 * D_SWA_KV] = wt_ref[src_skv:src_skv + 2 * D_SWA_KV].astype(bf16)
        gate = wt_ref[src_ff:src_ff + N_FOX_HEADS]
        w_ref[C_FF:C_FF + LANES] = jnp.tile(gate, (LANES // N_FOX_HEADS, 1)).astype(bf16)

    @pl.when(pl.program_id(1) == 0)
    def _():
        carry_ref[...] = jnp.zeros_like(carry_ref)

    lane = lax.broadcasted_iota(jnp.int32, (PROJ_STRIP, LANES), 1)
    lower = lane < HEAD_DIM
    term = (lane // N_FOX_HEADS) % N_TERMS
    head_lane = lax.broadcasted_iota(jnp.int32, (1, LANES), 1) % N_FOX_HEADS
    gate_bias = jnp.zeros((1, LANES), jnp.float32)
    for head in range(N_FOX_HEADS):
        gate_bias = jnp.where(head_lane == head, bf_ref[head], gate_bias)

    for r in range(0, PROJ_TM, PROJ_STRIP):
        rows = slice(r, r + PROJ_STRIP)
        a = (_rms(x_ref[rows, :]) * g_ref[...]).astype(jnp.bfloat16)

        def proj(lo, width):
            return _dot_nt(a, w_ref[lo:lo + width])

        v = proj(C_FF, LANES) + gate_bias
        logf = (jnp.minimum(v, 0.0) - jnp.log1p(jnp.exp(-jnp.abs(v)))) * LOG2E
        x = _pick_piece(_split_bf16(logf), term, TERM_SRC)
        c = carry_ref[...] + _dot(tri_ref[...], x)
        carry_ref[...] = c[PROJ_STRIP - 1:PROJ_STRIP, :]
        terms = _pick_piece(_split_bf16(c), term, TERM_CUT)
        one = jnp.ones_like(terms)
        qa_ref[rows, :] = jnp.where(lower, terms, one)
        ka_ref[rows, :] = jnp.where(lower, one, -terms)

        kv = proj(C_SK, 2 * D_SWA_KV)
        for half, out_ref in enumerate((sk_ref, sv_ref)):
            z = kv[:, half * LANES:(half + 1) * LANES]
            zr = pltpu.roll(z, HEAD_DIM, 1)
            out_ref[rows, :LANES] = jnp.where(lower, z, zr).astype(jnp.bfloat16)
            out_ref[rows, LANES:] = jnp.where(lower, zr, z).astype(jnp.bfloat16)

        fq_ref[rows, :] = proj(C_FQ, D_FOX).astype(jnp.bfloat16)
        fk_ref[rows, :] = proj(C_FK, D_FOX).astype(jnp.bfloat16)
        fv_ref[rows, :] = proj(C_FV, D_FOX).astype(jnp.bfloat16)
        sq_ref[rows, :] = proj(C_SQ, D_SWA).astype(jnp.bfloat16)


def _split_bf16(v):
    pieces = []
    r = v
    for _ in range(N_SPLIT):
        t = r.astype(jnp.bfloat16)
        pieces.append(t)
        r = r - t.astype(jnp.float32)
    return pieces


def _pick_piece(pieces, term, piece_of_term):
    out = pieces[piece_of_term[-1]]
    for t in range(N_TERMS - 2, -1, -1):
        if piece_of_term[t] != piece_of_term[t + 1]:
            out = jnp.where(term <= t, pieces[piece_of_term[t]], out)
    return out


def _proj_call(x2, g_pre, w_in_t, b_forget, batch, seq):
    n = batch * seq
    steps = seq // PROJ_TM
    tri = np.tril(np.ones((PROJ_STRIP, PROJ_STRIP), np.float32))
    row = lambda b, s: (b * steps + s, 0)
    const = lambda b, s: (0, 0)
    bf16 = jnp.bfloat16

    def out(width):
        return (jax.ShapeDtypeStruct((n, width), bf16), pl.BlockSpec((PROJ_TM, width), row))

    outs = [out(D_FOX), out(D_FOX), out(D_FOX), out(LANES), out(LANES),
            out(D_SWA), out(2 * LANES), out(2 * LANES)]
    return pl.pallas_call(
        _proj_kernel,
        grid=(batch, steps),
        in_specs=[
            pl.BlockSpec((PROJ_TM, D_MODEL), row),
            pl.BlockSpec((1, D_MODEL), const),
            pl.BlockSpec(w_in_t.shape, const, pipeline_mode=pl.Buffered(1)),
            pl.BlockSpec(memory_space=pltpu.SMEM),
            pl.BlockSpec((PROJ_STRIP, PROJ_STRIP), const),
        ],
        out_specs=[o[1] for o in outs],
        out_shape=[o[0] for o in outs],
        scratch_shapes=[pltpu.VMEM((1, LANES), jnp.float32),
                        pltpu.VMEM((D_PROJ, D_MODEL), bf16)],
        compiler_params=pltpu.CompilerParams(
            dimension_semantics=("arbitrary", "arbitrary"),
            vmem_limit_bytes=VMEM_LIMIT),
        name="proj",
    )(x2, g_pre, w_in_t, b_forget, jnp.asarray(tri, bf16))


def _fox_kernel(*refs, seq, n_cast):
    q_ref, qa_ref, k_ref, ka_ref, v_ref = refs[:5]
    w_refs = refs[5:5 + n_cast]
    o_ref = refs[5 + n_cast]
    wb_refs = refs[6 + n_cast:6 + 2 * n_cast]
    qm_ref, s_ref, m_ref, acc_ref = refs[6 + 2 * n_cast:]
    for w_ref, wb_ref in zip(w_refs, wb_refs):
        wb_ref[...] = w_ref[...].astype(wb_ref.dtype)

    t = FOX_T
    lane = lax.broadcasted_iota(jnp.int32, (t, LANES), 1)
    lower = lane < HEAD_DIM
    head_of_lane = lane % N_FOX_HEADS
    d = FOX_DIAG
    causal = (lax.broadcasted_iota(jnp.int32, (d, d), 0)
              >= lax.broadcasted_iota(jnp.int32, (d, d), 1))

    def tile_rows(i):
        return slice(i * t, (i + 1) * t)

    def parts(i, k0, nk):
        if k0 + nk < (i + 1) * t:
            return [(0, t, nk)]
        return [(r, r + d, nk - t + r + d) for r in range(0, t, d)]

    def mask_queries(i):
        q = q_ref[0, tile_rows(i), :]
        qa = qa_ref[0, tile_rows(i), :]
        zero = jnp.zeros_like(q)
        first_head = 2 * pl.program_id(1)
        for e in range(2):
            qm_ref[i, e] = jnp.concatenate(
                [jnp.where(lower, q, zero) if e == 0 else jnp.where(lower, zero, q),
                 jnp.where(head_of_lane == first_head + e, qa, zero)], axis=-1)

    def scores(i, k0, nk, slot):
        for r0, r1, n in parts(i, k0, nk):
            keys = slice(k0, k0 + n)
            kk = jnp.concatenate([k_ref[0, keys, :], ka_ref[0, keys, :]], axis=-1)
            for e in range(2):
                s_ref[slot, e, r0:r1, :n] = _dot_nt(qm_ref[i, e, r0:r1], kk)

    def update(i, k0, nk, slot):
        diag = k0 + nk == (i + 1) * t
        for r0, r1, n in parts(i, k0, nk):
            v = v_ref[0, k0:k0 + n, :]
            one = jnp.ones_like(v)
            for e in range(2):
                s = s_ref[slot, e, r0:r1, :n]
                if diag:
                    last = jnp.where(causal, s[:, n - d:], NEG_BIG)
                    s = last if n == d else jnp.concatenate([s[:, :n - d], last], axis=-1)
                row_max = jnp.max(s, axis=-1, keepdims=True)
                m_new = (jnp.broadcast_to(row_max, (r1 - r0, LANES)) if k0 == 0
                         else jnp.maximum(m_ref[e, r0:r1], row_max))
                p = jnp.concatenate(
                    [jnp.exp2(s[:, c * LANES:(c + 1) * LANES] - m_new)
                     for c in range(n // LANES)], axis=-1).astype(jnp.bfloat16)
                v_lower = lax.broadcasted_iota(jnp.int32, v.shape, 1) < HEAD_DIM
                ve = jnp.where(v_lower, v, one) if e == 0 else jnp.where(v_lower, one, v)
                pv = _dot(p, ve)
                if k0 > 0:
                    pv = jnp.exp2(m_ref[e, r0:r1] - m_new) * acc_ref[e, r0:r1] + pv
                acc_ref[e, r0:r1] = pv
                m_ref[e, r0:r1] = m_new

    def finish(i):
        outs = []
        for e in range(2):
            acc = acc_ref[e]
            outs.append(acc / pltpu.roll(acc, HEAD_DIM, 1))
        o_ref[0, tile_rows(i), :] = jnp.where(lower, outs[0], outs[1]).astype(o_ref.dtype)

    work = [(i, k0, min(FOX_KEYS, (i + 1) * t - k0))
            for i in range(seq // t) for k0 in range(0, (i + 1) * t, FOX_KEYS)]

    def issue_scores(idx):
        i, k0, nk = work[idx]
        if k0 == 0:
            mask_queries(i)
        scores(i, k0, nk, idx % 2)

    issue_scores(0)
    for idx, (i, k0, nk) in enumerate(work):
        if idx + 1 < len(work):
            issue_scores(idx + 1)
        update(i, k0, nk, idx % 2)
        if k0 + nk == (i + 1) * t:
            finish(i)


def _fox_call(fq, qa, fk, ka, fv, weights, batch, seq):
    t = FOX_T
    bf16 = jnp.bfloat16
    shape3 = (batch, seq, N_PAIRS * LANES)
    aug3 = (batch, seq, LANES)
    spec = pl.BlockSpec((1, seq, LANES), lambda b, g: (b, 0, g))
    aug_spec = pl.BlockSpec((1, seq, LANES), lambda b, g: (b, 0, 0))
    steps = batch * N_PAIRS
    w_specs = [pl.BlockSpec((w.shape[0] // steps, w.shape[1]), lambda b, g: (b * N_PAIRS + g, 0))
               for w in weights]
    return pl.pallas_call(
        functools.partial(_fox_kernel, seq=seq, n_cast=len(weights)),
        grid=(batch, N_PAIRS),
        in_specs=[spec, aug_spec, spec, aug_spec, spec] + w_specs,
        out_specs=[spec] + w_specs,
        out_shape=[jax.ShapeDtypeStruct(shape3, bf16)]
        + [jax.ShapeDtypeStruct(w.shape, bf16) for w in weights],
        scratch_shapes=[pltpu.VMEM((seq // t, 2, t, 2 * LANES), jnp.bfloat16),
                        pltpu.VMEM((2, 2, t, FOX_KEYS), jnp.float32),
                        pltpu.VMEM((2, t, LANES), jnp.float32),
                        pltpu.VMEM((2, t, LANES), jnp.float32)],
        compiler_params=pltpu.CompilerParams(
            dimension_semantics=("parallel", "parallel"),
            vmem_limit_bytes=VMEM_LIMIT),
        name="fox",
    )(fq.reshape(shape3), qa.reshape(aug3), fk.reshape(shape3), ka.reshape(aug3),
      fv.reshape(shape3), *weights)


def _t5_bucket(n):
    max_exact = N_BUCKETS // 2
    large = max_exact + (np.log(np.maximum(n, 1) / max_exact)
                         / np.log(MAX_DISTANCE / max_exact)
                         * (N_BUCKETS - max_exact)).astype(np.int32)
    large = np.minimum(large, N_BUCKETS - 1)
    return np.where(n < max_exact, n, large).astype(np.int32)


def _band_buckets():
    i = np.arange(Q_BLOCK)[:, None]
    j = np.arange(2 * Q_BLOCK)[None, :]
    dist = i + Q_BLOCK - j
    in_window = (dist >= 0) & (dist < WINDOW)
    return np.where(in_window, _t5_bucket(np.clip(dist, 0, None)), -1).astype(np.int32)


def _swa_kernel(bucket_ref, rel_ref, sink_ref, q_ref, k_ref, v_ref, o_ref, bias_ref, sinkc_ref,
                s_ref, *, seq):
    qb = Q_BLOCK
    qw = SWA_GROUP * HEAD_DIM

    @pl.when(pl.program_id(0) == 0)
    def _():
        bucket = bucket_ref[...]
        for h in range(N_SWA_HEADS):
            bias = jnp.full(bucket.shape, NEG_BIG, jnp.float32)
            for b in range(N_BUCKETS):
                bias = jnp.where(bucket == b, rel_ref[h, b] * LOG2E, bias)
            bias = jnp.where(lax.broadcasted_iota(jnp.int32, bucket.shape, 1) == 0,
                             sink_ref[h] * LOG2E, bias)
            group, g = divmod(h, SWA_GROUP)
            bias_ref[group, g * qb:(g + 1) * qb, :] = bias
            sinkc_ref[group, g * qb:(g + 1) * qb, :] = jnp.full((qb, LANES), sink_ref[h] * LOG2E)

    lower = lax.broadcasted_iota(jnp.int32, (qb, LANES), 1) < HEAD_DIM

    def band_of(n):
        if n == 0:
            return slice(0, qb), qb, slice(qb, 2 * qb)
        return slice((n - 1) * qb, (n + 1) * qb), 2 * qb, slice(0, 2 * qb)

    def scores(kv, n, slot):
        rows = slice(n * qb, (n + 1) * qb)
        band, width, cols = band_of(n)
        parts = []
        for pair in range(SWA_GROUP // 2):
            qp = q_ref[0, rows, kv * qw + pair * LANES:kv * qw + (pair + 1) * LANES]
            zero = jnp.zeros_like(qp)
            parts += [jnp.where(lower, qp, zero), jnp.where(lower, zero, qp)]
        s_ref[slot, :, :width] = (_dot_nt(jnp.concatenate(parts, axis=0), sink_slot(kv, n, k_ref))
                                  + bias_ref[kv, :, cols])

    def sink_slot(kv, n, ref):
        band, _, _ = band_of(n)
        x = ref[0, band, kv * LANES:(kv + 1) * LANES]
        if n == 0:
            return x
        return jnp.where(lax.broadcasted_iota(jnp.int32, x.shape, 0) == 0, jnp.zeros_like(x), x)

    def update(kv, n, slot):
        rows = slice(n * qb, (n + 1) * qb)
        _, width, _ = band_of(n)
        vb = sink_slot(kv, n, v_ref)
        vb = jnp.concatenate([vb, jnp.ones_like(vb)], axis=-1)
        s = s_ref[slot, :, :width]
        m = jnp.max(s, axis=-1, keepdims=True)
        if n == 0:
            sink = sinkc_ref[kv]
            m = jnp.maximum(m, sink)
        p = jnp.concatenate(
            [jnp.exp2(s[:, c * LANES:(c + 1) * LANES] - m) for c in range(width // LANES)],
            axis=-1).astype(jnp.bfloat16)
        acc = _dot(p, vb)
        denom = acc[:, LANES:] + jnp.exp2(sink - m) if n == 0 else acc[:, LANES:]
        out = acc[:, :LANES] / denom
        for pair in range(SWA_GROUP // 2):
            even = slice(2 * pair * qb, (2 * pair + 1) * qb)
            odd = slice((2 * pair + 1) * qb, (2 * pair + 2) * qb)
            o_ref[0, rows, kv * qw + pair * LANES:kv * qw + (pair + 1) * LANES] = jnp.where(
                lower, out[even], out[odd]).astype(o_ref.dtype)

    work = [(kv, n) for kv in range(N_SWA_KV_HEADS) for n in range(seq // qb)]
    scores(*work[0], 0)
    for idx, item in enumerate(work):
        if idx + 1 < len(work):
            scores(*work[idx + 1], (idx + 1) % 2)
        update(*item, idx % 2)


def _swa_call(sq, sk, sv, rel_bias, sinks, batch, seq):
    bucket = _band_buckets()
    smem = pl.BlockSpec(memory_space=pltpu.SMEM)
    q_spec = pl.BlockSpec((1, seq, D_SWA), lambda b: (b, 0, 0))
    kv_spec = pl.BlockSpec((1, seq, N_SWA_KV_HEADS * LANES), lambda b: (b, 0, 0))
    return pl.pallas_call(
        functools.partial(_swa_kernel, seq=seq),
        grid=(batch,),
        in_specs=[pl.BlockSpec(bucket.shape, lambda b: (0, 0)), smem, smem,
                  q_spec, kv_spec, kv_spec],
        out_specs=q_spec,
        out_shape=jax.ShapeDtypeStruct((batch, seq, D_SWA), jnp.bfloat16),
        scratch_shapes=[
            pltpu.VMEM((N_SWA_KV_HEADS, SWA_GROUP * Q_BLOCK, 2 * Q_BLOCK), jnp.float32),
            pltpu.VMEM((N_SWA_KV_HEADS, SWA_GROUP * Q_BLOCK, LANES), jnp.float32),
            pltpu.VMEM((2, SWA_GROUP * Q_BLOCK, 2 * Q_BLOCK), jnp.float32)],
        compiler_params=pltpu.CompilerParams(
            dimension_semantics=("arbitrary",),
            vmem_limit_bytes=VMEM_LIMIT),
        name="swa",
    )(jnp.asarray(bucket), rel_bias.T, sinks, sq.reshape(batch, seq, D_SWA),
      sk.reshape(batch, seq, 2 * LANES), sv.reshape(batch, seq, 2 * LANES))


def _post_kernel(fox_ref, swa_ref, x_ref, p_ref, wo_ref, w1_ref, w2_ref, wg_ref, wp_ref,
                 g_attn_ref, g_pre_ref, g_ff_ref, g_ple_ref, o_ref):
    bf16 = jnp.bfloat16
    h_parts, m_parts = [], []
    for r in range(0, POST_TM, POST_STRIP):
        rows = slice(r, r + POST_STRIP)
        mix = _dot(jnp.concatenate([fox_ref[rows, :], swa_ref[rows, :]], axis=-1), wo_ref[...])
        hh = x_ref[rows, :] + _rms(mix) * g_attn_ref[...]
        h_parts.append(hh)
        m_parts.append((_rms(hh) * g_pre_ref[...]).astype(bf16))
    h = jnp.concatenate(h_parts, axis=0)
    m = jnp.concatenate(m_parts, axis=0)
    ple = _dot(p_ref[...].astype(bf16), wp_ref[...])

    y = jnp.zeros((POST_TM, D_MODEL), jnp.float32)
    for c in range(D_FF // FF_CHUNK):
        cols = slice(c * FF_CHUNK, (c + 1) * FF_CHUNK)
        hid = jnp.square(jnp.maximum(_dot(m, w1_ref[:, cols]), 0.0))
        y = y + _dot(hid.astype(bf16), w2_ref[cols, :])

    for r in range(0, POST_TM, POST_STRIP):
        rows = slice(r, r + POST_STRIP)
        hh = h[rows] + _rms(y[rows]) * g_ff_ref[...]
        gate = jax.nn.sigmoid(_dot(hh.astype(bf16), wg_ref[...]))
        o_ref[rows, :] = hh + _rms(ple[rows] * gate) * g_ple_ref[...]


def _post_call(fox, swa, x2, p2, wo, w1, w2, wg, wp, gains):
    n = x2.shape[0]
    row = lambda i: (i, 0)
    const = lambda i: (0, 0)

    def resident(shape):
        return pl.BlockSpec(shape, const, pipeline_mode=pl.Buffered(1))

    return pl.pallas_call(
        _post_kernel,
        grid=(n // POST_TM,),
        in_specs=[
            pl.BlockSpec((POST_TM, D_FOX), row),
            pl.BlockSpec((POST_TM, D_SWA), row),
            pl.BlockSpec((POST_TM, D_MODEL), row),
            pl.BlockSpec((POST_TM, D_PLE), row),
            resident(wo.shape), resident(w1.shape), resident(w2.shape),
            resident(wg.shape), resident(wp.shape), *[resident(g.shape) for g in gains],
        ],
        out_specs=pl.BlockSpec((POST_TM, D_MODEL), row),
        out_shape=jax.ShapeDtypeStruct((n, D_MODEL), jnp.float32),
        compiler_params=pltpu.CompilerParams(
            dimension_semantics=("parallel",),
            vmem_limit_bytes=VMEM_LIMIT),
        name="post",
    )(fox, swa, x2, p2, wo, w1, w2, wg, wp, *gains)


def kernel(x, p, w_in, b_forget, w_out, rel_bias, swa_sinks, g_attn_pre, g_attn_post,
           w_ff1, w_ff2, g_ff_pre, g_ff_post, w_ple, w_ple_gate, g_ple_post):
    batch, seq, d_model = x.shape
    n = batch * seq
    assert d_model == D_MODEL and seq % PROJ_TM == 0 and seq % FOX_KEYS == 0
    assert FOX_KEYS % FOX_T == 0 and seq % Q_BLOCK == 0 and n % POST_TM == 0
    h = x.reshape(n, D_MODEL)
    for i in range(p.shape[0]):
        fq, fk, fv, qa, ka, sq, sk, sv = _proj_call(
            h, g_attn_pre[i].reshape(1, D_MODEL), w_in[i].T, b_forget[i], batch, seq)
        fox, wo, w1, w2, wg = _fox_call(
            fq, qa, fk, ka, fv, [w_out[i], w_ff1[i], w_ff2[i], w_ple_gate[i]], batch, seq)
        wp = w_ple[i].astype(jnp.bfloat16)
        swa = _swa_call(sq, sk, sv, rel_bias, swa_sinks[i], batch, seq).reshape(n, D_SWA)
        gains = [g[i].reshape(1, D_MODEL) for g in (g_attn_post, g_ff_pre, g_ff_post, g_ple_post)]
        h = _post_call(fox.reshape(n, D_FOX), swa, h, p[i].reshape(n, D_PLE),
                       wo, w1, w2, wg, wp, gains)
    return h.reshape(batch, seq, D_MODEL)
```

```python
import functools

import jax
import jax.numpy as jnp
import numpy as np
from jax import lax
from jax.experimental import pallas as pl
from jax.experimental.pallas import tpu as pltpu

D_MODEL = 1024
HEAD_DIM = 64
N_FOX_HEADS = 8
N_SWA_HEADS = 8
N_SWA_KV_HEADS = 2
SWA_GROUP = N_SWA_HEADS // N_SWA_KV_HEADS
D_FOX = N_FOX_HEADS * HEAD_DIM
D_SWA = N_SWA_HEADS * HEAD_DIM
D_SWA_KV = N_SWA_KV_HEADS * HEAD_DIM
D_FF = 4 * D_MODEL
D_PLE = 256
WINDOW = 128
Q_BLOCK = 128
N_BUCKETS = 32
MAX_DISTANCE = 128
RMS_EPS = 1e-6

LANES = 128
N_PAIRS = N_FOX_HEADS // 2
N_SPLIT = 3
N_TERMS = HEAD_DIM // N_FOX_HEADS
TERM_SRC = (0, 0, 0, 1, 1, 1, 2, 2)
TERM_CUT = (0, 1, 2, 0, 1, 2, 0, 1)
assert len(TERM_SRC) == len(TERM_CUT) == N_TERMS and 2 * N_TERMS * N_FOX_HEADS == LANES
NEG_BIG = -1e30
LOG2E = float(np.log2(np.e))

PROJ_TM = 1024
PROJ_STRIP = 512
FOX_T = 512
FOX_DIAG = 256
FOX_KEYS = 1536
POST_TM = 512
POST_STRIP = 256
FF_CHUNK = 1024
VMEM_LIMIT = 56 * 1024 * 1024

C_FQ, C_FK, C_FV = 0, D_FOX, 2 * D_FOX
C_SQ = 3 * D_FOX
C_SK = C_SQ + D_SWA
C_SV = C_SK + D_SWA_KV
C_FF = C_SV + D_SWA_KV
D_PROJ = C_FF + LANES


def _rms(v):
    return v * lax.rsqrt(jnp.mean(v * v, axis=-1, keepdims=True) + RMS_EPS)


def _dot(a, b):
    return jnp.dot(a, b, preferred_element_type=jnp.float32)


def _dot_nt(a, b):
    return lax.dot_general(a, b, (((1,), (1,)), ((), ())),
                           preferred_element_type=jnp.float32)


def _proj_kernel(x_ref, g_ref, wt_ref, bf_ref, tri_ref,
                 fq_ref, fk_ref, fv_ref, qa_ref, ka_ref, sq_ref, sk_ref, sv_ref,
                 carry_ref, w_ref):
    @pl.when((pl.program_id(0) == 0) & (pl.program_id(1) == 0))
    def _():
        bf16 = jnp.bfloat16
        q_scale = HEAD_DIM ** -0.5 * LOG2E
        src = np.cumsum([0, D_FOX, D_FOX, D_FOX, N_FOX_HEADS, D_SWA])
        src_fq, src_fk, src_fv, src_ff, src_sq, src_skv = (int(c) for c in src)
        w_ref[C_FQ:C_FQ + D_FOX] = (wt_ref[src_fq:src_fq + D_FOX] * q_scale).astype(bf16)
        w_ref[C_FK:C_FK + D_FOX] = wt_ref[src_fk:src_fk + D_FOX].astype(bf16)
        w_ref[C_FV:C_FV + D_FOX] = wt_ref[src_fv:src_fv + D_FOX].astype(bf16)
        w_ref[C_SQ:C_SQ + D_SWA] = (wt_ref[src_sq:src_sq + D_SWA] * q_scale).astype(bf16)
        w_ref[C_SK:C_SK + 2 * D_SWA_KV] = wt_ref[src_skv:src_skv + 2 * D_SWA_KV].astype(bf16)
        gate = wt_ref[src_ff:src_ff + N_FOX_HEADS]
        w_ref[C_FF:C_FF + LANES] = jnp.tile(gate, (LANES // N_FOX_HEADS, 1)).astype(bf16)

    @pl.when(pl.program_id(1) == 0)
    def _():
        carry_ref[...] = jnp.zeros_like(carry_ref)

    lane = lax.broadcasted_iota(jnp.int32, (PROJ_STRIP, LANES), 1)
    lower = lane < HEAD_DIM
    term = (lane // N_FOX_HEADS) % N_TERMS
    head_lane = lax.broadcasted_iota(jnp.int32, (1, LANES), 1) % N_FOX_HEADS
    gate_bias = jnp.zeros((1, LANES), jnp.float32)
    for head in range(N_FOX_HEADS):
        gate_bias = jnp.where(head_lane == head, bf_ref[head], gate_bias)

    for r in range(0, PROJ_TM, PROJ_STRIP):
        rows = slice(r, r + PROJ_STRIP)
        a = (_rms(x_ref[rows, :]) * g_ref[...]).astype(jnp.bfloat16)

        def proj(lo, width):
            return _dot_nt(a, w_ref[lo:lo + width])

        v = proj(C_FF, LANES) + gate_bias
        logf = (jnp.minimum(v, 0.0) - jnp.log1p(jnp.exp(-jnp.abs(v)))) * LOG2E
        x = _pick_piece(_split_bf16(logf), term, TERM_SRC)
        c = carry_ref[...] + _dot(tri_ref[...], x)
        carry_ref[...] = c[PROJ_STRIP - 1:PROJ_STRIP, :]
        terms = _pick_piece(_split_bf16(c), term, TERM_CUT)
        one = jnp.ones_like(terms)
        qa_ref[rows, :] = jnp.where(lower, terms, one)
        ka_ref[rows, :] = jnp.where(lower, one, -terms)

        kv = proj(C_SK, 2 * D_SWA_KV)
        for half, out_ref in enumerate((sk_ref, sv_ref)):
            z = kv[:, half * LANES:(half + 1) * LANES]
            zr = pltpu.roll(z, HEAD_DIM, 1)
            out_ref[rows, :LANES] = jnp.where(lower, z, zr).astype(jnp.bfloat16)
            out_ref[rows, LANES:] = jnp.where(lower, zr, z).astype(jnp.bfloat16)

        fq_ref[rows, :] = proj(C_FQ, D_FOX).astype(jnp.bfloat16)
        fk_ref[rows, :] = proj(C_FK, D_FOX).astype(jnp.bfloat16)
        fv_ref[rows, :] = proj(C_FV, D_FOX).astype(jnp.bfloat16)
        sq_ref[rows, :] = proj(C_SQ, D_SWA).astype(jnp.bfloat16)


def _split_bf16(v):
    pieces = []
    r = v
    for _ in range(N_SPLIT):
        t = r.astype(jnp.bfloat16)
        pieces.append(t)
        r = r - t.astype(jnp.float32)
    return pieces


def _pick_piece(pieces, term, piece_of_term):
    out = pieces[piece_of_term[-1]]
    for t in range(N_TERMS - 2, -1, -1):
        if piece_of_term[t] != piece_of_term[t + 1]:
            out = jnp.where(term <= t, pieces[piece_of_term[t]], out)
    return out


def _proj_call(x2, g_pre, w_in_t, b_forget, batch, seq):
    n = batch * seq
    steps = seq // PROJ_TM
    tri = np.tril(np.ones((PROJ_STRIP, PROJ_STRIP), np.float32))
    row = lambda b, s: (b * steps + s, 0)
    const = lambda b, s: (0, 0)
    bf16 = jnp.bfloat16

    def out(width):
        return (jax.ShapeDtypeStruct((n, width), bf16), pl.BlockSpec((PROJ_TM, width), row))

    outs = [out(D_FOX), out(D_FOX), out(D_FOX), out(LANES), out(LANES),
            out(D_SWA), out(2 * LANES), out(2 * LANES)]
    return pl.pallas_call(
        _proj_kernel,
        grid=(batch, steps),
        in_specs=[
            pl.BlockSpec((PROJ_TM, D_MODEL), row),
            pl.BlockSpec((1, D_MODEL), const),
            pl.BlockSpec(w_in_t.shape, const, pipeline_mode=pl.Buffered(1)),
            pl.BlockSpec(memory_space=pltpu.SMEM),
            pl.BlockSpec((PROJ_STRIP, PROJ_STRIP), const),
        ],
        out_specs=[o[1] for o in outs],
        out_shape=[o[0] for o in outs],
        scratch_shapes=[pltpu.VMEM((1, LANES), jnp.float32),
                        pltpu.VMEM((D_PROJ, D_MODEL), bf16)],
        compiler_params=pltpu.CompilerParams(
            dimension_semantics=("arbitrary", "arbitrary"),
            vmem_limit_bytes=VMEM_LIMIT),
        name="proj",
    )(x2, g_pre, w_in_t, b_forget, jnp.asarray(tri, bf16))


def _fox_kernel(*refs, seq, n_cast):
    q_ref, qa_ref, k_ref, ka_ref, v_ref = refs[:5]
    w_refs = refs[5:5 + n_cast]
    o_ref = refs[5 + n_cast]
    wb_refs = refs[6 + n_cast:6 + 2 * n_cast]
    qm_ref, s_ref, m_ref, acc_ref = refs[6 + 2 * n_cast:]
    for w_ref, wb_ref in zip(w_refs, wb_refs):
        wb_ref[...] = w_ref[...].astype(wb_ref.dtype)

    t = FOX_T
    lane = lax.broadcasted_iota(jnp.int32, (t, LANES), 1)
    lower = lane < HEAD_DIM
    head_of_lane = lane % N_FOX_HEADS
    d = FOX_DIAG
    causal = (lax.broadcasted_iota(jnp.int32, (d, d), 0)
              >= lax.broadcasted_iota(jnp.int32, (d, d), 1))

    def tile_rows(i):
        return slice(i * t, (i + 1) * t)

    def parts(i, k0, nk):
        if k0 + nk < (i + 1) * t:
            return [(0, t, nk)]
        return [(r, r + d, nk - t + r + d) for r in range(0, t, d)]

    def mask_queries(i):
        q = q_ref[0, tile_rows(i), :]
        qa = qa_ref[0, tile_rows(i), :]
        zero = jnp.zeros_like(q)
        first_head = 2 * pl.program_id(1)
        for e in range(2):
            qm_ref[i, e] = jnp.concatenate(
                [jnp.where(lower, q, zero) if e == 0 else jnp.where(lower, zero, q),
                 jnp.where(head_of_lane == first_head + e, qa, zero)], axis=-1)

    def scores(i, k0, nk, slot):
        for r0, r1, n in parts(i, k0, nk):
            keys = slice(k0, k0 + n)
            kk = jnp.concatenate([k_ref[0, keys, :], ka_ref[0, keys, :]], axis=-1)
            for e in range(2):
                s_ref[slot, e, r0:r1, :n] = _dot_nt(qm_ref[i, e, r0:r1], kk)

    def update(i, k0, nk, slot):
        diag = k0 + nk == (i + 1) * t
        for r0, r1, n in parts(i, k0, nk):
            v = v_ref[0, k0:k0 + n, :]
            one = jnp.ones_like(v)
            for e in range(2):
                s = s_ref[slot, e, r0:r1, :n]
                if diag:
                    last = jnp.where(causal, s[:, n - d:], NEG_BIG)
                    s = last if n == d else jnp.concatenate([s[:, :n - d], last], axis=-1)
                row_max = jnp.max(s, axis=-1, keepdims=True)
                m_new = (jnp.broadcast_to(row_max, (r1 - r0, LANES)) if k0 == 0
                         else jnp.maximum(m_ref[e, r0:r1], row_max))
                p = jnp.concatenate(
                    [jnp.exp2(s[:, c * LANES:(c + 1) * LANES] - m_new)
                     for c in range(n // LANES)], axis=-1).astype(jnp.bfloat16)
                v_lower = lax.broadcasted_iota(jnp.int32, v.shape, 1) < HEAD_DIM
                ve = jnp.where(v_lower, v, one) if e == 0 else jnp.where(v_lower, one, v)
                pv = _dot(p, ve)
                if k0 > 0:
                    pv = jnp.exp2(m_ref[e, r0:r1] - m_new) * acc_ref[e, r0:r1] + pv
                acc_ref[e, r0:r1] = pv
                m_ref[e, r0:r1] = m_new

    def finish(i):
        outs = []
        for e in range(2):
            acc = acc_ref[e]
            outs.append(acc / pltpu.roll(acc, HEAD_DIM, 1))
        o_ref[0, tile_rows(i), :] = jnp.where(lower, outs[0], outs[1]).astype(o_ref.dtype)

    work = [(i, k0, min(FOX_KEYS, (i + 1) * t - k0))
            for i in range(seq // t) for k0 in range(0, (i + 1) * t, FOX_KEYS)]

    def issue_scores(idx):
        i, k0, nk = work[idx]
        if k0 == 0:
            mask_queries(i)
        scores(i, k0, nk, idx % 2)

    issue_scores(0)
    for idx, (i, k0, nk) in enumerate(work):
        if idx + 1 < len(work):
            issue_scores(idx + 1)
        update(i, k0, nk, idx % 2)
        if k0 + nk == (i + 1) * t:
            finish(i)


def _fox_call(fq, qa, fk, ka, fv, weights, batch, seq):
    t = FOX_T
    bf16 = jnp.bfloat16
    shape3 = (batch, seq, N_PAIRS * LANES)
    aug3 = (batch, seq, LANES)
    spec = pl.BlockSpec((1, seq, LANES), lambda b, g: (b, 0, g))
    aug_spec = pl.BlockSpec((1, seq, LANES), lambda b, g: (b, 0, 0))
    steps = batch * N_PAIRS
    w_specs = [pl.BlockSpec((w.shape[0] // steps, w.shape[1]), lambda b, g: (b * N_PAIRS + g, 0))
               for w in weights]
    return pl.pallas_call(
        functools.partial(_fox_kernel, seq=seq, n_cast=len(weights)),
        grid=(batch, N_PAIRS),
        in_specs=[spec, aug_spec, spec, aug_spec, spec] + w_specs,
        out_specs=[spec] + w_specs,
        out_shape=[jax.ShapeDtypeStruct(shape3, bf16)]
        + [jax.ShapeDtypeStruct(w.shape, bf16) for w in weights],
        scratch_shapes=[pltpu.VMEM((seq // t, 2, t, 2 * LANES), jnp.bfloat16),
                        pltpu.VMEM((2, 2, t, FOX_KEYS), jnp.float32),
                        pltpu.VMEM((2, t, LANES), jnp.float32),
                        pltpu.VMEM((2, t, LANES), jnp.float32)],
        compiler_params=pltpu.CompilerParams(
            dimension_semantics=("parallel", "parallel"),
            vmem_limit_bytes=VMEM_LIMIT),
        name="fox",
    )(fq.reshape(shape3), qa.reshape(aug3), fk.reshape(shape3), ka.reshape(aug3),
      fv.reshape(shape3), *weights)


def _t5_bucket(n):
    max_exact = N_BUCKETS // 2
    large = max_exact + (np.log(np.maximum(n, 1) / max_exact)
                         / np.log(MAX_DISTANCE / max_exact)
                         * (N_BUCKETS - max_exact)).astype(np.int32)
    large = np.minimum(large, N_BUCKETS - 1)
    return np.where(n < max_exact, n, large).astype(np.int32)


def _band_buckets():
    i = np.arange(Q_BLOCK)[:, None]
    j = np.arange(2 * Q_BLOCK)[None, :]
    dist = i + Q_BLOCK - j
    in_window = (dist >= 0) & (dist < WINDOW)
    return np.where(in_window, _t5_bucket(np.clip(dist, 0, None)), -1).astype(np.int32)


def _swa_kernel(bucket_ref, rel_ref, sink_ref, q_ref, k_ref, v_ref, o_ref, bias_ref, sinkc_ref,
                s_ref, *, seq):
    qb = Q_BLOCK
    qw = SWA_GROUP * HEAD_DIM

    @pl.when(pl.program_id(0) == 0)
    def _():
        bucket = bucket_ref[...]
        for h in range(N_SWA_HEADS):
            bias = jnp.full(bucket.shape, NEG_BIG, jnp.float32)
            for b in range(N_BUCKETS):
                bias = jnp.where(bucket == b, rel_ref[h, b] * LOG2E, bias)
            bias = jnp.where(lax.broadcasted_iota(jnp.int32, bucket.shape, 1) == 0,
                             sink_ref[h] * LOG2E, bias)
            group, g = divmod(h, SWA_GROUP)
            bias_ref[group, g * qb:(g + 1) * qb, :] = bias
            sinkc_ref[group, g * qb:(g + 1) * qb, :] = jnp.full((qb, LANES), sink_ref[h] * LOG2E)

    lower = lax.broadcasted_iota(jnp.int32, (qb, LANES), 1) < HEAD_DIM

    def band_of(n):
        if n == 0:
            return slice(0, qb), qb, slice(qb, 2 * qb)
        return slice((n - 1) * qb, (n + 1) * qb), 2 * qb, slice(0, 2 * qb)

    def scores(kv, n, slot):
        rows = slice(n * qb, (n + 1) * qb)
        band, width, cols = band_of(n)
        parts = []
        for pair in range(SWA_GROUP // 2):
            qp = q_ref[0, rows, kv * qw + pair * LANES:kv * qw + (pair + 1) * LANES]
            zero = jnp.zeros_like(qp)
            parts += [jnp.where(lower, qp, zero), jnp.where(lower, zero, qp)]
        s_ref[slot, :, :width] = (_dot_nt(jnp.concatenate(parts, axis=0), sink_slot(kv, n, k_ref))
                                  + bias_ref[kv, :, cols])

    def sink_slot(kv, n, ref):
        band, _, _ = band_of(n)
        x = ref[0, band, kv * LANES:(kv + 1) * LANES]
        if n == 0:
            return x
        return jnp.where(lax.broadcasted_iota(jnp.int32, x.shape, 0) == 0, jnp.zeros_like(x), x)

    def update(kv, n, slot):
        rows = slice(n * qb, (n + 1) * qb)
        _, width, _ = band_of(n)
        vb = sink_slot(kv, n, v_ref)
        vb = jnp.concatenate([vb, jnp.ones_like(vb)], axis=-1)
        s = s_ref[slot, :, :width]
        m = jnp.max(s, axis=-1, keepdims=True)
        if n == 0:
            sink = sinkc_ref[kv]
            m = jnp.maximum(m, sink)
        p = jnp.concatenate(
            [jnp.exp2(s[:, c * LANES:(c + 1) * LANES] - m) for c in range(width // LANES)],
            axis=-1).astype(jnp.bfloat16)
        acc = _dot(p, vb)
        denom = acc[:, LANES:] + jnp.exp2(sink - m) if n == 0 else acc[:, LANES:]
        out = acc[:, :LANES] / denom
        for pair in range(SWA_GROUP // 2):
            even = slice(2 * pair * qb, (2 * pair + 1) * qb)
            odd = slice((2 * pair + 1) * qb, (2 * pair + 2) * qb)
            o_ref[0, rows, kv * qw + pair * LANES:kv * qw + (pair + 1) * LANES] = jnp.where(
                lower, out[even], out[odd]).astype(o_ref.dtype)

    work = [(kv, n) for kv in range(N_SWA_KV_HEADS) for n in range(seq // qb)]
    scores(*work[0], 0)
    for idx, item in enumerate(work):
        if idx + 1 < len(work):
            scores(*work[idx + 1], (idx + 1) % 2)
        update(*item, idx % 2)


def _swa_call(sq, sk, sv, rel_bias, sinks, batch, seq):
    bucket = _band_buckets()
    smem = pl.BlockSpec(memory_space=pltpu.SMEM)
    q_spec = pl.BlockSpec((1, seq, D_SWA), lambda b: (b, 0, 0))
    kv_spec = pl.BlockSpec((1, seq, N_SWA_KV_HEADS * LANES), lambda b: (b, 0, 0))
    return pl.pallas_call(
        functools.partial(_swa_kernel, seq=seq),
        grid=(batch,),
        in_specs=[pl.BlockSpec(bucket.shape, lambda b: (0, 0)), smem, smem,
                  q_spec, kv_spec, kv_spec],
        out_specs=q_spec,
        out_shape=jax.ShapeDtypeStruct((batch, seq, D_SWA), jnp.bfloat16),
        scratch_shapes=[
            pltpu.VMEM((N_SWA_KV_HEADS, SWA_GROUP * Q_BLOCK, 2 * Q_BLOCK), jnp.float32),
            pltpu.VMEM((N_SWA_KV_HEADS, SWA_GROUP * Q_BLOCK, LANES), jnp.float32),
            pltpu.VMEM((2, SWA_GROUP * Q_BLOCK, 2 * Q_BLOCK), jnp.float32)],
        compiler_params=pltpu.CompilerParams(
            dimension_semantics=("arbitrary",),
            vmem_limit_bytes=VMEM_LIMIT),
        name="swa",
    )(jnp.asarray(bucket), rel_bias.T, sinks, sq.reshape(batch, seq, D_SWA),
      sk.reshape(batch, seq, 2 * LANES), sv.reshape(batch, seq, 2 * LANES))


def _post_kernel(fox_ref, swa_ref, x_ref, p_ref, wo_ref, w1_ref, w2_ref, wg_ref, wp_ref,
                 g_attn_ref, g_pre_ref, g_ff_ref, g_ple_ref, o_ref):
    bf16 = jnp.bfloat16
    h_parts, m_parts = [], []
    for r in range(0, POST_TM, POST_STRIP):
        rows = slice(r, r + POST_STRIP)
        mix = _dot(jnp.concatenate([fox_ref[rows, :], swa_ref[rows, :]], axis=-1), wo_ref[...])
        hh = x_ref[rows, :] + _rms(mix) * g_attn_ref[...]
        h_parts.append(hh)
        m_parts.append((_rms(hh) * g_pre_ref[...]).astype(bf16))
    h = jnp.concatenate(h_parts, axis=0)
    m = jnp.concatenate(m_parts, axis=0)
    ple = _dot(p_ref[...].astype(bf16), wp_ref[...])

    y = jnp.zeros((POST_TM, D_MODEL), jnp.float32)
    for c in range(D_FF // FF_CHUNK):
        cols = slice(c * FF_CHUNK, (c + 1) * FF_CHUNK)
        hid = jnp.square(jnp.maximum(_dot(m, w1_ref[:, cols]), 0.0))
        y = y + _dot(hid.astype(bf16), w2_ref[cols, :])

    for r in range(0, POST_TM, POST_STRIP):
        rows = slice(r, r + POST_STRIP)
        hh = h[rows] + _rms(y[rows]) * g_ff_ref[...]
        gate = jax.nn.sigmoid(_dot(hh.astype(bf16), wg_ref[...]))
        o_ref[rows, :] = hh + _rms(ple[rows] * gate) * g_ple_ref[...]


def _post_call(fox, swa, x2, p2, wo, w1, w2, wg, wp, gains):
    n = x2.shape[0]
    row = lambda i: (i, 0)
    const = lambda i: (0, 0)

    def resident(shape):
        return pl.BlockSpec(shape, const, pipeline_mode=pl.Buffered(1))

    return pl.pallas_call(
        _post_kernel,
        grid=(n // POST_TM,),
        in_specs=[
            pl.BlockSpec((POST_TM, D_FOX), row),
            pl.BlockSpec((POST_TM, D_SWA), row),
            pl.BlockSpec((POST_TM, D_MODEL), row),
            pl.BlockSpec((POST_TM, D_PLE), row),
            resident(wo.shape), resident(w1.shape), resident(w2.shape),
            resident(wg.shape), resident(wp.shape), *[resident(g.shape) for g in gains],
        ],
        out_specs=pl.BlockSpec((POST_TM, D_MODEL), row),
        out_shape=jax.ShapeDtypeStruct((n, D_MODEL), jnp.float32),
        compiler_params=pltpu.CompilerParams(
            dimension_semantics=("parallel",),
            vmem_limit_bytes=VMEM_LIMIT),
        name="post",
    )(fox, swa, x2, p2, wo, w1, w2, wg, wp, *gains)


def kernel(x, p, w_in, b_forget, w_out, rel_bias, swa_sinks, g_attn_pre, g_attn_post,
           w_ff1, w_ff2, g_ff_pre, g_ff_post, w_ple, w_ple_gate, g_ple_post):
    batch, seq, d_model = x.shape
    n = batch * seq
    assert d_model == D_MODEL and seq % PROJ_TM == 0 and seq % FOX_T == 0
    assert FOX_KEYS % FOX_T == 0 and seq % Q_BLOCK == 0 and n % POST_TM == 0
    h = x.reshape(n, D_MODEL)
    for i in range(p.shape[0]):
        fq, fk, fv, qa, ka, sq, sk, sv = _proj_call(
            h, g_attn_pre[i].reshape(1, D_MODEL), w_in[i].T, b_forget[i], batch, seq)
        fox, wo, w1, w2, wg = _fox_call(
            fq, qa, fk, ka, fv, [w_out[i], w_ff1[i], w_ff2[i], w_ple_gate[i]], batch, seq)
        wp = w_ple[i].astype(jnp.bfloat16)
        swa = _swa_call(sq, sk, sv, rel_bias, swa_sinks[i], batch, seq).reshape(n, D_SWA)
        gains = [g[i].reshape(1, D_MODEL) for g in (g_attn_post, g_ff_pre, g_ff_post, g_ple_post)]
        h = _post_call(fox.reshape(n, D_FOX), swa, h, p[i].reshape(n, D_PLE),
                       wo, w1, w2, wg, wp, gains)
    return h.reshape(batch, seq, D_MODEL)
```

```python
import functools

import jax
import jax.numpy as jnp
import numpy as np
from jax import lax
from jax.experimental import pallas as pl
from jax.experimental.pallas import tpu as pltpu

D_MODEL = 1024
HEAD_DIM = 64
N_FOX_HEADS = 8
N_SWA_HEADS = 8
N_SWA_KV_HEADS = 2
SWA_GROUP = N_SWA_HEADS // N_SWA_KV_HEADS
D_FOX = N_FOX_HEADS * HEAD_DIM
D_SWA = N_SWA_HEADS * HEAD_DIM
D_SWA_KV = N_SWA_KV_HEADS * HEAD_DIM
D_FF = 4 * D_MODEL
D_PLE = 256
WINDOW = 128
Q_BLOCK = 128
N_BUCKETS = 32
MAX_DISTANCE = 128
RMS_EPS = 1e-6

LANES = 128
N_PAIRS = N_FOX_HEADS // 2
N_SPLIT = 3
N_TERMS = HEAD_DIM // N_FOX_HEADS
TERM_SRC = (0, 0, 0, 1, 1, 1, 2, 2)
TERM_CUT = (0, 1, 2, 0, 1, 2, 0, 1)
assert len(TERM_SRC) == len(TERM_CUT) == N_TERMS and 2 * N_TERMS * N_FOX_HEADS == LANES
NEG_BIG = -1e30
LOG2E = float(np.log2(np.e))

PROJ_TM = 1024
PROJ_STRIP = 512
FOX_T = 512
FOX_DIAG = 256
FOX_KEYS = 1536
POST_TM = 512
POST_STRIP = 256
FF_CHUNK = 1024
VMEM_LIMIT = 56 * 1024 * 1024
POST_VMEM_LIMIT = 60000 * 1024

C_FQ, C_FK, C_FV = 0, D_FOX, 2 * D_FOX
C_SQ = 3 * D_FOX
C_SK = C_SQ + D_SWA
C_SV = C_SK + D_SWA_KV
C_FF = C_SV + D_SWA_KV
D_PROJ = C_FF + LANES


def _rms(v):
    return v * lax.rsqrt(jnp.mean(v * v, axis=-1, keepdims=True) + RMS_EPS)


def _dot(a, b):
    return jnp.dot(a, b, preferred_element_type=jnp.float32)


def _dot_nt(a, b):
    return lax.dot_general(a, b, (((1,), (1,)), ((), ())),
                           preferred_element_type=jnp.float32)


def _proj_kernel(x_ref, g_ref, wt_ref, bf_ref, tri_ref,
                 fq_ref, fk_ref, fv_ref, qa_ref, ka_ref, sq_ref, sk_ref, sv_ref,
                 carry_ref, w_ref):
    @pl.when((pl.program_id(0) == 0) & (pl.program_id(1) == 0))
    def _():
        bf16 = jnp.bfloat16
        q_scale = HEAD_DIM ** -0.5 * LOG2E
        src = np.cumsum([0, D_FOX, D_FOX, D_FOX, N_FOX_HEADS, D_SWA])
        src_fq, src_fk, src_fv, src_ff, src_sq, src_skv = (int(c) for c in src)
        w_ref[C_FQ:C_FQ + D_FOX] = (wt_ref[src_fq:src_fq + D_FOX] * q_scale).astype(bf16)
        w_ref[C_FK:C_FK + D_FOX] = wt_ref[src_fk:src_fk + D_FOX].astype(bf16)
        w_ref[C_FV:C_FV + D_FOX] = wt_ref[src_fv:src_fv + D_FOX].astype(bf16)
        w_ref[C_SQ:C_SQ + D_SWA] = (wt_ref[src_sq:src_sq + D_SWA] * q_scale).astype(bf16)
        w_ref[C_SK:C_SK + 2 * D_SWA_KV] = wt_ref[src_skv:src_skv + 2 * D_SWA_KV].astype(bf16)
        gate = wt_ref[src_ff:src_ff + N_FOX_HEADS]
        w_ref[C_FF:C_FF + LANES] = jnp.tile(gate, (LANES // N_FOX_HEADS, 1)).astype(bf16)

    @pl.when(pl.program_id(1) == 0)
    def _():
        carry_ref[...] = jnp.zeros_like(carry_ref)

    lane = lax.broadcasted_iota(jnp.int32, (PROJ_STRIP, LANES), 1)
    lower = lane < HEAD_DIM
    term = (lane // N_FOX_HEADS) % N_TERMS
    head_lane = lax.broadcasted_iota(jnp.int32, (1, LANES), 1) % N_FOX_HEADS
    gate_bias = jnp.zeros((1, LANES), jnp.float32)
    for head in range(N_FOX_HEADS):
        gate_bias = jnp.where(head_lane == head, bf_ref[head], gate_bias)

    for r in range(0, PROJ_TM, PROJ_STRIP):
        rows = slice(r, r + PROJ_STRIP)
        a = (_rms(x_ref[rows, :]) * g_ref[...]).astype(jnp.bfloat16)

        def proj(lo, width):
            return _dot_nt(a, w_ref[lo:lo + width])

        v = proj(C_FF, LANES) + gate_bias
        logf = (jnp.minimum(v, 0.0) - jnp.log1p(jnp.exp(-jnp.abs(v)))) * LOG2E
        x = _pick_piece(_split_bf16(logf), term, TERM_SRC)
        c = carry_ref[...] + _dot(tri_ref[...], x)
        carry_ref[...] = c[PROJ_STRIP - 1:PROJ_STRIP, :]
        terms = _pick_piece(_split_bf16(c), term, TERM_CUT)
        one = jnp.ones_like(terms)
        qa_ref[rows, :] = jnp.where(lower, terms, one)
        ka_ref[rows, :] = jnp.where(lower, one, -terms)

        kv = proj(C_SK, 2 * D_SWA_KV)
        for half, out_ref in enumerate((sk_ref, sv_ref)):
            z = kv[:, half * LANES:(half + 1) * LANES]
            zr = pltpu.roll(z, HEAD_DIM, 1)
            out_ref[rows, :LANES] = jnp.where(lower, z, zr).astype(jnp.bfloat16)
            out_ref[rows, LANES:] = jnp.where(lower, zr, z).astype(jnp.bfloat16)

        fq_ref[rows, :] = proj(C_FQ, D_FOX).astype(jnp.bfloat16)
        fk_ref[rows, :] = proj(C_FK, D_FOX).astype(jnp.bfloat16)
        fv_ref[rows, :] = proj(C_FV, D_FOX).astype(jnp.bfloat16)
        sq_ref[rows, :] = proj(C_SQ, D_SWA).astype(jnp.bfloat16)


def _split_bf16(v):
    pieces = []
    r = v
    for _ in range(N_SPLIT):
        t = r.astype(jnp.bfloat16)
        pieces.append(t)
        r = r - t.astype(jnp.float32)
    return pieces


def _pick_piece(pieces, term, piece_of_term):
    out = pieces[piece_of_term[-1]]
    for t in range(N_TERMS - 2, -1, -1):
        if piece_of_term[t] != piece_of_term[t + 1]:
            out = jnp.where(term <= t, pieces[piece_of_term[t]], out)
    return out


def _proj_call(x2, g_pre, w_in_t, b_forget, batch, seq):
    n = batch * seq
    steps = seq // PROJ_TM
    tri = np.tril(np.ones((PROJ_STRIP, PROJ_STRIP), np.float32))
    row = lambda b, s: (b * steps + s, 0)
    const = lambda b, s: (0, 0)
    bf16 = jnp.bfloat16

    def out(width):
        return (jax.ShapeDtypeStruct((n, width), bf16), pl.BlockSpec((PROJ_TM, width), row))

    outs = [out(D_FOX), out(D_FOX), out(D_FOX), out(LANES), out(LANES),
            out(D_SWA), out(2 * LANES), out(2 * LANES)]
    return pl.pallas_call(
        _proj_kernel,
        grid=(batch, steps),
        in_specs=[
            pl.BlockSpec((PROJ_TM, D_MODEL), row),
            pl.BlockSpec((1, D_MODEL), const),
            pl.BlockSpec(w_in_t.shape, const, pipeline_mode=pl.Buffered(1)),
            pl.BlockSpec(memory_space=pltpu.SMEM),
            pl.BlockSpec((PROJ_STRIP, PROJ_STRIP), const),
        ],
        out_specs=[o[1] for o in outs],
        out_shape=[o[0] for o in outs],
        scratch_shapes=[pltpu.VMEM((1, LANES), jnp.float32),
                        pltpu.VMEM((D_PROJ, D_MODEL), bf16)],
        compiler_params=pltpu.CompilerParams(
            dimension_semantics=("arbitrary", "arbitrary"),
            vmem_limit_bytes=VMEM_LIMIT),
        name="proj",
    )(x2, g_pre, w_in_t, b_forget, jnp.asarray(tri, bf16))


def _fox_kernel(*refs, seq, n_cast):
    q_ref, qa_ref, k_ref, ka_ref, v_ref = refs[:5]
    w_refs = refs[5:5 + n_cast]
    o_ref = refs[5 + n_cast]
    wb_refs = refs[6 + n_cast:6 + 2 * n_cast]
    qm_ref, s_ref, m_ref, acc_ref = refs[6 + 2 * n_cast:]
    for w_ref, wb_ref in zip(w_refs, wb_refs):
        wb_ref[...] = w_ref[...].astype(wb_ref.dtype)

    t = FOX_T
    lane = lax.broadcasted_iota(jnp.int32, (t, LANES), 1)
    lower = lane < HEAD_DIM
    head_of_lane = lane % N_FOX_HEADS
    d = FOX_DIAG
    causal = (lax.broadcasted_iota(jnp.int32, (d, d), 0)
              >= lax.broadcasted_iota(jnp.int32, (d, d), 1))

    def tile_rows(i):
        return slice(i * t, (i + 1) * t)

    def parts(i, k0, nk):
        if k0 + nk < (i + 1) * t:
            return [(0, t, nk)]
        return [(r, r + d, nk - t + r + d) for r in range(0, t, d)]

    def mask_queries(i):
        q = q_ref[0, tile_rows(i), :]
        qa = qa_ref[0, tile_rows(i), :]
        zero = jnp.zeros_like(q)
        first_head = 2 * pl.program_id(1)
        for e in range(2):
            qm_ref[i, e] = jnp.concatenate(
                [jnp.where(lower, q, zero) if e == 0 else jnp.where(lower, zero, q),
                 jnp.where(head_of_lane == first_head + e, qa, zero)], axis=-1)

    def scores(i, k0, nk, slot):
        for r0, r1, n in parts(i, k0, nk):
            keys = slice(k0, k0 + n)
            kk = jnp.concatenate([k_ref[0, keys, :], ka_ref[0, keys, :]], axis=-1)
            for e in range(2):
                s_ref[slot, e, r0:r1, :n] = _dot_nt(qm_ref[i, e, r0:r1], kk)

    def update(i, k0, nk, slot):
        diag = k0 + nk == (i + 1) * t
        for r0, r1, n in parts(i, k0, nk):
            v = v_ref[0, k0:k0 + n, :]
            one = jnp.ones_like(v)
            for e in range(2):
                s = s_ref[slot, e, r0:r1, :n]
                if diag:
                    last = jnp.where(causal, s[:, n - d:], NEG_BIG)
                    s = last if n == d else jnp.concatenate([s[:, :n - d], last], axis=-1)
                row_max = jnp.max(s, axis=-1, keepdims=True)
                m_new = (jnp.broadcast_to(row_max, (r1 - r0, LANES)) if k0 == 0
                         else jnp.maximum(m_ref[e, r0:r1], row_max))
                p = jnp.concatenate(
                    [jnp.exp2(s[:, c * LANES:(c + 1) * LANES] - m_new)
                     for c in range(n // LANES)], axis=-1).astype(jnp.bfloat16)
                v_lower = lax.broadcasted_iota(jnp.int32, v.shape, 1) < HEAD_DIM
                ve = jnp.where(v_lower, v, one) if e == 0 else jnp.where(v_lower, one, v)
                pv = _dot(p, ve)
                if k0 > 0:
                    pv = jnp.exp2(m_ref[e, r0:r1] - m_new) * acc_ref[e, r0:r1] + pv
                acc_ref[e, r0:r1] = pv
                m_ref[e, r0:r1] = m_new

    def finish(i):
        outs = []
        for e in range(2):
            acc = acc_ref[e]
            outs.append(acc / pltpu.roll(acc, HEAD_DIM, 1))
        o_ref[0, tile_rows(i), :] = jnp.where(lower, outs[0], outs[1]).astype(o_ref.dtype)

    work = [(i, k0, min(FOX_KEYS, (i + 1) * t - k0))
            for i in range(seq // t) for k0 in range(0, (i + 1) * t, FOX_KEYS)]

    def issue_scores(idx):
        i, k0, nk = work[idx]
        if k0 == 0:
            mask_queries(i)
        scores(i, k0, nk, idx % 2)

    issue_scores(0)
    for idx, (i, k0, nk) in enumerate(work):
        if idx + 1 < len(work):
            issue_scores(idx + 1)
        update(i, k0, nk, idx % 2)
        if k0 + nk == (i + 1) * t:
            finish(i)


def _fox_call(fq, qa, fk, ka, fv, weights, batch, seq):
    t = FOX_T
    bf16 = jnp.bfloat16
    shape3 = (batch, seq, N_PAIRS * LANES)
    aug3 = (batch, seq, LANES)
    spec = pl.BlockSpec((1, seq, LANES), lambda b, g: (b, 0, g))
    aug_spec = pl.BlockSpec((1, seq, LANES), lambda b, g: (b, 0, 0))
    steps = batch * N_PAIRS
    w_specs = [pl.BlockSpec((w.shape[0] // steps, w.shape[1]), lambda b, g: (b * N_PAIRS + g, 0))
               for w in weights]
    return pl.pallas_call(
        functools.partial(_fox_kernel, seq=seq, n_cast=len(weights)),
        grid=(batch, N_PAIRS),
        in_specs=[spec, aug_spec, spec, aug_spec, spec] + w_specs,
        out_specs=[spec] + w_specs,
        out_shape=[jax.ShapeDtypeStruct(shape3, bf16)]
        + [jax.ShapeDtypeStruct(w.shape, bf16) for w in weights],
        scratch_shapes=[pltpu.VMEM((seq // t, 2, t, 2 * LANES), jnp.bfloat16),
                        pltpu.VMEM((2, 2, t, FOX_KEYS), jnp.float32),
                        pltpu.VMEM((2, t, LANES), jnp.float32),
                        pltpu.VMEM((2, t, LANES), jnp.float32)],
        compiler_params=pltpu.CompilerParams(
            dimension_semantics=("parallel", "parallel"),
            vmem_limit_bytes=VMEM_LIMIT),
        name="fox",
    )(fq.reshape(shape3), qa.reshape(aug3), fk.reshape(shape3), ka.reshape(aug3),
      fv.reshape(shape3), *weights)


def _t5_bucket(n):
    max_exact = N_BUCKETS // 2
    large = max_exact + (np.log(np.maximum(n, 1) / max_exact)
                         / np.log(MAX_DISTANCE / max_exact)
                         * (N_BUCKETS - max_exact)).astype(np.int32)
    large = np.minimum(large, N_BUCKETS - 1)
    return np.where(n < max_exact, n, large).astype(np.int32)


def _band_buckets():
    i = np.arange(Q_BLOCK)[:, None]
    j = np.arange(2 * Q_BLOCK)[None, :]
    dist = i + Q_BLOCK - j
    in_window = (dist >= 0) & (dist < WINDOW)
    return np.where(in_window, _t5_bucket(np.clip(dist, 0, None)), -1).astype(np.int32)


def _post_kernel(bucket_ref, rel_ref, sink_ref, fox_ref, sq_ref, k_ref, v_ref, x_ref, p_ref,
                 wo_ref, w1_ref, w2_ref, wg_ref, wp_ref,
                 g_attn_ref, g_pre_ref, g_ff_ref, g_ple_ref, o_ref,
                 swa_ref, bias_ref, s_ref, *, n_tiles, tiles_per_seq):
    bf16 = jnp.bfloat16
    j = pl.program_id(0)
    qb = Q_BLOCK
    qw = SWA_GROUP * HEAD_DIM
    tile = jnp.minimum(j, n_tiles - 1)
    seq_row = (tile % tiles_per_seq) * POST_TM
    first = seq_row == 0
    lower = lax.broadcasted_iota(jnp.int32, (qb, LANES), 1) < HEAD_DIM
    band_row = lax.broadcasted_iota(jnp.int32, (2 * qb, LANES), 0)

    def band(n, kv, ref):
        if n > 0:
            start, sink_row = seq_row + (n - 1) * qb, 0
        else:
            start, sink_row = jnp.maximum(seq_row - qb, 0), jnp.where(first, 2 * qb - 1, 0)
        x = ref[0, pl.ds(pl.multiple_of(start, qb), 2 * qb), kv * LANES:(kv + 1) * LANES]
        return jnp.where(band_row == sink_row, jnp.zeros_like(x), x)

    def swa_scores(kv, n, slot):
        rows = slice(n * qb, (n + 1) * qb)
        parts = []
        for pair in range(SWA_GROUP // 2):
            qp = sq_ref[rows, kv * qw + pair * LANES:kv * qw + (pair + 1) * LANES]
            zero = jnp.zeros_like(qp)
            parts += [jnp.where(lower, qp, zero), jnp.where(lower, zero, qp)]
        bias = bias_ref[first.astype(jnp.int32), kv] if n == 0 else bias_ref[0, kv]
        s_ref[slot] = _dot_nt(jnp.concatenate(parts, axis=0), band(n, kv, k_ref)) + bias

    def swa_update(kv, n, slot):
        rows = slice(n * qb, (n + 1) * qb)
        vb = band(n, kv, v_ref)
        vb = jnp.concatenate([vb, jnp.ones_like(vb)], axis=-1)
        s = s_ref[slot]
        m = jnp.max(s, axis=-1, keepdims=True)
        p = jnp.concatenate(
            [jnp.exp2(s[:, c * LANES:(c + 1) * LANES] - m) for c in range(2 * qb // LANES)],
            axis=-1).astype(bf16)
        acc = _dot(p, vb)
        out = acc[:, :LANES] / acc[:, LANES:]
        for pair in range(SWA_GROUP // 2):
            even = slice(2 * pair * qb, (2 * pair + 1) * qb)
            odd = slice((2 * pair + 1) * qb, (2 * pair + 2) * qb)
            swa_ref[rows, kv * qw + pair * LANES:kv * qw + (pair + 1) * LANES] = jnp.where(
                lower, out[even], out[odd]).astype(swa_ref.dtype)

    items = [(kv, n) for kv in range(N_SWA_KV_HEADS) for n in range(POST_TM // qb)]

    @pl.when(j == 0)
    def _():
        bucket = bucket_ref[...]
        col = lax.broadcasted_iota(jnp.int32, bucket.shape, 1)
        masked = jnp.full((qb, qb), NEG_BIG, jnp.float32)
        for h in range(N_SWA_HEADS):
            bias = jnp.full(bucket.shape, NEG_BIG, jnp.float32)
            for b in range(N_BUCKETS):
                bias = jnp.where(bucket == b, rel_ref[h, b] * LOG2E, bias)
            sink = sink_ref[h] * LOG2E
            group, g = divmod(h, SWA_GROUP)
            bias_ref[0, group, g * qb:(g + 1) * qb, :] = jnp.where(col == 0, sink, bias)
            bias_ref[1, group, g * qb:(g + 1) * qb, :] = jnp.where(
                col == 2 * qb - 1, sink, jnp.concatenate([bias[:, qb:], masked], axis=-1))
        swa_scores(*items[0], 0)
        for idx, item in enumerate(items):
            if idx + 1 < len(items):
                swa_scores(*items[idx + 1], (idx + 1) % 2)
            swa_update(*item, idx % 2)

    @pl.when(j > 0)
    def _():
        h_parts, m_parts = [], []
        for r in range(0, POST_TM, POST_STRIP):
            rows = slice(r, r + POST_STRIP)
            mix = _dot(jnp.concatenate([fox_ref[rows, :], swa_ref[rows, :]], axis=-1), wo_ref[...])
            hh = x_ref[rows, :] + _rms(mix) * g_attn_ref[...]
            h_parts.append(hh)
            m_parts.append((_rms(hh) * g_pre_ref[...]).astype(bf16))
        h = jnp.concatenate(h_parts, axis=0)
        m = jnp.concatenate(m_parts, axis=0)
        ple = _dot(p_ref[...].astype(bf16), wp_ref[...])

        done = [0]

        def attention_step():
            k = done[0]
            if k < len(items):
                if k + 1 < len(items):
                    swa_scores(*items[k + 1], (k + 1) % 2)
                swa_update(*items[k], k % 2)
                done[0] = k + 1

        swa_scores(*items[0], 0)
        y = jnp.zeros((POST_TM, D_MODEL), jnp.float32)
        for c in range(D_FF // FF_CHUNK):
            cols = slice(c * FF_CHUNK, (c + 1) * FF_CHUNK)
            hid = jnp.square(jnp.maximum(_dot(m, w1_ref[:, cols]), 0.0))
            attention_step()
            y = y + _dot(hid.astype(bf16), w2_ref[cols, :])
            attention_step()
        while done[0] < len(items):
            attention_step()

        for r in range(0, POST_TM, POST_STRIP):
            rows = slice(r, r + POST_STRIP)
            hh = h[rows] + _rms(y[rows]) * g_ff_ref[...]
            gate = jax.nn.sigmoid(_dot(hh.astype(bf16), wg_ref[...]))
            o_ref[rows, :] = hh + _rms(ple[rows] * gate) * g_ple_ref[...]


def _post_call(fox, sq, sk, sv, rel_bias, sinks, x2, p2, wo, w1, w2, wg, wp, gains, seq):
    n = x2.shape[0]
    n_tiles = n // POST_TM
    tiles_per_seq = seq // POST_TM
    bucket = _band_buckets()
    prev = lambda j: (jnp.maximum(j - 1, 0), 0)
    cur = lambda j: (jnp.minimum(j, n_tiles - 1), 0)
    cur_seq = lambda j: (jnp.minimum(j, n_tiles - 1) // tiles_per_seq, 0, 0)
    const = lambda j: (0, 0)
    smem = pl.BlockSpec(memory_space=pltpu.SMEM)

    def resident(shape):
        return pl.BlockSpec(shape, const, pipeline_mode=pl.Buffered(1))

    kv_spec = pl.BlockSpec((1, seq, N_SWA_KV_HEADS * LANES), cur_seq)
    return pl.pallas_call(
        functools.partial(_post_kernel, n_tiles=n_tiles, tiles_per_seq=tiles_per_seq),
        grid=(n_tiles + 1,),
        in_specs=[
            resident(bucket.shape), smem, smem,
            pl.BlockSpec((POST_TM, D_FOX), prev),
            pl.BlockSpec((POST_TM, D_SWA), cur),
            kv_spec, kv_spec,
            pl.BlockSpec((POST_TM, D_MODEL), prev),
            pl.BlockSpec((POST_TM, D_PLE), prev),
            resident(wo.shape), resident(w1.shape), resident(w2.shape),
            resident(wg.shape), resident(wp.shape), *[resident(g.shape) for g in gains],
        ],
        out_specs=pl.BlockSpec((POST_TM, D_MODEL), prev),
        out_shape=jax.ShapeDtypeStruct((n, D_MODEL), jnp.float32),
        scratch_shapes=[
            pltpu.VMEM((POST_TM, D_SWA), jnp.bfloat16),
            pltpu.VMEM((2, N_SWA_KV_HEADS, SWA_GROUP * Q_BLOCK, 2 * Q_BLOCK), jnp.float32),
            pltpu.VMEM((2, SWA_GROUP * Q_BLOCK, 2 * Q_BLOCK), jnp.float32)],
        compiler_params=pltpu.CompilerParams(
            dimension_semantics=("arbitrary",),
            vmem_limit_bytes=POST_VMEM_LIMIT),
        name="post",
    )(jnp.asarray(bucket), rel_bias.T, sinks, fox, sq, sk, sv, x2, p2,
      wo, w1, w2, wg, wp, *gains)


def kernel(x, p, w_in, b_forget, w_out, rel_bias, swa_sinks, g_attn_pre, g_attn_post,
           w_ff1, w_ff2, g_ff_pre, g_ff_post, w_ple, w_ple_gate, g_ple_post):
    batch, seq, d_model = x.shape
    n = batch * seq
    assert d_model == D_MODEL and seq % PROJ_TM == 0 and seq % FOX_T == 0
    assert FOX_KEYS % FOX_T == 0 and POST_TM % Q_BLOCK == 0 and seq % POST_TM == 0
    h = x.reshape(n, D_MODEL)
    for i in range(p.shape[0]):
        fq, fk, fv, qa, ka, sq, sk, sv = _proj_call(
            h, g_attn_pre[i].reshape(1, D_MODEL), w_in[i].T, b_forget[i], batch, seq)
        fox, wo, w1, w2, wg = _fox_call(
            fq, qa, fk, ka, fv, [w_out[i], w_ff1[i], w_ff2[i], w_ple_gate[i]], batch, seq)
        wp = w_ple[i].astype(jnp.bfloat16)
        gains = [g[i].reshape(1, D_MODEL) for g in (g_attn_post, g_ff_pre, g_ff_post, g_ple_post)]
        kv_shape = (batch, seq, N_SWA_KV_HEADS * LANES)
        h = _post_call(fox.reshape(n, D_FOX), sq, sk.reshape(kv_shape), sv.reshape(kv_shape),
                       rel_bias, swa_sinks[i], h, p[i].reshape(n, D_PLE),
                       wo, w1, w2, wg, wp, gains, seq)
    return h.reshape(batch, seq, D_MODEL)
```

```python
import functools

import jax
import jax.numpy as jnp
import numpy as np
from jax import lax
from jax.experimental import pallas as pl
from jax.experimental.pallas import tpu as pltpu

D_MODEL = 1024
HEAD_DIM = 64
N_FOX_HEADS = 8
N_SWA_HEADS = 8
N_SWA_KV_HEADS = 2
SWA_GROUP = N_SWA_HEADS // N_SWA_KV_HEADS
D_FOX = N_FOX_HEADS * HEAD_DIM
D_SWA = N_SWA_HEADS * HEAD_DIM
D_SWA_KV = N_SWA_KV_HEADS * HEAD_DIM
D_FF = 4 * D_MODEL
D_PLE = 256
WINDOW = 128
Q_BLOCK = 128
N_BUCKETS = 32
MAX_DISTANCE = 128
RMS_EPS = 1e-6

LANES = 128
N_PAIRS = N_FOX_HEADS // 2
N_SPLIT = 3
N_TERMS = HEAD_DIM // N_FOX_HEADS
TERM_SRC = (0, 0, 0, 1, 1, 1, 2, 2)
TERM_CUT = (0, 1, 2, 0, 1, 2, 0, 1)
assert len(TERM_SRC) == len(TERM_CUT) == N_TERMS and 2 * N_TERMS * N_FOX_HEADS == LANES
NEG_BIG = -1e30
LOG2E = float(np.log2(np.e))

PROJ_TM = 1024
PROJ_STRIP = 512
FOX_T = 512
FOX_DIAG = 256
FOX_KEYS = 1536
POST_TM = 512
POST_STRIP = 256
FF_CHUNK = 1024
VMEM_LIMIT = 56 * 1024 * 1024

C_FQ, C_FK, C_FV = 0, D_FOX, 2 * D_FOX
C_SQ = 3 * D_FOX
C_SK = C_SQ + D_SWA
C_SV = C_SK + D_SWA_KV
C_FF = C_SV + D_SWA_KV
D_PROJ = C_FF + LANES


def _rms(v):
    return v * lax.rsqrt(jnp.mean(v * v, axis=-1, keepdims=True) + RMS_EPS)


def _dot(a, b):
    return jnp.dot(a, b, preferred_element_type=jnp.float32)


def _dot_nt(a, b):
    return lax.dot_general(a, b, (((1,), (1,)), ((), ())),
                           preferred_element_type=jnp.float32)


def _proj_kernel(x_ref, g_ref, wt_ref, bf_ref, tri_ref,
                 fq_ref, fk_ref, fv_ref, qa_ref, ka_ref, sq_ref, sk_ref, sv_ref,
                 carry_ref, w_ref):
    @pl.when((pl.program_id(0) == 0) & (pl.program_id(1) == 0))
    def _():
        bf16 = jnp.bfloat16
        q_scale = HEAD_DIM ** -0.5 * LOG2E
        src = np.cumsum([0, D_FOX, D_FOX, D_FOX, N_FOX_HEADS, D_SWA])
        src_fq, src_fk, src_fv, src_ff, src_sq, src_skv = (int(c) for c in src)
        w_ref[C_FQ:C_FQ + D_FOX] = (wt_ref[src_fq:src_fq + D_FOX] * q_scale).astype(bf16)
        w_ref[C_FK:C_FK + D_FOX] = wt_ref[src_fk:src_fk + D_FOX].astype(bf16)
        w_ref[C_FV:C_FV + D_FOX] = wt_ref[src_fv:src_fv + D_FOX].astype(bf16)
        w_ref[C_SQ:C_SQ + D_SWA] = (wt_ref[src_sq:src_sq + D_SWA] * q_scale).astype(bf16)
        w_ref[C_SK:C_SK + 2 * D_SWA_KV] = wt_ref[src_skv:src_skv + 2 * D_SWA_KV].astype(bf16)
        gate = wt_ref[src_ff:src_ff + N_FOX_HEADS]
        w_ref[C_FF:C_FF + LANES] = jnp.tile(gate, (LANES // N_FOX_HEADS, 1)).astype(bf16)

    @pl.when(pl.program_id(1) == 0)
    def _():
        carry_ref[...] = jnp.zeros_like(carry_ref)

    lane = lax.broadcasted_iota(jnp.int32, (PROJ_STRIP, LANES), 1)
    lower = lane < HEAD_DIM
    term = (lane // N_FOX_HEADS) % N_TERMS
    head_lane = lax.broadcasted_iota(jnp.int32, (1, LANES), 1) % N_FOX_HEADS
    gate_bias = jnp.zeros((1, LANES), jnp.float32)
    for head in range(N_FOX_HEADS):
        gate_bias = jnp.where(head_lane == head, bf_ref[head], gate_bias)

    for r in range(0, PROJ_TM, PROJ_STRIP):
        rows = slice(r, r + PROJ_STRIP)
        a = (_rms(x_ref[rows, :]) * g_ref[...]).astype(jnp.bfloat16)

        def proj(lo, width):
            return _dot_nt(a, w_ref[lo:lo + width])

        v = proj(C_FF, LANES) + gate_bias
        logf = (jnp.minimum(v, 0.0) - jnp.log1p(jnp.exp(-jnp.abs(v)))) * LOG2E
        x = _pick_piece(_split_bf16(logf), term, TERM_SRC)
        c = carry_ref[...] + _dot(tri_ref[...], x)
        carry_ref[...] = c[PROJ_STRIP - 1:PROJ_STRIP, :]
        terms = _pick_piece(_split_bf16(c), term, TERM_CUT)
        one = jnp.ones_like(terms)
        qa_ref[rows, :] = jnp.where(lower, terms, one)
        ka_ref[rows, :] = jnp.where(lower, one, -terms)

        kv = proj(C_SK, 2 * D_SWA_KV)
        for half, out_ref in enumerate((sk_ref, sv_ref)):
            z = kv[:, half * LANES:(half + 1) * LANES]
            zr = pltpu.roll(z, HEAD_DIM, 1)
            out_ref[rows, :LANES] = jnp.where(lower, z, zr).astype(jnp.bfloat16)
            out_ref[rows, LANES:] = jnp.where(lower, zr, z).astype(jnp.bfloat16)

        fq_ref[rows, :] = proj(C_FQ, D_FOX).astype(jnp.bfloat16)
        fk_ref[rows, :] = proj(C_FK, D_FOX).astype(jnp.bfloat16)
        fv_ref[rows, :] = proj(C_FV, D_FOX).astype(jnp.bfloat16)
        sq_ref[rows, :] = proj(C_SQ, D_SWA).astype(jnp.bfloat16)


def _split_bf16(v):
    pieces = []
    r = v
    for _ in range(N_SPLIT):
        t = r.astype(jnp.bfloat16)
        pieces.append(t)
        r = r - t.astype(jnp.float32)
    return pieces


def _pick_piece(pieces, term, piece_of_term):
    out = pieces[piece_of_term[-1]]
    for t in range(N_TERMS - 2, -1, -1):
        if piece_of_term[t] != piece_of_term[t + 1]:
            out = jnp.where(term <= t, pieces[piece_of_term[t]], out)
    return out


def _proj_call(x2, g_pre, w_in_t, b_forget, batch, seq):
    n = batch * seq
    steps = seq // PROJ_TM
    tri = np.tril(np.ones((PROJ_STRIP, PROJ_STRIP), np.float32))
    row = lambda b, s: (b * steps + s, 0)
    const = lambda b, s: (0, 0)
    bf16 = jnp.bfloat16

    def out(width):
        return (jax.ShapeDtypeStruct((n, width), bf16), pl.BlockSpec((PROJ_TM, width), row))

    outs = [out(D_FOX), out(D_FOX), out(D_FOX), out(LANES), out(LANES),
            out(D_SWA), out(2 * LANES), out(2 * LANES)]
    return pl.pallas_call(
        _proj_kernel,
        grid=(batch, steps),
        in_specs=[
            pl.BlockSpec((PROJ_TM, D_MODEL), row),
            pl.BlockSpec((1, D_MODEL), const),
            pl.BlockSpec(w_in_t.shape, const, pipeline_mode=pl.Buffered(1)),
            pl.BlockSpec(memory_space=pltpu.SMEM),
            pl.BlockSpec((PROJ_STRIP, PROJ_STRIP), const),
        ],
        out_specs=[o[1] for o in outs],
        out_shape=[o[0] for o in outs],
        scratch_shapes=[pltpu.VMEM((1, LANES), jnp.float32),
                        pltpu.VMEM((D_PROJ, D_MODEL), bf16)],
        compiler_params=pltpu.CompilerParams(
            dimension_semantics=("arbitrary", "arbitrary"),
            vmem_limit_bytes=VMEM_LIMIT),
        name="proj",
    )(x2, g_pre, w_in_t, b_forget, jnp.asarray(tri, bf16))


def _fox_kernel(*refs, seq, n_cast):
    q_ref, qa_ref, k_ref, ka_ref, v_ref = refs[:5]
    w_refs = refs[5:5 + n_cast]
    o_ref = refs[5 + n_cast]
    wb_refs = refs[6 + n_cast:6 + 2 * n_cast]
    qm_ref, s_ref, m_ref, acc_ref = refs[6 + 2 * n_cast:]
    for w_ref, wb_ref in zip(w_refs, wb_refs):
        wb_ref[...] = w_ref[...].astype(wb_ref.dtype)

    t = FOX_T
    lane = lax.broadcasted_iota(jnp.int32, (t, LANES), 1)
    lower = lane < HEAD_DIM
    head_of_lane = lane % N_FOX_HEADS
    d = FOX_DIAG
    causal = (lax.broadcasted_iota(jnp.int32, (d, d), 0)
              >= lax.broadcasted_iota(jnp.int32, (d, d), 1))

    def tile_rows(i):
        return slice(i * t, (i + 1) * t)

    def parts(i, k0, nk):
        if k0 + nk < (i + 1) * t:
            return [(0, t, nk)]
        return [(r, r + d, nk - t + r + d) for r in range(0, t, d)]

    def mask_queries(i):
        q = q_ref[0, tile_rows(i), :]
        qa = qa_ref[0, tile_rows(i), :]
        zero = jnp.zeros_like(q)
        first_head = 2 * pl.program_id(1)
        for e in range(2):
            qm_ref[i, e] = jnp.concatenate(
                [jnp.where(lower, q, zero) if e == 0 else jnp.where(lower, zero, q),
                 jnp.where(head_of_lane == first_head + e, qa, zero)], axis=-1)

    def scores(i, k0, nk, slot):
        for r0, r1, n in parts(i, k0, nk):
            keys = slice(k0, k0 + n)
            kk = jnp.concatenate([k_ref[0, keys, :], ka_ref[0, keys, :]], axis=-1)
            for e in range(2):
                s_ref[slot, e, r0:r1, :n] = _dot_nt(qm_ref[i, e, r0:r1], kk)

    def update(i, k0, nk, slot):
        diag = k0 + nk == (i + 1) * t
        for r0, r1, n in parts(i, k0, nk):
            v = v_ref[0, k0:k0 + n, :]
            one = jnp.ones_like(v)
            for e in range(2):
                s = s_ref[slot, e, r0:r1, :n]
                if diag:
                    last = jnp.where(causal, s[:, n - d:], NEG_BIG)
                    s = last if n == d else jnp.concatenate([s[:, :n - d], last], axis=-1)
                row_max = jnp.max(s, axis=-1, keepdims=True)
                m_new = (jnp.broadcast_to(row_max, (r1 - r0, LANES)) if k0 == 0
                         else jnp.maximum(m_ref[e, r0:r1], row_max))
                p = jnp.concatenate(
                    [jnp.exp2(s[:, c * LANES:(c + 1) * LANES] - m_new)
                     for c in range(n // LANES)], axis=-1).astype(jnp.bfloat16)
                v_lower = lax.broadcasted_iota(jnp.int32, v.shape, 1) < HEAD_DIM
                ve = jnp.where(v_lower, v, one) if e == 0 else jnp.where(v_lower, one, v)
                pv = _dot(p, ve)
                if k0 > 0:
                    pv = jnp.exp2(m_ref[e, r0:r1] - m_new) * acc_ref[e, r0:r1] + pv
                acc_ref[e, r0:r1] = pv
                m_ref[e, r0:r1] = m_new

    def finish(i):
        outs = []
        for e in range(2):
            acc = acc_ref[e]
            outs.append(acc / pltpu.roll(acc, HEAD_DIM, 1))
        o_ref[0, tile_rows(i), :] = jnp.where(lower, outs[0], outs[1]).astype(o_ref.dtype)

    work = [(i, k0, min(FOX_KEYS, (i + 1) * t - k0))
            for i in range(seq // t) for k0 in range(0, (i + 1) * t, FOX_KEYS)]

    def issue_scores(idx):
        i, k0, nk = work[idx]
        if k0 == 0:
            mask_queries(i)
        scores(i, k0, nk, idx % 2)

    issue_scores(0)
    for idx, (i, k0, nk) in enumerate(work):
        if idx + 1 < len(work):
            issue_scores(idx + 1)
        update(i, k0, nk, idx % 2)
        if k0 + nk == (i + 1) * t:
            finish(i)


def _fox_call(fq, qa, fk, ka, fv, weights, batch, seq):
    t = FOX_T
    bf16 = jnp.bfloat16
    shape3 = (batch, seq, N_PAIRS * LANES)
    aug3 = (batch, seq, LANES)
    spec = pl.BlockSpec((1, seq, LANES), lambda b, g: (b, 0, g))
    aug_spec = pl.BlockSpec((1, seq, LANES), lambda b, g: (b, 0, 0))
    steps = batch * N_PAIRS
    w_specs = [pl.BlockSpec((w.shape[0] // steps, w.shape[1]), lambda b, g: (b * N_PAIRS + g, 0))
               for w in weights]
    return pl.pallas_call(
        functools.partial(_fox_kernel, seq=seq, n_cast=len(weights)),
        grid=(batch, N_PAIRS),
        in_specs=[spec, aug_spec, spec, aug_spec, spec] + w_specs,
        out_specs=[spec] + w_specs,
        out_shape=[jax.ShapeDtypeStruct(shape3, bf16)]
        + [jax.ShapeDtypeStruct(w.shape, bf16) for w in weights],
        scratch_shapes=[pltpu.VMEM((seq // t, 2, t, 2 * LANES), jnp.bfloat16),
                        pltpu.VMEM((2, 2, t, FOX_KEYS), jnp.float32),
                        pltpu.VMEM((2, t, LANES), jnp.float32),
                        pltpu.VMEM((2, t, LANES), jnp.float32)],
        compiler_params=pltpu.CompilerParams(
            dimension_semantics=("parallel", "parallel"),
            vmem_limit_bytes=VMEM_LIMIT),
        name="fox",
    )(fq.reshape(shape3), qa.reshape(aug3), fk.reshape(shape3), ka.reshape(aug3),
      fv.reshape(shape3), *weights)


def _t5_bucket(n):
    max_exact = N_BUCKETS // 2
    large = max_exact + (np.log(np.maximum(n, 1) / max_exact)
                         / np.log(MAX_DISTANCE / max_exact)
                         * (N_BUCKETS - max_exact)).astype(np.int32)
    large = np.minimum(large, N_BUCKETS - 1)
    return np.where(n < max_exact, n, large).astype(np.int32)


def _band_buckets():
    i = np.arange(Q_BLOCK)[:, None]
    j = np.arange(2 * Q_BLOCK)[None, :]
    dist = i + Q_BLOCK - j
    in_window = (dist >= 0) & (dist < WINDOW)
    return np.where(in_window, _t5_bucket(np.clip(dist, 0, None)), -1).astype(np.int32)


def _swa_kernel(bucket_ref, rel_ref, sink_ref, q_ref, k_ref, v_ref, o_ref, bias_ref, sinkc_ref,
                s_ref, *, seq):
    qb = Q_BLOCK
    qw = SWA_GROUP * HEAD_DIM

    @pl.when(pl.program_id(0) == 0)
    def _():
        bucket = bucket_ref[...]
        for h in range(N_SWA_HEADS):
            bias = jnp.full(bucket.shape, NEG_BIG, jnp.float32)
            for b in range(N_BUCKETS):
                bias = jnp.where(bucket == b, rel_ref[h, b] * LOG2E, bias)
            bias = jnp.where(lax.broadcasted_iota(jnp.int32, bucket.shape, 1) == 0,
                             sink_ref[h] * LOG2E, bias)
            group, g = divmod(h, SWA_GROUP)
            bias_ref[group, g * qb:(g + 1) * qb, :] = bias
            sinkc_ref[group, g * qb:(g + 1) * qb, :] = jnp.full((qb, LANES), sink_ref[h] * LOG2E)

    lower = lax.broadcasted_iota(jnp.int32, (qb, LANES), 1) < HEAD_DIM

    def band_of(n):
        if n == 0:
            return slice(0, qb), qb, slice(qb, 2 * qb)
        return slice((n - 1) * qb, (n + 1) * qb), 2 * qb, slice(0, 2 * qb)

    def scores(kv, n, slot):
        rows = slice(n * qb, (n + 1) * qb)
        band, width, cols = band_of(n)
        parts = []
        for pair in range(SWA_GROUP // 2):
            qp = q_ref[0, rows, kv * qw + pair * LANES:kv * qw + (pair + 1) * LANES]
            zero = jnp.zeros_like(qp)
            parts += [jnp.where(lower, qp, zero), jnp.where(lower, zero, qp)]
        s_ref[slot, :, :width] = (_dot_nt(jnp.concatenate(parts, axis=0), sink_slot(kv, n, k_ref))
                                  + bias_ref[kv, :, cols])

    def sink_slot(kv, n, ref):
        band, _, _ = band_of(n)
        x = ref[0, band, kv * LANES:(kv + 1) * LANES]
        if n == 0:
            return x
        return jnp.where(lax.broadcasted_iota(jnp.int32, x.shape, 0) == 0, jnp.zeros_like(x), x)

    def update(kv, n, slot):
        rows = slice(n * qb, (n + 1) * qb)
        _, width, _ = band_of(n)
        vb = sink_slot(kv, n, v_ref)
        vb = jnp.concatenate([vb, jnp.ones_like(vb)], axis=-1)
        s = s_ref[slot, :, :width]
        m = jnp.max(s, axis=-1, keepdims=True)
        if n == 0:
            sink = sinkc_ref[kv]
            m = jnp.maximum(m, sink)
        p = jnp.concatenate(
            [jnp.exp2(s[:, c * LANES:(c + 1) * LANES] - m) for c in range(width // LANES)],
            axis=-1).astype(jnp.bfloat16)
        acc = _dot(p, vb)
        denom = acc[:, LANES:] + jnp.exp2(sink - m) if n == 0 else acc[:, LANES:]
        out = acc[:, :LANES] / denom
        for pair in range(SWA_GROUP // 2):
            even = slice(2 * pair * qb, (2 * pair + 1) * qb)
            odd = slice((2 * pair + 1) * qb, (2 * pair + 2) * qb)
            o_ref[0, rows, kv * qw + pair * LANES:kv * qw + (pair + 1) * LANES] = jnp.where(
                lower, out[even], out[odd]).astype(o_ref.dtype)

    work = [(kv, n) for kv in range(N_SWA_KV_HEADS) for n in range(seq // qb)]
    scores(*work[0], 0)
    for idx, item in enumerate(work):
        if idx + 1 < len(work):
            scores(*work[idx + 1], (idx + 1) % 2)
        update(*item, idx % 2)


def _swa_call(sq, sk, sv, rel_bias, sinks, batch, seq):
    bucket = _band_buckets()
    smem = pl.BlockSpec(memory_space=pltpu.SMEM)
    q_spec = pl.BlockSpec((1, seq, D_SWA), lambda b: (b, 0, 0))
    kv_spec = pl.BlockSpec((1, seq, N_SWA_KV_HEADS * LANES), lambda b: (b, 0, 0))
    return pl.pallas_call(
        functools.partial(_swa_kernel, seq=seq),
        grid=(batch,),
        in_specs=[pl.BlockSpec(bucket.shape, lambda b: (0, 0)), smem, smem,
                  q_spec, kv_spec, kv_spec],
        out_specs=q_spec,
        out_shape=jax.ShapeDtypeStruct((batch, seq, D_SWA), jnp.bfloat16),
        scratch_shapes=[
            pltpu.VMEM((N_SWA_KV_HEADS, SWA_GROUP * Q_BLOCK, 2 * Q_BLOCK), jnp.float32),
            pltpu.VMEM((N_SWA_KV_HEADS, SWA_GROUP * Q_BLOCK, LANES), jnp.float32),
            pltpu.VMEM((2, SWA_GROUP * Q_BLOCK, 2 * Q_BLOCK), jnp.float32)],
        compiler_params=pltpu.CompilerParams(
            dimension_semantics=("arbitrary",),
            vmem_limit_bytes=VMEM_LIMIT),
        name="swa",
    )(jnp.asarray(bucket), rel_bias.T, sinks, sq.reshape(batch, seq, D_SWA),
      sk.reshape(batch, seq, 2 * LANES), sv.reshape(batch, seq, 2 * LANES))


def _post_kernel(fox_ref, swa_ref, x_ref, p_ref, wo_ref, w1_hbm, w2_hbm, wg_hbm, wp_ref,
                 g_attn_ref, g_pre_ref, g_ff_ref, g_ple_ref, o_ref,
                 w1_ref, w2_ref, wg_ref, sem_ref):
    bf16 = jnp.bfloat16
    first_step = pl.program_id(0) == 0

    def weight_copies():
        pairs = ((w1_hbm, w1_ref), (w2_hbm, w2_ref), (wg_hbm, wg_ref))
        return [pltpu.make_async_copy(src, dst, sem_ref.at[k]) for k, (src, dst) in enumerate(pairs)]

    @pl.when(first_step)
    def _():
        for copy in weight_copies():
            copy.start()

    h_parts, m_parts = [], []
    for r in range(0, POST_TM, POST_STRIP):
        rows = slice(r, r + POST_STRIP)
        mix = _dot(jnp.concatenate([fox_ref[rows, :], swa_ref[rows, :]], axis=-1), wo_ref[...])
        hh = x_ref[rows, :] + _rms(mix) * g_attn_ref[...]
        h_parts.append(hh)
        m_parts.append((_rms(hh) * g_pre_ref[...]).astype(bf16))
    h = jnp.concatenate(h_parts, axis=0)
    m = jnp.concatenate(m_parts, axis=0)
    ple = _dot(p_ref[...].astype(bf16), wp_ref[...])

    @pl.when(first_step)
    def _():
        for copy in weight_copies():
            copy.wait()

    y = jnp.zeros((POST_TM, D_MODEL), jnp.float32)
    for c in range(D_FF // FF_CHUNK):
        cols = slice(c * FF_CHUNK, (c + 1) * FF_CHUNK)
        hid = jnp.square(jnp.maximum(_dot(m, w1_ref[:, cols]), 0.0))
        y = y + _dot(hid.astype(bf16), w2_ref[cols, :])

    for r in range(0, POST_TM, POST_STRIP):
        rows = slice(r, r + POST_STRIP)
        hh = h[rows] + _rms(y[rows]) * g_ff_ref[...]
        gate = jax.nn.sigmoid(_dot(hh.astype(bf16), wg_ref[...]))
        o_ref[rows, :] = hh + _rms(ple[rows] * gate) * g_ple_ref[...]


def _post_call(fox, swa, x2, p2, wo, w1, w2, wg, wp, gains):
    n = x2.shape[0]
    row = lambda i: (i, 0)
    const = lambda i: (0, 0)

    def resident(shape):
        return pl.BlockSpec(shape, const, pipeline_mode=pl.Buffered(1))

    return pl.pallas_call(
        _post_kernel,
        grid=(n // POST_TM,),
        in_specs=[
            pl.BlockSpec((POST_TM, D_FOX), row),
            pl.BlockSpec((POST_TM, D_SWA), row),
            pl.BlockSpec((POST_TM, D_MODEL), row),
            pl.BlockSpec((POST_TM, D_PLE), row),
            resident(wo.shape), *[pl.BlockSpec(memory_space=pl.ANY) for _ in (w1, w2, wg)],
            resident(wp.shape), *[resident(g.shape) for g in gains],
        ],
        out_specs=pl.BlockSpec((POST_TM, D_MODEL), row),
        out_shape=jax.ShapeDtypeStruct((n, D_MODEL), jnp.float32),
        scratch_shapes=[pltpu.VMEM(w1.shape, w1.dtype), pltpu.VMEM(w2.shape, w2.dtype),
                        pltpu.VMEM(wg.shape, wg.dtype), pltpu.SemaphoreType.DMA((3,))],
        compiler_params=pltpu.CompilerParams(
            dimension_semantics=("arbitrary",),
            vmem_limit_bytes=VMEM_LIMIT),
        name="post",
    )(fox, swa, x2, p2, wo, w1, w2, wg, wp, *gains)


def kernel(x, p, w_in, b_forget, w_out, rel_bias, swa_sinks, g_attn_pre, g_attn_post,
           w_ff1, w_ff2, g_ff_pre, g_ff_post, w_ple, w_ple_gate, g_ple_post):
    batch, seq, d_model = x.shape
    n = batch * seq
    assert d_model == D_MODEL and seq % PROJ_TM == 0 and seq % FOX_T == 0
    assert FOX_KEYS % FOX_T == 0 and seq % Q_BLOCK == 0 and n % POST_TM == 0
    h = x.reshape(n, D_MODEL)
    for i in range(p.shape[0]):
        fq, fk, fv, qa, ka, sq, sk, sv = _proj_call(
            h, g_attn_pre[i].reshape(1, D_MODEL), w_in[i].T, b_forget[i], batch, seq)
        fox, wo, w1, w2, wg = _fox_call(
            fq, qa, fk, ka, fv, [w_out[i], w_ff1[i], w_ff2[i], w_ple_gate[i]], batch, seq)
        wp = w_ple[i].astype(jnp.bfloat16)
        swa = _swa_call(sq, sk, sv, rel_bias, swa_sinks[i], batch, seq).reshape(n, D_SWA)
        gains = [g[i].reshape(1, D_MODEL) for g in (g_attn_post, g_ff_pre, g_ff_post, g_ple_post)]
        h = _post_call(fox.reshape(n, D_FOX), swa, h, p[i].reshape(n, D_PLE),
                       wo, w1, w2, wg, wp, gains)
    return h.reshape(batch, seq, D_MODEL)
```

```python
import functools

import jax
import jax.numpy as jnp
import numpy as np
from jax import lax
from jax.experimental import pallas as pl
from jax.experimental.pallas import tpu as pltpu

D_MODEL = 1024
HEAD_DIM = 64
N_FOX_HEADS = 8
N_SWA_HEADS = 8
N_SWA_KV_HEADS = 2
SWA_GROUP = N_SWA_HEADS // N_SWA_KV_HEADS
D_FOX = N_FOX_HEADS * HEAD_DIM
D_SWA = N_SWA_HEADS * HEAD_DIM
D_SWA_KV = N_SWA_KV_HEADS * HEAD_DIM
D_FF = 4 * D_MODEL
D_PLE = 256
WINDOW = 128
Q_BLOCK = 128
N_BUCKETS = 32
MAX_DISTANCE = 128
RMS_EPS = 1e-6

LANES = 128
N_PAIRS = N_FOX_HEADS // 2
N_SPLIT = 3
N_TERMS = HEAD_DIM // N_FOX_HEADS
TERM_SRC = (0, 0, 0, 1, 1, 1, 2, 2)
TERM_CUT = (0, 1, 2, 0, 1, 2, 0, 1)
assert len(TERM_SRC) == len(TERM_CUT) == N_TERMS and 2 * N_TERMS * N_FOX_HEADS == LANES
NEG_BIG = -1e30
LOG2E = float(np.log2(np.e))

PROJ_TM = 1024
PROJ_STRIP = 512
FOX_T = 512
FOX_DIAG = 256
FOX_KEYS = 1536
POST_TM = 512
POST_STRIP = 256
FF_CHUNK = 1024
VMEM_LIMIT = 56 * 1024 * 1024

C_FQ, C_FK, C_FV = 0, D_FOX, 2 * D_FOX
C_SQ = 3 * D_FOX
C_SK = C_SQ + D_SWA
C_SV = C_SK + D_SWA_KV
C_FF = C_SV + D_SWA_KV
D_PROJ = C_FF + LANES


def _rms(v):
    return v * lax.rsqrt(jnp.mean(v * v, axis=-1, keepdims=True) + RMS_EPS)


def _dot(a, b):
    return jnp.dot(a, b, preferred_element_type=jnp.float32)


def _dot_nt(a, b):
    return lax.dot_general(a, b, (((1,), (1,)), ((), ())),
                           preferred_element_type=jnp.float32)


def _proj_kernel(x_ref, g_ref, wt_ref, bf_ref, tri_ref, wple_ref,
                 fq_ref, fk_ref, fv_ref, qa_ref, ka_ref, sq_ref, sk_ref, sv_ref, wple_bf16_ref,
                 carry_ref, w_ref):
    @pl.when((pl.program_id(0) == 0) & (pl.program_id(1) == 0))
    def _():
        bf16 = jnp.bfloat16
        q_scale = HEAD_DIM ** -0.5 * LOG2E
        src = np.cumsum([0, D_FOX, D_FOX, D_FOX, N_FOX_HEADS, D_SWA])
        src_fq, src_fk, src_fv, src_ff, src_sq, src_skv = (int(c) for c in src)
        w_ref[C_FQ:C_FQ + D_FOX] = (wt_ref[src_fq:src_fq + D_FOX] * q_scale).astype(bf16)
        w_ref[C_FK:C_FK + D_FOX] = wt_ref[src_fk:src_fk + D_FOX].astype(bf16)
        w_ref[C_FV:C_FV + D_FOX] = wt_ref[src_fv:src_fv + D_FOX].astype(bf16)
        w_ref[C_SQ:C_SQ + D_SWA] = (wt_ref[src_sq:src_sq + D_SWA] * q_scale).astype(bf16)
        w_ref[C_SK:C_SK + 2 * D_SWA_KV] = wt_ref[src_skv:src_skv + 2 * D_SWA_KV].astype(bf16)
        gate = wt_ref[src_ff:src_ff + N_FOX_HEADS]
        w_ref[C_FF:C_FF + LANES] = jnp.tile(gate, (LANES // N_FOX_HEADS, 1)).astype(bf16)
        wple_bf16_ref[...] = wple_ref[...].astype(bf16)

    @pl.when(pl.program_id(1) == 0)
    def _():
        carry_ref[...] = jnp.zeros_like(carry_ref)

    lane = lax.broadcasted_iota(jnp.int32, (PROJ_STRIP, LANES), 1)
    lower = lane < HEAD_DIM
    term = (lane // N_FOX_HEADS) % N_TERMS
    head_lane = lax.broadcasted_iota(jnp.int32, (1, LANES), 1) % N_FOX_HEADS
    gate_bias = jnp.zeros((1, LANES), jnp.float32)
    for head in range(N_FOX_HEADS):
        gate_bias = jnp.where(head_lane == head, bf_ref[head], gate_bias)

    for r in range(0, PROJ_TM, PROJ_STRIP):
        rows = slice(r, r + PROJ_STRIP)
        a = (_rms(x_ref[rows, :]) * g_ref[...]).astype(jnp.bfloat16)

        def proj(lo, width):
            return _dot_nt(a, w_ref[lo:lo + width])

        v = proj(C_FF, LANES) + gate_bias
        logf = (jnp.minimum(v, 0.0) - jnp.log1p(jnp.exp(-jnp.abs(v)))) * LOG2E
        x = _pick_piece(_split_bf16(logf), term, TERM_SRC)
        c = carry_ref[...] + _dot(tri_ref[...], x)
        carry_ref[...] = c[PROJ_STRIP - 1:PROJ_STRIP, :]
        terms = _pick_piece(_split_bf16(c), term, TERM_CUT)
        one = jnp.ones_like(terms)
        qa_ref[rows, :] = jnp.where(lower, terms, one)
        ka_ref[rows, :] = jnp.where(lower, one, -terms)

        kv = proj(C_SK, 2 * D_SWA_KV)
        for half, out_ref in enumerate((sk_ref, sv_ref)):
            z = kv[:, half * LANES:(half + 1) * LANES]
            zr = pltpu.roll(z, HEAD_DIM, 1)
            out_ref[rows, :LANES] = jnp.where(lower, z, zr).astype(jnp.bfloat16)
            out_ref[rows, LANES:] = jnp.where(lower, zr, z).astype(jnp.bfloat16)

        fq_ref[rows, :] = proj(C_FQ, D_FOX).astype(jnp.bfloat16)
        fk_ref[rows, :] = proj(C_FK, D_FOX).astype(jnp.bfloat16)
        fv_ref[rows, :] = proj(C_FV, D_FOX).astype(jnp.bfloat16)
        sq_ref[rows, :] = proj(C_SQ, D_SWA).astype(jnp.bfloat16)


def _split_bf16(v):
    pieces = []
    r = v
    for _ in range(N_SPLIT):
        t = r.astype(jnp.bfloat16)
        pieces.append(t)
        r = r - t.astype(jnp.float32)
    return pieces


def _pick_piece(pieces, term, piece_of_term):
    out = pieces[piece_of_term[-1]]
    for t in range(N_TERMS - 2, -1, -1):
        if piece_of_term[t] != piece_of_term[t + 1]:
            out = jnp.where(term <= t, pieces[piece_of_term[t]], out)
    return out


def _proj_call(x2, g_pre, w_in_t, b_forget, w_ple, batch, seq):
    n = batch * seq
    steps = seq // PROJ_TM
    tri = np.tril(np.ones((PROJ_STRIP, PROJ_STRIP), np.float32))
    row = lambda b, s: (b * steps + s, 0)
    const = lambda b, s: (0, 0)
    bf16 = jnp.bfloat16

    def out(width):
        return (jax.ShapeDtypeStruct((n, width), bf16), pl.BlockSpec((PROJ_TM, width), row))

    outs = [out(D_FOX), out(D_FOX), out(D_FOX), out(LANES), out(LANES),
            out(D_SWA), out(2 * LANES), out(2 * LANES),
            (jax.ShapeDtypeStruct(w_ple.shape, bf16), pl.BlockSpec(w_ple.shape, const))]
    return pl.pallas_call(
        _proj_kernel,
        grid=(batch, steps),
        in_specs=[
            pl.BlockSpec((PROJ_TM, D_MODEL), row),
            pl.BlockSpec((1, D_MODEL), const),
            pl.BlockSpec(w_in_t.shape, const, pipeline_mode=pl.Buffered(1)),
            pl.BlockSpec(memory_space=pltpu.SMEM),
            pl.BlockSpec((PROJ_STRIP, PROJ_STRIP), const),
            pl.BlockSpec(w_ple.shape, const, pipeline_mode=pl.Buffered(1)),
        ],
        out_specs=[o[1] for o in outs],
        out_shape=[o[0] for o in outs],
        scratch_shapes=[pltpu.VMEM((1, LANES), jnp.float32),
                        pltpu.VMEM((D_PROJ, D_MODEL), bf16)],
        compiler_params=pltpu.CompilerParams(
            dimension_semantics=("arbitrary", "arbitrary"),
            vmem_limit_bytes=VMEM_LIMIT),
        name="proj",
    )(x2, g_pre, w_in_t, b_forget, jnp.asarray(tri, bf16), w_ple)


def _fox_kernel(*refs, seq, n_cast):
    q_ref, qa_ref, k_ref, ka_ref, v_ref = refs[:5]
    w_refs = refs[5:5 + n_cast]
    o_ref = refs[5 + n_cast]
    wb_refs = refs[6 + n_cast:6 + 2 * n_cast]
    qm_ref, s_ref, m_ref, acc_ref = refs[6 + 2 * n_cast:]
    for w_ref, wb_ref in zip(w_refs, wb_refs):
        wb_ref[...] = w_ref[...].astype(wb_ref.dtype)

    t = FOX_T
    lane = lax.broadcasted_iota(jnp.int32, (t, LANES), 1)
    lower = lane < HEAD_DIM
    head_of_lane = lane % N_FOX_HEADS
    d = FOX_DIAG
    causal = (lax.broadcasted_iota(jnp.int32, (d, d), 0)
              >= lax.broadcasted_iota(jnp.int32, (d, d), 1))

    def tile_rows(i):
        return slice(i * t, (i + 1) * t)

    def parts(i, k0, nk):
        if k0 + nk < (i + 1) * t:
            return [(0, t, nk)]
        return [(r, r + d, nk - t + r + d) for r in range(0, t, d)]

    def mask_queries(i):
        q = q_ref[0, tile_rows(i), :]
        qa = qa_ref[0, tile_rows(i), :]
        zero = jnp.zeros_like(q)
        first_head = 2 * pl.program_id(1)
        for e in range(2):
            qm_ref[i, e] = jnp.concatenate(
                [jnp.where(lower, q, zero) if e == 0 else jnp.where(lower, zero, q),
                 jnp.where(head_of_lane == first_head + e, qa, zero)], axis=-1)

    def scores(i, k0, nk, slot):
        for r0, r1, n in parts(i, k0, nk):
            keys = slice(k0, k0 + n)
            kk = jnp.concatenate([k_ref[0, keys, :], ka_ref[0, keys, :]], axis=-1)
            for e in range(2):
                s_ref[slot, e, r0:r1, :n] = _dot_nt(qm_ref[i, e, r0:r1], kk)

    def update(i, k0, nk, slot):
        diag = k0 + nk == (i + 1) * t
        for r0, r1, n in parts(i, k0, nk):
            v = v_ref[0, k0:k0 + n, :]
            one = jnp.ones_like(v)
            for e in range(2):
                s = s_ref[slot, e, r0:r1, :n]
                if diag:
                    last = jnp.where(causal, s[:, n - d:], NEG_BIG)
                    s = last if n == d else jnp.concatenate([s[:, :n - d], last], axis=-1)
                row_max = jnp.max(s, axis=-1, keepdims=True)
                m_new = (jnp.broadcast_to(row_max, (r1 - r0, LANES)) if k0 == 0
                         else jnp.maximum(m_ref[e, r0:r1], row_max))
                p = jnp.concatenate(
                    [jnp.exp2(s[:, c * LANES:(c + 1) * LANES] - m_new)
                     for c in range(n // LANES)], axis=-1).astype(jnp.bfloat16)
                v_lower = lax.broadcasted_iota(jnp.int32, v.shape, 1) < HEAD_DIM
                ve = jnp.where(v_lower, v, one) if e == 0 else jnp.where(v_lower, one, v)
                pv = _dot(p, ve)
                if k0 > 0:
                    pv = jnp.exp2(m_ref[e, r0:r1] - m_new) * acc_ref[e, r0:r1] + pv
                acc_ref[e, r0:r1] = pv
                m_ref[e, r0:r1] = m_new

    def finish(i):
        outs = []
        for e in range(2):
            acc = acc_ref[e]
            outs.append(acc / pltpu.roll(acc, HEAD_DIM, 1))
        o_ref[0, tile_rows(i), :] = jnp.where(lower, outs[0], outs[1]).astype(o_ref.dtype)

    work = [(i, k0, min(FOX_KEYS, (i + 1) * t - k0))
            for i in range(seq // t) for k0 in range(0, (i + 1) * t, FOX_KEYS)]

    def issue_scores(idx):
        i, k0, nk = work[idx]
        if k0 == 0:
            mask_queries(i)
        scores(i, k0, nk, idx % 2)

    issue_scores(0)
    for idx, (i, k0, nk) in enumerate(work):
        if idx + 1 < len(work):
            issue_scores(idx + 1)
        update(i, k0, nk, idx % 2)
        if k0 + nk == (i + 1) * t:
            finish(i)


def _fox_call(fq, qa, fk, ka, fv, weights, batch, seq):
    t = FOX_T
    bf16 = jnp.bfloat16
    shape3 = (batch, seq, N_PAIRS * LANES)
    aug3 = (batch, seq, LANES)
    spec = pl.BlockSpec((1, seq, LANES), lambda b, g: (b, 0, g))
    aug_spec = pl.BlockSpec((1, seq, LANES), lambda b, g: (b, 0, 0))
    steps = batch * N_PAIRS
    w_specs = [pl.BlockSpec((w.shape[0] // steps, w.shape[1]), lambda b, g: (b * N_PAIRS + g, 0))
               for w in weights]
    return pl.pallas_call(
        functools.partial(_fox_kernel, seq=seq, n_cast=len(weights)),
        grid=(batch, N_PAIRS),
        in_specs=[spec, aug_spec, spec, aug_spec, spec] + w_specs,
        out_specs=[spec] + w_specs,
        out_shape=[jax.ShapeDtypeStruct(shape3, bf16)]
        + [jax.ShapeDtypeStruct(w.shape, bf16) for w in weights],
        scratch_shapes=[pltpu.VMEM((seq // t, 2, t, 2 * LANES), jnp.bfloat16),
                        pltpu.VMEM((2, 2, t, FOX_KEYS), jnp.float32),
                        pltpu.VMEM((2, t, LANES), jnp.float32),
                        pltpu.VMEM((2, t, LANES), jnp.float32)],
        compiler_params=pltpu.CompilerParams(
            dimension_semantics=("parallel", "parallel"),
            vmem_limit_bytes=VMEM_LIMIT),
        name="fox",
    )(fq.reshape(shape3), qa.reshape(aug3), fk.reshape(shape3), ka.reshape(aug3),
      fv.reshape(shape3), *weights)


def _t5_bucket(n):
    max_exact = N_BUCKETS // 2
    large = max_exact + (np.log(np.maximum(n, 1) / max_exact)
                         / np.log(MAX_DISTANCE / max_exact)
                         * (N_BUCKETS - max_exact)).astype(np.int32)
    large = np.minimum(large, N_BUCKETS - 1)
    return np.where(n < max_exact, n, large).astype(np.int32)


def _band_buckets():
    i = np.arange(Q_BLOCK)[:, None]
    j = np.arange(2 * Q_BLOCK)[None, :]
    dist = i + Q_BLOCK - j
    in_window = (dist >= 0) & (dist < WINDOW)
    return np.where(in_window, _t5_bucket(np.clip(dist, 0, None)), -1).astype(np.int32)


def _swa_kernel(bucket_ref, rel_ref, sink_ref, q_ref, k_ref, v_ref, o_ref, bias_ref, sinkc_ref,
                s_ref, *, seq):
    qb = Q_BLOCK
    qw = SWA_GROUP * HEAD_DIM

    @pl.when(pl.program_id(0) == 0)
    def _():
        bucket = bucket_ref[...]
        for h in range(N_SWA_HEADS):
            bias = jnp.full(bucket.shape, NEG_BIG, jnp.float32)
            for b in range(N_BUCKETS):
                bias = jnp.where(bucket == b, rel_ref[h, b] * LOG2E, bias)
            bias = jnp.where(lax.broadcasted_iota(jnp.int32, bucket.shape, 1) == 0,
                             sink_ref[h] * LOG2E, bias)
            group, g = divmod(h, SWA_GROUP)
            bias_ref[group, g * qb:(g + 1) * qb, :] = bias
            sinkc_ref[group, g * qb:(g + 1) * qb, :] = jnp.full((qb, LANES), sink_ref[h] * LOG2E)

    lower = lax.broadcasted_iota(jnp.int32, (qb, LANES), 1) < HEAD_DIM

    def band_of(n):
        if n == 0:
            return slice(0, qb), qb, slice(qb, 2 * qb)
        return slice((n - 1) * qb, (n + 1) * qb), 2 * qb, slice(0, 2 * qb)

    def scores(kv, n, slot):
        rows = slice(n * qb, (n + 1) * qb)
        band, width, cols = band_of(n)
        parts = []
        for pair in range(SWA_GROUP // 2):
            qp = q_ref[0, rows, kv * qw + pair * LANES:kv * qw + (pair + 1) * LANES]
            zero = jnp.zeros_like(qp)
            parts += [jnp.where(lower, qp, zero), jnp.where(lower, zero, qp)]
        s_ref[slot, :, :width] = (_dot_nt(jnp.concatenate(parts, axis=0), sink_slot(kv, n, k_ref))
                                  + bias_ref[kv, :, cols])

    def sink_slot(kv, n, ref):
        band, _, _ = band_of(n)
        x = ref[0, band, kv * LANES:(kv + 1) * LANES]
        if n == 0:
            return x
        return jnp.where(lax.broadcasted_iota(jnp.int32, x.shape, 0) == 0, jnp.zeros_like(x), x)

    def update(kv, n, slot):
        rows = slice(n * qb, (n + 1) * qb)
        _, width, _ = band_of(n)
        vb = sink_slot(kv, n, v_ref)
        vb = jnp.concatenate([vb, jnp.ones_like(vb)], axis=-1)
        s = s_ref[slot, :, :width]
        m = jnp.max(s, axis=-1, keepdims=True)
        if n == 0:
            sink = sinkc_ref[kv]
            m = jnp.maximum(m, sink)
        p = jnp.concatenate(
            [jnp.exp2(s[:, c * LANES:(c + 1) * LANES] - m) for c in range(width // LANES)],
            axis=-1).astype(jnp.bfloat16)
        acc = _dot(p, vb)
        denom = acc[:, LANES:] + jnp.exp2(sink - m) if n == 0 else acc[:, LANES:]
        out = acc[:, :LANES] / denom
        for pair in range(SWA_GROUP // 2):
            even = slice(2 * pair * qb, (2 * pair + 1) * qb)
            odd = slice((2 * pair + 1) * qb, (2 * pair + 2) * qb)
            o_ref[0, rows, kv * qw + pair * LANES:kv * qw + (pair + 1) * LANES] = jnp.where(
                lower, out[even], out[odd]).astype(o_ref.dtype)

    work = [(kv, n) for kv in range(N_SWA_KV_HEADS) for n in range(seq // qb)]
    scores(*work[0], 0)
    for idx, item in enumerate(work):
        if idx + 1 < len(work):
            scores(*work[idx + 1], (idx + 1) % 2)
        update(*item, idx % 2)


def _swa_call(sq, sk, sv, rel_bias, sinks, batch, seq):
    bucket = _band_buckets()
    smem = pl.BlockSpec(memory_space=pltpu.SMEM)
    q_spec = pl.BlockSpec((1, seq, D_SWA), lambda b: (b, 0, 0))
    kv_spec = pl.BlockSpec((1, seq, N_SWA_KV_HEADS * LANES), lambda b: (b, 0, 0))
    return pl.pallas_call(
        functools.partial(_swa_kernel, seq=seq),
        grid=(batch,),
        in_specs=[pl.BlockSpec(bucket.shape, lambda b: (0, 0)), smem, smem,
                  q_spec, kv_spec, kv_spec],
        out_specs=q_spec,
        out_shape=jax.ShapeDtypeStruct((batch, seq, D_SWA), jnp.bfloat16),
        scratch_shapes=[
            pltpu.VMEM((N_SWA_KV_HEADS, SWA_GROUP * Q_BLOCK, 2 * Q_BLOCK), jnp.float32),
            pltpu.VMEM((N_SWA_KV_HEADS, SWA_GROUP * Q_BLOCK, LANES), jnp.float32),
            pltpu.VMEM((2, SWA_GROUP * Q_BLOCK, 2 * Q_BLOCK), jnp.float32)],
        compiler_params=pltpu.CompilerParams(
            dimension_semantics=("arbitrary",),
            vmem_limit_bytes=VMEM_LIMIT),
        name="swa",
    )(jnp.asarray(bucket), rel_bias.T, sinks, sq.reshape(batch, seq, D_SWA),
      sk.reshape(batch, seq, 2 * LANES), sv.reshape(batch, seq, 2 * LANES))


def _post_kernel(fox_ref, swa_ref, x_ref, p_ref, wo_ref, w1_ref, w2_ref, wg_ref, wp_ref,
                 g_attn_ref, g_pre_ref, g_ff_ref, g_ple_ref, o_ref):
    bf16 = jnp.bfloat16
    h_parts, m_parts = [], []
    for r in range(0, POST_TM, POST_STRIP):
        rows = slice(r, r + POST_STRIP)
        mix = _dot(jnp.concatenate([fox_ref[rows, :], swa_ref[rows, :]], axis=-1), wo_ref[...])
        hh = x_ref[rows, :] + _rms(mix) * g_attn_ref[...]
        h_parts.append(hh)
        m_parts.append((_rms(hh) * g_pre_ref[...]).astype(bf16))
    h = jnp.concatenate(h_parts, axis=0)
    m = jnp.concatenate(m_parts, axis=0)
    ple = _dot(p_ref[...].astype(bf16), wp_ref[...])

    y = jnp.zeros((POST_TM, D_MODEL), jnp.float32)
    for c in range(D_FF // FF_CHUNK):
        cols = slice(c * FF_CHUNK, (c + 1) * FF_CHUNK)
        hid = jnp.square(jnp.maximum(_dot(m, w1_ref[:, cols]), 0.0))
        y = y + _dot(hid.astype(bf16), w2_ref[cols, :])

    for r in range(0, POST_TM, POST_STRIP):
        rows = slice(r, r + POST_STRIP)
        hh = h[rows] + _rms(y[rows]) * g_ff_ref[...]
        gate = jax.nn.sigmoid(_dot(hh.astype(bf16), wg_ref[...]))
        o_ref[rows, :] = hh + _rms(ple[rows] * gate) * g_ple_ref[...]


def _post_call(fox, swa, x2, p2, wo, w1, w2, wg, wp, gains):
    n = x2.shape[0]
    row = lambda i: (i, 0)
    const = lambda i: (0, 0)

    def resident(shape):
        return pl.BlockSpec(shape, const, pipeline_mode=pl.Buffered(1))

    return pl.pallas_call(
        _post_kernel,
        grid=(n // POST_TM,),
        in_specs=[
            pl.BlockSpec((POST_TM, D_FOX), row),
            pl.BlockSpec((POST_TM, D_SWA), row),
            pl.BlockSpec((POST_TM, D_MODEL), row),
            pl.BlockSpec((POST_TM, D_PLE), row),
            resident(wo.shape), resident(w1.shape), resident(w2.shape),
            resident(wg.shape), resident(wp.shape), *[resident(g.shape) for g in gains],
        ],
        out_specs=pl.BlockSpec((POST_TM, D_MODEL), row),
        out_shape=jax.ShapeDtypeStruct((n, D_MODEL), jnp.float32),
        compiler_params=pltpu.CompilerParams(
            dimension_semantics=("parallel",),
            vmem_limit_bytes=VMEM_LIMIT),
        name="post",
    )(fox, swa, x2, p2, wo, w1, w2, wg, wp, *gains)


def kernel(x, p, w_in, b_forget, w_out, rel_bias, swa_sinks, g_attn_pre, g_attn_post,
           w_ff1, w_ff2, g_ff_pre, g_ff_post, w_ple, w_ple_gate, g_ple_post):
    batch, seq, d_model = x.shape
    n = batch * seq
    assert d_model == D_MODEL and seq % PROJ_TM == 0 and seq % FOX_T == 0
    assert FOX_KEYS % FOX_T == 0 and seq % Q_BLOCK == 0 and n % POST_TM == 0
    h = x.reshape(n, D_MODEL)
    for i in range(p.shape[0]):
        fq, fk, fv, qa, ka, sq, sk, sv, wp = _proj_call(
            h, g_attn_pre[i].reshape(1, D_MODEL), w_in[i].T, b_forget[i], w_ple[i], batch, seq)
        fox, wo, w1, w2, wg = _fox_call(
            fq, qa, fk, ka, fv, [w_out[i], w_ff1[i], w_ff2[i], w_ple_gate[i]], batch, seq)
        swa = _swa_call(sq, sk, sv, rel_bias, swa_sinks[i], batch, seq).reshape(n, D_SWA)
        gains = [g[i].reshape(1, D_MODEL) for g in (g_attn_post, g_ff_pre, g_ff_post, g_ple_post)]
        h = _post_call(fox.reshape(n, D_FOX), swa, h, p[i].reshape(n, D_PLE),
                       wo, w1, w2, wg, wp, gains)
    return h.reshape(batch, seq, D_MODEL)
```

```python
import functools

import jax
import jax.numpy as jnp
import numpy as np
from jax import lax
from jax.experimental import pallas as pl
from jax.experimental.pallas import tpu as pltpu

D_MODEL = 1024
HEAD_DIM = 64
N_FOX_HEADS = 8
N_SWA_HEADS = 8
N_SWA_KV_HEADS = 2
SWA_GROUP = N_SWA_HEADS // N_SWA_KV_HEADS
D_FOX = N_FOX_HEADS * HEAD_DIM
D_SWA = N_SWA_HEADS * HEAD_DIM
D_SWA_KV = N_SWA_KV_HEADS * HEAD_DIM
D_FF = 4 * D_MODEL
D_PLE = 256
WINDOW = 128
Q_BLOCK = 128
N_BUCKETS = 32
MAX_DISTANCE = 128
RMS_EPS = 1e-6

LANES = 128
N_PAIRS = N_FOX_HEADS // 2
N_SPLIT = 3
N_TERMS = HEAD_DIM // N_FOX_HEADS
TERM_SRC = (0, 0, 0, 1, 1, 1, 2, 2)
TERM_CUT = (0, 1, 2, 0, 1, 2, 0, 1)
assert len(TERM_SRC) == len(TERM_CUT) == N_TERMS and 2 * N_TERMS * N_FOX_HEADS == LANES
NEG_BIG = -1e30
LOG2E = float(np.log2(np.e))

PROJ_TM = 1024
PROJ_STRIP = 512
FOX_T = 512
FOX_DIAG = 256
FOX_KEYS = 1536
POST_TM = 512
POST_STRIP = 256
FF_CHUNK = 2048
VMEM_LIMIT = 56 * 1024 * 1024

C_FQ, C_FK, C_FV = 0, D_FOX, 2 * D_FOX
C_SQ = 3 * D_FOX
C_SK = C_SQ + D_SWA
C_SV = C_SK + D_SWA_KV
C_FF = C_SV + D_SWA_KV
D_PROJ = C_FF + LANES


def _rms(v):
    return v * lax.rsqrt(jnp.mean(v * v, axis=-1, keepdims=True) + RMS_EPS)


def _dot(a, b):
    return jnp.dot(a, b, preferred_element_type=jnp.float32)


def _dot_nt(a, b):
    return lax.dot_general(a, b, (((1,), (1,)), ((), ())),
                           preferred_element_type=jnp.float32)


def _proj_kernel(x_ref, g_ref, wt_ref, bf_ref, tri_ref, wple_ref,
                 fq_ref, fk_ref, fv_ref, qa_ref, ka_ref, sq_ref, sk_ref, sv_ref, wple_bf16_ref,
                 carry_ref, w_ref):
    @pl.when((pl.program_id(0) == 0) & (pl.program_id(1) == 0))
    def _():
        bf16 = jnp.bfloat16
        q_scale = HEAD_DIM ** -0.5 * LOG2E
        src = np.cumsum([0, D_FOX, D_FOX, D_FOX, N_FOX_HEADS, D_SWA])
        src_fq, src_fk, src_fv, src_ff, src_sq, src_skv = (int(c) for c in src)
        w_ref[C_FQ:C_FQ + D_FOX] = (wt_ref[src_fq:src_fq + D_FOX] * q_scale).astype(bf16)
        w_ref[C_FK:C_FK + D_FOX] = wt_ref[src_fk:src_fk + D_FOX].astype(bf16)
        w_ref[C_FV:C_FV + D_FOX] = wt_ref[src_fv:src_fv + D_FOX].astype(bf16)
        w_ref[C_SQ:C_SQ + D_SWA] = (wt_ref[src_sq:src_sq + D_SWA] * q_scale).astype(bf16)
        w_ref[C_SK:C_SK + 2 * D_SWA_KV] = wt_ref[src_skv:src_skv + 2 * D_SWA_KV].astype(bf16)
        gate = wt_ref[src_ff:src_ff + N_FOX_HEADS]
        w_ref[C_FF:C_FF + LANES] = jnp.tile(gate, (LANES // N_FOX_HEADS, 1)).astype(bf16)
        wple_bf16_ref[...] = wple_ref[...].astype(bf16)

    @pl.when(pl.program_id(1) == 0)
    def _():
        carry_ref[...] = jnp.zeros_like(carry_ref)

    lane = lax.broadcasted_iota(jnp.int32, (PROJ_STRIP, LANES), 1)
    lower = lane < HEAD_DIM
    term = (lane // N_FOX_HEADS) % N_TERMS
    head_lane = lax.broadcasted_iota(jnp.int32, (1, LANES), 1) % N_FOX_HEADS
    gate_bias = jnp.zeros((1, LANES), jnp.float32)
    for head in range(N_FOX_HEADS):
        gate_bias = jnp.where(head_lane == head, bf_ref[head], gate_bias)

    for r in range(0, PROJ_TM, PROJ_STRIP):
        rows = slice(r, r + PROJ_STRIP)
        a = (_rms(x_ref[rows, :]) * g_ref[...]).astype(jnp.bfloat16)

        def proj(lo, width):
            return _dot_nt(a, w_ref[lo:lo + width])

        v = proj(C_FF, LANES) + gate_bias
        logf = (jnp.minimum(v, 0.0) - jnp.log1p(jnp.exp(-jnp.abs(v)))) * LOG2E
        x = _pick_piece(_split_bf16(logf), term, TERM_SRC)
        c = carry_ref[...] + _dot(tri_ref[...], x)
        carry_ref[...] = c[PROJ_STRIP - 1:PROJ_STRIP, :]
        terms = _pick_piece(_split_bf16(c), term, TERM_CUT)
        one = jnp.ones_like(terms)
        qa_ref[rows, :] = jnp.where(lower, terms, one)
        ka_ref[rows, :] = jnp.where(lower, one, -terms)

        kv = proj(C_SK, 2 * D_SWA_KV)
        for half, out_ref in enumerate((sk_ref, sv_ref)):
            z = kv[:, half * LANES:(half + 1) * LANES]
            zr = pltpu.roll(z, HEAD_DIM, 1)
            out_ref[rows, :LANES] = jnp.where(lower, z, zr).astype(jnp.bfloat16)
            out_ref[rows, LANES:] = jnp.where(lower, zr, z).astype(jnp.bfloat16)

        fq_ref[rows, :] = proj(C_FQ, D_FOX).astype(jnp.bfloat16)
        fk_ref[rows, :] = proj(C_FK, D_FOX).astype(jnp.bfloat16)
        fv_ref[rows, :] = proj(C_FV, D_FOX).astype(jnp.bfloat16)
        sq_ref[rows, :] = proj(C_SQ, D_SWA).astype(jnp.bfloat16)


def _split_bf16(v):
    pieces = []
    r = v
    for _ in range(N_SPLIT):
        t = r.astype(jnp.bfloat16)
        pieces.append(t)
        r = r - t.astype(jnp.float32)
    return pieces


def _pick_piece(pieces, term, piece_of_term):
    out = pieces[piece_of_term[-1]]
    for t in range(N_TERMS - 2, -1, -1):
        if piece_of_term[t] != piece_of_term[t + 1]:
            out = jnp.where(term <= t, pieces[piece_of_term[t]], out)
    return out


def _proj_call(x2, g_pre, w_in_t, b_forget, w_ple, batch, seq):
    n = batch * seq
    steps = seq // PROJ_TM
    tri = np.tril(np.ones((PROJ_STRIP, PROJ_STRIP), np.float32))
    row = lambda b, s: (b * steps + s, 0)
    const = lambda b, s: (0, 0)
    bf16 = jnp.bfloat16

    def out(width):
        return (jax.ShapeDtypeStruct((n, width), bf16), pl.BlockSpec((PROJ_TM, width), row))

    outs = [out(D_FOX), out(D_FOX), out(D_FOX), out(LANES), out(LANES),
            out(D_SWA), out(2 * LANES), out(2 * LANES),
            (jax.ShapeDtypeStruct(w_ple.shape, bf16), pl.BlockSpec(w_ple.shape, const))]
    return pl.pallas_call(
        _proj_kernel,
        grid=(batch, steps),
        in_specs=[
            pl.BlockSpec((PROJ_TM, D_MODEL), row),
            pl.BlockSpec((1, D_MODEL), const),
            pl.BlockSpec(w_in_t.shape, const, pipeline_mode=pl.Buffered(1)),
            pl.BlockSpec(memory_space=pltpu.SMEM),
            pl.BlockSpec((PROJ_STRIP, PROJ_STRIP), const),
            pl.BlockSpec(w_ple.shape, const, pipeline_mode=pl.Buffered(1)),
        ],
        out_specs=[o[1] for o in outs],
        out_shape=[o[0] for o in outs],
        scratch_shapes=[pltpu.VMEM((1, LANES), jnp.float32),
                        pltpu.VMEM((D_PROJ, D_MODEL), bf16)],
        compiler_params=pltpu.CompilerParams(
            dimension_semantics=("arbitrary", "arbitrary"),
            vmem_limit_bytes=VMEM_LIMIT),
        name="proj",
    )(x2, g_pre, w_in_t, b_forget, jnp.asarray(tri, bf16), w_ple)


def _fox_kernel(*refs, seq, n_cast):
    q_ref, qa_ref, k_ref, ka_ref, v_ref = refs[:5]
    w_refs = refs[5:5 + n_cast]
    o_ref = refs[5 + n_cast]
    wb_refs = refs[6 + n_cast:6 + 2 * n_cast]
    qm_ref, s_ref, m_ref, acc_ref = refs[6 + 2 * n_cast:]
    for w_ref, wb_ref in zip(w_refs, wb_refs):
        wb_ref[...] = w_ref[...].astype(wb_ref.dtype)

    t = FOX_T
    lane = lax.broadcasted_iota(jnp.int32, (t, LANES), 1)
    lower = lane < HEAD_DIM
    head_of_lane = lane % N_FOX_HEADS
    d = FOX_DIAG
    causal = (lax.broadcasted_iota(jnp.int32, (d, d), 0)
              >= lax.broadcasted_iota(jnp.int32, (d, d), 1))

    def tile_rows(i):
        return slice(i * t, (i + 1) * t)

    def parts(i, k0, nk):
        if k0 + nk < (i + 1) * t:
            return [(0, t, nk)]
        return [(r, r + d, nk - t + r + d) for r in range(0, t, d)]

    def mask_queries(i):
        q = q_ref[0, tile_rows(i), :]
        qa = qa_ref[0, tile_rows(i), :]
        zero = jnp.zeros_like(q)
        first_head = 2 * pl.program_id(1)
        for e in range(2):
            qm_ref[i, e] = jnp.concatenate(
                [jnp.where(lower, q, zero) if e == 0 else jnp.where(lower, zero, q),
                 jnp.where(head_of_lane == first_head + e, qa, zero)], axis=-1)

    def scores(i, k0, nk, slot):
        for r0, r1, n in parts(i, k0, nk):
            keys = slice(k0, k0 + n)
            kk = jnp.concatenate([k_ref[0, keys, :], ka_ref[0, keys, :]], axis=-1)
            for e in range(2):
                s_ref[slot, e, r0:r1, :n] = _dot_nt(qm_ref[i, e, r0:r1], kk)

    def update(i, k0, nk, slot):
        diag = k0 + nk == (i + 1) * t
        for r0, r1, n in parts(i, k0, nk):
            v = v_ref[0, k0:k0 + n, :]
            one = jnp.ones_like(v)
            for e in range(2):
                s = s_ref[slot, e, r0:r1, :n]
                if diag:
                    last = jnp.where(causal, s[:, n - d:], NEG_BIG)
                    s = last if n == d else jnp.concatenate([s[:, :n - d], last], axis=-1)
                row_max = jnp.max(s, axis=-1, keepdims=True)
                m_new = (jnp.broadcast_to(row_max, (r1 - r0, LANES)) if k0 == 0
                         else jnp.maximum(m_ref[e, r0:r1], row_max))
                p = jnp.concatenate(
                    [jnp.exp2(s[:, c * LANES:(c + 1) * LANES] - m_new)
                     for c in range(n // LANES)], axis=-1).astype(jnp.bfloat16)
                v_lower = lax.broadcasted_iota(jnp.int32, v.shape, 1) < HEAD_DIM
                ve = jnp.where(v_lower, v, one) if e == 0 else jnp.where(v_lower, one, v)
                pv = _dot(p, ve)
                if k0 > 0:
                    pv = jnp.exp2(m_ref[e, r0:r1] - m_new) * acc_ref[e, r0:r1] + pv
                acc_ref[e, r0:r1] = pv
                m_ref[e, r0:r1] = m_new

    def finish(i):
        outs = []
        for e in range(2):
            acc = acc_ref[e]
            outs.append(acc / pltpu.roll(acc, HEAD_DIM, 1))
        o_ref[0, tile_rows(i), :] = jnp.where(lower, outs[0], outs[1]).astype(o_ref.dtype)

    work = [(i, k0, min(FOX_KEYS, (i + 1) * t - k0))
            for i in range(seq // t) for k0 in range(0, (i + 1) * t, FOX_KEYS)]

    def issue_scores(idx):
        i, k0, nk = work[idx]
        if k0 == 0:
            mask_queries(i)
        scores(i, k0, nk, idx % 2)

    issue_scores(0)
    for idx, (i, k0, nk) in enumerate(work):
        if idx + 1 < len(work):
            issue_scores(idx + 1)
        update(i, k0, nk, idx % 2)
        if k0 + nk == (i + 1) * t:
            finish(i)


def _fox_call(fq, qa, fk, ka, fv, weights, batch, seq):
    t = FOX_T
    bf16 = jnp.bfloat16
    shape3 = (batch, seq, N_PAIRS * LANES)
    aug3 = (batch, seq, LANES)
    spec = pl.BlockSpec((1, seq, LANES), lambda b, g: (b, 0, g))
    aug_spec = pl.BlockSpec((1, seq, LANES), lambda b, g: (b, 0, 0))
    steps = batch * N_PAIRS
    w_specs = [pl.BlockSpec((w.shape[0] // steps, w.shape[1]), lambda b, g: (b * N_PAIRS + g, 0))
               for w in weights]
    return pl.pallas_call(
        functools.partial(_fox_kernel, seq=seq, n_cast=len(weights)),
        grid=(batch, N_PAIRS),
        in_specs=[spec, aug_spec, spec, aug_spec, spec] + w_specs,
        out_specs=[spec] + w_specs,
        out_shape=[jax.ShapeDtypeStruct(shape3, bf16)]
        + [jax.ShapeDtypeStruct(w.shape, bf16) for w in weights],
        scratch_shapes=[pltpu.VMEM((seq // t, 2, t, 2 * LANES), jnp.bfloat16),
                        pltpu.VMEM((2, 2, t, FOX_KEYS), jnp.float32),
                        pltpu.VMEM((2, t, LANES), jnp.float32),
                        pltpu.VMEM((2, t, LANES), jnp.float32)],
        compiler_params=pltpu.CompilerParams(
            dimension_semantics=("parallel", "parallel"),
            vmem_limit_bytes=VMEM_LIMIT),
        name="fox",
    )(fq.reshape(shape3), qa.reshape(aug3), fk.reshape(shape3), ka.reshape(aug3),
      fv.reshape(shape3), *weights)


def _t5_bucket(n):
    max_exact = N_BUCKETS // 2
    large = max_exact + (np.log(np.maximum(n, 1) / max_exact)
                         / np.log(MAX_DISTANCE / max_exact)
                         * (N_BUCKETS - max_exact)).astype(np.int32)
    large = np.minimum(large, N_BUCKETS - 1)
    return np.where(n < max_exact, n, large).astype(np.int32)


def _band_buckets():
    i = np.arange(Q_BLOCK)[:, None]
    j = np.arange(2 * Q_BLOCK)[None, :]
    dist = i + Q_BLOCK - j
    in_window = (dist >= 0) & (dist < WINDOW)
    return np.where(in_window, _t5_bucket(np.clip(dist, 0, None)), -1).astype(np.int32)


def _swa_kernel(bucket_ref, rel_ref, sink_ref, q_ref, k_ref, v_ref, o_ref, bias_ref, sinkc_ref,
                s_ref, *, seq):
    qb = Q_BLOCK
    qw = SWA_GROUP * HEAD_DIM

    @pl.when(pl.program_id(0) == 0)
    def _():
        bucket = bucket_ref[...]
        for h in range(N_SWA_HEADS):
            bias = jnp.full(bucket.shape, NEG_BIG, jnp.float32)
            for b in range(N_BUCKETS):
                bias = jnp.where(bucket == b, rel_ref[h, b] * LOG2E, bias)
            bias = jnp.where(lax.broadcasted_iota(jnp.int32, bucket.shape, 1) == 0,
                             sink_ref[h] * LOG2E, bias)
            group, g = divmod(h, SWA_GROUP)
            bias_ref[group, g * qb:(g + 1) * qb, :] = bias
            sinkc_ref[group, g * qb:(g + 1) * qb, :] = jnp.full((qb, LANES), sink_ref[h] * LOG2E)

    lower = lax.broadcasted_iota(jnp.int32, (qb, LANES), 1) < HEAD_DIM

    def band_of(n):
        if n == 0:
            return slice(0, qb), qb, slice(qb, 2 * qb)
        return slice((n - 1) * qb, (n + 1) * qb), 2 * qb, slice(0, 2 * qb)

    def scores(kv, n, slot):
        rows = slice(n * qb, (n + 1) * qb)
        band, width, cols = band_of(n)
        parts = []
        for pair in range(SWA_GROUP // 2):
            qp = q_ref[0, rows, kv * qw + pair * LANES:kv * qw + (pair + 1) * LANES]
            zero = jnp.zeros_like(qp)
            parts += [jnp.where(lower, qp, zero), jnp.where(lower, zero, qp)]
        s_ref[slot, :, :width] = (_dot_nt(jnp.concatenate(parts, axis=0), sink_slot(kv, n, k_ref))
                                  + bias_ref[kv, :, cols])

    def sink_slot(kv, n, ref):
        band, _, _ = band_of(n)
        x = ref[0, band, kv * LANES:(kv + 1) * LANES]
        if n == 0:
            return x
        return jnp.where(lax.broadcasted_iota(jnp.int32, x.shape, 0) == 0, jnp.zeros_like(x), x)

    def update(kv, n, slot):
        rows = slice(n * qb, (n + 1) * qb)
        _, width, _ = band_of(n)
        vb = sink_slot(kv, n, v_ref)
        vb = jnp.concatenate([vb, jnp.ones_like(vb)], axis=-1)
        s = s_ref[slot, :, :width]
        m = jnp.max(s, axis=-1, keepdims=True)
        if n == 0:
            sink = sinkc_ref[kv]
            m = jnp.maximum(m, sink)
        p = jnp.concatenate(
            [jnp.exp2(s[:, c * LANES:(c + 1) * LANES] - m) for c in range(width // LANES)],
            axis=-1).astype(jnp.bfloat16)
        acc = _dot(p, vb)
        denom = acc[:, LANES:] + jnp.exp2(sink - m) if n == 0 else acc[:, LANES:]
        out = acc[:, :LANES] / denom
        for pair in range(SWA_GROUP // 2):
            even = slice(2 * pair * qb, (2 * pair + 1) * qb)
            odd = slice((2 * pair + 1) * qb, (2 * pair + 2) * qb)
            o_ref[0, rows, kv * qw + pair * LANES:kv * qw + (pair + 1) * LANES] = jnp.where(
                lower, out[even], out[odd]).astype(o_ref.dtype)

    work = [(kv, n) for kv in range(N_SWA_KV_HEADS) for n in range(seq // qb)]
    scores(*work[0], 0)
    for idx, item in enumerate(work):
        if idx + 1 < len(work):
            scores(*work[idx + 1], (idx + 1) % 2)
        update(*item, idx % 2)


def _swa_call(sq, sk, sv, rel_bias, sinks, batch, seq):
    bucket = _band_buckets()
    smem = pl.BlockSpec(memory_space=pltpu.SMEM)
    q_spec = pl.BlockSpec((1, seq, D_SWA), lambda b: (b, 0, 0))
    kv_spec = pl.BlockSpec((1, seq, N_SWA_KV_HEADS * LANES), lambda b: (b, 0, 0))
    return pl.pallas_call(
        functools.partial(_swa_kernel, seq=seq),
        grid=(batch,),
        in_specs=[pl.BlockSpec(bucket.shape, lambda b: (0, 0)), smem, smem,
                  q_spec, kv_spec, kv_spec],
        out_specs=q_spec,
        out_shape=jax.ShapeDtypeStruct((batch, seq, D_SWA), jnp.bfloat16),
        scratch_shapes=[
            pltpu.VMEM((N_SWA_KV_HEADS, SWA_GROUP * Q_BLOCK, 2 * Q_BLOCK), jnp.float32),
            pltpu.VMEM((N_SWA_KV_HEADS, SWA_GROUP * Q_BLOCK, LANES), jnp.float32),
            pltpu.VMEM((2, SWA_GROUP * Q_BLOCK, 2 * Q_BLOCK), jnp.float32)],
        compiler_params=pltpu.CompilerParams(
            dimension_semantics=("arbitrary",),
            vmem_limit_bytes=VMEM_LIMIT),
        name="swa",
    )(jnp.asarray(bucket), rel_bias.T, sinks, sq.reshape(batch, seq, D_SWA),
      sk.reshape(batch, seq, 2 * LANES), sv.reshape(batch, seq, 2 * LANES))


def _post_kernel(fox_ref, swa_ref, x_ref, p_ref, wo_ref, w1_ref, w2_ref, wg_ref, wp_ref,
                 g_attn_ref, g_pre_ref, g_ff_ref, g_ple_ref, o_ref):
    bf16 = jnp.bfloat16
    h_parts, m_parts = [], []
    for r in range(0, POST_TM, POST_STRIP):
        rows = slice(r, r + POST_STRIP)
        mix = _dot(jnp.concatenate([fox_ref[rows, :], swa_ref[rows, :]], axis=-1), wo_ref[...])
        hh = x_ref[rows, :] + _rms(mix) * g_attn_ref[...]
        h_parts.append(hh)
        m_parts.append((_rms(hh) * g_pre_ref[...]).astype(bf16))
    h = jnp.concatenate(h_parts, axis=0)
    m = jnp.concatenate(m_parts, axis=0)
    ple = _dot(p_ref[...].astype(bf16), wp_ref[...])

    y = jnp.zeros((POST_TM, D_MODEL), jnp.float32)
    for c in range(D_FF // FF_CHUNK):
        cols = slice(c * FF_CHUNK, (c + 1) * FF_CHUNK)
        hid = jnp.square(jnp.maximum(_dot(m, w1_ref[:, cols]), 0.0))
        y = y + _dot(hid.astype(bf16), w2_ref[cols, :])

    for r in range(0, POST_TM, POST_STRIP):
        rows = slice(r, r + POST_STRIP)
        hh = h[rows] + _rms(y[rows]) * g_ff_ref[...]
        gate = jax.nn.sigmoid(_dot(hh.astype(bf16), wg_ref[...]))
        o_ref[rows, :] = hh + _rms(ple[rows] * gate) * g_ple_ref[...]


def _post_call(fox, swa, x2, p2, wo, w1, w2, wg, wp, gains):
    n = x2.shape[0]
    row = lambda i: (i, 0)
    const = lambda i: (0, 0)

    def resident(shape):
        return pl.BlockSpec(shape, const, pipeline_mode=pl.Buffered(1))

    return pl.pallas_call(
        _post_kernel,
        grid=(n // POST_TM,),
        in_specs=[
            pl.BlockSpec((POST_TM, D_FOX), row),
            pl.BlockSpec((POST_TM, D_SWA), row),
            pl.BlockSpec((POST_TM, D_MODEL), row),
            pl.BlockSpec((POST_TM, D_PLE), row),
            resident(wo.shape), resident(w1.shape), resident(w2.shape),
            resident(wg.shape), resident(wp.shape), *[resident(g.shape) for g in gains],
        ],
        out_specs=pl.BlockSpec((POST_TM, D_MODEL), row),
        out_shape=jax.ShapeDtypeStruct((n, D_MODEL), jnp.float32),
        compiler_params=pltpu.CompilerParams(
            dimension_semantics=("parallel",),
            vmem_limit_bytes=VMEM_LIMIT),
        name="post",
    )(fox, swa, x2, p2, wo, w1, w2, wg, wp, *gains)


def kernel(x, p, w_in, b_forget, w_out, rel_bias, swa_sinks, g_attn_pre, g_attn_post,
           w_ff1, w_ff2, g_ff_pre, g_ff_post, w_ple, w_ple_gate, g_ple_post):
    batch, seq, d_model = x.shape
    n = batch * seq
    assert d_model == D_MODEL and seq % PROJ_TM == 0 and seq % FOX_T == 0
    assert FOX_KEYS % FOX_T == 0 and seq % Q_BLOCK == 0 and n % POST_TM == 0
    h = x.reshape(n, D_MODEL)
    for i in range(p.shape[0]):
        fq, fk, fv, qa, ka, sq, sk, sv, wp = _proj_call(
            h, g_attn_pre[i].reshape(1, D_MODEL), w_in[i].T, b_forget[i], w_ple[i], batch, seq)
        fox, wo, w1, w2, wg = _fox_call(
            fq, qa, fk, ka, fv, [w_out[i], w_ff1[i], w_ff2[i], w_ple_gate[i]], batch, seq)
        swa = _swa_call(sq, sk, sv, rel_bias, swa_sinks[i], batch, seq).reshape(n, D_SWA)
        gains = [g[i].reshape(1, D_MODEL) for g in (g_attn_post, g_ff_pre, g_ff_post, g_ple_post)]
        h = _post_call(fox.reshape(n, D_FOX), swa, h, p[i].reshape(n, D_PLE),
                       wo, w1, w2, wg, wp, gains)
    return h.reshape(batch, seq, D_MODEL)
```

```python
import functools

import jax
import jax.numpy as jnp
import numpy as np
from jax import lax
from jax.experimental import pallas as pl
from jax.experimental.pallas import tpu as pltpu

D_MODEL = 1024
HEAD_DIM = 64
N_FOX_HEADS = 8
N_SWA_HEADS = 8
N_SWA_KV_HEADS = 2
SWA_GROUP = N_SWA_HEADS // N_SWA_KV_HEADS
D_FOX = N_FOX_HEADS * HEAD_DIM
D_SWA = N_SWA_HEADS * HEAD_DIM
D_SWA_KV = N_SWA_KV_HEADS * HEAD_DIM
D_FF = 4 * D_MODEL
D_PLE = 256
WINDOW = 128
Q_BLOCK = 128
N_BUCKETS = 32
MAX_DISTANCE = 128
RMS_EPS = 1e-6

LANES = 128
N_PAIRS = N_FOX_HEADS // 2
N_SPLIT = 3
N_TERMS = HEAD_DIM // N_FOX_HEADS
TERM_SRC = (0, 0, 0, 1, 1, 1, 2, 2)
TERM_CUT = (0, 1, 2, 0, 1, 2, 0, 1)
assert len(TERM_SRC) == len(TERM_CUT) == N_TERMS and 2 * N_TERMS * N_FOX_HEADS == LANES
NEG_BIG = -1e30
LOG2E = float(np.log2(np.e))

PROJ_TM = 1024
PROJ_STRIP = 512
FOX_T = 512
FOX_DIAG = 256
FOX_KEYS = 1536
POST_TM = 512
POST_STRIP = 256
FF_CHUNK = 4096
VMEM_LIMIT = 56 * 1024 * 1024

C_FQ, C_FK, C_FV = 0, D_FOX, 2 * D_FOX
C_SQ = 3 * D_FOX
C_SK = C_SQ + D_SWA
C_SV = C_SK + D_SWA_KV
C_FF = C_SV + D_SWA_KV
D_PROJ = C_FF + LANES


def _rms(v):
    return v * lax.rsqrt(jnp.mean(v * v, axis=-1, keepdims=True) + RMS_EPS)


def _dot(a, b):
    return jnp.dot(a, b, preferred_element_type=jnp.float32)


def _dot_nt(a, b):
    return lax.dot_general(a, b, (((1,), (1,)), ((), ())),
                           preferred_element_type=jnp.float32)


def _proj_kernel(x_ref, g_ref, wt_ref, bf_ref, tri_ref, wple_ref,
                 fq_ref, fk_ref, fv_ref, qa_ref, ka_ref, sq_ref, sk_ref, sv_ref, wple_bf16_ref,
                 carry_ref, w_ref):
    @pl.when((pl.program_id(0) == 0) & (pl.program_id(1) == 0))
    def _():
        bf16 = jnp.bfloat16
        q_scale = HEAD_DIM ** -0.5 * LOG2E
        src = np.cumsum([0, D_FOX, D_FOX, D_FOX, N_FOX_HEADS, D_SWA])
        src_fq, src_fk, src_fv, src_ff, src_sq, src_skv = (int(c) for c in src)
        w_ref[C_FQ:C_FQ + D_FOX] = (wt_ref[src_fq:src_fq + D_FOX] * q_scale).astype(bf16)
        w_ref[C_FK:C_FK + D_FOX] = wt_ref[src_fk:src_fk + D_FOX].astype(bf16)
        w_ref[C_FV:C_FV + D_FOX] = wt_ref[src_fv:src_fv + D_FOX].astype(bf16)
        w_ref[C_SQ:C_SQ + D_SWA] = (wt_ref[src_sq:src_sq + D_SWA] * q_scale).astype(bf16)
        w_ref[C_SK:C_SK + 2 * D_SWA_KV] = wt_ref[src_skv:src_skv + 2 * D_SWA_KV].astype(bf16)
        gate = wt_ref[src_ff:src_ff + N_FOX_HEADS]
        w_ref[C_FF:C_FF + LANES] = jnp.tile(gate, (LANES // N_FOX_HEADS, 1)).astype(bf16)
        wple_bf16_ref[...] = wple_ref[...].astype(bf16)

    @pl.when(pl.program_id(1) == 0)
    def _():
        carry_ref[...] = jnp.zeros_like(carry_ref)

    lane = lax.broadcasted_iota(jnp.int32, (PROJ_STRIP, LANES), 1)
    lower = lane < HEAD_DIM
    term = (lane // N_FOX_HEADS) % N_TERMS
    head_lane = lax.broadcasted_iota(jnp.int32, (1, LANES), 1) % N_FOX_HEADS
    gate_bias = jnp.zeros((1, LANES), jnp.float32)
    for head in range(N_FOX_HEADS):
        gate_bias = jnp.where(head_lane == head, bf_ref[head], gate_bias)

    for r in range(0, PROJ_TM, PROJ_STRIP):
        rows = slice(r, r + PROJ_STRIP)
        a = (_rms(x_ref[rows, :]) * g_ref[...]).astype(jnp.bfloat16)

        def proj(lo, width):
            return _dot_nt(a, w_ref[lo:lo + width])

        v = proj(C_FF, LANES) + gate_bias
        logf = (jnp.minimum(v, 0.0) - jnp.log1p(jnp.exp(-jnp.abs(v)))) * LOG2E
        x = _pick_piece(_split_bf16(logf), term, TERM_SRC)
        c = carry_ref[...] + _dot(tri_ref[...], x)
        carry_ref[...] = c[PROJ_STRIP - 1:PROJ_STRIP, :]
        terms = _pick_piece(_split_bf16(c), term, TERM_CUT)
        one = jnp.ones_like(terms)
        qa_ref[rows, :] = jnp.where(lower, terms, one)
        ka_ref[rows, :] = jnp.where(lower, one, -terms)

        kv = proj(C_SK, 2 * D_SWA_KV)
        for half, out_ref in enumerate((sk_ref, sv_ref)):
            z = kv[:, half * LANES:(half + 1) * LANES]
            zr = pltpu.roll(z, HEAD_DIM, 1)
            out_ref[rows, :LANES] = jnp.where(lower, z, zr).astype(jnp.bfloat16)
            out_ref[rows, LANES:] = jnp.where(lower, zr, z).astype(jnp.bfloat16)

        fq_ref[rows, :] = proj(C_FQ, D_FOX).astype(jnp.bfloat16)
        fk_ref[rows, :] = proj(C_FK, D_FOX).astype(jnp.bfloat16)
        fv_ref[rows, :] = proj(C_FV, D_FOX).astype(jnp.bfloat16)
        sq_ref[rows, :] = proj(C_SQ, D_SWA).astype(jnp.bfloat16)


def _split_bf16(v):
    pieces = []
    r = v
    for _ in range(N_SPLIT):
        t = r.astype(jnp.bfloat16)
        pieces.append(t)
        r = r - t.astype(jnp.float32)
    return pieces


def _pick_piece(pieces, term, piece_of_term):
    out = pieces[piece_of_term[-1]]
    for t in range(N_TERMS - 2, -1, -1):
        if piece_of_term[t] != piece_of_term[t + 1]:
            out = jnp.where(term <= t, pieces[piece_of_term[t]], out)
    return out


def _proj_call(x2, g_pre, w_in_t, b_forget, w_ple, batch, seq):
    n = batch * seq
    steps = seq // PROJ_TM
    tri = np.tril(np.ones((PROJ_STRIP, PROJ_STRIP), np.float32))
    row = lambda b, s: (b * steps + s, 0)
    const = lambda b, s: (0, 0)
    bf16 = jnp.bfloat16

    def out(width):
        return (jax.ShapeDtypeStruct((n, width), bf16), pl.BlockSpec((PROJ_TM, width), row))

    outs = [out(D_FOX), out(D_FOX), out(D_FOX), out(LANES), out(LANES),
            out(D_SWA), out(2 * LANES), out(2 * LANES),
            (jax.ShapeDtypeStruct(w_ple.shape, bf16), pl.BlockSpec(w_ple.shape, const))]
    return pl.pallas_call(
        _proj_kernel,
        grid=(batch, steps),
        in_specs=[
            pl.BlockSpec((PROJ_TM, D_MODEL), row),
            pl.BlockSpec((1, D_MODEL), const),
            pl.BlockSpec(w_in_t.shape, const, pipeline_mode=pl.Buffered(1)),
            pl.BlockSpec(memory_space=pltpu.SMEM),
            pl.BlockSpec((PROJ_STRIP, PROJ_STRIP), const),
            pl.BlockSpec(w_ple.shape, const, pipeline_mode=pl.Buffered(1)),
        ],
        out_specs=[o[1] for o in outs],
        out_shape=[o[0] for o in outs],
        scratch_shapes=[pltpu.VMEM((1, LANES), jnp.float32),
                        pltpu.VMEM((D_PROJ, D_MODEL), bf16)],
        compiler_params=pltpu.CompilerParams(
            dimension_semantics=("arbitrary", "arbitrary"),
            vmem_limit_bytes=VMEM_LIMIT),
        name="proj",
    )(x2, g_pre, w_in_t, b_forget, jnp.asarray(tri, bf16), w_ple)


def _fox_kernel(*refs, seq, n_cast):
    q_ref, qa_ref, k_ref, ka_ref, v_ref = refs[:5]
    w_refs = refs[5:5 + n_cast]
    o_ref = refs[5 + n_cast]
    wb_refs = refs[6 + n_cast:6 + 2 * n_cast]
    qm_ref, s_ref, m_ref, acc_ref = refs[6 + 2 * n_cast:]
    for w_ref, wb_ref in zip(w_refs, wb_refs):
        wb_ref[...] = w_ref[...].astype(wb_ref.dtype)

    t = FOX_T
    lane = lax.broadcasted_iota(jnp.int32, (t, LANES), 1)
    lower = lane < HEAD_DIM
    head_of_lane = lane % N_FOX_HEADS
    d = FOX_DIAG
    causal = (lax.broadcasted_iota(jnp.int32, (d, d), 0)
              >= lax.broadcasted_iota(jnp.int32, (d, d), 1))

    def tile_rows(i):
        return slice(i * t, (i + 1) * t)

    def parts(i, k0, nk):
        if k0 + nk < (i + 1) * t:
            return [(0, t, nk)]
        return [(r, r + d, nk - t + r + d) for r in range(0, t, d)]

    def mask_queries(i):
        q = q_ref[0, tile_rows(i), :]
        qa = qa_ref[0, tile_rows(i), :]
        zero = jnp.zeros_like(q)
        first_head = 2 * pl.program_id(1)
        for e in range(2):
            qm_ref[i, e] = jnp.concatenate(
                [jnp.where(lower, q, zero) if e == 0 else jnp.where(lower, zero, q),
                 jnp.where(head_of_lane == first_head + e, qa, zero)], axis=-1)

    def scores(i, k0, nk, slot):
        for r0, r1, n in parts(i, k0, nk):
            keys = slice(k0, k0 + n)
            kk = jnp.concatenate([k_ref[0, keys, :], ka_ref[0, keys, :]], axis=-1)
            for e in range(2):
                s_ref[slot, e, r0:r1, :n] = _dot_nt(qm_ref[i, e, r0:r1], kk)

    def update(i, k0, nk, slot):
        diag = k0 + nk == (i + 1) * t
        for r0, r1, n in parts(i, k0, nk):
            v = v_ref[0, k0:k0 + n, :]
            one = jnp.ones_like(v)
            for e in range(2):
                s = s_ref[slot, e, r0:r1, :n]
                if diag:
                    last = jnp.where(causal, s[:, n - d:], NEG_BIG)
                    s = last if n == d else jnp.concatenate([s[:, :n - d], last], axis=-1)
                row_max = jnp.max(s, axis=-1, keepdims=True)
                m_new = (jnp.broadcast_to(row_max, (r1 - r0, LANES)) if k0 == 0
                         else jnp.maximum(m_ref[e, r0:r1], row_max))
                p = jnp.concatenate(
                    [jnp.exp2(s[:, c * LANES:(c + 1) * LANES] - m_new)
                     for c in range(n // LANES)], axis=-1).astype(jnp.bfloat16)
                v_lower = lax.broadcasted_iota(jnp.int32, v.shape, 1) < HEAD_DIM
                ve = jnp.where(v_lower, v, one) if e == 0 else jnp.where(v_lower, one, v)
                pv = _dot(p, ve)
                if k0 > 0:
                    pv = jnp.exp2(m_ref[e, r0:r1] - m_new) * acc_ref[e, r0:r1] + pv
                acc_ref[e, r0:r1] = pv
                m_ref[e, r0:r1] = m_new

    def finish(i):
        outs = []
        for e in range(2):
            acc = acc_ref[e]
            outs.append(acc / pltpu.roll(acc, HEAD_DIM, 1))
        o_ref[0, tile_rows(i), :] = jnp.where(lower, outs[0], outs[1]).astype(o_ref.dtype)

    work = [(i, k0, min(FOX_KEYS, (i + 1) * t - k0))
            for i in range(seq // t) for k0 in range(0, (i + 1) * t, FOX_KEYS)]

    def issue_scores(idx):
        i, k0, nk = work[idx]
        if k0 == 0:
            mask_queries(i)
        scores(i, k0, nk, idx % 2)

    issue_scores(0)
    for idx, (i, k0, nk) in enumerate(work):
        if idx + 1 < len(work):
            issue_scores(idx + 1)
        update(i, k0, nk, idx % 2)
        if k0 + nk == (i + 1) * t:
            finish(i)


def _fox_call(fq, qa, fk, ka, fv, weights, batch, seq):
    t = FOX_T
    bf16 = jnp.bfloat16
    shape3 = (batch, seq, N_PAIRS * LANES)
    aug3 = (batch, seq, LANES)
    spec = pl.BlockSpec((1, seq, LANES), lambda b, g: (b, 0, g))
    aug_spec = pl.BlockSpec((1, seq, LANES), lambda b, g: (b, 0, 0))
    steps = batch * N_PAIRS
    w_specs = [pl.BlockSpec((w.shape[0] // steps, w.shape[1]), lambda b, g: (b * N_PAIRS + g, 0))
               for w in weights]
    return pl.pallas_call(
        functools.partial(_fox_kernel, seq=seq, n_cast=len(weights)),
        grid=(batch, N_PAIRS),
        in_specs=[spec, aug_spec, spec, aug_spec, spec] + w_specs,
        out_specs=[spec] + w_specs,
        out_shape=[jax.ShapeDtypeStruct(shape3, bf16)]
        + [jax.ShapeDtypeStruct(w.shape, bf16) for w in weights],
        scratch_shapes=[pltpu.VMEM((seq // t, 2, t, 2 * LANES), jnp.bfloat16),
                        pltpu.VMEM((2, 2, t, FOX_KEYS), jnp.float32),
                        pltpu.VMEM((2, t, LANES), jnp.float32),
                        pltpu.VMEM((2, t, LANES), jnp.float32)],
        compiler_params=pltpu.CompilerParams(
            dimension_semantics=("parallel", "parallel"),
            vmem_limit_bytes=VMEM_LIMIT),
        name="fox",
    )(fq.reshape(shape3), qa.reshape(aug3), fk.reshape(shape3), ka.reshape(aug3),
      fv.reshape(shape3), *weights)


def _t5_bucket(n):
    max_exact = N_BUCKETS // 2
    large = max_exact + (np.log(np.maximum(n, 1) / max_exact)
                         / np.log(MAX_DISTANCE / max_exact)
                         * (N_BUCKETS - max_exact)).astype(np.int32)
    large = np.minimum(large, N_BUCKETS - 1)
    return np.where(n < max_exact, n, large).astype(np.int32)


def _band_buckets():
    i = np.arange(Q_BLOCK)[:, None]
    j = np.arange(2 * Q_BLOCK)[None, :]
    dist = i + Q_BLOCK - j
    in_window = (dist >= 0) & (dist < WINDOW)
    return np.where(in_window, _t5_bucket(np.clip(dist, 0, None)), -1).astype(np.int32)


def _swa_kernel(bucket_ref, rel_ref, sink_ref, q_ref, k_ref, v_ref, o_ref, bias_ref, sinkc_ref,
                s_ref, *, seq):
    qb = Q_BLOCK
    qw = SWA_GROUP * HEAD_DIM

    @pl.when(pl.program_id(0) == 0)
    def _():
        bucket = bucket_ref[...]
        for h in range(N_SWA_HEADS):
            bias = jnp.full(bucket.shape, NEG_BIG, jnp.float32)
            for b in range(N_BUCKETS):
                bias = jnp.where(bucket == b, rel_ref[h, b] * LOG2E, bias)
            bias = jnp.where(lax.broadcasted_iota(jnp.int32, bucket.shape, 1) == 0,
                             sink_ref[h] * LOG2E, bias)
            group, g = divmod(h, SWA_GROUP)
            bias_ref[group, g * qb:(g + 1) * qb, :] = bias
            sinkc_ref[group, g * qb:(g + 1) * qb, :] = jnp.full((qb, LANES), sink_ref[h] * LOG2E)

    lower = lax.broadcasted_iota(jnp.int32, (qb, LANES), 1) < HEAD_DIM

    def band_of(n):
        if n == 0:
            return slice(0, qb), qb, slice(qb, 2 * qb)
        return slice((n - 1) * qb, (n + 1) * qb), 2 * qb, slice(0, 2 * qb)

    def scores(kv, n, slot):
        rows = slice(n * qb, (n + 1) * qb)
        band, width, cols = band_of(n)
        parts = []
        for pair in range(SWA_GROUP // 2):
            qp = q_ref[0, rows, kv * qw + pair * LANES:kv * qw + (pair + 1) * LANES]
            zero = jnp.zeros_like(qp)
            parts += [jnp.where(lower, qp, zero), jnp.where(lower, zero, qp)]
        s_ref[slot, :, :width] = (_dot_nt(jnp.concatenate(parts, axis=0), sink_slot(kv, n, k_ref))
                                  + bias_ref[kv, :, cols])

    def sink_slot(kv, n, ref):
        band, _, _ = band_of(n)
        x = ref[0, band, kv * LANES:(kv + 1) * LANES]
        if n == 0:
            return x
        return jnp.where(lax.broadcasted_iota(jnp.int32, x.shape, 0) == 0, jnp.zeros_like(x), x)

    def update(kv, n, slot):
        rows = slice(n * qb, (n + 1) * qb)
        _, width, _ = band_of(n)
        vb = sink_slot(kv, n, v_ref)
        vb = jnp.concatenate([vb, jnp.ones_like(vb)], axis=-1)
        s = s_ref[slot, :, :width]
        m = jnp.max(s, axis=-1, keepdims=True)
        if n == 0:
            sink = sinkc_ref[kv]
            m = jnp.maximum(m, sink)
        p = jnp.concatenate(
            [jnp.exp2(s[:, c * LANES:(c + 1) * LANES] - m) for c in range(width // LANES)],
            axis=-1).astype(jnp.bfloat16)
        acc = _dot(p, vb)
        denom = acc[:, LANES:] + jnp.exp2(sink - m) if n == 0 else acc[:, LANES:]
        out = acc[:, :LANES] / denom
        for pair in range(SWA_GROUP // 2):
            even = slice(2 * pair * qb, (2 * pair + 1) * qb)
            odd = slice((2 * pair + 1) * qb, (2 * pair + 2) * qb)
            o_ref[0, rows, kv * qw + pair * LANES:kv * qw + (pair + 1) * LANES] = jnp.where(
                lower, out[even], out[odd]).astype(o_ref.dtype)

    work = [(kv, n) for kv in range(N_SWA_KV_HEADS) for n in range(seq // qb)]
    scores(*work[0], 0)
    for idx, item in enumerate(work):
        if idx + 1 < len(work):
            scores(*work[idx + 1], (idx + 1) % 2)
        update(*item, idx % 2)


def _swa_call(sq, sk, sv, rel_bias, sinks, batch, seq):
    bucket = _band_buckets()
    smem = pl.BlockSpec(memory_space=pltpu.SMEM)
    q_spec = pl.BlockSpec((1, seq, D_SWA), lambda b: (b, 0, 0))
    kv_spec = pl.BlockSpec((1, seq, N_SWA_KV_HEADS * LANES), lambda b: (b, 0, 0))
    return pl.pallas_call(
        functools.partial(_swa_kernel, seq=seq),
        grid=(batch,),
        in_specs=[pl.BlockSpec(bucket.shape, lambda b: (0, 0)), smem, smem,
                  q_spec, kv_spec, kv_spec],
        out_specs=q_spec,
        out_shape=jax.ShapeDtypeStruct((batch, seq, D_SWA), jnp.bfloat16),
        scratch_shapes=[
            pltpu.VMEM((N_SWA_KV_HEADS, SWA_GROUP * Q_BLOCK, 2 * Q_BLOCK), jnp.float32),
            pltpu.VMEM((N_SWA_KV_HEADS, SWA_GROUP * Q_BLOCK, LANES), jnp.float32),
            pltpu.VMEM((2, SWA_GROUP * Q_BLOCK, 2 * Q_BLOCK), jnp.float32)],
        compiler_params=pltpu.CompilerParams(
            dimension_semantics=("arbitrary",),
            vmem_limit_bytes=VMEM_LIMIT),
        name="swa",
    )(jnp.asarray(bucket), rel_bias.T, sinks, sq.reshape(batch, seq, D_SWA),
      sk.reshape(batch, seq, 2 * LANES), sv.reshape(batch, seq, 2 * LANES))


def _post_kernel(fox_ref, swa_ref, x_ref, p_ref, wo_ref, w1_ref, w2_ref, wg_ref, wp_ref,
                 g_attn_ref, g_pre_ref, g_ff_ref, g_ple_ref, o_ref):
    bf16 = jnp.bfloat16
    h_parts, m_parts = [], []
    for r in range(0, POST_TM, POST_STRIP):
        rows = slice(r, r + POST_STRIP)
        mix = _dot(jnp.concatenate([fox_ref[rows, :], swa_ref[rows, :]], axis=-1), wo_ref[...])
        hh = x_ref[rows, :] + _rms(mix) * g_attn_ref[...]
        h_parts.append(hh)
        m_parts.append((_rms(hh) * g_pre_ref[...]).astype(bf16))
    h = jnp.concatenate(h_parts, axis=0)
    m = jnp.concatenate(m_parts, axis=0)
    ple = _dot(p_ref[...].astype(bf16), wp_ref[...])

    y = jnp.zeros((POST_TM, D_MODEL), jnp.float32)
    for c in range(D_FF // FF_CHUNK):
        cols = slice(c * FF_CHUNK, (c + 1) * FF_CHUNK)
        hid = jnp.square(jnp.maximum(_dot(m, w1_ref[:, cols]), 0.0))
        y = y + _dot(hid.astype(bf16), w2_ref[cols, :])

    for r in range(0, POST_TM, POST_STRIP):
        rows = slice(r, r + POST_STRIP)
        hh = h[rows] + _rms(y[rows]) * g_ff_ref[...]
        gate = jax.nn.sigmoid(_dot(hh.astype(bf16), wg_ref[...]))
        o_ref[rows, :] = hh + _rms(ple[rows] * gate) * g_ple_ref[...]


def _post_call(fox, swa, x2, p2, wo, w1, w2, wg, wp, gains):
    n = x2.shape[0]
    row = lambda i: (i, 0)
    const = lambda i: (0, 0)

    def resident(shape):
        return pl.BlockSpec(shape, const, pipeline_mode=pl.Buffered(1))

    return pl.pallas_call(
        _post_kernel,
        grid=(n // POST_TM,),
        in_specs=[
            pl.BlockSpec((POST_TM, D_FOX), row),
            pl.BlockSpec((POST_TM, D_SWA), row),
            pl.BlockSpec((POST_TM, D_MODEL), row),
            pl.BlockSpec((POST_TM, D_PLE), row),
            resident(wo.shape), resident(w1.shape), resident(w2.shape),
            resident(wg.shape), resident(wp.shape), *[resident(g.shape) for g in gains],
        ],
        out_specs=pl.BlockSpec((POST_TM, D_MODEL), row),
        out_shape=jax.ShapeDtypeStruct((n, D_MODEL), jnp.float32),
        compiler_params=pltpu.CompilerParams(
            dimension_semantics=("parallel",),
            vmem_limit_bytes=VMEM_LIMIT),
        name="post",
    )(fox, swa, x2, p2, wo, w1, w2, wg, wp, *gains)


def kernel(x, p, w_in, b_forget, w_out, rel_bias, swa_sinks, g_attn_pre, g_attn_post,
           w_ff1, w_ff2, g_ff_pre, g_ff_post, w_ple, w_ple_gate, g_ple_post):
    batch, seq, d_model = x.shape
    n = batch * seq
    assert d_model == D_MODEL and seq % PROJ_TM == 0 and seq % FOX_T == 0
    assert FOX_KEYS % FOX_T == 0 and seq % Q_BLOCK == 0 and n % POST_TM == 0
    h = x.reshape(n, D_MODEL)
    for i in range(p.shape[0]):
        fq, fk, fv, qa, ka, sq, sk, sv, wp = _proj_call(
            h, g_attn_pre[i].reshape(1, D_MODEL), w_in[i].T, b_forget[i], w_ple[i], batch, seq)
        fox, wo, w1, w2, wg = _fox_call(
            fq, qa, fk, ka, fv, [w_out[i], w_ff1[i], w_ff2[i], w_ple_gate[i]], batch, seq)
        swa = _swa_call(sq, sk, sv, rel_bias, swa_sinks[i], batch, seq).reshape(n, D_SWA)
        gains = [g[i].reshape(1, D_MODEL) for g in (g_attn_post, g_ff_pre, g_ff_post, g_ple_post)]
        h = _post_call(fox.reshape(n, D_FOX), swa, h, p[i].reshape(n, D_PLE),
                       wo, w1, w2, wg, wp, gains)
    return h.reshape(batch, seq, D_MODEL)
```
